```python
import math
import jax
import jax.numpy as jnp
from jax import lax
import numpy as np

D_MODEL = 1024
BATCH = 2
SEQ = 8192
DEPTH = 1
DEC_BATCH = 32
DEC_SEQ = 8
PAST_LEN = 8192
PAGE_SIZE = 128

ATT_GROUPS = ((128, 1), (512, 4), (2048, 16))
N_GROUPS = 3
ATT_HEADS = 4
ATT_DH = 128
ATT_W = ATT_HEADS * ATT_DH
Q_ATT = N_GROUPS * ATT_W
BAND_BLK = 128
REL_BUCKETS = 32
REL_MAX_DIST = 2048
HG_HEADS = 4
HG_DK = 128
HG_DV = 128
HG_W = HG_HEADS * HG_DK
HG_VW = HG_HEADS * HG_DV
HG_CHUNK = 64
MEM_TOKENS = 256
MEM_HEADS = 4
MEM_DH = 128
MEM_W = MEM_HEADS * MEM_DH
D_FF = 4 * D_MODEL
CONV_W = 3
NORM_EPS = 1e-6
NEG_INF = -1e30
IN_SPLITS = (Q_ATT, Q_ATT, Q_ATT, HG_W, HG_W, HG_VW, HG_VW, MEM_W, D_MODEL, D_MODEL, D_MODEL)
IN_COLS = 3 * Q_ATT + 2 * HG_W + 2 * HG_VW + MEM_W + 3 * D_MODEL

kernel_name = "hybrid_dilated_hgrn2_memory_decoder_step"


def rms_norm(x, gain):
    xf = x.astype(jnp.float32)
    y = xf * lax.rsqrt(jnp.mean(xf * xf, axis=-1, keepdims=True) + NORM_EPS)
    return (y * gain.astype(jnp.float32)).astype(x.dtype)


def rel_bucket(dist):
    max_exact = REL_BUCKETS // 2
    d = jnp.maximum(dist, 1).astype(jnp.float32)
    large = max_exact + (jnp.log(d / max_exact) / math.log(REL_MAX_DIST / max_exact)
                         * (REL_BUCKETS - max_exact)).astype(jnp.int32)
    large = jnp.minimum(large, REL_BUCKETS - 1)
    return jnp.where(dist < max_exact, dist, large)


def group_lag_bias(rel_bias, g):
    window, dil = ATT_GROUPS[g]
    dist = jnp.arange(window // dil + 1, dtype=jnp.int32) * dil
    table = rel_bias[rel_bucket(dist)].astype(jnp.float32)
    return table[:, g * ATT_HEADS:(g + 1) * ATT_HEADS].T


def masked_softmax_parts(logits, valid):
    logits = jnp.where(valid, logits, NEG_INF)
    m = jnp.max(logits, axis=-1, keepdims=True)
    p = jnp.exp(logits - m)
    s = jnp.sum(p, axis=-1, keepdims=True)
    return p / s, s, m


def dilated_group_prompt(q, k, v, bias_lag, window, dil):
    B, S, H, Dh = q.shape
    n_lags = window // dil
    span = dil * BAND_BLK
    s_pad = -(-S // span) * span
    L = s_pad // dil
    nb = L // BAND_BLK

    def to_blocks(t):
        t = jnp.pad(t.astype(jnp.float32), ((0, 0), (0, s_pad - S), (0, 0), (0, 0)))
        t = t.reshape(B, L, dil, H, Dh).transpose(0, 2, 3, 1, 4)
        return t.reshape(B, dil, H, nb, BAND_BLK, Dh)

    def with_prev(t):
        prev = jnp.pad(t, ((0, 0), (0, 0), (0, 0), (1, 0), (0, 0), (0, 0)))[:, :, :, :-1]
        return jnp.concatenate([prev, t], axis=4)

    qb = to_blocks(q)
    kb = with_prev(to_blocks(k))
    vb = with_prev(to_blocks(v))
    qi = jnp.arange(BAND_BLK)[:, None]
    kj = jnp.arange(2 * BAND_BLK)[None, :]
    lag = BAND_BLK + qi - kj
    in_band = (lag >= 0) & (lag <= n_lags)
    blk = jnp.arange(nb)[:, None, None]
    valid = in_band[None] & ((blk > 0) | (kj >= BAND_BLK)[None])
    bias = bias_lag[:, jnp.clip(lag, 0, n_lags)]
    logits = (jnp.einsum('brhnqd,brhnkd->brhnqk', qb, kb) / math.sqrt(Dh)
              + bias[None, None, :, None])
    p, s, m = masked_softmax_parts(logits, valid[None, None, None])
    o = jnp.einsum('brhnqk,brhnkd->brhnqd', p, vb)
    lse = (m + jnp.log(s))[..., 0]
    o = o.reshape(B, dil, H, L, Dh).transpose(0, 3, 1, 2, 4).reshape(B, s_pad, H, Dh)[:, :S]
    lse = lse.reshape(B, dil, H, L).transpose(0, 3, 1, 2).reshape(B, s_pad, H)[:, :S]
    return o, lse


def dilated_group_sample(q, k_ctx, v_ctx, bias_lag, window, dil, n_past):
    T, Dh = q.shape[1], q.shape[3]
    n_lags = window // dil
    idx = n_past + jnp.arange(T)[:, None] - dil * jnp.arange(n_lags + 1)[None, :]
    valid = idx >= 0
    idx = jnp.maximum(idx, 0)
    kg = k_ctx.astype(jnp.float32)[:, idx]
    vg = v_ctx.astype(jnp.float32)[:, idx]
    logits = (jnp.einsum('bthd,btjhd->bhtj', q.astype(jnp.float32), kg) / math.sqrt(Dh)
              + bias_lag[:, None, :])
    p, s, m = masked_softmax_parts(logits, valid[None, None])
    o = jnp.einsum('bhtj,btjhd->bthd', p, vg)
    lse = (m + jnp.log(s))[..., 0].transpose(0, 2, 1)
    return o, lse


def dilated_attention(q, k, v, bias_lags, win_bufs):
    outs, lses, new_bufs = [], [], []
    for g, (window, dil) in enumerate(ATT_GROUPS):
        qg, kg, vg = q[:, :, g], k[:, :, g], v[:, :, g]
        if win_bufs is None:
            o, lse = dilated_group_prompt(qg, kg, vg, bias_lags[g], window, dil)
            kv = jnp.stack([kg, vg], axis=2)[:, -min(window, qg.shape[1]):]
        else:
            buf = win_bufs[g]
            n_past = buf.shape[1]
            k_ctx = jnp.concatenate([buf[:, :, 0].astype(kg.dtype), kg], axis=1)
            v_ctx = jnp.concatenate([buf[:, :, 1].astype(vg.dtype), vg], axis=1)
            o, lse = dilated_group_sample(qg, k_ctx, v_ctx, bias_lags[g], window, dil, n_past)
            kv = jnp.stack([k_ctx, v_ctx], axis=2)[:, -n_past:]
        outs.append(o)
        lses.append(lse)
        new_bufs.append(kv)
    w = jax.nn.softmax(jnp.stack(lses), axis=0)
    o = jnp.einsum('gbth,gbthd->bthd', w, jnp.stack(outs))
    return o, new_bufs


def hgrn2_recurrence(q, f_logit, i, lb, state0):
    f32 = jnp.float32
    B, T, H, Dk = q.shape
    Dv = i.shape[-1]
    f = lb + (1.0 - lb) * jax.nn.sigmoid(f_logit.astype(f32))
    g = jnp.log(f)
    k = 1.0 - f
    C = min(HG_CHUNK, T)
    t_pad = -(-T // C) * C
    n = t_pad // C

    def chunks(t):
        t = jnp.pad(t.astype(f32), ((0, 0), (0, t_pad - T), (0, 0), (0, 0)))
        return t.reshape(B, n, C, H, t.shape[-1]).transpose(1, 0, 3, 2, 4)

    causal = (jnp.arange(C)[:, None] >= jnp.arange(C)[None, :])[:, :, None]

    def step(S, inp):
        qc, gc, kc, vc = inp
        G = jnp.cumsum(gc, axis=2)
        o_inter = jnp.einsum('bhtk,bhkv->bhtv', qc * jnp.exp(G), S)
        diff = G[:, :, :, None, :] - G[:, :, None, :, :]
        decay = jnp.where(causal, jnp.exp(jnp.minimum(diff, 0.0)), 0.0)
        scores = jnp.einsum('bhtk,bhsk,bhtsk->bhts', qc, kc, decay)
        o_intra = jnp.einsum('bhts,bhsv->bhtv', scores, vc)
        G_end = G[:, :, -1:, :]
        S_new = (jnp.exp(G_end[:, :, 0, :, None]) * S
                 + jnp.einsum('bhsk,bhsv->bhkv', kc * jnp.exp(G_end - G), vc))
        return S_new, o_inter + o_intra

    S_fin, o = lax.scan(step, state0.astype(f32), (chunks(q), chunks(g), chunks(k), chunks(i)))
    o = o.transpose(1, 0, 3, 2, 4).reshape(B, t_pad, H, Dv)[:, :T]
    return o, S_fin


def memory_attention(q, mem_k, mem_v):
    logits = jnp.einsum('bthd,bmhd->bhtm', q.astype(jnp.float32), mem_k.astype(jnp.float32)) / math.sqrt(MEM_DH)
    p = jax.nn.softmax(logits, axis=-1)
    return jnp.einsum('bhtm,bmhd->bthd', p, mem_v.astype(jnp.float32))


def memory_kv(mem, gain, w_kv):
    B, M, _ = mem.shape
    return (rms_norm(mem, gain) @ w_kv).reshape(B, M, 2, MEM_HEADS, MEM_DH)


def token_mixer(h, win_bufs, hg_state0, mem_k, mem_v, bias_lags, lb,
                w_in, b_in, hg_norm, w_br_att, w_br_hg, w_br_mem, w_out):
    B, T, _ = h.shape
    dt = h.dtype
    offsets = np.cumsum(IN_SPLITS)[:-1].tolist()
    proj = h @ w_in + b_in
    q_a, k_a, v_a, hq, hf, hi, hgate, mq, ga, gh, gm = jnp.split(proj, offsets, axis=-1)
    att_shape = (B, T, N_GROUPS, ATT_HEADS, ATT_DH)
    att_o, new_win = dilated_attention(q_a.reshape(att_shape), k_a.reshape(att_shape),
                                       v_a.reshape(att_shape), bias_lags, win_bufs)
    hg_o, hg_state = hgrn2_recurrence(hq.reshape(B, T, HG_HEADS, HG_DK), hf.reshape(B, T, HG_HEADS, HG_DK),
                                      hi.reshape(B, T, HG_HEADS, HG_DV), lb.reshape(HG_HEADS, HG_DK), hg_state0)
    hg_o = rms_norm(hg_o, hg_norm) * jax.nn.sigmoid(hgate.reshape(B, T, HG_HEADS, HG_DV).astype(jnp.float32))
    mem_o = memory_attention(mq.reshape(B, T, MEM_HEADS, MEM_DH), mem_k, mem_v)
    merged = (jax.nn.sigmoid(ga) * (att_o.reshape(B, T, ATT_W).astype(dt) @ w_br_att)
              + jax.nn.sigmoid(gh) * (hg_o.reshape(B, T, HG_VW).astype(dt) @ w_br_hg)
              + jax.nn.sigmoid(gm) * (mem_o.reshape(B, T, MEM_W).astype(dt) @ w_br_mem))
    return merged @ w_out, new_win, hg_state


def conv_ffn(h, conv_buf, w_a, w_b, conv_w, conv_b, w_d):
    T = h.shape[1]
    a = h @ w_a
    ctx = jnp.concatenate([conv_buf.astype(a.dtype), a], axis=1)
    c = conv_b
    for j in range(CONV_W):
        c = c + ctx[:, j:j + T] * conv_w[j]
    y = (jax.nn.silu(c) * (h @ w_b)) @ w_d
    return y, ctx[:, -(CONV_W - 1):]


def decoder_layer(x, win_bufs, hg_state0, conv_buf0, mem_k, mem_v, bias_lags, lb,
                  norm_mix_pre, norm_mix_post, w_in, b_in, hg_norm, w_br_att, w_br_hg, w_br_mem, w_out,
                  norm_ffn_pre, norm_ffn_post, w_ffn_a, w_ffn_b, ffn_conv_w, ffn_conv_b, w_ffn_d):
    mix, new_win, hg_state = token_mixer(rms_norm(x, norm_mix_pre), win_bufs, hg_state0, mem_k, mem_v,
                                         bias_lags, lb, w_in, b_in, hg_norm, w_br_att, w_br_hg, w_br_mem, w_out)
    x = x + rms_norm(mix, norm_mix_post)
    f, conv_buf = conv_ffn(rms_norm(x, norm_ffn_pre), conv_buf0, w_ffn_a, w_ffn_b, ffn_conv_w, ffn_conv_b, w_ffn_d)
    x = x + rms_norm(f, norm_ffn_post)
    return x, new_win, hg_state, conv_buf


def setup_inputs(seed: int = 0) -> dict:
    key = jax.random.key(seed)
    keys = iter(jax.random.split(key, 40))

    def nrm(shape, scale=1.0):
        return jax.random.normal(next(keys), shape, jnp.float32) * scale

    def gain(shape):
        return 1.0 + nrm(shape, 0.05)

    win_lens = [min(w, PAST_LEN) for w, _ in ATT_GROUPS]
    return {
        'x_prompt': nrm((BATCH, SEQ, D_MODEL)),
        'x_sample': nrm((DEC_BATCH, DEC_SEQ, D_MODEL)),
        'mem_prompt': nrm((BATCH, MEM_TOKENS, D_MODEL)),
        'cache_win1_kv': nrm((DEPTH, DEC_BATCH, win_lens[0], 2, ATT_HEADS, ATT_DH)),
        'cache_win2_kv': nrm((DEPTH, DEC_BATCH, win_lens[1], 2, ATT_HEADS, ATT_DH)),
        'cache_win3_kv': nrm((DEPTH, DEC_BATCH, win_lens[2], 2, ATT_HEADS, ATT_DH)),
        'cache_mem_kv': nrm((DEPTH, DEC_BATCH, MEM_TOKENS, 2, MEM_HEADS, MEM_DH)),
        'state_hgrn': nrm((DEPTH, DEC_BATCH, HG_HEADS, HG_DK, HG_DV), 0.5),
        'state_ffn_conv': nrm((DEPTH, DEC_BATCH, CONV_W - 1, D_FF)),
        'rel_bias': nrm((REL_BUCKETS, N_GROUPS * ATT_HEADS), 0.5),
        'hg_lb_logits': nrm((DEPTH + 1, HG_W), 0.5),
        'norm_mix_pre': gain((DEPTH, D_MODEL)),
        'norm_mix_post': gain((DEPTH, D_MODEL)),
        'w_in': nrm((DEPTH, D_MODEL, IN_COLS), D_MODEL ** -0.5),
        'b_in': nrm((DEPTH, IN_COLS), 0.02),
        'hg_norm': gain((DEPTH, HG_DV)),
        'mem_norm': gain((DEPTH, D_MODEL)),
        'w_mem_kv': nrm((DEPTH, D_MODEL, 2 * MEM_W), D_MODEL ** -0.5),
        'w_br_att': nrm((DEPTH, ATT_W, D_MODEL), ATT_W ** -0.5),
        'w_br_hg': nrm((DEPTH, HG_VW, D_MODEL), HG_VW ** -0.5),
        'w_br_mem': nrm((DEPTH, MEM_W, D_MODEL), MEM_W ** -0.5),
        'w_out': nrm((DEPTH, D_MODEL, D_MODEL), D_MODEL ** -0.5),
        'norm_ffn_pre': gain((DEPTH, D_MODEL)),
        'norm_ffn_post': gain((DEPTH, D_MODEL)),
        'w_ffn_a': nrm((DEPTH, D_MODEL, D_FF), D_MODEL ** -0.5),
        'w_ffn_b': nrm((DEPTH, D_MODEL, D_FF), D_MODEL ** -0.5),
        'ffn_conv_w': nrm((DEPTH, CONV_W, D_FF), CONV_W ** -0.5),
        'ffn_conv_b': nrm((DEPTH, D_FF), 0.02),
        'w_ffn_d': nrm((DEPTH, D_FF, D_MODEL), D_FF ** -0.5),
    }


def reference(x_prompt, x_sample, mem_prompt, cache_win1_kv, cache_win2_kv, cache_win3_kv, cache_mem_kv,
              state_hgrn, state_ffn_conv, rel_bias, hg_lb_logits, norm_mix_pre, norm_mix_post, w_in, b_in,
              hg_norm, mem_norm, w_mem_kv, w_br_att, w_br_hg, w_br_mem, w_out, norm_ffn_pre, norm_ffn_post,
              w_ffn_a, w_ffn_b, ffn_conv_w, ffn_conv_b, w_ffn_d):
    f32 = jnp.float32
    bias_lags = [group_lag_bias(rel_bias, g) for g in range(N_GROUPS)]
    lower_bounds = jnp.cumsum(jax.nn.softmax(hg_lb_logits.astype(f32), axis=0), axis=0)
    B = x_prompt.shape[0]
    xp, xs = x_prompt, x_sample
    p_win = [[] for _ in range(N_GROUPS)]
    s_win = [[] for _ in range(N_GROUPS)]
    p_hg, p_conv, p_mem, s_hg, s_conv = [], [], [], [], []
    for layer in range(DEPTH):
        lw = (norm_mix_pre[layer], norm_mix_post[layer], w_in[layer], b_in[layer], hg_norm[layer],
              w_br_att[layer], w_br_hg[layer], w_br_mem[layer], w_out[layer], norm_ffn_pre[layer],
              norm_ffn_post[layer], w_ffn_a[layer], w_ffn_b[layer], ffn_conv_w[layer], ffn_conv_b[layer],
              w_ffn_d[layer])
        lb = lower_bounds[layer]
        mem_kv = memory_kv(mem_prompt, mem_norm[layer], w_mem_kv[layer])
        xp, win_p, hg_p, conv_p = decoder_layer(
            xp, None, jnp.zeros((B, HG_HEADS, HG_DK, HG_DV), f32),
            jnp.zeros((B, CONV_W - 1, D_FF), xp.dtype), mem_kv[:, :, 0], mem_kv[:, :, 1], bias_lags, lb, *lw)
        win_bufs = (cache_win1_kv[layer], cache_win2_kv[layer], cache_win3_kv[layer])
        xs, win_s, hg_s, conv_s = decoder_layer(
            xs, win_bufs, state_hgrn[layer], state_ffn_conv[layer],
            cache_mem_kv[layer][:, :, 0], cache_mem_kv[layer][:, :, 1], bias_lags, lb, *lw)
        for g in range(N_GROUPS):
            p_win[g].append(win_p[g])
            s_win[g].append(win_s[g])
        p_hg.append(hg_p.astype(x_prompt.dtype))
        p_conv.append(conv_p)
        p_mem.append(mem_kv)
        s_hg.append(hg_s.astype(x_sample.dtype))
        s_conv.append(conv_s)
    return (xp, xs,
            jnp.stack(p_win[0]), jnp.stack(p_win[1]), jnp.stack(p_win[2]),
            jnp.stack(p_hg), jnp.stack(p_conv), jnp.stack(p_mem),
            jnp.stack(s_win[0]), jnp.stack(s_win[1]), jnp.stack(s_win[2]),
            jnp.stack(s_hg), jnp.stack(s_conv))
```

```python
import functools
import math

import numpy as np
import jax
import jax.numpy as jnp
from jax import lax
from jax.experimental import pallas as pl
from jax.experimental.pallas import tpu as pltpu

F32 = jnp.float32
BF16 = jnp.bfloat16

NORM_EPS = 1e-6
NEG_INF = -1e30
HEAD_DIM = 128
N_HEADS = 4
BRANCH_W = N_HEADS * HEAD_DIM
N_LAGS = 128
BAND = 128
ATT_GROUPS = ((128, 1), (512, 4), (2048, 16))
REL_BUCKETS = 32
REL_MAX_DIST = 2048
LSE_LANES = 32
HG_BASE = 8
VMEM_LIMIT = 56 * 1024 * 1024

NT_DIMS = (((1,), (1,)), ((), ()))
TN_DIMS = (((0,), (0,)), ((), ()))

A_COLS = 12 * BRANCH_W
A_HQ, A_HI, A_MQ = 9, 10, 11
G_COLS = 8 * BRANCH_W


def _cparams(*sem):
    return pltpu.CompilerParams(dimension_semantics=sem, vmem_limit_bytes=VMEM_LIMIT)


def _rms(x, gain):
    return x * lax.rsqrt(jnp.mean(x * x, axis=-1, keepdims=True) + NORM_EPS) * gain


def _sigmoid(x):
    return 1.0 / (1.0 + jnp.exp(-x))


def _norm_matmul_kernel(x_ref, g_ref, w_ref, b_ref, o_ref, hn_ref):
    @pl.when(pl.program_id(1) == 0)
    def _():
        hn_ref[...] = _rms(x_ref[...], g_ref[...]).astype(BF16)

    acc = jnp.dot(hn_ref[...], w_ref[...], preferred_element_type=F32)
    o_ref[...] = (acc + b_ref[...]).astype(o_ref.dtype)


def norm_matmul(x, gain, w, bias, out_dtype, tm, tn):
    m, k = x.shape
    n = w.shape[1]
    assert m % tm == 0 and n % tn == 0
    return pl.pallas_call(
        _norm_matmul_kernel,
        grid=(m // tm, n // tn),
        in_specs=[
            pl.BlockSpec((tm, k), lambda i, j: (i, 0)),
            pl.BlockSpec((1, k), lambda i, j: (0, 0)),
            pl.BlockSpec((k, tn), lambda i, j: (0, j)),
            pl.BlockSpec((1, tn), lambda i, j: (0, j)),
        ],
        out_specs=pl.BlockSpec((tm, tn), lambda i, j: (i, j)),
        out_shape=jax.ShapeDtypeStruct((m, n), out_dtype),
        scratch_shapes=[pltpu.VMEM((tm, k), BF16)],
        compiler_params=_cparams("parallel", "arbitrary"),
        name="norm_matmul",
    )(x, gain.reshape(1, k), w, bias.reshape(1, n))


def _dil_prompt_kernel(q_ref, kp_ref, kc_ref, vp_ref, vc_ref, bp_ref, bc_ref, o_ref, lse_ref):
    first = pl.program_id(2) == 0
    scale = 1.0 / math.sqrt(HEAD_DIM)
    for h in range(N_HEADS):
        hs = slice(h * HEAD_DIM, (h + 1) * HEAD_DIM)
        q = q_ref[:, hs]
        lp = lax.dot_general(q, kp_ref[:, hs], NT_DIMS, preferred_element_type=F32) * scale + bp_ref[h]
        lp = jnp.where(first, NEG_INF, lp)
        lc = lax.dot_general(q, kc_ref[:, hs], NT_DIMS, preferred_element_type=F32) * scale + bc_ref[h]
        m = jnp.maximum(jnp.max(lp, axis=-1, keepdims=True), jnp.max(lc, axis=-1, keepdims=True))
        pp = jnp.exp(lp - m)
        pc = jnp.exp(lc - m)
        s = jnp.sum(pp, axis=-1, keepdims=True) + jnp.sum(pc, axis=-1, keepdims=True)
        o = (jnp.dot(pp.astype(BF16), vp_ref[:, hs], preferred_element_type=F32)
             + jnp.dot(pc.astype(BF16), vc_ref[:, hs], preferred_element_type=F32))
        o_ref[:, hs] = (o / s).astype(o_ref.dtype)
        lse_ref[:, h * LSE_LANES:(h + 1) * LSE_LANES] = jnp.broadcast_to(m + jnp.log(s), (BAND, LSE_LANES))


def dilated_prompt(act, bias_prev, bias_cur, g, dil, batch, seq):
    assert seq % (dil * BAND) == 0
    sub = seq // dil
    nblk = A_COLS // BRANCH_W
    x = act.reshape(batch, sub, dil * A_COLS)
    blk = (None, BAND, BRANCH_W)
    q_spec = pl.BlockSpec(blk, lambda b, r, i: (b, i, r * nblk + g))
    kp_spec = pl.BlockSpec(blk, lambda b, r, i: (b, jnp.maximum(i - 1, 0), r * nblk + 3 + g))
    kc_spec = pl.BlockSpec(blk, lambda b, r, i: (b, i, r * nblk + 3 + g))
    vp_spec = pl.BlockSpec(blk, lambda b, r, i: (b, jnp.maximum(i - 1, 0), r * nblk + 6 + g))
    vc_spec = pl.BlockSpec(blk, lambda b, r, i: (b, i, r * nblk + 6 + g))
    bias_spec = pl.BlockSpec((N_HEADS, BAND, BAND), lambda b, r, i: (0, 0, 0))
    o, lse = pl.pallas_call(
        _dil_prompt_kernel,
        grid=(batch, dil, sub // BAND),
        in_specs=[q_spec, kp_spec, kc_spec, vp_spec, vc_spec, bias_spec, bias_spec],
        out_specs=[pl.BlockSpec(blk, lambda b, r, i: (b, i, r)),
                   pl.BlockSpec((None, BAND, N_HEADS * LSE_LANES), lambda b, r, i: (b, i, r))],
        out_shape=[jax.ShapeDtypeStruct((batch, sub, dil * BRANCH_W), BF16),
                   jax.ShapeDtypeStruct((batch, sub, dil * N_HEADS * LSE_LANES), F32)],
        compiler_params=_cparams("parallel", "parallel", "arbitrary"),
        name=f"dilated_prompt_g{g}",
    )(x, x, x, x, x, bias_prev, bias_cur)
    return o.reshape(batch * seq, BRANCH_W), lse.reshape(batch * seq, N_HEADS * LSE_LANES)


def _dil_sample_kernel(q_ref, kn_ref, vn_ref, cache_ref, bc_ref, bn_ref, o_ref, lse_ref, newc_ref, *, win, t_new):
    scale = 1.0 / math.sqrt(HEAD_DIM)
    for h in range(N_HEADS):
        hs = slice(h * HEAD_DIM, (h + 1) * HEAD_DIM)
        vs = slice(BRANCH_W + h * HEAD_DIM, BRANCH_W + (h + 1) * HEAD_DIM)
        q = q_ref[:, hs]
        kn = kn_ref[:, hs]
        vn = vn_ref[:, hs]
        lc = lax.dot_general(q.astype(BF16), cache_ref[:, hs].astype(BF16), NT_DIMS,
                             preferred_element_type=F32) * scale + bc_ref[h]
        ln = lax.dot_general(q, kn, NT_DIMS, preferred_element_type=F32) * scale + bn_ref[h]
        m = jnp.maximum(jnp.max(lc, axis=-1, keepdims=True), jnp.max(ln, axis=-1, keepdims=True))
        pc = jnp.exp(lc - m)
        pn = jnp.exp(ln - m)
        s = jnp.sum(pc, axis=-1, keepdims=True) + jnp.sum(pn, axis=-1, keepdims=True)
        o = (jnp.dot(pc.astype(BF16), cache_ref[:, vs].astype(BF16), preferred_element_type=F32)
             + jnp.dot(pn, vn, preferred_element_type=F32))
        o_ref[:, hs] = o / s
        lse_ref[:, h * LSE_LANES:(h + 1) * LSE_LANES] = jnp.broadcast_to(m + jnp.log(s), (t_new, LSE_LANES))
    newc_ref[0:win - t_new, :] = cache_ref[t_new:win, :]
    newc_ref[win - t_new:win, 0:BRANCH_W] = kn_ref[...]
    newc_ref[win - t_new:win, BRANCH_W:2 * BRANCH_W] = vn_ref[...]


def dilated_sample(act, cache, bias_cache, bias_new, g, batch, t_new):
    win = cache.shape[1]
    x = act.reshape(batch, t_new, A_COLS)
    blk = (None, t_new, BRANCH_W)
    o, lse, newc = pl.pallas_call(
        functools.partial(_dil_sample_kernel, win=win, t_new=t_new),
        grid=(batch,),
        in_specs=[pl.BlockSpec(blk, lambda b: (b, 0, g)),
                  pl.BlockSpec(blk, lambda b: (b, 0, 3 + g)),
                  pl.BlockSpec(blk, lambda b: (b, 0, 6 + g)),
                  pl.BlockSpec((None, win, 2 * BRANCH_W), lambda b: (b, 0, 0)),
                  pl.BlockSpec((N_HEADS, t_new, win), lambda b: (0, 0, 0)),
                  pl.BlockSpec((N_HEADS, t_new, t_new), lambda b: (0, 0, 0))],
        out_specs=[pl.BlockSpec(blk, lambda b: (b, 0, 0)),
                   pl.BlockSpec((None, t_new, N_HEADS * LSE_LANES), lambda b: (b, 0, 0)),
                   pl.BlockSpec((None, win, 2 * BRANCH_W), lambda b: (b, 0, 0))],
        out_shape=[jax.ShapeDtypeStruct((batch, t_new, BRANCH_W), F32),
                   jax.ShapeDtypeStruct((batch, t_new, N_HEADS * LSE_LANES), F32),
                   jax.ShapeDtypeStruct(cache.shape, F32)],
        compiler_params=_cparams("parallel"),
        name=f"dilated_sample_g{g}",
    )(x, x, x, cache, bias_cache, bias_new)
    return o.reshape(batch * t_new, BRANCH_W), lse.reshape(batch * t_new, N_HEADS * LSE_LANES), newc


def _mem_attn_kernel(q_ref, kv_ref, o_ref):
    scale = 1.0 / math.sqrt(HEAD_DIM)
    for h in range(N_HEADS):
        hs = slice(h * HEAD_DIM, (h + 1) * HEAD_DIM)
        vs = slice(BRANCH_W + h * HEAD_DIM, BRANCH_W + (h + 1) * HEAD_DIM)
        logits = lax.dot_general(q_ref[:, hs].astype(BF16), kv_ref[:, hs].astype(BF16), NT_DIMS,
                                 preferred_element_type=F32) * scale
        m = jnp.max(logits, axis=-1, keepdims=True)
        p = jnp.exp(logits - m)
        s = jnp.sum(p, axis=-1, keepdims=True)
        o = jnp.dot(p.astype(BF16), kv_ref[:, vs].astype(BF16), preferred_element_type=F32)
        o_ref[:, hs] = (o / s).astype(o_ref.dtype)


def memory_attention(act, mem_kv, batch, t, tq, out_dtype):
    mem = mem_kv.shape[1]
    x = act.reshape(batch, t, A_COLS)
    o = pl.pallas_call(
        _mem_attn_kernel,
        grid=(batch, t // tq),
        in_specs=[pl.BlockSpec((None, tq, BRANCH_W), lambda b, i: (b, i, A_MQ)),
                  pl.BlockSpec((None, mem, 2 * BRANCH_W), lambda b, i: (b, 0, 0))],
        out_specs=pl.BlockSpec((None, tq, BRANCH_W), lambda b, i: (b, i, 0)),
        out_shape=jax.ShapeDtypeStruct((batch, t, BRANCH_W), out_dtype),
        compiler_params=_cparams("parallel", "parallel"),
        name="memory_attention",
    )(x, mem_kv)
    return o.reshape(batch * t, BRANCH_W)


def _hgrn_tables(chunk):
    c, b = chunk, HG_BASE
    t = np.arange(c)[:, None]
    u = np.arange(c)[None, :]
    mats = [(u <= t)]
    masks = []
    size = 2 * b
    while size <= c:
        half = size // 2
        ref = (t // size) * size + half - 1
        second = (t % size) >= half
        mats.append(second & (u > ref) & (u <= t))
        mats.append((~second) & (u > t) & (u <= ref))
        masks.append(((t // size) == (u // size)) & second & ((u % size) < half))
        size *= 2
    for s in range(b):
        key = (t // b) * b + s
        mats.append((u > key) & (u <= t))
    mats.append(u > t)
    masks.append(((t // b) == (u // b)) & (u <= t))
    wall = np.concatenate(mats, axis=0).astype(np.float32)
    erep = np.zeros((b * HEAD_DIM, c), np.float32)
    for s in range(b):
        erep[s * HEAD_DIM:(s + 1) * HEAD_DIM, np.arange(c) % b == s] = 1.0
    return wall, erep, np.stack(masks).astype(np.float32)


def _hgrn_kernel(q_ref, v_ref, f_ref, gate_ref, lbl_ref, gn_ref, s0_ref, wall_ref, erep_ref, mask_ref,
                 o_ref, s_out_ref, st_ref, *, chunk, rows_in, n_levels):
    c = chunk
    step = pl.program_id(1)

    @pl.when(step == 0)
    def _():
        for h in range(N_HEADS):
            st_ref[h] = s0_ref[h].T

    def rows(x):
        x = x.astype(F32)
        if rows_in < c:
            x = jnp.concatenate([x, jnp.zeros((c - rows_in, x.shape[1]), F32)], axis=0)
        return x

    q = rows(q_ref[...])
    v = rows(v_ref[...]).astype(BF16)
    f_logit = rows(f_ref[...])

    lbl = lbl_ref[...]
    e = jnp.exp(lbl - jnp.max(lbl, axis=0, keepdims=True))
    lb = e[0:1] / jnp.sum(e, axis=0, keepdims=True)
    f = lb + (1.0 - lb) * _sigmoid(f_logit)
    g = jnp.log(f)
    kk = 1.0 - f
    if rows_in < c:
        live = lax.broadcasted_iota(jnp.int32, (c, 1), 0) < rows_in
        g = jnp.where(live, g, 0.0)
        kk = jnp.where(live, kk, 0.0)

    g1 = g.astype(BF16)
    r1 = g - g1.astype(F32)
    g2 = r1.astype(BF16)
    g3 = (r1 - g2.astype(F32)).astype(BF16)
    wall = wall_ref[...]
    spans = (jnp.dot(wall, g1, preferred_element_type=F32) + jnp.dot(wall, g2, preferred_element_type=F32)
             + jnp.dot(wall, g3, preferred_element_type=F32))

    def span(i):
        return spans[i * c:(i + 1) * c]

    q_inter = (q * jnp.exp(span(0))).astype(BF16)
    g_end = spans[c - 1:c]
    q_lvl = [(q * jnp.exp(span(1 + 2 * l))).astype(BF16) for l in range(n_levels)]
    k_lvl = [(kk * jnp.exp(span(2 + 2 * l))).astype(BF16) for l in range(n_levels)]
    base = 1 + 2 * n_levels
    kk3 = kk.reshape(c // HG_BASE, HG_BASE, kk.shape[1])
    x_diag = []
    for s in range(HG_BASE):
        kb = jnp.broadcast_to(kk3[:, s:s + 1, :], kk3.shape).reshape(c, kk.shape[1])
        x_diag.append((q * kb * jnp.exp(span(base + s))).astype(BF16))
    k_end = (kk * jnp.exp(span(base + HG_BASE))).astype(BF16)
    gate = _sigmoid(rows(gate_ref[...]))

    for h in range(N_HEADS):
        hs = slice(h * HEAD_DIM, (h + 1) * HEAD_DIM)
        scores = mask_ref[n_levels] * jnp.dot(jnp.concatenate([x[:, hs] for x in x_diag], axis=1), erep_ref[...],
                                              preferred_element_type=F32)
        for l in range(n_levels):
            scores += mask_ref[l] * lax.dot_general(q_lvl[l][:, hs], k_lvl[l][:, hs], NT_DIMS,
                                                    preferred_element_type=F32)
        st = st_ref[h]
        o = (lax.dot_general(q_inter[:, hs], st.astype(BF16), NT_DIMS, preferred_element_type=F32)
             + jnp.dot(scores.astype(BF16), v[:, hs], preferred_element_type=F32))
        st_ref[h] = (st * jnp.exp(g_end[:, hs])
                     + lax.dot_general(v[:, hs], k_end[:, hs], TN_DIMS, preferred_element_type=F32))
        y = _rms(o, gn_ref[...]) * gate[:, hs]
        o_ref[:, hs] = y[0:rows_in].astype(o_ref.dtype)

    @pl.when(step == pl.num_programs(1) - 1)
    def _():
        for h in range(N_HEADS):
            s_out_ref[h] = st_ref[h].T


def hgrn2(act, gates, lb_logits, hg_norm, state0, batch, t, chunk, out_dtype):
    rows_in = min(chunk, t)
    assert t % rows_in == 0 and chunk % HG_BASE == 0
    wall, erep, masks = _hgrn_tables(chunk)
    n_levels = masks.shape[0] - 1
    a = act.reshape(batch, t, A_COLS)
    gt = gates.reshape(batch, t, G_COLS)
    blk = (None, rows_in, BRANCH_W)
    full = lambda arr: pl.BlockSpec(arr.shape, lambda b, i: (0,) * arr.ndim)
    st_spec = pl.BlockSpec((None, N_HEADS, HEAD_DIM, HEAD_DIM), lambda b, i: (b, 0, 0, 0))
    wall = jnp.asarray(wall, BF16)
    erep = jnp.asarray(erep, BF16)
    masks = jnp.asarray(masks, F32)
    lbl = lb_logits.astype(F32)
    gn = hg_norm.reshape(1, HEAD_DIM).astype(F32)
    o, s_fin = pl.pallas_call(
        functools.partial(_hgrn_kernel, chunk=chunk, rows_in=rows_in, n_levels=n_levels),
        grid=(batch, t // rows_in),
        in_specs=[pl.BlockSpec(blk, lambda b, i: (b, i, A_HQ)),
                  pl.BlockSpec(blk, lambda b, i: (b, i, A_HI)),
                  pl.BlockSpec(blk, lambda b, i: (b, i, 0)),
                  pl.BlockSpec(blk, lambda b, i: (b, i, 1)),
                  full(lbl), full(gn), st_spec, full(wall), full(erep), full(masks)],
        out_specs=[pl.BlockSpec(blk, lambda b, i: (b, i, 0)), st_spec],
        out_shape=[jax.ShapeDtypeStruct((batch, t, BRANCH_W), out_dtype),
                   jax.ShapeDtypeStruct(state0.shape, F32)],
        scratch_shapes=[pltpu.VMEM((N_HEADS, HEAD_DIM, HEAD_DIM), F32)],
        compiler_params=_cparams("parallel", "arbitrary"),
        name="hgrn2",
    )(a, a, gt, gt, lbl, gn, state0, wall, erep, masks)
    return o.reshape(batch * t, BRANCH_W), s_fin


def _merge_kernel(x_ref, o1_ref, o2_ref, o3_ref, l1_ref, l2_ref, l3_ref, hg_ref, mem_ref, ga_ref, gh_ref, gm_ref,
                  wa_ref, wh_ref, wm_ref, wo_ref, gain_ref, out_ref):
    tm = x_ref.shape[0]
    l1, l2, l3 = l1_ref[...], l2_ref[...], l3_ref[...]
    m = jnp.maximum(jnp.maximum(l1, l2), l3)
    e1, e2, e3 = jnp.exp(l1 - m), jnp.exp(l2 - m), jnp.exp(l3 - m)
    den = e1 + e2 + e3
    w1, w2, w3 = e1 / den, e2 / den, e3 / den
    att = []
    for h in range(N_HEADS):
        hs = slice(h * HEAD_DIM, (h + 1) * HEAD_DIM)
        col = slice(h * LSE_LANES, h * LSE_LANES + 1)
        bc = lambda w: jnp.broadcast_to(w[:, col], (tm, HEAD_DIM))
        att.append((bc(w1) * o1_ref[:, hs].astype(F32) + bc(w2) * o2_ref[:, hs].astype(F32)
                    + bc(w3) * o3_ref[:, hs].astype(F32)).astype(BF16))
    att = jnp.concatenate(att, axis=1)
    merged = (_sigmoid(ga_ref[...]) * jnp.dot(att, wa_ref[...], preferred_element_type=F32)
              + _sigmoid(gh_ref[...]) * jnp.dot(hg_ref[...].astype(BF16), wh_ref[...], preferred_element_type=F32)
              + _sigmoid(gm_ref[...]) * jnp.dot(mem_ref[...].astype(BF16), wm_ref[...], preferred_element_type=F32))
    y = jnp.dot(merged.astype(BF16), wo_ref[...], preferred_element_type=F32)
    out_ref[...] = x_ref[...] + _rms(y, gain_ref[...])


def merge(x, o_groups, lse_groups, hg_o, mem_o, gates, w_att, w_hg, w_mem, w_out, gain, tm):
    m, d = x.shape
    assert m % tm == 0
    row = lambda w: pl.BlockSpec((tm, w), lambda i: (i, 0))
    full = lambda arr: pl.BlockSpec(arr.shape, lambda i: (0, 0))
    gate_spec = lambda j: pl.BlockSpec((tm, d), lambda i: (i, j))
    gain = gain.reshape(1, d)
    return pl.pallas_call(
        _merge_kernel,
        grid=(m // tm,),
        in_specs=[row(d), row(BRANCH_W), row(BRANCH_W), row(BRANCH_W),
                  row(N_HEADS * LSE_LANES), row(N_HEADS * LSE_LANES), row(N_HEADS * LSE_LANES),
                  row(BRANCH_W), row(BRANCH_W), gate_spec(1), gate_spec(2), gate_spec(3),
                  full(w_att), full(w_hg), full(w_mem), full(w_out), full(gain)],
        out_specs=row(d),
        out_shape=jax.ShapeDtypeStruct((m, d), F32),
        compiler_params=_cparams("parallel"),
        name="merge",
    )(x, *o_groups, *lse_groups, hg_o, mem_o, gates, gates, gates, w_att, w_hg, w_mem, w_out, gain)


def _ffn_kernel(x_ref, gpre_ref, wa_ref, wb_ref, cw_ref, cb_ref, wd_ref, gpost_ref, cbuf_ref,
                out_ref, tail_ref, hn_ref, acc_ref, carry_ref, *, t_seq, tiles_per_seq):
    tm = x_ref.shape[0]
    tf = wa_ref.shape[1]
    i = pl.program_id(0)
    j = pl.program_id(1)

    @pl.when(j == 0)
    def _():
        hn_ref[...] = _rms(x_ref[...], gpre_ref[...]).astype(BF16)
        acc_ref[...] = jnp.zeros_like(acc_ref)

    hn = hn_ref[...]
    a = jnp.dot(hn, wa_ref[...], preferred_element_type=F32)
    up = jnp.dot(hn, wb_ref[...], preferred_element_type=F32)

    if tiles_per_seq >= 1:
        @pl.when((i % tiles_per_seq) == 0)
        def _():
            carry_ref[j, 6:8, :] = cbuf_ref[0]

        prev1 = carry_ref[j, 7:8, :]
        prev2 = carry_ref[j, 6:7, :]
        t_idx = lax.broadcasted_iota(jnp.int32, (tm, 1), 0)
    else:
        n_seq = tm // t_seq
        cb3 = cbuf_ref[...]
        prev1 = jnp.broadcast_to(cb3[:, 1:2, :], (n_seq, t_seq, tf)).reshape(tm, tf)
        prev2 = jnp.broadcast_to(cb3[:, 0:1, :], (n_seq, t_seq, tf)).reshape(tm, tf)
        t_idx = lax.broadcasted_iota(jnp.int32, (tm, 1), 0) % t_seq
    a1 = jnp.where(t_idx >= 1, pltpu.roll(a, 1, 0), prev1)
    a2 = jnp.where(t_idx >= 2, pltpu.roll(a, 2, 0), jnp.where(t_idx == 1, prev1, prev2))
    conv = cb_ref[...] + a2 * cw_ref[0:1, :] + a1 * cw_ref[1:2, :] + a * cw_ref[2:3, :]
    act = conv * _sigmoid(conv) * up
    acc_ref[...] += jnp.dot(act.astype(BF16), wd_ref[...], preferred_element_type=F32)

    if tiles_per_seq >= 1:
        carry_ref[j] = a[tm - 8:tm]
        tail_ref[0] = a[tm - 8:tm]
    else:
        tail_ref[...] = a.reshape(tm // t_seq, t_seq, tf)

    @pl.when(j == pl.num_programs(1) - 1)
    def _():
        out_ref[...] = x_ref[...] + _rms(acc_ref[...], gpost_ref[...])


def conv_ffn(x, conv_buf, g_pre, w_a, w_b, conv_w, conv_b, w_d, g_post, t_seq, tm, tf):
    m, d = x.shape
    dff = w_a.shape[1]
    n_seq = m // t_seq
    assert m % tm == 0 and dff % tf == 0 and t_seq >= 8
    if tm <= t_seq:
        assert t_seq % tm == 0
        tiles_per_seq = t_seq // tm
        seq_blk = 1
        seq_idx = lambda i, j: (i // tiles_per_seq, 0, j)
    else:
        assert tm % t_seq == 0 and t_seq == 8
        tiles_per_seq = 0
        seq_blk = tm // t_seq
        seq_idx = lambda i, j: (i, 0, j)
    nf = dff // tf
    n_tail = (m // tm) * seq_blk
    vec = lambda w: pl.BlockSpec((1, w), lambda i, j: (0, 0))
    y, tail = pl.pallas_call(
        functools.partial(_ffn_kernel, t_seq=t_seq, tiles_per_seq=tiles_per_seq),
        grid=(m // tm, nf),
        in_specs=[pl.BlockSpec((tm, d), lambda i, j: (i, 0)), vec(d),
                  pl.BlockSpec((d, tf), lambda i, j: (0, j)),
                  pl.BlockSpec((d, tf), lambda i, j: (0, j)),
                  pl.BlockSpec((3, tf), lambda i, j: (0, j)),
                  pl.BlockSpec((1, tf), lambda i, j: (0, j)),
                  pl.BlockSpec((tf, d), lambda i, j: (j, 0)), vec(d),
                  pl.BlockSpec((seq_blk, 2, tf), seq_idx)],
        out_specs=[pl.BlockSpec((tm, d), lambda i, j: (i, 0)),
                   pl.BlockSpec((seq_blk, 8, tf), lambda i, j: (i, 0, j))],
        out_shape=[jax.ShapeDtypeStruct((m, d), F32), jax.ShapeDtypeStruct((n_tail, 8, dff), F32)],
        scratch_shapes=[pltpu.VMEM((tm, d), BF16), pltpu.VMEM((tm, d), F32), pltpu.VMEM((nf, 8, tf), F32)],
        compiler_params=_cparams("arbitrary", "arbitrary"),
        name="conv_ffn",
    )(x, g_pre.reshape(1, d), w_a, w_b, conv_w, conv_b.reshape(1, dff), w_d, g_post.reshape(1, d), conv_buf)
    if tiles_per_seq >= 1:
        tail = tail.reshape(n_seq, tiles_per_seq, 8, dff)[:, -1]
    return y, tail


def _rel_bucket(dist):
    max_exact = REL_BUCKETS // 2
    d = jnp.maximum(dist, 1).astype(F32)
    large = max_exact + (jnp.log(d / max_exact) / math.log(REL_MAX_DIST / max_exact)
                         * (REL_BUCKETS - max_exact)).astype(jnp.int32)
    large = jnp.minimum(large, REL_BUCKETS - 1)
    return jnp.where(dist < max_exact, dist, large)


def _lag_bias(rel_bias, g, dil):
    dist = jnp.arange(N_LAGS + 1, dtype=jnp.int32) * dil
    table = rel_bias[_rel_bucket(dist)].astype(F32)
    return table[:, g * N_HEADS:(g + 1) * N_HEADS].T


def _band_bias(lag_bias, lag):
    ok = (lag >= 0) & (lag <= N_LAGS)
    return jnp.where(ok[None], lag_bias[:, jnp.clip(lag, 0, N_LAGS)], NEG_INF)


def _prompt_bias(lag_bias):
    qi = jnp.arange(BAND)[:, None]
    kj = jnp.arange(BAND)[None, :]
    return _band_bias(lag_bias, BAND + qi - kj), _band_bias(lag_bias, qi - kj)


def _sample_bias(lag_bias, dil, win, t_new):
    t = jnp.arange(t_new)[:, None]

    def table(dist):
        return jnp.where((dist % dil == 0)[None], _band_bias(lag_bias, dist // dil), NEG_INF)

    return table(win + t - jnp.arange(win)[None, :]), table(t - jnp.arange(t_new)[None, :])


def _layer(x, batch, t, weights, lag_biases, lb_logits, win_caches, hg_state0, conv_buf0, mem_kv,
           *, prompt, tm_proj, tm_merge, tm_ffn, tq_mem, chunk):
    (w_a16, b_a16, w_g32, b_g32, gain_pre, gain_post, hg_norm, w_att, w_hg, w_mem, w_out,
     gain_fpre, gain_fpost, w_fa, w_fb, conv_w, conv_b, w_fd) = weights
    act_dtype = BF16 if prompt else F32
    act = norm_matmul(x, gain_pre, w_a16, b_a16, act_dtype, tm_proj, 1024)
    gates = norm_matmul(x, gain_pre, w_g32, b_g32, F32, tm_proj, 1024)

    o_groups, lse_groups, new_caches = [], [], []
    for g, (win, dil) in enumerate(ATT_GROUPS):
        if prompt:
            bp, bc = _prompt_bias(lag_biases[g])
            o, lse = dilated_prompt(act, bp, bc, g, dil, batch, t)
        else:
            cache = win_caches[g]
            assert cache.shape[1] >= win
            bcache, bnew = _sample_bias(lag_biases[g], dil, cache.shape[1], t)
            o, lse, newc = dilated_sample(act, cache, bcache, bnew, g, batch, t)
            new_caches.append(newc)
        o_groups.append(o)
        lse_groups.append(lse)

    hg_o, hg_state = hgrn2(act, gates, lb_logits, hg_norm, hg_state0, batch, t, chunk, act_dtype)
    mem_o = memory_attention(act, mem_kv, batch, t, tq_mem, act_dtype)
    x1 = merge(x, o_groups, lse_groups, hg_o, mem_o, gates, w_att, w_hg, w_mem, w_out, gain_post, tm_merge)
    y, tail = conv_ffn(x1, conv_buf0, gain_fpre, w_fa, w_fb, conv_w, conv_b, w_fd, gain_fpost, t, tm_ffn, 512)
    return y, act, new_caches, hg_state, tail[:, 6:8, :]


def kernel(x_prompt, x_sample, mem_prompt, cache_win1_kv, cache_win2_kv, cache_win3_kv, cache_mem_kv, state_hgrn, state_ffn_conv, rel_bias, hg_lb_logits, norm_mix_pre, norm_mix_post, w_in, b_in, hg_norm, mem_norm, w_mem_kv, w_br_att, w_br_hg, w_br_mem, w_out, norm_ffn_pre, norm_ffn_post, w_ffn_a, w_ffn_b, ffn_conv_w, ffn_conv_b, w_ffn_d):
    depth = w_in.shape[0]
    assert depth == 1
    bsz, seq, d = x_prompt.shape
    dbsz, dseq, _ = x_sample.shape
    mem_tokens = mem_prompt.shape[1]
    dff = w_ffn_a.shape[2]
    layer = 0

    w = w_in[layer]
    b = b_in[layer]
    q3 = 3 * 3 * BRANCH_W
    off = {name: q3 + i * BRANCH_W for i, name in enumerate(("hq", "hf", "hi", "hgate", "mq"))}
    off_gates = q3 + 5 * BRANCH_W
    pick_a = lambda z: jnp.concatenate(
        [z[..., :q3], z[..., off["hq"]:off["hq"] + BRANCH_W], z[..., off["hi"]:off["hi"] + BRANCH_W],
         z[..., off["mq"]:off["mq"] + BRANCH_W]], axis=-1)
    pick_g = lambda z: jnp.concatenate(
        [z[..., off["hf"]:off["hf"] + BRANCH_W], z[..., off["hgate"]:off["hgate"] + BRANCH_W], z[..., off_gates:]],
        axis=-1)
    weights = (pick_a(w).astype(BF16), pick_a(b), pick_g(w).astype(BF16), pick_g(b),
               norm_mix_pre[layer], norm_mix_post[layer], hg_norm[layer],
               w_br_att[layer].astype(BF16), w_br_hg[layer].astype(BF16), w_br_mem[layer].astype(BF16),
               w_out[layer].astype(BF16), norm_ffn_pre[layer], norm_ffn_post[layer],
               w_ffn_a[layer].astype(BF16), w_ffn_b[layer].astype(BF16), ffn_conv_w[layer], ffn_conv_b[layer],
               w_ffn_d[layer].astype(BF16))
    lag_biases = [_lag_bias(rel_bias, g, dil) for g, (_, dil) in enumerate(ATT_GROUPS)]

    mem_kv = norm_matmul(mem_prompt.reshape(bsz * mem_tokens, d), mem_norm[layer], w_mem_kv[layer].astype(BF16),
                         jnp.zeros((2 * BRANCH_W,), F32), F32, bsz * mem_tokens, 1024)
    mem_kv = mem_kv.reshape(bsz, mem_tokens, 2 * BRANCH_W)
    yp, act_p, _, hg_p, conv_p = _layer(
        x_prompt.reshape(bsz * seq, d), bsz, seq, weights, lag_biases, hg_lb_logits, None,
        jnp.zeros((bsz, N_HEADS, HEAD_DIM, HEAD_DIM), F32), jnp.zeros((bsz, 2, dff), F32), mem_kv,
        prompt=True, tm_proj=1024, tm_merge=256, tm_ffn=512, tq_mem=512, chunk=64)
    act_p = act_p.reshape(bsz, seq, A_COLS)
    p_win = []
    for g, (win, _) in enumerate(ATT_GROUPS):
        n = min(win, seq)
        kg = act_p[:, seq - n:, (3 + g) * BRANCH_W:(4 + g) * BRANCH_W]
        vg = act_p[:, seq - n:, (6 + g) * BRANCH_W:(7 + g) * BRANCH_W]
        p_win.append(jnp.stack([kg, vg], axis=2).astype(F32).reshape(1, bsz, n, 2, N_HEADS, HEAD_DIM))

    caches = [c[layer].reshape(dbsz, c.shape[2], 2 * BRANCH_W) for c in (cache_win1_kv, cache_win2_kv, cache_win3_kv)]
    ys, _, new_caches, hg_s, conv_s = _layer(
        x_sample.reshape(dbsz * dseq, d), dbsz, dseq, weights, lag_biases, hg_lb_logits, caches,
        state_hgrn[layer], state_ffn_conv[layer], cache_mem_kv[layer].reshape(dbsz, mem_tokens, 2 * BRANCH_W),
        prompt=False, tm_proj=dbsz * dseq, tm_merge=dbsz * dseq, tm_ffn=dbsz * dseq, tq_mem=dseq, chunk=16)
    s_win = [c.reshape(1, dbsz, c.shape[1], 2, N_HEADS, HEAD_DIM) for c in new_caches]

    return (yp.reshape(bsz, seq, d), ys.reshape(dbsz, dseq, d),
            p_win[0], p_win[1], p_win[2],
            hg_p[None], conv_p[None], mem_kv.reshape(1, bsz, mem_tokens, 2, N_HEADS, HEAD_DIM),
            s_win[0], s_win[1], s_win[2],
            hg_s[None], conv_s[None])
```

```python
import functools
import math

import numpy as np
import jax
import jax.numpy as jnp
from jax import lax
from jax.experimental import pallas as pl
from jax.experimental.pallas import tpu as pltpu

F32 = jnp.float32
BF16 = jnp.bfloat16

NORM_EPS = 1e-6
NEG_INF = -1e30
HEAD_DIM = 128
N_HEADS = 4
BRANCH_W = N_HEADS * HEAD_DIM
N_LAGS = 128
BAND = 128
ATT_GROUPS = ((128, 1), (512, 4), (2048, 16))
REL_BUCKETS = 32
REL_MAX_DIST = 2048
LSE_LANES = 32
HG_BASE = 8
VMEM_LIMIT = 56 * 1024 * 1024

NT_DIMS = (((1,), (1,)), ((), ()))
TN_DIMS = (((0,), (0,)), ((), ()))

PROJ_STEP_BLOCK = (0, 3, 6, 1, 4, 7, 2, 5, 8, 9, 11, 13, 10, 12, 14, 15, 16, 17, 18, 19)
QKV_COLS = 3 * BRANCH_W
REC_COLS = 3 * BRANCH_W
REC_HQ, REC_HI, REC_MQ = 0, 1, 2
G_COLS = 8 * BRANCH_W
KV_ROWS = 2 * N_HEADS


def _cparams(*sem):
    return pltpu.CompilerParams(dimension_semantics=sem, vmem_limit_bytes=VMEM_LIMIT)


def _rms(x, gain):
    return x * lax.rsqrt(jnp.mean(x * x, axis=-1, keepdims=True) + NORM_EPS) * gain


def _sigmoid(x):
    return 1.0 / (1.0 + jnp.exp(-x))


def _norm_matmul_kernel(x_ref, g_ref, w_ref, b_ref, o_ref, hn_ref):
    @pl.when(pl.program_id(1) == 0)
    def _():
        hn_ref[...] = _rms(x_ref[...], g_ref[...]).astype(BF16)

    acc = jnp.dot(hn_ref[...], w_ref[...], preferred_element_type=F32)
    o_ref[...] = (acc + b_ref[...]).astype(o_ref.dtype)


def norm_matmul(x, gain, w, bias, out_dtype, tm, tn):
    m, k = x.shape
    n = w.shape[1]
    assert m % tm == 0 and n % tn == 0
    return pl.pallas_call(
        _norm_matmul_kernel,
        grid=(m // tm, n // tn),
        in_specs=[
            pl.BlockSpec((tm, k), lambda i, j: (i, 0)),
            pl.BlockSpec((1, k), lambda i, j: (0, 0)),
            pl.BlockSpec((k, tn), lambda i, j: (0, j)),
            pl.BlockSpec((1, tn), lambda i, j: (0, j)),
        ],
        out_specs=pl.BlockSpec((tm, tn), lambda i, j: (i, j)),
        out_shape=jax.ShapeDtypeStruct((m, n), out_dtype),
        scratch_shapes=[pltpu.VMEM((tm, k), BF16)],
        compiler_params=_cparams("parallel", "arbitrary"),
        name="norm_matmul",
    )(x, gain.reshape(1, k), w, bias.reshape(1, n))


def _in_proj_kernel(tbl_ref, x_ref, g_ref, w_ref, b_ref, o1_ref, o2_ref, o3_ref, rec_ref, gate_ref,
                    hn_ref, scr_ref, *, dils):
    del tbl_ref
    j = pl.program_id(1)
    tm = x_ref.shape[0]

    @pl.when(j == 0)
    def _():
        hn_ref[...] = _rms(x_ref[...], g_ref[...]).astype(BF16)

    acc = jnp.dot(hn_ref[...], w_ref[...], preferred_element_type=F32) + b_ref[...]

    for g, (o_ref, dil) in enumerate(zip((o1_ref, o2_ref, o3_ref), dils)):
        @pl.when((j >= 3 * g) & (j < 3 * g + 3))
        def _(o_ref=o_ref, dil=dil):
            if dil == 1:
                o_ref[0] = acc.astype(o_ref.dtype)
                return
            for c in range(N_HEADS):
                scr_ref[c] = acc[:, c * HEAD_DIM:(c + 1) * HEAD_DIM]
            for r in range(dil):
                for c in range(N_HEADS):
                    o_ref[r, :, c * HEAD_DIM:(c + 1) * HEAD_DIM] = (
                        scr_ref[c, pl.ds(r, tm // dil, stride=dil), :].astype(o_ref.dtype))

    @pl.when((j >= 9) & (j < 12))
    def _():
        rec_ref[...] = acc.astype(rec_ref.dtype)

    @pl.when(j >= 12)
    def _():
        gate_ref[...] = acc


def in_proj(x, gain, w, bias, batch, seq, dils, act_dtype, tm):
    m, k = x.shape
    assert m == batch * seq and seq % tm == 0 and all(tm % d == 0 for d in dils)
    tiles_per_b = seq // tm
    n_steps = len(PROJ_STEP_BLOCK)
    tbl = jnp.asarray(PROJ_STEP_BLOCK, jnp.int32)

    def qkv_spec(g, dil):
        return pl.BlockSpec((None, dil, tm // dil, BRANCH_W),
                            lambda i, j, t: (i // tiles_per_b, 0, i % tiles_per_b, jnp.clip(j - 3 * g, 0, 2)))

    grid_spec = pltpu.PrefetchScalarGridSpec(
        num_scalar_prefetch=1,
        grid=(m // tm, n_steps),
        in_specs=[pl.BlockSpec((tm, k), lambda i, j, t: (i, 0)),
                  pl.BlockSpec((1, k), lambda i, j, t: (0, 0)),
                  pl.BlockSpec((k, BRANCH_W), lambda i, j, t: (0, t[j])),
                  pl.BlockSpec((1, BRANCH_W), lambda i, j, t: (0, t[j]))],
        out_specs=[qkv_spec(0, dils[0]), qkv_spec(1, dils[1]), qkv_spec(2, dils[2]),
                   pl.BlockSpec((tm, BRANCH_W), lambda i, j, t: (i, jnp.clip(j - 9, 0, 2))),
                   pl.BlockSpec((tm, BRANCH_W), lambda i, j, t: (i, jnp.clip(j - 12, 0, 7)))],
        scratch_shapes=[pltpu.VMEM((tm, k), BF16), pltpu.VMEM((N_HEADS, tm, HEAD_DIM), F32)])
    return pl.pallas_call(
        functools.partial(_in_proj_kernel, dils=dils),
        grid_spec=grid_spec,
        out_shape=[jax.ShapeDtypeStruct((batch, d, seq // d, QKV_COLS), act_dtype) for d in dils]
        + [jax.ShapeDtypeStruct((m, REC_COLS), act_dtype), jax.ShapeDtypeStruct((m, G_COLS), F32)],
        compiler_params=_cparams("parallel", "arbitrary"),
        name="in_proj",
    )(tbl, x, gain.reshape(1, k), w, bias.reshape(1, -1))


def _dil_prompt_kernel(q_ref, kp_ref, kc_ref, vp_ref, vc_ref, bp_ref, bc_ref, o_ref, lse_ref, *scratch, dil):
    first = pl.program_id(1) == 0
    scale = 1.0 / math.sqrt(HEAD_DIM)

    def residue(r):
        outs, lses = [], []
        for h in range(N_HEADS):
            hs = slice(h * HEAD_DIM, (h + 1) * HEAD_DIM)
            q = q_ref[r, :, hs]
            lp = lax.dot_general(q, kp_ref[r, :, hs], NT_DIMS, preferred_element_type=F32) * scale + bp_ref[h]
            lp = jnp.where(first, NEG_INF, lp)
            lc = lax.dot_general(q, kc_ref[r, :, hs], NT_DIMS, preferred_element_type=F32) * scale + bc_ref[h]
            m = jnp.maximum(jnp.max(lp, axis=-1, keepdims=True), jnp.max(lc, axis=-1, keepdims=True))
            pp = jnp.exp(lp - m)
            pc = jnp.exp(lc - m)
            s = jnp.sum(pp, axis=-1, keepdims=True) + jnp.sum(pc, axis=-1, keepdims=True)
            o = (jnp.dot(pp.astype(BF16), vp_ref[r, :, hs], preferred_element_type=F32)
                 + jnp.dot(pc.astype(BF16), vc_ref[r, :, hs], preferred_element_type=F32))
            outs.append(o / s)
            lses.append(jnp.broadcast_to(m + jnp.log(s), (BAND, LSE_LANES)))
        return outs, jnp.concatenate(lses, axis=1)

    if dil == 1:
        outs, lse = residue(0)
        for h in range(N_HEADS):
            o_ref[:, h * HEAD_DIM:(h + 1) * HEAD_DIM] = outs[h].astype(o_ref.dtype)
        lse_ref[...] = lse
        return

    o_scr, lse_scr = scratch

    def body(r, carry):
        outs, lse = residue(r)
        for h in range(N_HEADS):
            o_scr[h, pl.ds(r, BAND, stride=dil), :] = outs[h]
        lse_scr[pl.ds(r, BAND, stride=dil), :] = lse
        return carry

    lax.fori_loop(0, dil, body, 0)
    for h in range(N_HEADS):
        o_ref[:, h * HEAD_DIM:(h + 1) * HEAD_DIM] = o_scr[h].astype(o_ref.dtype)
    lse_ref[...] = lse_scr[...]


def dilated_prompt(qkv, bias_prev, bias_cur, g, dil, batch, seq):
    assert seq % (dil * BAND) == 0
    sub = seq // dil
    span = dil * BAND
    blk = (None, dil, BAND, BRANCH_W)
    cur = lambda part: pl.BlockSpec(blk, lambda b, i: (b, 0, i, part))
    prev = lambda part: pl.BlockSpec(blk, lambda b, i: (b, 0, jnp.maximum(i - 1, 0), part))
    bias_spec = pl.BlockSpec((N_HEADS, BAND, BAND), lambda b, i: (0, 0, 0))
    scratch = [] if dil == 1 else [pltpu.VMEM((N_HEADS, span, HEAD_DIM), F32),
                                   pltpu.VMEM((span, N_HEADS * LSE_LANES), F32)]
    o, lse = pl.pallas_call(
        functools.partial(_dil_prompt_kernel, dil=dil),
        grid=(batch, sub // BAND),
        in_specs=[cur(0), prev(1), cur(1), prev(2), cur(2), bias_spec, bias_spec],
        out_specs=[pl.BlockSpec((None, span, BRANCH_W), lambda b, i: (b, i, 0)),
                   pl.BlockSpec((None, span, N_HEADS * LSE_LANES), lambda b, i: (b, i, 0))],
        out_shape=[jax.ShapeDtypeStruct((batch, seq, BRANCH_W), BF16),
                   jax.ShapeDtypeStruct((batch, seq, N_HEADS * LSE_LANES), F32)],
        scratch_shapes=scratch,
        compiler_params=_cparams("parallel", "arbitrary"),
        name=f"dilated_prompt_g{g}",
    )(qkv, qkv, qkv, qkv, qkv, bias_prev, bias_cur)
    return o.reshape(batch * seq, BRANCH_W), lse.reshape(batch * seq, N_HEADS * LSE_LANES)


def _dil_sample_kernel(q_ref, kn_ref, vn_ref, cache_ref, bc_ref, bn_ref, o_ref, lse_ref, newc_ref, *, win, t_new):
    scale = 1.0 / math.sqrt(HEAD_DIM)
    keep = (win - t_new) * KV_ROWS
    newc_ref[0:keep, :] = cache_ref[t_new * KV_ROWS:win * KV_ROWS, :]
    for h in range(N_HEADS):
        hs = slice(h * HEAD_DIM, (h + 1) * HEAD_DIM)
        q = q_ref[:, hs]
        kn = kn_ref[:, hs]
        vn = vn_ref[:, hs]
        kc = cache_ref[pl.ds(h, win, stride=KV_ROWS), :].astype(BF16)
        vc = cache_ref[pl.ds(N_HEADS + h, win, stride=KV_ROWS), :].astype(BF16)
        lc = lax.dot_general(q.astype(BF16), kc, NT_DIMS, preferred_element_type=F32) * scale + bc_ref[h]
        ln = lax.dot_general(q, kn, NT_DIMS, preferred_element_type=F32) * scale + bn_ref[h]
        m = jnp.maximum(jnp.max(lc, axis=-1, keepdims=True), jnp.max(ln, axis=-1, keepdims=True))
        pc = jnp.exp(lc - m)
        pn = jnp.exp(ln - m)
        s = jnp.sum(pc, axis=-1, keepdims=True) + jnp.sum(pn, axis=-1, keepdims=True)
        o = jnp.dot(pc.astype(BF16), vc, preferred_element_type=F32) + jnp.dot(pn, vn, preferred_element_type=F32)
        o_ref[:, hs] = o / s
        lse_ref[:, h * LSE_LANES:(h + 1) * LSE_LANES] = jnp.broadcast_to(m + jnp.log(s), (t_new, LSE_LANES))
        newc_ref[pl.ds(keep + h, t_new, stride=KV_ROWS), :] = kn
        newc_ref[pl.ds(keep + N_HEADS + h, t_new, stride=KV_ROWS), :] = vn


def dilated_sample(qkv, cache, bias_cache, bias_new, g, batch, t_new):
    win = cache.shape[1] // KV_ROWS
    blk = (None, t_new, BRANCH_W)
    cache_spec = pl.BlockSpec((None, win * KV_ROWS, HEAD_DIM), lambda b: (b, 0, 0))
    o, lse, newc = pl.pallas_call(
        functools.partial(_dil_sample_kernel, win=win, t_new=t_new),
        grid=(batch,),
        in_specs=[pl.BlockSpec(blk, lambda b: (b, 0, 0)),
                  pl.BlockSpec(blk, lambda b: (b, 0, 1)),
                  pl.BlockSpec(blk, lambda b: (b, 0, 2)),
                  cache_spec,
                  pl.BlockSpec((N_HEADS, t_new, win), lambda b: (0, 0, 0)),
                  pl.BlockSpec((N_HEADS, t_new, t_new), lambda b: (0, 0, 0))],
        out_specs=[pl.BlockSpec(blk, lambda b: (b, 0, 0)),
                   pl.BlockSpec((None, t_new, N_HEADS * LSE_LANES), lambda b: (b, 0, 0)),
                   cache_spec],
        out_shape=[jax.ShapeDtypeStruct((batch, t_new, BRANCH_W), F32),
                   jax.ShapeDtypeStruct((batch, t_new, N_HEADS * LSE_LANES), F32),
                   jax.ShapeDtypeStruct(cache.shape, F32)],
        compiler_params=_cparams("parallel"),
        name=f"dilated_sample_g{g}",
    )(qkv, qkv, qkv, cache, bias_cache, bias_new)
    return o.reshape(batch * t_new, BRANCH_W), lse.reshape(batch * t_new, N_HEADS * LSE_LANES), newc


def _mem_attn_kernel(q_ref, kv_ref, o_ref, *, mem):
    scale = 1.0 / math.sqrt(HEAD_DIM)
    for h in range(N_HEADS):
        hs = slice(h * HEAD_DIM, (h + 1) * HEAD_DIM)
        k = kv_ref[pl.ds(h, mem, stride=KV_ROWS), :].astype(BF16)
        v = kv_ref[pl.ds(N_HEADS + h, mem, stride=KV_ROWS), :].astype(BF16)
        logits = lax.dot_general(q_ref[:, hs].astype(BF16), k, NT_DIMS, preferred_element_type=F32) * scale
        m = jnp.max(logits, axis=-1, keepdims=True)
        p = jnp.exp(logits - m)
        s = jnp.sum(p, axis=-1, keepdims=True)
        o = jnp.dot(p.astype(BF16), v, preferred_element_type=F32)
        o_ref[:, hs] = (o / s).astype(o_ref.dtype)


def memory_attention(rec, mem_kv, batch, t, tq, out_dtype):
    mem = mem_kv.shape[1] // KV_ROWS
    x = rec.reshape(batch, t, REC_COLS)
    o = pl.pallas_call(
        functools.partial(_mem_attn_kernel, mem=mem),
        grid=(batch, t // tq),
        in_specs=[pl.BlockSpec((None, tq, BRANCH_W), lambda b, i: (b, i, REC_MQ)),
                  pl.BlockSpec((None, mem * KV_ROWS, HEAD_DIM), lambda b, i: (b, 0, 0))],
        out_specs=pl.BlockSpec((None, tq, BRANCH_W), lambda b, i: (b, i, 0)),
        out_shape=jax.ShapeDtypeStruct((batch, t, BRANCH_W), out_dtype),
        compiler_params=_cparams("parallel", "parallel"),
        name="memory_attention",
    )(x, mem_kv)
    return o.reshape(batch * t, BRANCH_W)


def _hgrn_tables(chunk):
    c, b = chunk, HG_BASE
    t = np.arange(c)[:, None]
    u = np.arange(c)[None, :]
    mats = [(u <= t)]
    masks = []
    size = 2 * b
    while size <= c:
        half = size // 2
        ref = (t // size) * size + half - 1
        second = (t % size) >= half
        mats.append(second & (u > ref) & (u <= t))
        mats.append((~second) & (u > t) & (u <= ref))
        masks.append(((t // size) == (u // size)) & second & ((u % size) < half))
        size *= 2
    for s in range(b):
        key = (t // b) * b + s
        mats.append((u > key) & (u <= t))
    mats.append(u > t)
    masks.append(((t // b) == (u // b)) & (u <= t))
    wall = np.concatenate(mats, axis=0).astype(np.float32)
    erep = np.zeros((b * HEAD_DIM, c), np.float32)
    for s in range(b):
        erep[s * HEAD_DIM:(s + 1) * HEAD_DIM, np.arange(c) % b == s] = 1.0
    return wall, erep, np.stack(masks).astype(np.float32)


def _hgrn_kernel(q_ref, v_ref, f_ref, gate_ref, lbl_ref, gn_ref, s0_ref, wall_ref, erep_ref, mask_ref,
                 o_ref, s_out_ref, st_ref, *, chunk, rows_in, n_levels):
    c = chunk
    step = pl.program_id(1)

    @pl.when(step == 0)
    def _():
        for h in range(N_HEADS):
            st_ref[h] = s0_ref[h].T

    def rows(x):
        x = x.astype(F32)
        if rows_in < c:
            x = jnp.concatenate([x, jnp.zeros((c - rows_in, x.shape[1]), F32)], axis=0)
        return x

    q = rows(q_ref[...])
    v = rows(v_ref[...]).astype(BF16)
    f_logit = rows(f_ref[...])

    lbl = lbl_ref[...]
    e = jnp.exp(lbl - jnp.max(lbl, axis=0, keepdims=True))
    lb = e[0:1] / jnp.sum(e, axis=0, keepdims=True)
    f = lb + (1.0 - lb) * _sigmoid(f_logit)
    g = jnp.log(f)
    kk = 1.0 - f
    if rows_in < c:
        live = lax.broadcasted_iota(jnp.int32, (c, 1), 0) < rows_in
        g = jnp.where(live, g, 0.0)
        kk = jnp.where(live, kk, 0.0)

    g1 = g.astype(BF16)
    r1 = g - g1.astype(F32)
    g2 = r1.astype(BF16)
    g3 = (r1 - g2.astype(F32)).astype(BF16)
    wall = wall_ref[...]
    spans = (jnp.dot(wall, g1, preferred_element_type=F32) + jnp.dot(wall, g2, preferred_element_type=F32)
             + jnp.dot(wall, g3, preferred_element_type=F32))

    def span(i):
        return spans[i * c:(i + 1) * c]

    q_inter = (q * jnp.exp(span(0))).astype(BF16)
    g_end = spans[c - 1:c]
    q_lvl = [(q * jnp.exp(span(1 + 2 * l))).astype(BF16) for l in range(n_levels)]
    k_lvl = [(kk * jnp.exp(span(2 + 2 * l))).astype(BF16) for l in range(n_levels)]
    base = 1 + 2 * n_levels
    kk3 = kk.reshape(c // HG_BASE, HG_BASE, kk.shape[1])
    x_diag = []
    for s in range(HG_BASE):
        kb = jnp.broadcast_to(kk3[:, s:s + 1, :], kk3.shape).reshape(c, kk.shape[1])
        x_diag.append((q * kb * jnp.exp(span(base + s))).astype(BF16))
    k_end = (kk * jnp.exp(span(base + HG_BASE))).astype(BF16)
    gate = _sigmoid(rows(gate_ref[...]))

    for h in range(N_HEADS):
        hs = slice(h * HEAD_DIM, (h + 1) * HEAD_DIM)
        scores = mask_ref[n_levels] * jnp.dot(jnp.concatenate([x[:, hs] for x in x_diag], axis=1), erep_ref[...],
                                              preferred_element_type=F32)
        for l in range(n_levels):
            scores += mask_ref[l] * lax.dot_general(q_lvl[l][:, hs], k_lvl[l][:, hs], NT_DIMS,
                                                    preferred_element_type=F32)
        st = st_ref[h]
        o = (lax.dot_general(q_inter[:, hs], st.astype(BF16), NT_DIMS, preferred_element_type=F32)
             + jnp.dot(scores.astype(BF16), v[:, hs], preferred_element_type=F32))
        st_ref[h] = (st * jnp.exp(g_end[:, hs])
                     + lax.dot_general(v[:, hs], k_end[:, hs], TN_DIMS, preferred_element_type=F32))
        y = _rms(o, gn_ref[...]) * gate[:, hs]
        o_ref[:, hs] = y[0:rows_in].astype(o_ref.dtype)

    @pl.when(step == pl.num_programs(1) - 1)
    def _():
        for h in range(N_HEADS):
            s_out_ref[h] = st_ref[h].T


def hgrn2(rec, gates, lb_logits, hg_norm, state0, batch, t, chunk, out_dtype):
    rows_in = min(chunk, t)
    assert t % rows_in == 0 and chunk % HG_BASE == 0
    wall, erep, masks = _hgrn_tables(chunk)
    n_levels = masks.shape[0] - 1
    a = rec.reshape(batch, t, REC_COLS)
    gt = gates.reshape(batch, t, G_COLS)
    blk = (None, rows_in, BRANCH_W)
    full = lambda arr: pl.BlockSpec(arr.shape, lambda b, i: (0,) * arr.ndim)
    st_spec = pl.BlockSpec((None, N_HEADS, HEAD_DIM, HEAD_DIM), lambda b, i: (b, 0, 0, 0))
    wall = jnp.asarray(wall, BF16)
    erep = jnp.asarray(erep, BF16)
    masks = jnp.asarray(masks, F32)
    lbl = lb_logits.astype(F32)
    gn = hg_norm.reshape(1, HEAD_DIM).astype(F32)
    o, s_fin = pl.pallas_call(
        functools.partial(_hgrn_kernel, chunk=chunk, rows_in=rows_in, n_levels=n_levels),
        grid=(batch, t // rows_in),
        in_specs=[pl.BlockSpec(blk, lambda b, i: (b, i, REC_HQ)),
                  pl.BlockSpec(blk, lambda b, i: (b, i, REC_HI)),
                  pl.BlockSpec(blk, lambda b, i: (b, i, 0)),
                  pl.BlockSpec(blk, lambda b, i: (b, i, 1)),
                  full(lbl), full(gn), st_spec, full(wall), full(erep), full(masks)],
        out_specs=[pl.BlockSpec(blk, lambda b, i: (b, i, 0)), st_spec],
        out_shape=[jax.ShapeDtypeStruct((batch, t, BRANCH_W), out_dtype),
                   jax.ShapeDtypeStruct(state0.shape, F32)],
        scratch_shapes=[pltpu.VMEM((N_HEADS, HEAD_DIM, HEAD_DIM), F32)],
        compiler_params=_cparams("parallel", "arbitrary"),
        name="hgrn2",
    )(a, a, gt, gt, lbl, gn, state0, wall, erep, masks)
    return o.reshape(batch * t, BRANCH_W), s_fin


def _merge_kernel(x_ref, o1_ref, o2_ref, o3_ref, l1_ref, l2_ref, l3_ref, hg_ref, mem_ref, ga_ref, gh_ref, gm_ref,
                  wa_ref, wh_ref, wm_ref, wo_ref, gain_ref, out_ref):
    tm = x_ref.shape[0]
    l1, l2, l3 = l1_ref[...], l2_ref[...], l3_ref[...]
    m = jnp.maximum(jnp.maximum(l1, l2), l3)
    e1, e2, e3 = jnp.exp(l1 - m), jnp.exp(l2 - m), jnp.exp(l3 - m)
    den = e1 + e2 + e3
    w1, w2, w3 = e1 / den, e2 / den, e3 / den
    att = []
    for h in range(N_HEADS):
        hs = slice(h * HEAD_DIM, (h + 1) * HEAD_DIM)
        col = slice(h * LSE_LANES, h * LSE_LANES + 1)
        bc = lambda w: jnp.broadcast_to(w[:, col], (tm, HEAD_DIM))
        att.append((bc(w1) * o1_ref[:, hs].astype(F32) + bc(w2) * o2_ref[:, hs].astype(F32)
                    + bc(w3) * o3_ref[:, hs].astype(F32)).astype(BF16))
    att = jnp.concatenate(att, axis=1)
    merged = (_sigmoid(ga_ref[...]) * jnp.dot(att, wa_ref[...], preferred_element_type=F32)
              + _sigmoid(gh_ref[...]) * jnp.dot(hg_ref[...].astype(BF16), wh_ref[...], preferred_element_type=F32)
              + _sigmoid(gm_ref[...]) * jnp.dot(mem_ref[...].astype(BF16), wm_ref[...], preferred_element_type=F32))
    y = jnp.dot(merged.astype(BF16), wo_ref[...], preferred_element_type=F32)
    out_ref[...] = x_ref[...] + _rms(y, gain_ref[...])


def merge(x, o_groups, lse_groups, hg_o, mem_o, gates, w_att, w_hg, w_mem, w_out, gain, tm):
    m, d = x.shape
    assert m % tm == 0
    row = lambda w: pl.BlockSpec((tm, w), lambda i: (i, 0))
    full = lambda arr: pl.BlockSpec(arr.shape, lambda i: (0, 0))
    gate_spec = lambda j: pl.BlockSpec((tm, d), lambda i: (i, j))
    gain = gain.reshape(1, d)
    return pl.pallas_call(
        _merge_kernel,
        grid=(m // tm,),
        in_specs=[row(d), row(BRANCH_W), row(BRANCH_W), row(BRANCH_W),
                  row(N_HEADS * LSE_LANES), row(N_HEADS * LSE_LANES), row(N_HEADS * LSE_LANES),
                  row(BRANCH_W), row(BRANCH_W), gate_spec(1), gate_spec(2), gate_spec(3),
                  full(w_att), full(w_hg), full(w_mem), full(w_out), full(gain)],
        out_specs=row(d),
        out_shape=jax.ShapeDtypeStruct((m, d), F32),
        compiler_params=_cparams("parallel"),
        name="merge",
    )(x, *o_groups, *lse_groups, hg_o, mem_o, gates, gates, gates, w_att, w_hg, w_mem, w_out, gain)


def _ffn_kernel(x_ref, gpre_ref, wa_ref, wb_ref, cw_ref, cb_ref, wd_ref, gpost_ref, cbuf_ref,
                out_ref, tail_ref, hn_ref, acc_ref, carry_ref, *, t_seq, tiles_per_seq):
    tm = x_ref.shape[0]
    tf = wa_ref.shape[1]
    i = pl.program_id(0)
    j = pl.program_id(1)

    @pl.when(j == 0)
    def _():
        hn_ref[...] = _rms(x_ref[...], gpre_ref[...]).astype(BF16)
        acc_ref[...] = jnp.zeros_like(acc_ref)

    hn = hn_ref[...]
    a = jnp.dot(hn, wa_ref[...], preferred_element_type=F32)
    up = jnp.dot(hn, wb_ref[...], preferred_element_type=F32)

    if tiles_per_seq >= 1:
        @pl.when((i % tiles_per_seq) == 0)
        def _():
            carry_ref[j, 6:8, :] = cbuf_ref[0]

        prev1 = carry_ref[j, 7:8, :]
        prev2 = carry_ref[j, 6:7, :]
        t_idx = lax.broadcasted_iota(jnp.int32, (tm, 1), 0)
    else:
        n_seq = tm // t_seq
        cb3 = cbuf_ref[...]
        prev1 = jnp.broadcast_to(cb3[:, 1:2, :], (n_seq, t_seq, tf)).reshape(tm, tf)
        prev2 = jnp.broadcast_to(cb3[:, 0:1, :], (n_seq, t_seq, tf)).reshape(tm, tf)
        t_idx = lax.broadcasted_iota(jnp.int32, (tm, 1), 0) % t_seq
    a1 = jnp.where(t_idx >= 1, pltpu.roll(a, 1, 0), prev1)
    a2 = jnp.where(t_idx >= 2, pltpu.roll(a, 2, 0), jnp.where(t_idx == 1, prev1, prev2))
    conv = cb_ref[...] + a2 * cw_ref[0:1, :] + a1 * cw_ref[1:2, :] + a * cw_ref[2:3, :]
    act = conv * _sigmoid(conv) * up
    acc_ref[...] += jnp.dot(act.astype(BF16), wd_ref[...], preferred_element_type=F32)

    if tiles_per_seq >= 1:
        carry_ref[j] = a[tm - 8:tm]
        tail_ref[0] = a[tm - 8:tm]
    else:
        tail_ref[...] = a.reshape(tm // t_seq, t_seq, tf)

    @pl.when(j == pl.num_programs(1) - 1)
    def _():
        out_ref[...] = x_ref[...] + _rms(acc_ref[...], gpost_ref[...])


def conv_ffn(x, conv_buf, g_pre, w_a, w_b, conv_w, conv_b, w_d, g_post, t_seq, tm, tf):
    m, d = x.shape
    dff = w_a.shape[1]
    n_seq = m // t_seq
    assert m % tm == 0 and dff % tf == 0 and t_seq >= 8
    if tm <= t_seq:
        assert t_seq % tm == 0
        tiles_per_seq = t_seq // tm
        seq_blk = 1
        seq_idx = lambda i, j: (i // tiles_per_seq, 0, j)
    else:
        assert tm % t_seq == 0 and t_seq == 8
        tiles_per_seq = 0
        seq_blk = tm // t_seq
        seq_idx = lambda i, j: (i, 0, j)
    nf = dff // tf
    n_tail = (m // tm) * seq_blk
    vec = lambda w: pl.BlockSpec((1, w), lambda i, j: (0, 0))
    y, tail = pl.pallas_call(
        functools.partial(_ffn_kernel, t_seq=t_seq, tiles_per_seq=tiles_per_seq),
        grid=(m // tm, nf),
        in_specs=[pl.BlockSpec((tm, d), lambda i, j: (i, 0)), vec(d),
                  pl.BlockSpec((d, tf), lambda i, j: (0, j)),
                  pl.BlockSpec((d, tf), lambda i, j: (0, j)),
                  pl.BlockSpec((3, tf), lambda i, j: (0, j)),
                  pl.BlockSpec((1, tf), lambda i, j: (0, j)),
                  pl.BlockSpec((tf, d), lambda i, j: (j, 0)), vec(d),
                  pl.BlockSpec((seq_blk, 2, tf), seq_idx)],
        out_specs=[pl.BlockSpec((tm, d), lambda i, j: (i, 0)),
                   pl.BlockSpec((seq_blk, 8, tf), lambda i, j: (i, 0, j))],
        out_shape=[jax.ShapeDtypeStruct((m, d), F32), jax.ShapeDtypeStruct((n_tail, 8, dff), F32)],
        scratch_shapes=[pltpu.VMEM((tm, d), BF16), pltpu.VMEM((tm, d), F32), pltpu.VMEM((nf, 8, tf), F32)],
        compiler_params=_cparams("arbitrary", "arbitrary"),
        name="conv_ffn",
    )(x, g_pre.reshape(1, d), w_a, w_b, conv_w, conv_b.reshape(1, dff), w_d, g_post.reshape(1, d), conv_buf)
    if tiles_per_seq >= 1:
        tail = tail.reshape(n_seq, tiles_per_seq, 8, dff)[:, -1]
    return y, tail


def _rel_bucket(dist):
    max_exact = REL_BUCKETS // 2
    d = jnp.maximum(dist, 1).astype(F32)
    large = max_exact + (jnp.log(d / max_exact) / math.log(REL_MAX_DIST / max_exact)
                         * (REL_BUCKETS - max_exact)).astype(jnp.int32)
    large = jnp.minimum(large, REL_BUCKETS - 1)
    return jnp.where(dist < max_exact, dist, large)


def _lag_bias(rel_bias, g, dil):
    dist = jnp.arange(N_LAGS + 1, dtype=jnp.int32) * dil
    onehot = _rel_bucket(dist)[:, None] == jnp.arange(REL_BUCKETS, dtype=jnp.int32)[None, :]
    table = jnp.sum(jnp.where(onehot[:, :, None], rel_bias.astype(F32)[None], 0.0), axis=1)
    return table[:, g * N_HEADS:(g + 1) * N_HEADS].T


def _neg(h, n):
    return jnp.full((h, n), NEG_INF, F32)


def _prompt_bias(lag_bias):
    h = lag_bias.shape[0]
    v = jnp.concatenate([lag_bias[:, ::-1], _neg(h, BAND)], axis=1)
    period = 2 * BAND + 1
    assert v.shape[1] == period
    both = jnp.tile(v, (1, BAND))[:, :BAND * 2 * BAND].reshape(h, BAND, 2 * BAND)
    return both[:, :, :BAND], both[:, :, BAND:]


def _sample_bias(lag_bias, dil, n_past, t_new):
    h = lag_bias.shape[0]
    n = n_past + t_new
    per_dist = jnp.repeat(lag_bias, dil, axis=1)
    per_dist = jnp.concatenate([per_dist, _neg(h, max(n - per_dist.shape[1], 0))], axis=1)[:, :n]
    per_dist = jnp.where((np.arange(n) % dil == 0)[None], per_dist, NEG_INF)
    rev = per_dist[:, ::-1]
    cached = jnp.stack([rev[:, t_new - 1 - t:t_new - 1 - t + n_past] for t in range(t_new)], axis=1)
    padded = jnp.concatenate([_neg(h, t_new - 1), per_dist[:, :t_new]], axis=1)
    new = jnp.stack([padded[:, t:t + t_new][:, ::-1] for t in range(t_new)], axis=1)
    return cached, new


def _layer(x, batch, t, weights, lag_biases, lb_logits, win_caches, hg_state0, conv_buf0, mem_kv,
           *, prompt, tm_proj, tm_merge, tm_ffn, tq_mem, chunk):
    (w_in16, b_in, gain_pre, gain_post, hg_norm, w_att, w_hg, w_mem, w_out,
     gain_fpre, gain_fpost, w_fa, w_fb, conv_w, conv_b, w_fd) = weights
    act_dtype = BF16 if prompt else F32
    if prompt:
        dils = tuple(dil for _, dil in ATT_GROUPS)
        *qkv, rec, gates = in_proj(x, gain_pre, w_in16, b_in, batch, t, dils, act_dtype, tm_proj)
    else:
        *qkv, rec, gates = in_proj(x, gain_pre, w_in16, b_in, 1, batch * t, (1, 1, 1), act_dtype, tm_proj)
        qkv = [z.reshape(batch, t, QKV_COLS) for z in qkv]

    o_groups, lse_groups, new_caches = [], [], []
    for g, (win, dil) in enumerate(ATT_GROUPS):
        if prompt:
            bp, bc = _prompt_bias(lag_biases[g])
            o, lse = dilated_prompt(qkv[g], bp, bc, g, dil, batch, t)
        else:
            cache = win_caches[g]
            n_past = cache.shape[1] // KV_ROWS
            assert n_past >= win
            bcache, bnew = _sample_bias(lag_biases[g], dil, n_past, t)
            o, lse, newc = dilated_sample(qkv[g], cache, bcache, bnew, g, batch, t)
            new_caches.append(newc)
        o_groups.append(o)
        lse_groups.append(lse)

    hg_o, hg_state = hgrn2(rec, gates, lb_logits, hg_norm, hg_state0, batch, t, chunk, act_dtype)
    mem_o = memory_attention(rec, mem_kv, batch, t, tq_mem, act_dtype)
    x1 = merge(x, o_groups, lse_groups, hg_o, mem_o, gates, w_att, w_hg, w_mem, w_out, gain_post, tm_merge)
    y, tail = conv_ffn(x1, conv_buf0, gain_fpre, w_fa, w_fb, conv_w, conv_b, w_fd, gain_fpost, t, tm_ffn, 512)
    return y, qkv, new_caches, hg_state, tail[:, 6:8, :]


def kernel(x_prompt, x_sample, mem_prompt, cache_win1_kv, cache_win2_kv, cache_win3_kv, cache_mem_kv, state_hgrn, state_ffn_conv, rel_bias, hg_lb_logits, norm_mix_pre, norm_mix_post, w_in, b_in, hg_norm, mem_norm, w_mem_kv, w_br_att, w_br_hg, w_br_mem, w_out, norm_ffn_pre, norm_ffn_post, w_ffn_a, w_ffn_b, ffn_conv_w, ffn_conv_b, w_ffn_d):
    depth = w_in.shape[0]
    assert depth == 1
    bsz, seq, d = x_prompt.shape
    dbsz, dseq, _ = x_sample.shape
    mem_tokens = mem_prompt.shape[1]
    dff = w_ffn_a.shape[2]
    layer = 0

    assert w_in.shape[2] == len(PROJ_STEP_BLOCK) * BRANCH_W
    weights = (w_in[layer].astype(BF16), b_in[layer],
               norm_mix_pre[layer], norm_mix_post[layer], hg_norm[layer],
               w_br_att[layer].astype(BF16), w_br_hg[layer].astype(BF16), w_br_mem[layer].astype(BF16),
               w_out[layer].astype(BF16), norm_ffn_pre[layer], norm_ffn_post[layer],
               w_ffn_a[layer].astype(BF16), w_ffn_b[layer].astype(BF16), ffn_conv_w[layer], ffn_conv_b[layer],
               w_ffn_d[layer].astype(BF16))
    lag_biases = [_lag_bias(rel_bias, g, dil) for g, (_, dil) in enumerate(ATT_GROUPS)]

    mem_kv = norm_matmul(mem_prompt.reshape(bsz * mem_tokens, d), mem_norm[layer], w_mem_kv[layer].astype(BF16),
                         jnp.zeros((2 * BRANCH_W,), F32), F32, bsz * mem_tokens, 1024)
    yp, qkv_p, _, hg_p, conv_p = _layer(
        x_prompt.reshape(bsz * seq, d), bsz, seq, weights, lag_biases, hg_lb_logits, None,
        jnp.zeros((bsz, N_HEADS, HEAD_DIM, HEAD_DIM), F32), jnp.zeros((bsz, 2, dff), F32),
        mem_kv.reshape(bsz, mem_tokens * KV_ROWS, HEAD_DIM),
        prompt=True, tm_proj=1024, tm_merge=256, tm_ffn=512, tq_mem=512, chunk=64)
    p_win = []
    for g, (win, dil) in enumerate(ATT_GROUPS):
        n = min(win, seq)
        assert n % dil == 0
        tail = qkv_p[g][:, :, (seq - n) // dil:, BRANCH_W:]
        tail = jnp.swapaxes(tail, 1, 2).astype(F32)
        p_win.append(tail.reshape(1, bsz, n, 2, N_HEADS, HEAD_DIM))

    as_rows = lambda c: c[layer].reshape(dbsz, c.shape[2] * KV_ROWS, HEAD_DIM)
    ys, _, new_caches, hg_s, conv_s = _layer(
        x_sample.reshape(dbsz * dseq, d), dbsz, dseq, weights, lag_biases, hg_lb_logits,
        [as_rows(c) for c in (cache_win1_kv, cache_win2_kv, cache_win3_kv)],
        state_hgrn[layer], state_ffn_conv[layer], as_rows(cache_mem_kv),
        prompt=False, tm_proj=dbsz * dseq, tm_merge=dbsz * dseq, tm_ffn=dbsz * dseq, tq_mem=dseq, chunk=16)
    s_win = [c.reshape(1, dbsz, c.shape[1] // KV_ROWS, 2, N_HEADS, HEAD_DIM) for c in new_caches]

    return (yp.reshape(bsz, seq, d), ys.reshape(dbsz, dseq, d),
            p_win[0], p_win[1], p_win[2],
            hg_p[None], conv_p[None], mem_kv.reshape(1, bsz, mem_tokens, 2, N_HEADS, HEAD_DIM),
            s_win[0], s_win[1], s_win[2],
            hg_s[None], conv_s[None])
```

```python
import functools
import math

import numpy as np
import jax
import jax.numpy as jnp
from jax import lax
from jax.experimental import pallas as pl
from jax.experimental.pallas import tpu as pltpu

F32 = jnp.float32
BF16 = jnp.bfloat16

NORM_EPS = 1e-6
NEG_INF = -1e30
HEAD_DIM = 128
N_HEADS = 4
BRANCH_W = N_HEADS * HEAD_DIM
N_LAGS = 128
BAND = 128
ATT_GROUPS = ((128, 1), (512, 4), (2048, 16))
REL_BUCKETS = 32
REL_MAX_DIST = 2048
LSE_LANES = 32
HG_BASE = 8
VMEM_LIMIT = 56 * 1024 * 1024

NT_DIMS = (((1,), (1,)), ((), ()))
TN_DIMS = (((0,), (0,)), ((), ()))

W_IN_BLOCKS = 20
QKV_BLOCKS = lambda g: (g, 3 + g, 6 + g)
REC_BLOCKS = (9, 11, 13)
GATE_BLOCKS = (10, 12, 14, 15, 16, 17, 18, 19)
QKV_COLS = 3 * BRANCH_W
REC_COLS = 3 * BRANCH_W
REC_HQ, REC_HI, REC_MQ = 0, 1, 2
G_COLS = 8 * BRANCH_W
KV_ROWS = 2 * N_HEADS


def _cparams(*sem):
    return pltpu.CompilerParams(dimension_semantics=sem, vmem_limit_bytes=VMEM_LIMIT)


def _rms(x, gain):
    return x * lax.rsqrt(jnp.mean(x * x, axis=-1, keepdims=True) + NORM_EPS) * gain


def _sigmoid(x):
    return 1.0 / (1.0 + jnp.exp(-x))


def _norm_matmul_kernel(x_ref, g_ref, w_ref, b_ref, o_ref, hn_ref):
    @pl.when(pl.program_id(1) == 0)
    def _():
        hn_ref[...] = _rms(x_ref[...], g_ref[...]).astype(BF16)

    acc = jnp.dot(hn_ref[...], w_ref[...], preferred_element_type=F32)
    o_ref[...] = (acc + b_ref[...]).astype(o_ref.dtype)


def norm_matmul(x, gain, w, bias, out_dtype, tm, tn):
    m, k = x.shape
    n = w.shape[1]
    assert m % tm == 0 and n % tn == 0
    return pl.pallas_call(
        _norm_matmul_kernel,
        grid=(m // tm, n // tn),
        in_specs=[
            pl.BlockSpec((tm, k), lambda i, j: (i, 0)),
            pl.BlockSpec((1, k), lambda i, j: (0, 0)),
            pl.BlockSpec((k, tn), lambda i, j: (0, j)),
            pl.BlockSpec((1, tn), lambda i, j: (0, j)),
        ],
        out_specs=pl.BlockSpec((tm, tn), lambda i, j: (i, j)),
        out_shape=jax.ShapeDtypeStruct((m, n), out_dtype),
        scratch_shapes=[pltpu.VMEM((tm, k), BF16)],
        compiler_params=_cparams("parallel", "arbitrary"),
        name="norm_matmul",
    )(x, gain.reshape(1, k), w, bias.reshape(1, n))


def _proj_kernel(x_ref, g_ref, *refs, n_blk, dil):
    w_refs, b_refs, o_ref = refs[:n_blk], refs[n_blk:2 * n_blk], refs[2 * n_blk]
    tm = x_ref.shape[0]
    hn = _rms(x_ref[...], g_ref[...]).astype(BF16)
    for n in range(n_blk):
        acc = jnp.dot(hn, w_refs[n][...], preferred_element_type=F32) + b_refs[n][...]
        if dil == 1:
            o_ref[:, n * BRANCH_W:(n + 1) * BRANCH_W] = acc.astype(o_ref.dtype)
            continue
        scr_ref = refs[2 * n_blk + 1]
        for c in range(N_HEADS):
            scr_ref[n * N_HEADS + c] = acc[:, c * HEAD_DIM:(c + 1) * HEAD_DIM]
        for r in range(dil):
            for c in range(N_HEADS):
                lo = n * BRANCH_W + c * HEAD_DIM
                o_ref[r, :, lo:lo + HEAD_DIM] = (
                    scr_ref[n * N_HEADS + c, pl.ds(r, tm // dil, stride=dil), :].astype(o_ref.dtype))


def proj(x, gain, w, bias, blocks, batch, seq, dil, out_dtype, tm):
    m, k = x.shape
    n_blk = len(blocks)
    cols = n_blk * BRANCH_W
    assert m == batch * seq and seq % tm == 0 and tm % dil == 0
    tiles_per_b = seq // tm
    w_specs = [pl.BlockSpec((k, BRANCH_W), lambda i, c=c: (0, c)) for c in blocks]
    b_specs = [pl.BlockSpec((1, BRANCH_W), lambda i, c=c: (0, c)) for c in blocks]
    if dil == 1:
        out_spec = pl.BlockSpec((tm, cols), lambda i: (i, 0))
        out_shape = jax.ShapeDtypeStruct((m, cols), out_dtype)
        scratch = []
    else:
        out_spec = pl.BlockSpec((None, dil, tm // dil, cols), lambda i: (i // tiles_per_b, 0, i % tiles_per_b, 0))
        out_shape = jax.ShapeDtypeStruct((batch, dil, seq // dil, cols), out_dtype)
        scratch = [pltpu.VMEM((n_blk * N_HEADS, tm, HEAD_DIM), F32)]
    return pl.pallas_call(
        functools.partial(_proj_kernel, n_blk=n_blk, dil=dil),
        grid=(m // tm,),
        in_specs=[pl.BlockSpec((tm, k), lambda i: (i, 0)), pl.BlockSpec((1, k), lambda i: (0, 0))] + w_specs + b_specs,
        out_specs=out_spec,
        out_shape=out_shape,
        scratch_shapes=scratch,
        compiler_params=_cparams("parallel"),
        name="proj",
    )(x, gain.reshape(1, k), *([w] * n_blk), *([bias] * n_blk))


def _dil_prompt_kernel(q_ref, kp_ref, kc_ref, vp_ref, vc_ref, bp_ref, bc_ref, o_ref, lse_ref, *scratch, dil, qb):
    first = pl.program_id(1) == 0
    scale = 1.0 / math.sqrt(HEAD_DIM)

    def band_block(r, j):
        rows = slice(j * BAND, (j + 1) * BAND)
        before = slice((j - 1) * BAND, j * BAND)
        outs, lses = [], []
        for h in range(N_HEADS):
            hs = slice(h * HEAD_DIM, (h + 1) * HEAD_DIM)
            q = q_ref[r, rows, hs]
            k_prev = kp_ref[r, :, hs] if j == 0 else kc_ref[r, before, hs]
            v_prev = vp_ref[r, :, hs] if j == 0 else vc_ref[r, before, hs]
            lp = lax.dot_general(q, k_prev, NT_DIMS, preferred_element_type=F32) * scale + bp_ref[h]
            if j == 0:
                lp = jnp.where(first, NEG_INF, lp)
            lc = lax.dot_general(q, kc_ref[r, rows, hs], NT_DIMS, preferred_element_type=F32) * scale + bc_ref[h]
            m = jnp.maximum(jnp.max(lp, axis=-1, keepdims=True), jnp.max(lc, axis=-1, keepdims=True))
            pp = jnp.exp(lp - m)
            pc = jnp.exp(lc - m)
            s = jnp.sum(pp, axis=-1, keepdims=True) + jnp.sum(pc, axis=-1, keepdims=True)
            o = (jnp.dot(pp.astype(BF16), v_prev, preferred_element_type=F32)
                 + jnp.dot(pc.astype(BF16), vc_ref[r, rows, hs], preferred_element_type=F32))
            outs.append(o / s)
            lses.append(jnp.broadcast_to(m + jnp.log(s), (BAND, LSE_LANES)))
        return outs, jnp.concatenate(lses, axis=1)

    if dil == 1:
        for j in range(qb):
            outs, lse = band_block(0, j)
            for h in range(N_HEADS):
                o_ref[j * BAND:(j + 1) * BAND, h * HEAD_DIM:(h + 1) * HEAD_DIM] = outs[h].astype(o_ref.dtype)
            lse_ref[j * BAND:(j + 1) * BAND, :] = lse
        return

    o_scr, lse_scr = scratch

    def body(r, carry):
        for j in range(qb):
            outs, lse = band_block(r, j)
            for h in range(N_HEADS):
                o_scr[h, pl.ds(r + j * dil * BAND, BAND, stride=dil), :] = outs[h]
            lse_scr[pl.ds(r + j * dil * BAND, BAND, stride=dil), :] = lse
        return carry

    lax.fori_loop(0, dil, body, 0, unroll=min(dil, 4))
    for h in range(N_HEADS):
        o_ref[:, h * HEAD_DIM:(h + 1) * HEAD_DIM] = o_scr[h].astype(o_ref.dtype)
    lse_ref[...] = lse_scr[...]


def dilated_prompt(qkv, bias_prev, bias_cur, g, dil, batch, seq, qb):
    assert seq % (dil * BAND * qb) == 0
    sub = seq // dil
    span = dil * BAND * qb
    cur = lambda part: pl.BlockSpec((None, dil, qb * BAND, BRANCH_W), lambda b, i: (b, 0, i, part))
    prev = lambda part: pl.BlockSpec((None, dil, BAND, BRANCH_W),
                                     lambda b, i: (b, 0, jnp.maximum(i * qb - 1, 0), part))
    bias_spec = pl.BlockSpec((N_HEADS, BAND, BAND), lambda b, i: (0, 0, 0))
    scratch = [] if dil == 1 else [pltpu.VMEM((N_HEADS, span, HEAD_DIM), F32),
                                   pltpu.VMEM((span, N_HEADS * LSE_LANES), F32)]
    o, lse = pl.pallas_call(
        functools.partial(_dil_prompt_kernel, dil=dil, qb=qb),
        grid=(batch, sub // (BAND * qb)),
        in_specs=[cur(0), prev(1), cur(1), prev(2), cur(2), bias_spec, bias_spec],
        out_specs=[pl.BlockSpec((None, span, BRANCH_W), lambda b, i: (b, i, 0)),
                   pl.BlockSpec((None, span, N_HEADS * LSE_LANES), lambda b, i: (b, i, 0))],
        out_shape=[jax.ShapeDtypeStruct((batch, seq, BRANCH_W), BF16),
                   jax.ShapeDtypeStruct((batch, seq, N_HEADS * LSE_LANES), F32)],
        scratch_shapes=scratch,
        compiler_params=_cparams("parallel", "arbitrary"),
        name=f"dilated_prompt_g{g}",
    )(qkv, qkv, qkv, qkv, qkv, bias_prev, bias_cur)
    return o.reshape(batch * seq, BRANCH_W), lse.reshape(batch * seq, N_HEADS * LSE_LANES)


def _dil_sample_kernel(q_ref, kn_ref, vn_ref, cache_ref, bc_ref, bn_ref, o_ref, lse_ref, newc_ref, *, win, t_new):
    scale = 1.0 / math.sqrt(HEAD_DIM)
    keep = (win - t_new) * KV_ROWS
    newc_ref[0:keep, :] = cache_ref[t_new * KV_ROWS:win * KV_ROWS, :]
    for h in range(N_HEADS):
        hs = slice(h * HEAD_DIM, (h + 1) * HEAD_DIM)
        q = q_ref[:, hs]
        kn = kn_ref[:, hs]
        vn = vn_ref[:, hs]
        kc = cache_ref[pl.ds(h, win, stride=KV_ROWS), :].astype(BF16)
        vc = cache_ref[pl.ds(N_HEADS + h, win, stride=KV_ROWS), :].astype(BF16)
        lc = lax.dot_general(q.astype(BF16), kc, NT_DIMS, preferred_element_type=F32) * scale + bc_ref[h]
        ln = lax.dot_general(q, kn, NT_DIMS, preferred_element_type=F32) * scale + bn_ref[h]
        m = jnp.maximum(jnp.max(lc, axis=-1, keepdims=True), jnp.max(ln, axis=-1, keepdims=True))
        pc = jnp.exp(lc - m)
        pn = jnp.exp(ln - m)
        s = jnp.sum(pc, axis=-1, keepdims=True) + jnp.sum(pn, axis=-1, keepdims=True)
        o = jnp.dot(pc.astype(BF16), vc, preferred_element_type=F32) + jnp.dot(pn, vn, preferred_element_type=F32)
        o_ref[:, hs] = o / s
        lse_ref[:, h * LSE_LANES:(h + 1) * LSE_LANES] = jnp.broadcast_to(m + jnp.log(s), (t_new, LSE_LANES))
        newc_ref[pl.ds(keep + h, t_new, stride=KV_ROWS), :] = kn
        newc_ref[pl.ds(keep + N_HEADS + h, t_new, stride=KV_ROWS), :] = vn


def dilated_sample(qkv, cache, bias_cache, bias_new, g, batch, t_new):
    win = cache.shape[1] // KV_ROWS
    blk = (None, t_new, BRANCH_W)
    cache_spec = pl.BlockSpec((None, win * KV_ROWS, HEAD_DIM), lambda b: (b, 0, 0))
    o, lse, newc = pl.pallas_call(
        functools.partial(_dil_sample_kernel, win=win, t_new=t_new),
        grid=(batch,),
        in_specs=[pl.BlockSpec(blk, lambda b: (b, 0, 0)),
                  pl.BlockSpec(blk, lambda b: (b, 0, 1)),
                  pl.BlockSpec(blk, lambda b: (b, 0, 2)),
                  cache_spec,
                  pl.BlockSpec((N_HEADS, t_new, win), lambda b: (0, 0, 0)),
                  pl.BlockSpec((N_HEADS, t_new, t_new), lambda b: (0, 0, 0))],
        out_specs=[pl.BlockSpec(blk, lambda b: (b, 0, 0)),
                   pl.BlockSpec((None, t_new, N_HEADS * LSE_LANES), lambda b: (b, 0, 0)),
                   cache_spec],
        out_shape=[jax.ShapeDtypeStruct((batch, t_new, BRANCH_W), F32),
                   jax.ShapeDtypeStruct((batch, t_new, N_HEADS * LSE_LANES), F32),
                   jax.ShapeDtypeStruct(cache.shape, F32)],
        compiler_params=_cparams("parallel"),
        name=f"dilated_sample_g{g}",
    )(qkv, qkv, qkv, cache, bias_cache, bias_new)
    return o.reshape(batch * t_new, BRANCH_W), lse.reshape(batch * t_new, N_HEADS * LSE_LANES), newc


def _mem_attn_kernel(q_ref, kv_ref, o_ref, *, mem):
    scale = 1.0 / math.sqrt(HEAD_DIM)
    for h in range(N_HEADS):
        hs = slice(h * HEAD_DIM, (h + 1) * HEAD_DIM)
        k = kv_ref[pl.ds(h, mem, stride=KV_ROWS), :].astype(BF16)
        v = kv_ref[pl.ds(N_HEADS + h, mem, stride=KV_ROWS), :].astype(BF16)
        logits = lax.dot_general(q_ref[:, hs].astype(BF16), k, NT_DIMS, preferred_element_type=F32) * scale
        m = jnp.max(logits, axis=-1, keepdims=True)
        p = jnp.exp(logits - m)
        s = jnp.sum(p, axis=-1, keepdims=True)
        o = jnp.dot(p.astype(BF16), v, preferred_element_type=F32)
        o_ref[:, hs] = (o / s).astype(o_ref.dtype)


def memory_attention(rec, mem_kv, batch, t, tq, out_dtype):
    mem = mem_kv.shape[1] // KV_ROWS
    x = rec.reshape(batch, t, REC_COLS)
    o = pl.pallas_call(
        functools.partial(_mem_attn_kernel, mem=mem),
        grid=(batch, t // tq),
        in_specs=[pl.BlockSpec((None, tq, BRANCH_W), lambda b, i: (b, i, REC_MQ)),
                  pl.BlockSpec((None, mem * KV_ROWS, HEAD_DIM), lambda b, i: (b, 0, 0))],
        out_specs=pl.BlockSpec((None, tq, BRANCH_W), lambda b, i: (b, i, 0)),
        out_shape=jax.ShapeDtypeStruct((batch, t, BRANCH_W), out_dtype),
        compiler_params=_cparams("parallel", "parallel"),
        name="memory_attention",
    )(x, mem_kv)
    return o.reshape(batch * t, BRANCH_W)


def _hgrn_tables(chunk):
    c, b = chunk, HG_BASE
    t = np.arange(c)[:, None]
    u = np.arange(c)[None, :]
    masks = []
    size = 2 * b
    while size <= c:
        half = size // 2
        masks.append(((t // size) == (u // size)) & ((t % size) >= half) & ((u % size) < half))
        size *= 2
    masks.append(((t // b) == (u // b)) & (u <= t))
    erep = np.zeros((b * HEAD_DIM, c), np.float32)
    for s in range(b):
        erep[s * HEAD_DIM:(s + 1) * HEAD_DIM, np.arange(c) % b == s] = 1.0
    return (u <= t).astype(np.float32), erep, np.stack(masks).astype(np.float32)


def _hgrn_kernel(q_ref, v_ref, f_ref, gate_ref, lbl_ref, gn_ref, s0_ref, tri_ref, erep_ref, mask_ref,
                 o_ref, s_out_ref, st_ref, *, chunk, rows_in, n_sub, n_levels):
    c = chunk
    step = pl.program_id(1)

    @pl.when(step == 0)
    def _():
        for h in range(N_HEADS):
            st_ref[h] = s0_ref[h].T

    lbl = lbl_ref[...]
    e = jnp.exp(lbl - jnp.max(lbl, axis=0, keepdims=True))
    lb = e[0:1] / jnp.sum(e, axis=0, keepdims=True)
    tri = tri_ref[...]

    def rows(ref, ci):
        x = ref[ci * rows_in:(ci + 1) * rows_in, :].astype(F32)
        if rows_in < c:
            x = jnp.concatenate([x, jnp.zeros((c - rows_in, x.shape[1]), F32)], axis=0)
        return x

    for ci in range(n_sub):
        q = rows(q_ref, ci)
        v = rows(v_ref, ci).astype(BF16)
        f = lb + (1.0 - lb) * _sigmoid(rows(f_ref, ci))
        g = jnp.log(f)
        kk = 1.0 - f
        if rows_in < c:
            live = lax.broadcasted_iota(jnp.int32, (c, 1), 0) < rows_in
            g = jnp.where(live, g, 0.0)
            kk = jnp.where(live, kk, 0.0)

        g1 = g.astype(BF16)
        r1 = g - g1.astype(F32)
        g2 = r1.astype(BF16)
        g3 = (r1 - g2.astype(F32)).astype(BF16)
        cum = (jnp.dot(tri, g1, preferred_element_type=F32) + jnp.dot(tri, g2, preferred_element_type=F32)
               + jnp.dot(tri, g3, preferred_element_type=F32))

        def at_row(x, size, idx):
            x3 = x.reshape(c // size, size, x.shape[1])
            return jnp.broadcast_to(x3[:, idx:idx + 1, :], x3.shape).reshape(x.shape)

        q_inter = (q * jnp.exp(cum)).astype(BF16)
        g_end = cum[c - 1:c]
        k_end = (kk * jnp.exp(g_end - cum)).astype(BF16)
        q_lvl, k_lvl = [], []
        for l in range(n_levels):
            size = 2 * HG_BASE << l
            ref = at_row(cum, size, size // 2 - 1)
            q_lvl.append((q * jnp.exp(jnp.minimum(cum - ref, 0.0))).astype(BF16))
            k_lvl.append((kk * jnp.exp(jnp.minimum(ref - cum, 0.0))).astype(BF16))
        x_diag = []
        for s in range(HG_BASE):
            decay = jnp.exp(jnp.minimum(cum - at_row(cum, HG_BASE, s), 0.0))
            x_diag.append((q * at_row(kk, HG_BASE, s) * decay).astype(BF16))
        gate = _sigmoid(rows(gate_ref, ci))

        for h in range(N_HEADS):
            hs = slice(h * HEAD_DIM, (h + 1) * HEAD_DIM)
            scores = mask_ref[n_levels] * jnp.dot(jnp.concatenate([x[:, hs] for x in x_diag], axis=1),
                                                  erep_ref[...], preferred_element_type=F32)
            for l in range(n_levels):
                scores += mask_ref[l] * lax.dot_general(q_lvl[l][:, hs], k_lvl[l][:, hs], NT_DIMS,
                                                        preferred_element_type=F32)
            st = st_ref[h]
            o = (lax.dot_general(q_inter[:, hs], st.astype(BF16), NT_DIMS, preferred_element_type=F32)
                 + jnp.dot(scores.astype(BF16), v[:, hs], preferred_element_type=F32))
            st_ref[h] = (st * jnp.exp(g_end[:, hs])
                         + lax.dot_general(v[:, hs], k_end[:, hs], TN_DIMS, preferred_element_type=F32))
            y = _rms(o, gn_ref[...]) * gate[:, hs]
            o_ref[ci * rows_in:(ci + 1) * rows_in, hs] = y[0:rows_in].astype(o_ref.dtype)

    @pl.when(step == pl.num_programs(1) - 1)
    def _():
        for h in range(N_HEADS):
            s_out_ref[h] = st_ref[h].T


def hgrn2(rec, gates, lb_logits, hg_norm, state0, batch, t, chunk, n_sub, out_dtype):
    rows_in = min(chunk, t)
    step_rows = rows_in * n_sub
    assert t % step_rows == 0 and chunk % HG_BASE == 0 and (n_sub == 1 or rows_in == chunk)
    tri, erep, masks = _hgrn_tables(chunk)
    n_levels = masks.shape[0] - 1
    a = rec.reshape(batch, t, REC_COLS)
    gt = gates.reshape(batch, t, G_COLS)
    blk = (None, step_rows, BRANCH_W)
    full = lambda arr: pl.BlockSpec(arr.shape, lambda b, i: (0,) * arr.ndim)
    st_spec = pl.BlockSpec((None, N_HEADS, HEAD_DIM, HEAD_DIM), lambda b, i: (b, 0, 0, 0))
    tri = jnp.asarray(tri, BF16)
    erep = jnp.asarray(erep, BF16)
    masks = jnp.asarray(masks, F32)
    lbl = lb_logits.astype(F32)
    gn = hg_norm.reshape(1, HEAD_DIM).astype(F32)
    o, s_fin = pl.pallas_call(
        functools.partial(_hgrn_kernel, chunk=chunk, rows_in=rows_in, n_sub=n_sub, n_levels=n_levels),
        grid=(batch, t // step_rows),
        in_specs=[pl.BlockSpec(blk, lambda b, i: (b, i, REC_HQ)),
                  pl.BlockSpec(blk, lambda b, i: (b, i, REC_HI)),
                  pl.BlockSpec(blk, lambda b, i: (b, i, 0)),
                  pl.BlockSpec(blk, lambda b, i: (b, i, 1)),
                  full(lbl), full(gn), st_spec, full(tri), full(erep), full(masks)],
        out_specs=[pl.BlockSpec(blk, lambda b, i: (b, i, 0)), st_spec],
        out_shape=[jax.ShapeDtypeStruct((batch, t, BRANCH_W), out_dtype),
                   jax.ShapeDtypeStruct(state0.shape, F32)],
        scratch_shapes=[pltpu.VMEM((N_HEADS, HEAD_DIM, HEAD_DIM), F32)],
        compiler_params=_cparams("parallel", "arbitrary"),
        name="hgrn2",
    )(a, a, gt, gt, lbl, gn, state0, tri, erep, masks)
    return o.reshape(batch * t, BRANCH_W), s_fin


def _merge_kernel(x_ref, o1_ref, o2_ref, o3_ref, l1_ref, l2_ref, l3_ref, hg_ref, mem_ref, ga_ref, gh_ref, gm_ref,
                  wa_ref, wh_ref, wm_ref, wo_ref, gain_ref, out_ref):
    tm = x_ref.shape[0]
    l1, l2, l3 = l1_ref[...], l2_ref[...], l3_ref[...]
    m = jnp.maximum(jnp.maximum(l1, l2), l3)
    e1, e2, e3 = jnp.exp(l1 - m), jnp.exp(l2 - m), jnp.exp(l3 - m)
    den = e1 + e2 + e3
    w1, w2, w3 = e1 / den, e2 / den, e3 / den
    att = []
    for h in range(N_HEADS):
        hs = slice(h * HEAD_DIM, (h + 1) * HEAD_DIM)
        col = slice(h * LSE_LANES, h * LSE_LANES + 1)
        bc = lambda w: jnp.broadcast_to(w[:, col], (tm, HEAD_DIM))
        att.append((bc(w1) * o1_ref[:, hs].astype(F32) + bc(w2) * o2_ref[:, hs].astype(F32)
                    + bc(w3) * o3_ref[:, hs].astype(F32)).astype(BF16))
    att = jnp.concatenate(att, axis=1)
    merged = (_sigmoid(ga_ref[...]) * jnp.dot(att, wa_ref[...], preferred_element_type=F32)
              + _sigmoid(gh_ref[...]) * jnp.dot(hg_ref[...].astype(BF16), wh_ref[...], preferred_element_type=F32)
              + _sigmoid(gm_ref[...]) * jnp.dot(mem_ref[...].astype(BF16), wm_ref[...], preferred_element_type=F32))
    y = jnp.dot(merged.astype(BF16), wo_ref[...], preferred_element_type=F32)
    out_ref[...] = x_ref[...] + _rms(y, gain_ref[...])


def merge(x, o_groups, lse_groups, hg_o, mem_o, gates, w_att, w_hg, w_mem, w_out, gain, tm):
    m, d = x.shape
    assert m % tm == 0
    row = lambda w: pl.BlockSpec((tm, w), lambda i: (i, 0))
    full = lambda arr: pl.BlockSpec(arr.shape, lambda i: (0, 0))
    gate_spec = lambda j: pl.BlockSpec((tm, d), lambda i: (i, j))
    gain = gain.reshape(1, d)
    return pl.pallas_call(
        _merge_kernel,
        grid=(m // tm,),
        in_specs=[row(d), row(BRANCH_W), row(BRANCH_W), row(BRANCH_W),
                  row(N_HEADS * LSE_LANES), row(N_HEADS * LSE_LANES), row(N_HEADS * LSE_LANES),
                  row(BRANCH_W), row(BRANCH_W), gate_spec(1), gate_spec(2), gate_spec(3),
                  full(w_att), full(w_hg), full(w_mem), full(w_out), full(gain)],
        out_specs=row(d),
        out_shape=jax.ShapeDtypeStruct((m, d), F32),
        compiler_params=_cparams("parallel"),
        name="merge",
    )(x, *o_groups, *lse_groups, hg_o, mem_o, gates, gates, gates, w_att, w_hg, w_mem, w_out, gain)


def _ffn_kernel(x_ref, gpre_ref, wa_ref, wb_ref, cw_ref, cb_ref, wd_ref, gpost_ref, cbuf_ref,
                out_ref, tail_ref, hn_ref, acc_ref, carry_ref, *, t_seq, tiles_per_seq):
    tm = x_ref.shape[0]
    tf = wa_ref.shape[1]
    i = pl.program_id(0)
    j = pl.program_id(1)

    @pl.when(j == 0)
    def _():
        hn_ref[...] = _rms(x_ref[...], gpre_ref[...]).astype(BF16)
        acc_ref[...] = jnp.zeros_like(acc_ref)

    hn = hn_ref[...]
    a = jnp.dot(hn, wa_ref[...], preferred_element_type=F32)
    up = jnp.dot(hn, wb_ref[...], preferred_element_type=F32)

    if tiles_per_seq >= 1:
        @pl.when((i % tiles_per_seq) == 0)
        def _():
            carry_ref[j, 6:8, :] = cbuf_ref[0]

        prev1 = carry_ref[j, 7:8, :]
        prev2 = carry_ref[j, 6:7, :]
        t_idx = lax.broadcasted_iota(jnp.int32, (tm, 1), 0)
    else:
        n_seq = tm // t_seq
        cb3 = cbuf_ref[...]
        prev1 = jnp.broadcast_to(cb3[:, 1:2, :], (n_seq, t_seq, tf)).reshape(tm, tf)
        prev2 = jnp.broadcast_to(cb3[:, 0:1, :], (n_seq, t_seq, tf)).reshape(tm, tf)
        t_idx = lax.broadcasted_iota(jnp.int32, (tm, 1), 0) % t_seq
    a1 = jnp.where(t_idx >= 1, pltpu.roll(a, 1, 0), prev1)
    a2 = jnp.where(t_idx >= 2, pltpu.roll(a, 2, 0), jnp.where(t_idx == 1, prev1, prev2))
    conv = cb_ref[...] + a2 * cw_ref[0:1, :] + a1 * cw_ref[1:2, :] + a * cw_ref[2:3, :]
    act = conv * _sigmoid(conv) * up
    acc_ref[...] += jnp.dot(act.astype(BF16), wd_ref[...], preferred_element_type=F32)

    if tiles_per_seq >= 1:
        carry_ref[j] = a[tm - 8:tm]
        tail_ref[0] = a[tm - 8:tm]
    else:
        tail_ref[...] = a.reshape(tm // t_seq, t_seq, tf)

    @pl.when(j == pl.num_programs(1) - 1)
    def _():
        out_ref[...] = x_ref[...] + _rms(acc_ref[...], gpost_ref[...])


def conv_ffn(x, conv_buf, g_pre, w_a, w_b, conv_w, conv_b, w_d, g_post, t_seq, tm, tf):
    m, d = x.shape
    dff = w_a.shape[1]
    n_seq = m // t_seq
    assert m % tm == 0 and dff % tf == 0 and t_seq >= 8
    if tm <= t_seq:
        assert t_seq % tm == 0
        tiles_per_seq = t_seq // tm
        seq_blk = 1
        seq_idx = lambda i, j: (i // tiles_per_seq, 0, j)
    else:
        assert tm % t_seq == 0 and t_seq == 8
        tiles_per_seq = 0
        seq_blk = tm // t_seq
        seq_idx = lambda i, j: (i, 0, j)
    nf = dff // tf
    n_tail = (m // tm) * seq_blk
    vec = lambda w: pl.BlockSpec((1, w), lambda i, j: (0, 0))
    y, tail = pl.pallas_call(
        functools.partial(_ffn_kernel, t_seq=t_seq, tiles_per_seq=tiles_per_seq),
        grid=(m // tm, nf),
        in_specs=[pl.BlockSpec((tm, d), lambda i, j: (i, 0)), vec(d),
                  pl.BlockSpec((d, tf), lambda i, j: (0, j)),
                  pl.BlockSpec((d, tf), lambda i, j: (0, j)),
                  pl.BlockSpec((3, tf), lambda i, j: (0, j)),
                  pl.BlockSpec((1, tf), lambda i, j: (0, j)),
                  pl.BlockSpec((tf, d), lambda i, j: (j, 0)), vec(d),
                  pl.BlockSpec((seq_blk, 2, tf), seq_idx)],
        out_specs=[pl.BlockSpec((tm, d), lambda i, j: (i, 0)),
                   pl.BlockSpec((seq_blk, 8, tf), lambda i, j: (i, 0, j))],
        out_shape=[jax.ShapeDtypeStruct((m, d), F32), jax.ShapeDtypeStruct((n_tail, 8, dff), F32)],
        scratch_shapes=[pltpu.VMEM((tm, d), BF16), pltpu.VMEM((tm, d), F32), pltpu.VMEM((nf, 8, tf), F32)],
        compiler_params=_cparams("arbitrary", "arbitrary"),
        name="conv_ffn",
    )(x, g_pre.reshape(1, d), w_a, w_b, conv_w, conv_b.reshape(1, dff), w_d, g_post.reshape(1, d), conv_buf)
    if tiles_per_seq >= 1:
        tail = tail.reshape(n_seq, tiles_per_seq, 8, dff)[:, -1]
    return y, tail


def _rel_bucket(dist):
    max_exact = REL_BUCKETS // 2
    d = jnp.maximum(dist, 1).astype(F32)
    large = max_exact + (jnp.log(d / max_exact) / math.log(REL_MAX_DIST / max_exact)
                         * (REL_BUCKETS - max_exact)).astype(jnp.int32)
    large = jnp.minimum(large, REL_BUCKETS - 1)
    return jnp.where(dist < max_exact, dist, large)


def _lag_bias(rel_bias, g, dil):
    dist = jnp.arange(N_LAGS + 1, dtype=jnp.int32) * dil
    onehot = _rel_bucket(dist)[:, None] == jnp.arange(REL_BUCKETS, dtype=jnp.int32)[None, :]
    table = jnp.sum(jnp.where(onehot[:, :, None], rel_bias.astype(F32)[None], 0.0), axis=1)
    return table[:, g * N_HEADS:(g + 1) * N_HEADS].T


def _neg(h, n):
    return jnp.full((h, n), NEG_INF, F32)


def _prompt_bias(lag_bias):
    h = lag_bias.shape[0]
    v = jnp.concatenate([lag_bias[:, ::-1], _neg(h, BAND)], axis=1)
    period = 2 * BAND + 1
    assert v.shape[1] == period
    both = jnp.tile(v, (1, BAND))[:, :BAND * 2 * BAND].reshape(h, BAND, 2 * BAND)
    return both[:, :, :BAND], both[:, :, BAND:]


def _sample_bias(lag_bias, dil, n_past, t_new):
    h = lag_bias.shape[0]
    n = n_past + t_new
    per_dist = jnp.repeat(lag_bias, dil, axis=1)
    per_dist = jnp.concatenate([per_dist, _neg(h, max(n - per_dist.shape[1], 0))], axis=1)[:, :n]
    per_dist = jnp.where((np.arange(n) % dil == 0)[None], per_dist, NEG_INF)
    rev = per_dist[:, ::-1]
    cached = jnp.stack([rev[:, t_new - 1 - t:t_new - 1 - t + n_past] for t in range(t_new)], axis=1)
    padded = jnp.concatenate([_neg(h, t_new - 1), per_dist[:, :t_new]], axis=1)
    new = jnp.stack([padded[:, t:t + t_new][:, ::-1] for t in range(t_new)], axis=1)
    return cached, new


def _layer(x, batch, t, weights, lag_biases, lb_logits, win_caches, hg_state0, conv_buf0, mem_kv,
           *, prompt, tm_proj, tm_merge, tm_ffn, tq_mem, chunk, n_sub):
    (w_in16, b_in, gain_pre, gain_post, hg_norm, w_att, w_hg, w_mem, w_out,
     gain_fpre, gain_fpost, w_fa, w_fb, conv_w, conv_b, w_fd) = weights
    act_dtype = BF16 if prompt else F32
    b_in = b_in.reshape(1, -1)
    if prompt:
        qkv = [proj(x, gain_pre, w_in16, b_in, QKV_BLOCKS(g), batch, t, dil, act_dtype, tm_proj)
               for g, (_, dil) in enumerate(ATT_GROUPS)]
        qkv[0] = qkv[0].reshape(batch, 1, t, QKV_COLS)
    else:
        qkv = [proj(x, gain_pre, w_in16, b_in, QKV_BLOCKS(g), 1, batch * t, 1, act_dtype, tm_proj)
               .reshape(batch, t, QKV_COLS) for g in range(len(ATT_GROUPS))]
    rec = proj(x, gain_pre, w_in16, b_in, REC_BLOCKS, 1, batch * t, 1, act_dtype, tm_proj)
    gates = proj(x, gain_pre, w_in16, b_in, GATE_BLOCKS, 1, batch * t, 1, F32, min(tm_proj, 512))

    o_groups, lse_groups, new_caches = [], [], []
    for g, (win, dil) in enumerate(ATT_GROUPS):
        if prompt:
            bp, bc = _prompt_bias(lag_biases[g])
            o, lse = dilated_prompt(qkv[g], bp, bc, g, dil, batch, t, 4 if dil == 1 else 1)
        else:
            cache = win_caches[g]
            n_past = cache.shape[1] // KV_ROWS
            assert n_past >= win
            bcache, bnew = _sample_bias(lag_biases[g], dil, n_past, t)
            o, lse, newc = dilated_sample(qkv[g], cache, bcache, bnew, g, batch, t)
            new_caches.append(newc)
        o_groups.append(o)
        lse_groups.append(lse)

    hg_o, hg_state = hgrn2(rec, gates, lb_logits, hg_norm, hg_state0, batch, t, chunk, n_sub, act_dtype)
    mem_o = memory_attention(rec, mem_kv, batch, t, tq_mem, act_dtype)
    x1 = merge(x, o_groups, lse_groups, hg_o, mem_o, gates, w_att, w_hg, w_mem, w_out, gain_post, tm_merge)
    y, tail = conv_ffn(x1, conv_buf0, gain_fpre, w_fa, w_fb, conv_w, conv_b, w_fd, gain_fpost, t, tm_ffn, 512)
    return y, qkv, new_caches, hg_state, tail[:, 6:8, :]


def kernel(x_prompt, x_sample, mem_prompt, cache_win1_kv, cache_win2_kv, cache_win3_kv, cache_mem_kv, state_hgrn, state_ffn_conv, rel_bias, hg_lb_logits, norm_mix_pre, norm_mix_post, w_in, b_in, hg_norm, mem_norm, w_mem_kv, w_br_att, w_br_hg, w_br_mem, w_out, norm_ffn_pre, norm_ffn_post, w_ffn_a, w_ffn_b, ffn_conv_w, ffn_conv_b, w_ffn_d):
    depth = w_in.shape[0]
    assert depth == 1
    bsz, seq, d = x_prompt.shape
    dbsz, dseq, _ = x_sample.shape
    mem_tokens = mem_prompt.shape[1]
    dff = w_ffn_a.shape[2]
    layer = 0

    assert w_in.shape[2] == W_IN_BLOCKS * BRANCH_W
    weights = (w_in[layer].astype(BF16), b_in[layer],
               norm_mix_pre[layer], norm_mix_post[layer], hg_norm[layer],
               w_br_att[layer].astype(BF16), w_br_hg[layer].astype(BF16), w_br_mem[layer].astype(BF16),
               w_out[layer].astype(BF16), norm_ffn_pre[layer], norm_ffn_post[layer],
               w_ffn_a[layer].astype(BF16), w_ffn_b[layer].astype(BF16), ffn_conv_w[layer], ffn_conv_b[layer],
               w_ffn_d[layer].astype(BF16))
    lag_biases = [_lag_bias(rel_bias, g, dil) for g, (_, dil) in enumerate(ATT_GROUPS)]

    mem_kv = norm_matmul(mem_prompt.reshape(bsz * mem_tokens, d), mem_norm[layer], w_mem_kv[layer].astype(BF16),
                         jnp.zeros((2 * BRANCH_W,), F32), F32, bsz * mem_tokens, 1024)
    yp, qkv_p, _, hg_p, conv_p = _layer(
        x_prompt.reshape(bsz * seq, d), bsz, seq, weights, lag_biases, hg_lb_logits, None,
        jnp.zeros((bsz, N_HEADS, HEAD_DIM, HEAD_DIM), F32), jnp.zeros((bsz, 2, dff), F32),
        mem_kv.reshape(bsz, mem_tokens * KV_ROWS, HEAD_DIM),
        prompt=True, tm_proj=1024, tm_merge=512, tm_ffn=1024, tq_mem=512, chunk=64, n_sub=4)
    p_win = []
    for g, (win, dil) in enumerate(ATT_GROUPS):
        n = min(win, seq)
        assert n % dil == 0
        tail = qkv_p[g][:, :, (seq - n) // dil:, BRANCH_W:]
        tail = jnp.swapaxes(tail, 1, 2).astype(F32)
        p_win.append(tail.reshape(1, bsz, n, 2, N_HEADS, HEAD_DIM))

    as_rows = lambda c: c[layer].reshape(dbsz, c.shape[2] * KV_ROWS, HEAD_DIM)
    ys, _, new_caches, hg_s, conv_s = _layer(
        x_sample.reshape(dbsz * dseq, d), dbsz, dseq, weights, lag_biases, hg_lb_logits,
        [as_rows(c) for c in (cache_win1_kv, cache_win2_kv, cache_win3_kv)],
        state_hgrn[layer], state_ffn_conv[layer], as_rows(cache_mem_kv),
        prompt=False, tm_proj=dbsz * dseq, tm_merge=dbsz * dseq, tm_ffn=dbsz * dseq, tq_mem=dseq, chunk=16, n_sub=1)
    s_win = [c.reshape(1, dbsz, c.shape[1] // KV_ROWS, 2, N_HEADS, HEAD_DIM) for c in new_caches]

    return (yp.reshape(bsz, seq, d), ys.reshape(dbsz, dseq, d),
            p_win[0], p_win[1], p_win[2],
            hg_p[None], conv_p[None], mem_kv.reshape(1, bsz, mem_tokens, 2, N_HEADS, HEAD_DIM),
            s_win[0], s_win[1], s_win[2],
            hg_s[None], conv_s[None])
```

```python
import functools
import math

import numpy as np
import jax
import jax.numpy as jnp
from jax import lax
from jax.experimental import pallas as pl
from jax.experimental.pallas import tpu as pltpu

F32 = jnp.float32
BF16 = jnp.bfloat16

NORM_EPS = 1e-6
NEG_INF = -1e30
HEAD_DIM = 128
N_HEADS = 4
BRANCH_W = N_HEADS * HEAD_DIM
N_LAGS = 128
BAND = 128
ATT_GROUPS = ((128, 1), (512, 4), (2048, 16))
REL_BUCKETS = 32
REL_MAX_DIST = 2048
LSE_LANES = 32
HG_BASE = 8
VMEM_LIMIT = 56 * 1024 * 1024

NT_DIMS = (((1,), (1,)), ((), ()))
TN_DIMS = (((0,), (0,)), ((), ()))

W_IN_BLOCKS = 20
QKV_BLOCKS = lambda g: (g, 3 + g, 6 + g)
REC_BLOCKS = (9, 11, 13)
GATE_BLOCKS = (10, 12)
GATE_MERGE_BLOCKS = (7, 8, 9)
QKV_COLS = 3 * BRANCH_W
REC_COLS = 3 * BRANCH_W
REC_HQ, REC_HI, REC_MQ = 0, 1, 2
G_COLS = 2 * BRANCH_W
KV_ROWS = 2 * N_HEADS


def _cparams(*sem):
    return pltpu.CompilerParams(dimension_semantics=sem, vmem_limit_bytes=VMEM_LIMIT)


def _rms(x, gain):
    return x * lax.rsqrt(jnp.mean(x * x, axis=-1, keepdims=True) + NORM_EPS) * gain


def _sigmoid(x):
    return 1.0 / (1.0 + jnp.exp(-x))


def _norm_matmul_kernel(x_ref, g_ref, w_ref, b_ref, o_ref, hn_ref):
    @pl.when(pl.program_id(1) == 0)
    def _():
        hn_ref[...] = _rms(x_ref[...], g_ref[...]).astype(BF16)

    acc = jnp.dot(hn_ref[...], w_ref[...], preferred_element_type=F32)
    o_ref[...] = (acc + b_ref[...]).astype(o_ref.dtype)


def norm_matmul(x, gain, w, bias, out_dtype, tm, tn):
    m, k = x.shape
    n = w.shape[1]
    assert m % tm == 0 and n % tn == 0
    return pl.pallas_call(
        _norm_matmul_kernel,
        grid=(m // tm, n // tn),
        in_specs=[
            pl.BlockSpec((tm, k), lambda i, j: (i, 0)),
            pl.BlockSpec((1, k), lambda i, j: (0, 0)),
            pl.BlockSpec((k, tn), lambda i, j: (0, j)),
            pl.BlockSpec((1, tn), lambda i, j: (0, j)),
        ],
        out_specs=pl.BlockSpec((tm, tn), lambda i, j: (i, j)),
        out_shape=jax.ShapeDtypeStruct((m, n), out_dtype),
        scratch_shapes=[pltpu.VMEM((tm, k), BF16)],
        compiler_params=_cparams("parallel", "arbitrary"),
        name="norm_matmul",
    )(x, gain.reshape(1, k), w, bias.reshape(1, n))


def _proj_kernel(x_ref, g_ref, *refs, n_blk, dil):
    w_refs, b_refs, o_ref = refs[:n_blk], refs[n_blk:2 * n_blk], refs[2 * n_blk]
    tm = x_ref.shape[0]
    hn = _rms(x_ref[...], g_ref[...]).astype(BF16)
    for n in range(n_blk):
        acc = jnp.dot(hn, w_refs[n][...], preferred_element_type=F32) + b_refs[n][...]
        if dil == 1:
            o_ref[:, n * BRANCH_W:(n + 1) * BRANCH_W] = acc.astype(o_ref.dtype)
            continue
        scr_ref = refs[2 * n_blk + 1]
        for c in range(N_HEADS):
            scr_ref[n * N_HEADS + c] = acc[:, c * HEAD_DIM:(c + 1) * HEAD_DIM]
        for r in range(dil):
            for c in range(N_HEADS):
                lo = n * BRANCH_W + c * HEAD_DIM
                o_ref[r, :, lo:lo + HEAD_DIM] = (
                    scr_ref[n * N_HEADS + c, pl.ds(r, tm // dil, stride=dil), :].astype(o_ref.dtype))


def proj(x, gain, w, bias, blocks, batch, seq, dil, out_dtype, tm):
    m, k = x.shape
    n_blk = len(blocks)
    cols = n_blk * BRANCH_W
    assert m == batch * seq and seq % tm == 0 and tm % dil == 0
    tiles_per_b = seq // tm
    w_specs = [pl.BlockSpec((k, BRANCH_W), lambda i, c=c: (0, c)) for c in blocks]
    b_specs = [pl.BlockSpec((1, BRANCH_W), lambda i, c=c: (0, c)) for c in blocks]
    if dil == 1:
        out_spec = pl.BlockSpec((tm, cols), lambda i: (i, 0))
        out_shape = jax.ShapeDtypeStruct((m, cols), out_dtype)
        scratch = []
    else:
        out_spec = pl.BlockSpec((None, dil, tm // dil, cols), lambda i: (i // tiles_per_b, 0, i % tiles_per_b, 0))
        out_shape = jax.ShapeDtypeStruct((batch, dil, seq // dil, cols), out_dtype)
        scratch = [pltpu.VMEM((n_blk * N_HEADS, tm, HEAD_DIM), F32)]
    return pl.pallas_call(
        functools.partial(_proj_kernel, n_blk=n_blk, dil=dil),
        grid=(m // tm,),
        in_specs=[pl.BlockSpec((tm, k), lambda i: (i, 0)), pl.BlockSpec((1, k), lambda i: (0, 0))] + w_specs + b_specs,
        out_specs=out_spec,
        out_shape=out_shape,
        scratch_shapes=scratch,
        compiler_params=_cparams("parallel"),
        name="proj",
    )(x, gain.reshape(1, k), *([w] * n_blk), *([bias] * n_blk))


def _dil_prompt_kernel(q_ref, kp_ref, kc_ref, vp_ref, vc_ref, bp_ref, bc_ref, o_ref, lse_ref, *scratch, dil, qb):
    first = pl.program_id(1) == 0
    scale = 1.0 / math.sqrt(HEAD_DIM)

    def band_block(r, j):
        rows = slice(j * BAND, (j + 1) * BAND)
        before = slice((j - 1) * BAND, j * BAND)
        outs, lses = [], []
        for h in range(N_HEADS):
            hs = slice(h * HEAD_DIM, (h + 1) * HEAD_DIM)
            q = q_ref[r, rows, hs]
            k_prev = kp_ref[r, :, hs] if j == 0 else kc_ref[r, before, hs]
            v_prev = vp_ref[r, :, hs] if j == 0 else vc_ref[r, before, hs]
            lp = lax.dot_general(q, k_prev, NT_DIMS, preferred_element_type=F32) * scale + bp_ref[h]
            if j == 0:
                lp = jnp.where(first, NEG_INF, lp)
            lc = lax.dot_general(q, kc_ref[r, rows, hs], NT_DIMS, preferred_element_type=F32) * scale + bc_ref[h]
            m = jnp.maximum(jnp.max(lp, axis=-1, keepdims=True), jnp.max(lc, axis=-1, keepdims=True))
            pp = jnp.exp(lp - m)
            pc = jnp.exp(lc - m)
            s = jnp.sum(pp, axis=-1, keepdims=True) + jnp.sum(pc, axis=-1, keepdims=True)
            o = (jnp.dot(pp.astype(BF16), v_prev, preferred_element_type=F32)
                 + jnp.dot(pc.astype(BF16), vc_ref[r, rows, hs], preferred_element_type=F32))
            outs.append(o / s)
            lses.append(jnp.broadcast_to(m + jnp.log(s), (BAND, LSE_LANES)))
        return outs, jnp.concatenate(lses, axis=1)

    if dil == 1:
        for j in range(qb):
            outs, lse = band_block(0, j)
            for h in range(N_HEADS):
                o_ref[j * BAND:(j + 1) * BAND, h * HEAD_DIM:(h + 1) * HEAD_DIM] = outs[h].astype(o_ref.dtype)
            lse_ref[j * BAND:(j + 1) * BAND, :] = lse
        return

    o_scr, lse_scr = scratch

    def body(r, carry):
        for j in range(qb):
            outs, lse = band_block(r, j)
            for h in range(N_HEADS):
                o_scr[h, pl.ds(r + j * dil * BAND, BAND, stride=dil), :] = outs[h]
            lse_scr[pl.ds(r + j * dil * BAND, BAND, stride=dil), :] = lse
        return carry

    lax.fori_loop(0, dil, body, 0, unroll=min(dil, 4))
    for h in range(N_HEADS):
        o_ref[:, h * HEAD_DIM:(h + 1) * HEAD_DIM] = o_scr[h].astype(o_ref.dtype)
    lse_ref[...] = lse_scr[...]


def dilated_prompt(qkv, bias_prev, bias_cur, g, dil, batch, seq, qb):
    assert seq % (dil * BAND * qb) == 0
    sub = seq // dil
    span = dil * BAND * qb
    cur = lambda part: pl.BlockSpec((None, dil, qb * BAND, BRANCH_W), lambda b, i: (b, 0, i, part))
    prev = lambda part: pl.BlockSpec((None, dil, BAND, BRANCH_W),
                                     lambda b, i: (b, 0, jnp.maximum(i * qb - 1, 0), part))
    bias_spec = pl.BlockSpec((N_HEADS, BAND, BAND), lambda b, i: (0, 0, 0))
    scratch = [] if dil == 1 else [pltpu.VMEM((N_HEADS, span, HEAD_DIM), F32),
                                   pltpu.VMEM((span, N_HEADS * LSE_LANES), F32)]
    o, lse = pl.pallas_call(
        functools.partial(_dil_prompt_kernel, dil=dil, qb=qb),
        grid=(batch, sub // (BAND * qb)),
        in_specs=[cur(0), prev(1), cur(1), prev(2), cur(2), bias_spec, bias_spec],
        out_specs=[pl.BlockSpec((None, span, BRANCH_W), lambda b, i: (b, i, 0)),
                   pl.BlockSpec((None, span, N_HEADS * LSE_LANES), lambda b, i: (b, i, 0))],
        out_shape=[jax.ShapeDtypeStruct((batch, seq, BRANCH_W), BF16),
                   jax.ShapeDtypeStruct((batch, seq, N_HEADS * LSE_LANES), F32)],
        scratch_shapes=scratch,
        compiler_params=_cparams("parallel", "arbitrary"),
        name=f"dilated_prompt_g{g}",
    )(qkv, qkv, qkv, qkv, qkv, bias_prev, bias_cur)
    return o.reshape(batch * seq, BRANCH_W), lse.reshape(batch * seq, N_HEADS * LSE_LANES)


def _dil_sample_kernel(q_ref, kn_ref, vn_ref, cache_ref, bc_ref, bn_ref, o_ref, lse_ref, newc_ref, *, win, t_new):
    scale = 1.0 / math.sqrt(HEAD_DIM)
    keep = (win - t_new) * KV_ROWS
    newc_ref[0:keep, :] = cache_ref[t_new * KV_ROWS:win * KV_ROWS, :]
    for h in range(N_HEADS):
        hs = slice(h * HEAD_DIM, (h + 1) * HEAD_DIM)
        q = q_ref[:, hs]
        kn = kn_ref[:, hs]
        vn = vn_ref[:, hs]
        kc = cache_ref[pl.ds(h, win, stride=KV_ROWS), :].astype(BF16)
        vc = cache_ref[pl.ds(N_HEADS + h, win, stride=KV_ROWS), :].astype(BF16)
        lc = lax.dot_general(q.astype(BF16), kc, NT_DIMS, preferred_element_type=F32) * scale + bc_ref[h]
        ln = lax.dot_general(q, kn, NT_DIMS, preferred_element_type=F32) * scale + bn_ref[h]
        m = jnp.maximum(jnp.max(lc, axis=-1, keepdims=True), jnp.max(ln, axis=-1, keepdims=True))
        pc = jnp.exp(lc - m)
        pn = jnp.exp(ln - m)
        s = jnp.sum(pc, axis=-1, keepdims=True) + jnp.sum(pn, axis=-1, keepdims=True)
        o = jnp.dot(pc.astype(BF16), vc, preferred_element_type=F32) + jnp.dot(pn, vn, preferred_element_type=F32)
        o_ref[:, hs] = o / s
        lse_ref[:, h * LSE_LANES:(h + 1) * LSE_LANES] = jnp.broadcast_to(m + jnp.log(s), (t_new, LSE_LANES))
        newc_ref[pl.ds(keep + h, t_new, stride=KV_ROWS), :] = kn
        newc_ref[pl.ds(keep + N_HEADS + h, t_new, stride=KV_ROWS), :] = vn


def dilated_sample(qkv, cache, bias_cache, bias_new, g, batch, t_new):
    win = cache.shape[1] // KV_ROWS
    blk = (None, t_new, BRANCH_W)
    cache_spec = pl.BlockSpec((None, win * KV_ROWS, HEAD_DIM), lambda b: (b, 0, 0))
    o, lse, newc = pl.pallas_call(
        functools.partial(_dil_sample_kernel, win=win, t_new=t_new),
        grid=(batch,),
        in_specs=[pl.BlockSpec(blk, lambda b: (b, 0, 0)),
                  pl.BlockSpec(blk, lambda b: (b, 0, 1)),
                  pl.BlockSpec(blk, lambda b: (b, 0, 2)),
                  cache_spec,
                  pl.BlockSpec((N_HEADS, t_new, win), lambda b: (0, 0, 0)),
                  pl.BlockSpec((N_HEADS, t_new, t_new), lambda b: (0, 0, 0))],
        out_specs=[pl.BlockSpec(blk, lambda b: (b, 0, 0)),
                   pl.BlockSpec((None, t_new, N_HEADS * LSE_LANES), lambda b: (b, 0, 0)),
                   cache_spec],
        out_shape=[jax.ShapeDtypeStruct((batch, t_new, BRANCH_W), F32),
                   jax.ShapeDtypeStruct((batch, t_new, N_HEADS * LSE_LANES), F32),
                   jax.ShapeDtypeStruct(cache.shape, F32)],
        compiler_params=_cparams("parallel"),
        name=f"dilated_sample_g{g}",
    )(qkv, qkv, qkv, cache, bias_cache, bias_new)
    return o.reshape(batch * t_new, BRANCH_W), lse.reshape(batch * t_new, N_HEADS * LSE_LANES), newc


def _mem_attn_kernel(q_ref, kv_ref, o_ref, *, mem):
    scale = 1.0 / math.sqrt(HEAD_DIM)
    for h in range(N_HEADS):
        hs = slice(h * HEAD_DIM, (h + 1) * HEAD_DIM)
        k = kv_ref[pl.ds(h, mem, stride=KV_ROWS), :].astype(BF16)
        v = kv_ref[pl.ds(N_HEADS + h, mem, stride=KV_ROWS), :].astype(BF16)
        logits = lax.dot_general(q_ref[:, hs].astype(BF16), k, NT_DIMS, preferred_element_type=F32) * scale
        m = jnp.max(logits, axis=-1, keepdims=True)
        p = jnp.exp(logits - m)
        s = jnp.sum(p, axis=-1, keepdims=True)
        o = jnp.dot(p.astype(BF16), v, preferred_element_type=F32)
        o_ref[:, hs] = (o / s).astype(o_ref.dtype)


def memory_attention(rec, mem_kv, batch, t, tq, out_dtype):
    mem = mem_kv.shape[1] // KV_ROWS
    x = rec.reshape(batch, t, REC_COLS)
    o = pl.pallas_call(
        functools.partial(_mem_attn_kernel, mem=mem),
        grid=(batch, t // tq),
        in_specs=[pl.BlockSpec((None, tq, BRANCH_W), lambda b, i: (b, i, REC_MQ)),
                  pl.BlockSpec((None, mem * KV_ROWS, HEAD_DIM), lambda b, i: (b, 0, 0))],
        out_specs=pl.BlockSpec((None, tq, BRANCH_W), lambda b, i: (b, i, 0)),
        out_shape=jax.ShapeDtypeStruct((batch, t, BRANCH_W), out_dtype),
        compiler_params=_cparams("parallel", "parallel"),
        name="memory_attention",
    )(x, mem_kv)
    return o.reshape(batch * t, BRANCH_W)


def _hgrn_tables(chunk):
    c, b = chunk, HG_BASE
    t = np.arange(c)[:, None]
    u = np.arange(c)[None, :]
    masks = []
    size = 2 * b
    while size <= c:
        half = size // 2
        masks.append(((t // size) == (u // size)) & ((t % size) >= half) & ((u % size) < half))
        size *= 2
    masks.append(((t // b) == (u // b)) & (u <= t))
    erep = np.zeros((b * HEAD_DIM, c), np.float32)
    for s in range(b):
        erep[s * HEAD_DIM:(s + 1) * HEAD_DIM, np.arange(c) % b == s] = 1.0
    return (u <= t).astype(np.float32), erep, np.stack(masks).astype(np.float32)


def _hgrn_kernel(q_ref, v_ref, f_ref, gate_ref, lbl_ref, gn_ref, s0_ref, tri_ref, erep_ref, mask_ref,
                 o_ref, s_out_ref, st_ref, *, chunk, rows_in, n_sub, n_levels):
    c = chunk
    step = pl.program_id(1)

    @pl.when(step == 0)
    def _():
        for h in range(N_HEADS):
            st_ref[h] = s0_ref[h].T

    lbl = lbl_ref[...]
    e = jnp.exp(lbl - jnp.max(lbl, axis=0, keepdims=True))
    lb = e[0:1] / jnp.sum(e, axis=0, keepdims=True)
    tri = tri_ref[...]

    def rows(ref, ci):
        x = ref[ci * rows_in:(ci + 1) * rows_in, :].astype(F32)
        if rows_in < c:
            x = jnp.concatenate([x, jnp.zeros((c - rows_in, x.shape[1]), F32)], axis=0)
        return x

    for ci in range(n_sub):
        q = rows(q_ref, ci)
        v = rows(v_ref, ci).astype(BF16)
        f = lb + (1.0 - lb) * _sigmoid(rows(f_ref, ci))
        g = jnp.log(f)
        kk = 1.0 - f
        if rows_in < c:
            live = lax.broadcasted_iota(jnp.int32, (c, 1), 0) < rows_in
            g = jnp.where(live, g, 0.0)
            kk = jnp.where(live, kk, 0.0)

        g1 = g.astype(BF16)
        r1 = g - g1.astype(F32)
        g2 = r1.astype(BF16)
        g3 = (r1 - g2.astype(F32)).astype(BF16)
        cum = (jnp.dot(tri, g1, preferred_element_type=F32) + jnp.dot(tri, g2, preferred_element_type=F32)
               + jnp.dot(tri, g3, preferred_element_type=F32))

        def at_row(x, size, idx):
            x3 = x.reshape(c // size, size, x.shape[1])
            return jnp.broadcast_to(x3[:, idx:idx + 1, :], x3.shape).reshape(x.shape)

        q_inter = (q * jnp.exp(cum)).astype(BF16)
        g_end = cum[c - 1:c]
        k_end = (kk * jnp.exp(g_end - cum)).astype(BF16)
        q_lvl, k_lvl = [], []
        for l in range(n_levels):
            size = 2 * HG_BASE << l
            ref = at_row(cum, size, size // 2 - 1)
            q_lvl.append((q * jnp.exp(jnp.minimum(cum - ref, 0.0))).astype(BF16))
            k_lvl.append((kk * jnp.exp(jnp.minimum(ref - cum, 0.0))).astype(BF16))
        x_diag = []
        for s in range(HG_BASE):
            decay = jnp.exp(jnp.minimum(cum - at_row(cum, HG_BASE, s), 0.0))
            x_diag.append((q * at_row(kk, HG_BASE, s) * decay).astype(BF16))
        gate = _sigmoid(rows(gate_ref, ci))

        for h in range(N_HEADS):
            hs = slice(h * HEAD_DIM, (h + 1) * HEAD_DIM)
            scores = mask_ref[n_levels] * jnp.dot(jnp.concatenate([x[:, hs] for x in x_diag], axis=1),
                                                  erep_ref[...], preferred_element_type=F32)
            for l in range(n_levels):
                scores += mask_ref[l] * lax.dot_general(q_lvl[l][:, hs], k_lvl[l][:, hs], NT_DIMS,
                                                        preferred_element_type=F32)
            st = st_ref[h]
            o = (lax.dot_general(q_inter[:, hs], st.astype(BF16), NT_DIMS, preferred_element_type=F32)
                 + jnp.dot(scores.astype(BF16), v[:, hs], preferred_element_type=F32))
            st_ref[h] = (st * jnp.exp(g_end[:, hs])
                         + lax.dot_general(v[:, hs], k_end[:, hs], TN_DIMS, preferred_element_type=F32))
            y = _rms(o, gn_ref[...]) * gate[:, hs]
            o_ref[ci * rows_in:(ci + 1) * rows_in, hs] = y[0:rows_in].astype(o_ref.dtype)

    @pl.when(step == pl.num_programs(1) - 1)
    def _():
        for h in range(N_HEADS):
            s_out_ref[h] = st_ref[h].T


def hgrn2(rec, gates, lb_logits, hg_norm, state0, batch, t, chunk, n_sub, out_dtype):
    rows_in = min(chunk, t)
    step_rows = rows_in * n_sub
    assert t % step_rows == 0 and chunk % HG_BASE == 0 and (n_sub == 1 or rows_in == chunk)
    tri, erep, masks = _hgrn_tables(chunk)
    n_levels = masks.shape[0] - 1
    a = rec.reshape(batch, t, REC_COLS)
    gt = gates.reshape(batch, t, G_COLS)
    blk = (None, step_rows, BRANCH_W)
    full = lambda arr: pl.BlockSpec(arr.shape, lambda b, i: (0,) * arr.ndim)
    st_spec = pl.BlockSpec((None, N_HEADS, HEAD_DIM, HEAD_DIM), lambda b, i: (b, 0, 0, 0))
    tri = jnp.asarray(tri, BF16)
    erep = jnp.asarray(erep, BF16)
    masks = jnp.asarray(masks, F32)
    lbl = lb_logits.astype(F32)
    gn = hg_norm.reshape(1, HEAD_DIM).astype(F32)
    o, s_fin = pl.pallas_call(
        functools.partial(_hgrn_kernel, chunk=chunk, rows_in=rows_in, n_sub=n_sub, n_levels=n_levels),
        grid=(batch, t // step_rows),
        in_specs=[pl.BlockSpec(blk, lambda b, i: (b, i, REC_HQ)),
                  pl.BlockSpec(blk, lambda b, i: (b, i, REC_HI)),
                  pl.BlockSpec(blk, lambda b, i: (b, i, 0)),
                  pl.BlockSpec(blk, lambda b, i: (b, i, 1)),
                  full(lbl), full(gn), st_spec, full(tri), full(erep), full(masks)],
        out_specs=[pl.BlockSpec(blk, lambda b, i: (b, i, 0)), st_spec],
        out_shape=[jax.ShapeDtypeStruct((batch, t, BRANCH_W), out_dtype),
                   jax.ShapeDtypeStruct(state0.shape, F32)],
        scratch_shapes=[pltpu.VMEM((N_HEADS, HEAD_DIM, HEAD_DIM), F32)],
        compiler_params=_cparams("parallel", "arbitrary"),
        name="hgrn2",
    )(a, a, gt, gt, lbl, gn, state0, tri, erep, masks)
    return o.reshape(batch * t, BRANCH_W), s_fin


def _merge_kernel(x_ref, o1_ref, o2_ref, o3_ref, l1_ref, l2_ref, l3_ref, hg_ref, mem_ref, gpre_ref,
                  wga_ref, wgh_ref, wgm_ref, bga_ref, bgh_ref, bgm_ref,
                  wa_ref, wh_ref, wm_ref, wo_ref, gain_ref, out_ref):
    tm = x_ref.shape[0]
    x = x_ref[...]
    hn = _rms(x, gpre_ref[...]).astype(BF16)
    gate = lambda w_ref, b_ref: _sigmoid(jnp.dot(hn, w_ref[...], preferred_element_type=F32) + b_ref[...])
    l1, l2, l3 = l1_ref[...], l2_ref[...], l3_ref[...]
    m = jnp.maximum(jnp.maximum(l1, l2), l3)
    e1, e2, e3 = jnp.exp(l1 - m), jnp.exp(l2 - m), jnp.exp(l3 - m)
    den = e1 + e2 + e3
    w1, w2, w3 = e1 / den, e2 / den, e3 / den
    att = []
    for h in range(N_HEADS):
        hs = slice(h * HEAD_DIM, (h + 1) * HEAD_DIM)
        col = slice(h * LSE_LANES, h * LSE_LANES + 1)
        bc = lambda w: jnp.broadcast_to(w[:, col], (tm, HEAD_DIM))
        att.append((bc(w1) * o1_ref[:, hs].astype(F32) + bc(w2) * o2_ref[:, hs].astype(F32)
                    + bc(w3) * o3_ref[:, hs].astype(F32)).astype(BF16))
    att = jnp.concatenate(att, axis=1)
    merged = (gate(wga_ref, bga_ref) * jnp.dot(att, wa_ref[...], preferred_element_type=F32)
              + gate(wgh_ref, bgh_ref) * jnp.dot(hg_ref[...].astype(BF16), wh_ref[...], preferred_element_type=F32)
              + gate(wgm_ref, bgm_ref) * jnp.dot(mem_ref[...].astype(BF16), wm_ref[...],
                                                 preferred_element_type=F32))
    y = jnp.dot(merged.astype(BF16), wo_ref[...], preferred_element_type=F32)
    out_ref[...] = x + _rms(y, gain_ref[...])


def merge(x, o_groups, lse_groups, hg_o, mem_o, gain_pre, w_in, b_in, w_att, w_hg, w_mem, w_out, gain, tm):
    m, d = x.shape
    assert m % tm == 0 and GATE_MERGE_BLOCKS[0] * d == 14 * BRANCH_W
    row = lambda w: pl.BlockSpec((tm, w), lambda i: (i, 0))
    resident = lambda shape, j=0: pl.BlockSpec(shape, lambda i: (0, j), pipeline_mode=pl.Buffered(1))
    return pl.pallas_call(
        _merge_kernel,
        grid=(m // tm,),
        in_specs=[row(d), row(BRANCH_W), row(BRANCH_W), row(BRANCH_W),
                  row(N_HEADS * LSE_LANES), row(N_HEADS * LSE_LANES), row(N_HEADS * LSE_LANES),
                  row(BRANCH_W), row(BRANCH_W), resident((1, d))]
        + [resident((d, d), j) for j in GATE_MERGE_BLOCKS] + [resident((1, d), j) for j in GATE_MERGE_BLOCKS]
        + [resident(w_att.shape), resident(w_hg.shape), resident(w_mem.shape), resident(w_out.shape),
           resident((1, d))],
        out_specs=row(d),
        out_shape=jax.ShapeDtypeStruct((m, d), F32),
        compiler_params=_cparams("parallel"),
        name="merge",
    )(x, *o_groups, *lse_groups, hg_o, mem_o, gain_pre.reshape(1, d), w_in, w_in, w_in, b_in, b_in, b_in,
      w_att, w_hg, w_mem, w_out, gain.reshape(1, d))


def _ffn_kernel(x_ref, gpre_ref, wa_ref, wb_ref, cw_ref, cb_ref, wd_ref, gpost_ref, cbuf_ref,
                out_ref, tail_ref, carry_ref, *, t_seq, tiles_per_seq, tf):
    tm = x_ref.shape[0]
    dff = wa_ref.shape[1]
    x = x_ref[...]
    hn = _rms(x, gpre_ref[...]).astype(BF16)

    if tiles_per_seq >= 1:
        @pl.when((pl.program_id(0) % tiles_per_seq) == 0)
        def _():
            carry_ref[6:8, :] = cbuf_ref[0]

        t_idx = lax.broadcasted_iota(jnp.int32, (tm, 1), 0)
    else:
        n_seq = tm // t_seq
        t_idx = lax.broadcasted_iota(jnp.int32, (tm, 1), 0) % t_seq

    acc = jnp.zeros(x.shape, F32)
    for j in range(dff // tf):
        cols = slice(j * tf, (j + 1) * tf)
        a = jnp.dot(hn, wa_ref[:, cols], preferred_element_type=F32)
        up = jnp.dot(hn, wb_ref[:, cols], preferred_element_type=F32)
        if tiles_per_seq >= 1:
            prev1 = carry_ref[7:8, cols]
            prev2 = carry_ref[6:7, cols]
            carry_ref[:, cols] = a[tm - 8:tm]
            tail_ref[0, :, cols] = a[tm - 8:tm]
        else:
            prev1 = jnp.broadcast_to(cbuf_ref[:, 1:2, cols], (n_seq, t_seq, tf)).reshape(tm, tf)
            prev2 = jnp.broadcast_to(cbuf_ref[:, 0:1, cols], (n_seq, t_seq, tf)).reshape(tm, tf)
            tail_ref[:, :, cols] = a.reshape(n_seq, t_seq, tf)
        a1 = jnp.where(t_idx >= 1, pltpu.roll(a, 1, 0), prev1)
        a2 = jnp.where(t_idx >= 2, pltpu.roll(a, 2, 0), jnp.where(t_idx == 1, prev1, prev2))
        conv = cb_ref[:, cols] + a2 * cw_ref[0:1, cols] + a1 * cw_ref[1:2, cols] + a * cw_ref[2:3, cols]
        act = conv * _sigmoid(conv) * up
        acc += jnp.dot(act.astype(BF16), wd_ref[cols, :], preferred_element_type=F32)

    out_ref[...] = x + _rms(acc, gpost_ref[...])


def conv_ffn(x, conv_buf, g_pre, w_a, w_b, conv_w, conv_b, w_d, g_post, t_seq, tm, tf):
    m, d = x.shape
    dff = w_a.shape[1]
    n_seq = m // t_seq
    assert m % tm == 0 and dff % tf == 0 and t_seq >= 8
    if tm <= t_seq:
        assert t_seq % tm == 0
        tiles_per_seq = t_seq // tm
        seq_blk = 1
        seq_idx = lambda i: (i // tiles_per_seq, 0, 0)
    else:
        assert tm % t_seq == 0 and t_seq == 8
        tiles_per_seq = 0
        seq_blk = tm // t_seq
        seq_idx = lambda i: (i, 0, 0)
    n_tail = (m // tm) * seq_blk
    resident = lambda shape: pl.BlockSpec(shape, lambda i: (0, 0), pipeline_mode=pl.Buffered(1))
    y, tail = pl.pallas_call(
        functools.partial(_ffn_kernel, t_seq=t_seq, tiles_per_seq=tiles_per_seq, tf=tf),
        grid=(m // tm,),
        in_specs=[pl.BlockSpec((tm, d), lambda i: (i, 0)), resident((1, d)),
                  resident((d, dff)), resident((d, dff)), resident((3, dff)), resident((1, dff)),
                  resident((dff, d)), resident((1, d)),
                  pl.BlockSpec((seq_blk, 2, dff), seq_idx)],
        out_specs=[pl.BlockSpec((tm, d), lambda i: (i, 0)),
                   pl.BlockSpec((seq_blk, 8, dff), lambda i: (i, 0, 0))],
        out_shape=[jax.ShapeDtypeStruct((m, d), F32), jax.ShapeDtypeStruct((n_tail, 8, dff), F32)],
        scratch_shapes=[pltpu.VMEM((8, dff), F32)],
        compiler_params=_cparams("arbitrary"),
        name="conv_ffn",
    )(x, g_pre.reshape(1, d), w_a, w_b, conv_w, conv_b.reshape(1, dff), w_d, g_post.reshape(1, d), conv_buf)
    if tiles_per_seq >= 1:
        tail = tail.reshape(n_seq, tiles_per_seq, 8, dff)[:, -1]
    return y, tail


def _rel_bucket(dist):
    max_exact = REL_BUCKETS // 2
    d = jnp.maximum(dist, 1).astype(F32)
    large = max_exact + (jnp.log(d / max_exact) / math.log(REL_MAX_DIST / max_exact)
                         * (REL_BUCKETS - max_exact)).astype(jnp.int32)
    large = jnp.minimum(large, REL_BUCKETS - 1)
    return jnp.where(dist < max_exact, dist, large)


def _lag_bias(rel_bias, g, dil):
    dist = jnp.arange(N_LAGS + 1, dtype=jnp.int32) * dil
    onehot = _rel_bucket(dist)[:, None] == jnp.arange(REL_BUCKETS, dtype=jnp.int32)[None, :]
    table = jnp.sum(jnp.where(onehot[:, :, None], rel_bias.astype(F32)[None], 0.0), axis=1)
    return table[:, g * N_HEADS:(g + 1) * N_HEADS].T


def _neg(h, n):
    return jnp.full((h, n), NEG_INF, F32)


def _prompt_bias(lag_bias):
    h = lag_bias.shape[0]
    v = jnp.concatenate([lag_bias[:, ::-1], _neg(h, BAND)], axis=1)
    period = 2 * BAND + 1
    assert v.shape[1] == period
    both = jnp.tile(v, (1, BAND))[:, :BAND * 2 * BAND].reshape(h, BAND, 2 * BAND)
    return both[:, :, :BAND], both[:, :, BAND:]


def _sample_bias(lag_bias, dil, n_past, t_new):
    h = lag_bias.shape[0]
    n = n_past + t_new
    per_dist = jnp.repeat(lag_bias, dil, axis=1)
    per_dist = jnp.concatenate([per_dist, _neg(h, max(n - per_dist.shape[1], 0))], axis=1)[:, :n]
    per_dist = jnp.where((np.arange(n) % dil == 0)[None], per_dist, NEG_INF)
    rev = per_dist[:, ::-1]
    cached = jnp.stack([rev[:, t_new - 1 - t:t_new - 1 - t + n_past] for t in range(t_new)], axis=1)
    padded = jnp.concatenate([_neg(h, t_new - 1), per_dist[:, :t_new]], axis=1)
    new = jnp.stack([padded[:, t:t + t_new][:, ::-1] for t in range(t_new)], axis=1)
    return cached, new


def _layer(x, batch, t, weights, lag_biases, lb_logits, win_caches, hg_state0, conv_buf0, mem_kv,
           *, prompt, tm_proj, tm_merge, tm_ffn, tq_mem, chunk, n_sub):
    (w_in16, b_in, gain_pre, gain_post, hg_norm, w_att, w_hg, w_mem, w_out,
     gain_fpre, gain_fpost, w_fa, w_fb, conv_w, conv_b, w_fd) = weights
    act_dtype = BF16 if prompt else F32
    b_in = b_in.reshape(1, -1)
    if prompt:
        qkv = [proj(x, gain_pre, w_in16, b_in, QKV_BLOCKS(g), batch, t, dil, act_dtype, tm_proj)
               for g, (_, dil) in enumerate(ATT_GROUPS)]
        qkv[0] = qkv[0].reshape(batch, 1, t, QKV_COLS)
    else:
        qkv = [proj(x, gain_pre, w_in16, b_in, QKV_BLOCKS(g), 1, batch * t, 1, act_dtype, tm_proj)
               .reshape(batch, t, QKV_COLS) for g in range(len(ATT_GROUPS))]
    rec = proj(x, gain_pre, w_in16, b_in, REC_BLOCKS, 1, batch * t, 1, act_dtype, tm_proj)
    gates = proj(x, gain_pre, w_in16, b_in, GATE_BLOCKS, 1, batch * t, 1, F32, tm_proj)

    o_groups, lse_groups, new_caches = [], [], []
    for g, (win, dil) in enumerate(ATT_GROUPS):
        if prompt:
            bp, bc = _prompt_bias(lag_biases[g])
            o, lse = dilated_prompt(qkv[g], bp, bc, g, dil, batch, t, 4 if dil == 1 else 1)
        else:
            cache = win_caches[g]
            n_past = cache.shape[1] // KV_ROWS
            assert n_past >= win
            bcache, bnew = _sample_bias(lag_biases[g], dil, n_past, t)
            o, lse, newc = dilated_sample(qkv[g], cache, bcache, bnew, g, batch, t)
            new_caches.append(newc)
        o_groups.append(o)
        lse_groups.append(lse)

    hg_o, hg_state = hgrn2(rec, gates, lb_logits, hg_norm, hg_state0, batch, t, chunk, n_sub, act_dtype)
    mem_o = memory_attention(rec, mem_kv, batch, t, tq_mem, act_dtype)
    x1 = merge(x, o_groups, lse_groups, hg_o, mem_o, gain_pre, w_in16, b_in, w_att, w_hg, w_mem, w_out, gain_post,
               tm_merge)
    y, tail = conv_ffn(x1, conv_buf0, gain_fpre, w_fa, w_fb, conv_w, conv_b, w_fd, gain_fpost, t, tm_ffn, 512)
    return y, qkv, new_caches, hg_state, tail[:, 6:8, :]


def kernel(x_prompt, x_sample, mem_prompt, cache_win1_kv, cache_win2_kv, cache_win3_kv, cache_mem_kv, state_hgrn, state_ffn_conv, rel_bias, hg_lb_logits, norm_mix_pre, norm_mix_post, w_in, b_in, hg_norm, mem_norm, w_mem_kv, w_br_att, w_br_hg, w_br_mem, w_out, norm_ffn_pre, norm_ffn_post, w_ffn_a, w_ffn_b, ffn_conv_w, ffn_conv_b, w_ffn_d):
    depth = w_in.shape[0]
    assert depth == 1
    bsz, seq, d = x_prompt.shape
    dbsz, dseq, _ = x_sample.shape
    mem_tokens = mem_prompt.shape[1]
    dff = w_ffn_a.shape[2]
    layer = 0

    assert w_in.shape[2] == W_IN_BLOCKS * BRANCH_W
    weights = (w_in[layer].astype(BF16), b_in[layer],
               norm_mix_pre[layer], norm_mix_post[layer], hg_norm[layer],
               w_br_att[layer].astype(BF16), w_br_hg[layer].astype(BF16), w_br_mem[layer].astype(BF16),
               w_out[layer].astype(BF16), norm_ffn_pre[layer], norm_ffn_post[layer],
               w_ffn_a[layer].astype(BF16), w_ffn_b[layer].astype(BF16), ffn_conv_w[layer], ffn_conv_b[layer],
               w_ffn_d[layer].astype(BF16))
    lag_biases = [_lag_bias(rel_bias, g, dil) for g, (_, dil) in enumerate(ATT_GROUPS)]

    mem_kv = norm_matmul(mem_prompt.reshape(bsz * mem_tokens, d), mem_norm[layer], w_mem_kv[layer].astype(BF16),
                         jnp.zeros((2 * BRANCH_W,), F32), F32, bsz * mem_tokens, 1024)
    yp, qkv_p, _, hg_p, conv_p = _layer(
        x_prompt.reshape(bsz * seq, d), bsz, seq, weights, lag_biases, hg_lb_logits, None,
        jnp.zeros((bsz, N_HEADS, HEAD_DIM, HEAD_DIM), F32), jnp.zeros((bsz, 2, dff), F32),
        mem_kv.reshape(bsz, mem_tokens * KV_ROWS, HEAD_DIM),
        prompt=True, tm_proj=1024, tm_merge=512, tm_ffn=512, tq_mem=512, chunk=64, n_sub=4)
    p_win = []
    for g, (win, dil) in enumerate(ATT_GROUPS):
        n = min(win, seq)
        assert n % dil == 0
        tail = qkv_p[g][:, :, (seq - n) // dil:, BRANCH_W:]
        tail = jnp.swapaxes(tail, 1, 2).astype(F32)
        p_win.append(tail.reshape(1, bsz, n, 2, N_HEADS, HEAD_DIM))

    as_rows = lambda c: c[layer].reshape(dbsz, c.shape[2] * KV_ROWS, HEAD_DIM)
    ys, _, new_caches, hg_s, conv_s = _layer(
        x_sample.reshape(dbsz * dseq, d), dbsz, dseq, weights, lag_biases, hg_lb_logits,
        [as_rows(c) for c in (cache_win1_kv, cache_win2_kv, cache_win3_kv)],
        state_hgrn[layer], state_ffn_conv[layer], as_rows(cache_mem_kv),
        prompt=False, tm_proj=dbsz * dseq, tm_merge=dbsz * dseq, tm_ffn=dbsz * dseq, tq_mem=dseq, chunk=16, n_sub=1)
    s_win = [c.reshape(1, dbsz, c.shape[1] // KV_ROWS, 2, N_HEADS, HEAD_DIM) for c in new_caches]

    return (yp.reshape(bsz, seq, d), ys.reshape(dbsz, dseq, d),
            p_win[0], p_win[1], p_win[2],
            hg_p[None], conv_p[None], mem_kv.reshape(1, bsz, mem_tokens, 2, N_HEADS, HEAD_DIM),
            s_win[0], s_win[1], s_win[2],
            hg_s[None], conv_s[None])
```

```python
import functools
import math

import numpy as np
import jax
import jax.numpy as jnp
from jax import lax
from jax.experimental import pallas as pl
from jax.experimental.pallas import tpu as pltpu

F32 = jnp.float32
BF16 = jnp.bfloat16

NORM_EPS = 1e-6
NEG_INF = -1e30
HEAD_DIM = 128
N_HEADS = 4
BRANCH_W = N_HEADS * HEAD_DIM
N_LAGS = 128
BAND = 128
ATT_GROUPS = ((128, 1), (512, 4), (2048, 16))
REL_BUCKETS = 32
REL_MAX_DIST = 2048
LSE_LANES = 32
HG_BASE = 8
VMEM_LIMIT = 56 * 1024 * 1024

NT_DIMS = (((1,), (1,)), ((), ()))
TN_DIMS = (((0,), (0,)), ((), ()))

W_IN_BLOCKS = 20
QKV_BLOCKS = lambda g: (g, 3 + g, 6 + g)
REC_BLOCKS = (9, 11, 13)
GATE_BLOCKS = (10, 12)
GATE_MERGE_BLOCKS = (7, 8, 9)
QKV_COLS = 3 * BRANCH_W
REC_COLS = 3 * BRANCH_W
REC_HQ, REC_HI, REC_MQ = 0, 1, 2
G_COLS = 2 * BRANCH_W
KV_ROWS = 2 * N_HEADS


def _cparams(*sem):
    return pltpu.CompilerParams(dimension_semantics=sem, vmem_limit_bytes=VMEM_LIMIT)


def _rms(x, gain):
    return x * lax.rsqrt(jnp.mean(x * x, axis=-1, keepdims=True) + NORM_EPS) * gain


def _sigmoid(x):
    return 1.0 / (1.0 + jnp.exp(-x))


def _norm_matmul_kernel(x_ref, g_ref, w_ref, b_ref, o_ref, hn_ref):
    @pl.when(pl.program_id(1) == 0)
    def _():
        hn_ref[...] = _rms(x_ref[...], g_ref[...]).astype(BF16)

    acc = jnp.dot(hn_ref[...], w_ref[...], preferred_element_type=F32)
    o_ref[...] = (acc + b_ref[...]).astype(o_ref.dtype)


def norm_matmul(x, gain, w, bias, out_dtype, tm, tn):
    m, k = x.shape
    n = w.shape[1]
    assert m % tm == 0 and n % tn == 0
    return pl.pallas_call(
        _norm_matmul_kernel,
        grid=(m // tm, n // tn),
        in_specs=[
            pl.BlockSpec((tm, k), lambda i, j: (i, 0)),
            pl.BlockSpec((1, k), lambda i, j: (0, 0)),
            pl.BlockSpec((k, tn), lambda i, j: (0, j)),
            pl.BlockSpec((1, tn), lambda i, j: (0, j)),
        ],
        out_specs=pl.BlockSpec((tm, tn), lambda i, j: (i, j)),
        out_shape=jax.ShapeDtypeStruct((m, n), out_dtype),
        scratch_shapes=[pltpu.VMEM((tm, k), BF16)],
        compiler_params=_cparams("parallel", "arbitrary"),
        name="norm_matmul",
    )(x, gain.reshape(1, k), w, bias.reshape(1, n))


def _proj_kernel(x_ref, g_ref, *refs, n_blk, dil):
    w_refs, b_refs, o_ref = refs[:n_blk], refs[n_blk:2 * n_blk], refs[2 * n_blk]
    tm = x_ref.shape[0]
    hn = _rms(x_ref[...], g_ref[...]).astype(BF16)
    for n in range(n_blk):
        acc = jnp.dot(hn, w_refs[n][...], preferred_element_type=F32) + b_refs[n][...]
        if dil == 1:
            o_ref[:, n * BRANCH_W:(n + 1) * BRANCH_W] = acc.astype(o_ref.dtype)
            continue
        scr_ref = refs[2 * n_blk + 1]
        for c in range(N_HEADS):
            scr_ref[n * N_HEADS + c] = acc[:, c * HEAD_DIM:(c + 1) * HEAD_DIM]
        for r in range(dil):
            for c in range(N_HEADS):
                lo = n * BRANCH_W + c * HEAD_DIM
                o_ref[r, :, lo:lo + HEAD_DIM] = (
                    scr_ref[n * N_HEADS + c, pl.ds(r, tm // dil, stride=dil), :].astype(o_ref.dtype))


def proj(x, gain, w, bias, blocks, batch, seq, dil, out_dtype, tm):
    m, k = x.shape
    n_blk = len(blocks)
    cols = n_blk * BRANCH_W
    assert m == batch * seq and seq % tm == 0 and tm % dil == 0
    tiles_per_b = seq // tm
    w_specs = [pl.BlockSpec((k, BRANCH_W), lambda i, c=c: (0, c)) for c in blocks]
    b_specs = [pl.BlockSpec((1, BRANCH_W), lambda i, c=c: (0, c)) for c in blocks]
    if dil == 1:
        out_spec = pl.BlockSpec((tm, cols), lambda i: (i, 0))
        out_shape = jax.ShapeDtypeStruct((m, cols), out_dtype)
        scratch = []
    else:
        out_spec = pl.BlockSpec((None, dil, tm // dil, cols), lambda i: (i // tiles_per_b, 0, i % tiles_per_b, 0))
        out_shape = jax.ShapeDtypeStruct((batch, dil, seq // dil, cols), out_dtype)
        scratch = [pltpu.VMEM((n_blk * N_HEADS, tm, HEAD_DIM), F32)]
    return pl.pallas_call(
        functools.partial(_proj_kernel, n_blk=n_blk, dil=dil),
        grid=(m // tm,),
        in_specs=[pl.BlockSpec((tm, k), lambda i: (i, 0)), pl.BlockSpec((1, k), lambda i: (0, 0))] + w_specs + b_specs,
        out_specs=out_spec,
        out_shape=out_shape,
        scratch_shapes=scratch,
        compiler_params=_cparams("parallel"),
        name="proj",
    )(x, gain.reshape(1, k), *([w] * n_blk), *([bias] * n_blk))


def _dil_prompt_kernel(q_ref, kp_ref, kc_ref, vp_ref, vc_ref, bias_ref, o_ref, lse_ref, *scratch, dil, qb):
    scale = 1.0 / math.sqrt(HEAD_DIM)
    no_prev = (pl.program_id(1) == 0) & (lax.broadcasted_iota(jnp.int32, (1, 2 * BAND), 1) < BAND)

    def band_block(r, j):
        rows = slice(j * BAND, (j + 1) * BAND)
        both = slice((j - 1) * BAND, (j + 1) * BAND)
        outs, lses = [], []
        for h in range(N_HEADS):
            hs = slice(h * HEAD_DIM, (h + 1) * HEAD_DIM)
            if j == 0:
                k = jnp.concatenate([kp_ref[r, :, hs], kc_ref[r, rows, hs]], axis=0)
                v = jnp.concatenate([vp_ref[r, :, hs], vc_ref[r, rows, hs]], axis=0)
            else:
                k = kc_ref[r, both, hs]
                v = vc_ref[r, both, hs]
            logits = lax.dot_general(q_ref[r, rows, hs], k, NT_DIMS, preferred_element_type=F32) * scale + bias_ref[h]
            if j == 0:
                logits = jnp.where(no_prev, NEG_INF, logits)
            m = jnp.max(jnp.maximum(logits[:, :BAND], logits[:, BAND:]), axis=-1, keepdims=True)
            p = jnp.exp(logits - m)
            s = jnp.sum(p[:, :BAND] + p[:, BAND:], axis=-1, keepdims=True)
            o = jnp.dot(p.astype(BF16), v, preferred_element_type=F32)
            outs.append(o / s)
            lses.append(jnp.broadcast_to(m + jnp.log(s), (BAND, LSE_LANES)))
        return outs, jnp.concatenate(lses, axis=1)

    if dil == 1:
        for j in range(qb):
            outs, lse = band_block(0, j)
            for h in range(N_HEADS):
                o_ref[j * BAND:(j + 1) * BAND, h * HEAD_DIM:(h + 1) * HEAD_DIM] = outs[h].astype(o_ref.dtype)
            lse_ref[j * BAND:(j + 1) * BAND, :] = lse
        return

    o_scr, lse_scr = scratch

    def body(r, carry):
        for j in range(qb):
            outs, lse = band_block(r, j)
            for h in range(N_HEADS):
                o_scr[h, pl.ds(r + j * dil * BAND, BAND, stride=dil), :] = outs[h]
            lse_scr[pl.ds(r + j * dil * BAND, BAND, stride=dil), :] = lse
        return carry

    lax.fori_loop(0, dil, body, 0, unroll=min(dil, 4))
    for h in range(N_HEADS):
        o_ref[:, h * HEAD_DIM:(h + 1) * HEAD_DIM] = o_scr[h].astype(o_ref.dtype)
    lse_ref[...] = lse_scr[...]


def dilated_prompt(qkv, bias, g, dil, batch, seq, qb):
    assert seq % (dil * BAND * qb) == 0
    sub = seq // dil
    span = dil * BAND * qb
    cur = lambda part: pl.BlockSpec((None, dil, qb * BAND, BRANCH_W), lambda b, i: (b, 0, i, part))
    prev = lambda part: pl.BlockSpec((None, dil, BAND, BRANCH_W),
                                     lambda b, i: (b, 0, jnp.maximum(i * qb - 1, 0), part))
    bias_spec = pl.BlockSpec((N_HEADS, BAND, 2 * BAND), lambda b, i: (0, 0, 0))
    scratch = [] if dil == 1 else [pltpu.VMEM((N_HEADS, span, HEAD_DIM), F32),
                                   pltpu.VMEM((span, N_HEADS * LSE_LANES), F32)]
    o, lse = pl.pallas_call(
        functools.partial(_dil_prompt_kernel, dil=dil, qb=qb),
        grid=(batch, sub // (BAND * qb)),
        in_specs=[cur(0), prev(1), cur(1), prev(2), cur(2), bias_spec],
        out_specs=[pl.BlockSpec((None, span, BRANCH_W), lambda b, i: (b, i, 0)),
                   pl.BlockSpec((None, span, N_HEADS * LSE_LANES), lambda b, i: (b, i, 0))],
        out_shape=[jax.ShapeDtypeStruct((batch, seq, BRANCH_W), BF16),
                   jax.ShapeDtypeStruct((batch, seq, N_HEADS * LSE_LANES), F32)],
        scratch_shapes=scratch,
        compiler_params=_cparams("parallel", "arbitrary"),
        name=f"dilated_prompt_g{g}",
    )(qkv, qkv, qkv, qkv, qkv, bias)
    return o.reshape(batch * seq, BRANCH_W), lse.reshape(batch * seq, N_HEADS * LSE_LANES)


def _dil_sample_kernel(q_ref, kn_ref, vn_ref, cache_ref, bc_ref, bn_ref, o_ref, lse_ref, newc_ref, *, win, t_new):
    scale = 1.0 / math.sqrt(HEAD_DIM)
    keep = (win - t_new) * KV_ROWS
    newc_ref[0:keep, :] = cache_ref[t_new * KV_ROWS:win * KV_ROWS, :]
    for h in range(N_HEADS):
        hs = slice(h * HEAD_DIM, (h + 1) * HEAD_DIM)
        q = q_ref[:, hs]
        kn = kn_ref[:, hs]
        vn = vn_ref[:, hs]
        kc = cache_ref[pl.ds(h, win, stride=KV_ROWS), :].astype(BF16)
        vc = cache_ref[pl.ds(N_HEADS + h, win, stride=KV_ROWS), :].astype(BF16)
        lc = lax.dot_general(q.astype(BF16), kc, NT_DIMS, preferred_element_type=F32) * scale + bc_ref[h]
        ln = lax.dot_general(q, kn, NT_DIMS, preferred_element_type=F32) * scale + bn_ref[h]
        m = jnp.maximum(jnp.max(lc, axis=-1, keepdims=True), jnp.max(ln, axis=-1, keepdims=True))
        pc = jnp.exp(lc - m)
        pn = jnp.exp(ln - m)
        s = jnp.sum(pc, axis=-1, keepdims=True) + jnp.sum(pn, axis=-1, keepdims=True)
        o = jnp.dot(pc.astype(BF16), vc, preferred_element_type=F32) + jnp.dot(pn, vn, preferred_element_type=F32)
        o_ref[:, hs] = o / s
        lse_ref[:, h * LSE_LANES:(h + 1) * LSE_LANES] = jnp.broadcast_to(m + jnp.log(s), (t_new, LSE_LANES))
        newc_ref[pl.ds(keep + h, t_new, stride=KV_ROWS), :] = kn
        newc_ref[pl.ds(keep + N_HEADS + h, t_new, stride=KV_ROWS), :] = vn


def dilated_sample(qkv, cache, bias_cache, bias_new, g, batch, t_new):
    win = cache.shape[1] // KV_ROWS
    blk = (None, t_new, BRANCH_W)
    cache_spec = pl.BlockSpec((None, win * KV_ROWS, HEAD_DIM), lambda b: (b, 0, 0))
    o, lse, newc = pl.pallas_call(
        functools.partial(_dil_sample_kernel, win=win, t_new=t_new),
        grid=(batch,),
        in_specs=[pl.BlockSpec(blk, lambda b: (b, 0, 0)),
                  pl.BlockSpec(blk, lambda b: (b, 0, 1)),
                  pl.BlockSpec(blk, lambda b: (b, 0, 2)),
                  cache_spec,
                  pl.BlockSpec((N_HEADS, t_new, win), lambda b: (0, 0, 0)),
                  pl.BlockSpec((N_HEADS, t_new, t_new), lambda b: (0, 0, 0))],
        out_specs=[pl.BlockSpec(blk, lambda b: (b, 0, 0)),
                   pl.BlockSpec((None, t_new, N_HEADS * LSE_LANES), lambda b: (b, 0, 0)),
                   cache_spec],
        out_shape=[jax.ShapeDtypeStruct((batch, t_new, BRANCH_W), F32),
                   jax.ShapeDtypeStruct((batch, t_new, N_HEADS * LSE_LANES), F32),
                   jax.ShapeDtypeStruct(cache.shape, F32)],
        compiler_params=_cparams("parallel"),
        name=f"dilated_sample_g{g}",
    )(qkv, qkv, qkv, cache, bias_cache, bias_new)
    return o.reshape(batch * t_new, BRANCH_W), lse.reshape(batch * t_new, N_HEADS * LSE_LANES), newc


def _mem_attn_kernel(q_ref, kv_ref, o_ref, *, mem):
    scale = 1.0 / math.sqrt(HEAD_DIM)
    for h in range(N_HEADS):
        hs = slice(h * HEAD_DIM, (h + 1) * HEAD_DIM)
        k = kv_ref[pl.ds(h, mem, stride=KV_ROWS), :].astype(BF16)
        v = kv_ref[pl.ds(N_HEADS + h, mem, stride=KV_ROWS), :].astype(BF16)
        logits = lax.dot_general(q_ref[:, hs].astype(BF16), k, NT_DIMS, preferred_element_type=F32) * scale
        m = jnp.max(logits, axis=-1, keepdims=True)
        p = jnp.exp(logits - m)
        s = jnp.sum(p, axis=-1, keepdims=True)
        o = jnp.dot(p.astype(BF16), v, preferred_element_type=F32)
        o_ref[:, hs] = (o / s).astype(o_ref.dtype)


def memory_attention(rec, mem_kv, batch, t, tq, out_dtype):
    mem = mem_kv.shape[1] // KV_ROWS
    x = rec.reshape(batch, t, REC_COLS)
    o = pl.pallas_call(
        functools.partial(_mem_attn_kernel, mem=mem),
        grid=(batch, t // tq),
        in_specs=[pl.BlockSpec((None, tq, BRANCH_W), lambda b, i: (b, i, REC_MQ)),
                  pl.BlockSpec((None, mem * KV_ROWS, HEAD_DIM), lambda b, i: (b, 0, 0))],
        out_specs=pl.BlockSpec((None, tq, BRANCH_W), lambda b, i: (b, i, 0)),
        out_shape=jax.ShapeDtypeStruct((batch, t, BRANCH_W), out_dtype),
        compiler_params=_cparams("parallel", "parallel"),
        name="memory_attention",
    )(x, mem_kv)
    return o.reshape(batch * t, BRANCH_W)


def _hgrn_tables(chunk):
    c, b = chunk, HG_BASE
    t = np.arange(c)[:, None]
    u = np.arange(c)[None, :]
    masks = []
    size = 2 * b
    while size <= c:
        half = size // 2
        masks.append(((t // size) == (u // size)) & ((t % size) >= half) & ((u % size) < half))
        size *= 2
    masks.append(((t // b) == (u // b)) & (u <= t))
    erep = np.zeros((b * HEAD_DIM, c), np.float32)
    for s in range(b):
        erep[s * HEAD_DIM:(s + 1) * HEAD_DIM, np.arange(c) % b == s] = 1.0
    return (u <= t).astype(np.float32), erep, np.stack(masks).astype(np.float32)


def _hgrn_kernel(q_ref, v_ref, f_ref, gate_ref, lbl_ref, gn_ref, s0_ref, tri_ref, erep_ref, mask_ref,
                 o_ref, s_out_ref, st_ref, *, chunk, rows_in, n_sub, n_levels):
    c = chunk
    step = pl.program_id(1)

    @pl.when(step == 0)
    def _():
        for h in range(N_HEADS):
            st_ref[h] = s0_ref[h].T

    lbl = lbl_ref[...]
    e = jnp.exp(lbl - jnp.max(lbl, axis=0, keepdims=True))
    lb = e[0:1] / jnp.sum(e, axis=0, keepdims=True)
    tri = tri_ref[...]

    def rows(ref, ci):
        x = ref[ci * rows_in:(ci + 1) * rows_in, :].astype(F32)
        if rows_in < c:
            x = jnp.concatenate([x, jnp.zeros((c - rows_in, x.shape[1]), F32)], axis=0)
        return x

    for ci in range(n_sub):
        q = rows(q_ref, ci)
        v = rows(v_ref, ci).astype(BF16)
        f = lb + (1.0 - lb) * _sigmoid(rows(f_ref, ci))
        g = jnp.log(f)
        kk = 1.0 - f
        if rows_in < c:
            live = lax.broadcasted_iota(jnp.int32, (c, 1), 0) < rows_in
            g = jnp.where(live, g, 0.0)
            kk = jnp.where(live, kk, 0.0)

        g1 = g.astype(BF16)
        r1 = g - g1.astype(F32)
        g2 = r1.astype(BF16)
        g3 = (r1 - g2.astype(F32)).astype(BF16)
        cum = (jnp.dot(tri, g1, preferred_element_type=F32) + jnp.dot(tri, g2, preferred_element_type=F32)
               + jnp.dot(tri, g3, preferred_element_type=F32))

        def at_row(x, size, idx):
            x3 = x.reshape(c // size, size, x.shape[1])
            return jnp.broadcast_to(x3[:, idx:idx + 1, :], x3.shape).reshape(x.shape)

        q_inter = (q * jnp.exp(cum)).astype(BF16)
        g_end = cum[c - 1:c]
        k_end = (kk * jnp.exp(g_end - cum)).astype(BF16)
        q_lvl, k_lvl = [], []
        for l in range(n_levels):
            size = 2 * HG_BASE << l
            ref = at_row(cum, size, size // 2 - 1)
            q_lvl.append((q * jnp.exp(jnp.minimum(cum - ref, 0.0))).astype(BF16))
            k_lvl.append((kk * jnp.exp(jnp.minimum(ref - cum, 0.0))).astype(BF16))
        x_diag = []
        for s in range(HG_BASE):
            decay = jnp.exp(jnp.minimum(cum - at_row(cum, HG_BASE, s), 0.0))
            x_diag.append((q * at_row(kk, HG_BASE, s) * decay).astype(BF16))
        gate = _sigmoid(rows(gate_ref, ci))

        for h in range(N_HEADS):
            hs = slice(h * HEAD_DIM, (h + 1) * HEAD_DIM)
            scores = mask_ref[n_levels] * jnp.dot(jnp.concatenate([x[:, hs] for x in x_diag], axis=1),
                                                  erep_ref[...], preferred_element_type=F32)
            for l in range(n_levels):
                scores += mask_ref[l] * lax.dot_general(q_lvl[l][:, hs], k_lvl[l][:, hs], NT_DIMS,
                                                        preferred_element_type=F32)
            st = st_ref[h]
            o = (lax.dot_general(q_inter[:, hs], st.astype(BF16), NT_DIMS, preferred_element_type=F32)
                 + jnp.dot(scores.astype(BF16), v[:, hs], preferred_element_type=F32))
            st_ref[h] = (st * jnp.exp(g_end[:, hs])
                         + lax.dot_general(v[:, hs], k_end[:, hs], TN_DIMS, preferred_element_type=F32))
            y = _rms(o, gn_ref[...]) * gate[:, hs]
            o_ref[ci * rows_in:(ci + 1) * rows_in, hs] = y[0:rows_in].astype(o_ref.dtype)

    @pl.when(step == pl.num_programs(1) - 1)
    def _():
        for h in range(N_HEADS):
            s_out_ref[h] = st_ref[h].T


def hgrn2(rec, gates, lb_logits, hg_norm, state0, batch, t, chunk, n_sub, out_dtype):
    rows_in = min(chunk, t)
    step_rows = rows_in * n_sub
    assert t % step_rows == 0 and chunk % HG_BASE == 0 and (n_sub == 1 or rows_in == chunk)
    tri, erep, masks = _hgrn_tables(chunk)
    n_levels = masks.shape[0] - 1
    a = rec.reshape(batch, t, REC_COLS)
    gt = gates.reshape(batch, t, G_COLS)
    blk = (None, step_rows, BRANCH_W)
    full = lambda arr: pl.BlockSpec(arr.shape, lambda b, i: (0,) * arr.ndim)
    st_spec = pl.BlockSpec((None, N_HEADS, HEAD_DIM, HEAD_DIM), lambda b, i: (b, 0, 0, 0))
    tri = jnp.asarray(tri, BF16)
    erep = jnp.asarray(erep, BF16)
    masks = jnp.asarray(masks, F32)
    lbl = lb_logits.astype(F32)
    gn = hg_norm.reshape(1, HEAD_DIM).astype(F32)
    o, s_fin = pl.pallas_call(
        functools.partial(_hgrn_kernel, chunk=chunk, rows_in=rows_in, n_sub=n_sub, n_levels=n_levels),
        grid=(batch, t // step_rows),
        in_specs=[pl.BlockSpec(blk, lambda b, i: (b, i, REC_HQ)),
                  pl.BlockSpec(blk, lambda b, i: (b, i, REC_HI)),
                  pl.BlockSpec(blk, lambda b, i: (b, i, 0)),
                  pl.BlockSpec(blk, lambda b, i: (b, i, 1)),
                  full(lbl), full(gn), st_spec, full(tri), full(erep), full(masks)],
        out_specs=[pl.BlockSpec(blk, lambda b, i: (b, i, 0)), st_spec],
        out_shape=[jax.ShapeDtypeStruct((batch, t, BRANCH_W), out_dtype),
                   jax.ShapeDtypeStruct(state0.shape, F32)],
        scratch_shapes=[pltpu.VMEM((N_HEADS, HEAD_DIM, HEAD_DIM), F32)],
        compiler_params=_cparams("parallel", "arbitrary"),
        name="hgrn2",
    )(a, a, gt, gt, lbl, gn, state0, tri, erep, masks)
    return o.reshape(batch * t, BRANCH_W), s_fin


def _merge_kernel(x_ref, o1_ref, o2_ref, o3_ref, l1_ref, l2_ref, l3_ref, hg_ref, mem_ref, gpre_ref,
                  wga_ref, wgh_ref, wgm_ref, bga_ref, bgh_ref, bgm_ref,
                  wa_ref, wh_ref, wm_ref, wo_ref, gain_ref, out_ref):
    tm = x_ref.shape[0]
    x = x_ref[...]
    hn = _rms(x, gpre_ref[...]).astype(BF16)
    gate = lambda w_ref, b_ref: _sigmoid(jnp.dot(hn, w_ref[...], preferred_element_type=F32) + b_ref[...])
    l1, l2, l3 = l1_ref[...], l2_ref[...], l3_ref[...]
    m = jnp.maximum(jnp.maximum(l1, l2), l3)
    e1, e2, e3 = jnp.exp(l1 - m), jnp.exp(l2 - m), jnp.exp(l3 - m)
    den = e1 + e2 + e3
    w1, w2, w3 = e1 / den, e2 / den, e3 / den
    att = []
    for h in range(N_HEADS):
        hs = slice(h * HEAD_DIM, (h + 1) * HEAD_DIM)
        col = slice(h * LSE_LANES, h * LSE_LANES + 1)
        bc = lambda w: jnp.broadcast_to(w[:, col], (tm, HEAD_DIM))
        att.append((bc(w1) * o1_ref[:, hs].astype(F32) + bc(w2) * o2_ref[:, hs].astype(F32)
                    + bc(w3) * o3_ref[:, hs].astype(F32)).astype(BF16))
    att = jnp.concatenate(att, axis=1)
    merged = (gate(wga_ref, bga_ref) * jnp.dot(att, wa_ref[...], preferred_element_type=F32)
              + gate(wgh_ref, bgh_ref) * jnp.dot(hg_ref[...].astype(BF16), wh_ref[...], preferred_element_type=F32)
              + gate(wgm_ref, bgm_ref) * jnp.dot(mem_ref[...].astype(BF16), wm_ref[...],
                                                 preferred_element_type=F32))
    y = jnp.dot(merged.astype(BF16), wo_ref[...], preferred_element_type=F32)
    out_ref[...] = x + _rms(y, gain_ref[...])


def merge(x, o_groups, lse_groups, hg_o, mem_o, gain_pre, w_in, b_in, w_att, w_hg, w_mem, w_out, gain, tm):
    m, d = x.shape
    assert m % tm == 0 and GATE_MERGE_BLOCKS[0] * d == 14 * BRANCH_W
    row = lambda w: pl.BlockSpec((tm, w), lambda i: (i, 0))
    resident = lambda shape, j=0: pl.BlockSpec(shape, lambda i: (0, j), pipeline_mode=pl.Buffered(1))
    return pl.pallas_call(
        _merge_kernel,
        grid=(m // tm,),
        in_specs=[row(d), row(BRANCH_W), row(BRANCH_W), row(BRANCH_W),
                  row(N_HEADS * LSE_LANES), row(N_HEADS * LSE_LANES), row(N_HEADS * LSE_LANES),
                  row(BRANCH_W), row(BRANCH_W), resident((1, d))]
        + [resident((d, d), j) for j in GATE_MERGE_BLOCKS] + [resident((1, d), j) for j in GATE_MERGE_BLOCKS]
        + [resident(w_att.shape), resident(w_hg.shape), resident(w_mem.shape), resident(w_out.shape),
           resident((1, d))],
        out_specs=row(d),
        out_shape=jax.ShapeDtypeStruct((m, d), F32),
        compiler_params=_cparams("parallel"),
        name="merge",
    )(x, *o_groups, *lse_groups, hg_o, mem_o, gain_pre.reshape(1, d), w_in, w_in, w_in, b_in, b_in, b_in,
      w_att, w_hg, w_mem, w_out, gain.reshape(1, d))


def _ffn_kernel(x_ref, gpre_ref, wa_ref, wb_ref, cw_ref, cb_ref, wd_ref, gpost_ref, cbuf_ref,
                out_ref, tail_ref, carry_ref, *, t_seq, tiles_per_seq, tf):
    tm = x_ref.shape[0]
    dff = wa_ref.shape[1]
    x = x_ref[...]
    hn = _rms(x, gpre_ref[...]).astype(BF16)

    if tiles_per_seq >= 1:
        @pl.when((pl.program_id(0) % tiles_per_seq) == 0)
        def _():
            carry_ref[6:8, :] = cbuf_ref[0]

        t_idx = lax.broadcasted_iota(jnp.int32, (tm, 1), 0)
    else:
        n_seq = tm // t_seq
        t_idx = lax.broadcasted_iota(jnp.int32, (tm, 1), 0) % t_seq

    acc = jnp.zeros(x.shape, F32)
    for j in range(dff // tf):
        cols = slice(j * tf, (j + 1) * tf)
        a = jnp.dot(hn, wa_ref[:, cols], preferred_element_type=F32)
        up = jnp.dot(hn, wb_ref[:, cols], preferred_element_type=F32)
        if tiles_per_seq >= 1:
            prev1 = carry_ref[7:8, cols]
            prev2 = carry_ref[6:7, cols]
            carry_ref[:, cols] = a[tm - 8:tm]
            tail_ref[0, :, cols] = a[tm - 8:tm]
        else:
            prev1 = jnp.broadcast_to(cbuf_ref[:, 1:2, cols], (n_seq, t_seq, tf)).reshape(tm, tf)
            prev2 = jnp.broadcast_to(cbuf_ref[:, 0:1, cols], (n_seq, t_seq, tf)).reshape(tm, tf)
            tail_ref[:, :, cols] = a.reshape(n_seq, t_seq, tf)
        a1 = jnp.where(t_idx >= 1, pltpu.roll(a, 1, 0), prev1)
        a2 = jnp.where(t_idx >= 2, pltpu.roll(a, 2, 0), jnp.where(t_idx == 1, prev1, prev2))
        conv = cb_ref[:, cols] + a2 * cw_ref[0:1, cols] + a1 * cw_ref[1:2, cols] + a * cw_ref[2:3, cols]
        act = conv * _sigmoid(conv) * up
        acc += jnp.dot(act.astype(BF16), wd_ref[cols, :], preferred_element_type=F32)

    out_ref[...] = x + _rms(acc, gpost_ref[...])


def conv_ffn(x, conv_buf, g_pre, w_a, w_b, conv_w, conv_b, w_d, g_post, t_seq, tm, tf):
    m, d = x.shape
    dff = w_a.shape[1]
    n_seq = m // t_seq
    assert m % tm == 0 and dff % tf == 0 and t_seq >= 8
    if tm <= t_seq:
        assert t_seq % tm == 0
        tiles_per_seq = t_seq // tm
        seq_blk = 1
        seq_idx = lambda i: (i // tiles_per_seq, 0, 0)
    else:
        assert tm % t_seq == 0 and t_seq == 8
        tiles_per_seq = 0
        seq_blk = tm // t_seq
        seq_idx = lambda i: (i, 0, 0)
    n_tail = (m // tm) * seq_blk
    resident = lambda shape: pl.BlockSpec(shape, lambda i: (0, 0), pipeline_mode=pl.Buffered(1))
    y, tail = pl.pallas_call(
        functools.partial(_ffn_kernel, t_seq=t_seq, tiles_per_seq=tiles_per_seq, tf=tf),
        grid=(m // tm,),
        in_specs=[pl.BlockSpec((tm, d), lambda i: (i, 0)), resident((1, d)),
                  resident((d, dff)), resident((d, dff)), resident((3, dff)), resident((1, dff)),
                  resident((dff, d)), resident((1, d)),
                  pl.BlockSpec((seq_blk, 2, dff), seq_idx)],
        out_specs=[pl.BlockSpec((tm, d), lambda i: (i, 0)),
                   pl.BlockSpec((seq_blk, 8, dff), lambda i: (i, 0, 0))],
        out_shape=[jax.ShapeDtypeStruct((m, d), F32), jax.ShapeDtypeStruct((n_tail, 8, dff), F32)],
        scratch_shapes=[pltpu.VMEM((8, dff), F32)],
        compiler_params=_cparams("arbitrary"),
        name="conv_ffn",
    )(x, g_pre.reshape(1, d), w_a, w_b, conv_w, conv_b.reshape(1, dff), w_d, g_post.reshape(1, d), conv_buf)
    if tiles_per_seq >= 1:
        tail = tail.reshape(n_seq, tiles_per_seq, 8, dff)[:, -1]
    return y, tail


def _rel_bucket(dist):
    max_exact = REL_BUCKETS // 2
    d = jnp.maximum(dist, 1).astype(F32)
    large = max_exact + (jnp.log(d / max_exact) / math.log(REL_MAX_DIST / max_exact)
                         * (REL_BUCKETS - max_exact)).astype(jnp.int32)
    large = jnp.minimum(large, REL_BUCKETS - 1)
    return jnp.where(dist < max_exact, dist, large)


def _lag_bias(rel_bias, g, dil):
    dist = jnp.arange(N_LAGS + 1, dtype=jnp.int32) * dil
    onehot = _rel_bucket(dist)[:, None] == jnp.arange(REL_BUCKETS, dtype=jnp.int32)[None, :]
    table = jnp.sum(jnp.where(onehot[:, :, None], rel_bias.astype(F32)[None], 0.0), axis=1)
    return table[:, g * N_HEADS:(g + 1) * N_HEADS].T


def _neg(h, n):
    return jnp.full((h, n), NEG_INF, F32)


def _prompt_bias(lag_bias):
    h = lag_bias.shape[0]
    v = jnp.concatenate([lag_bias[:, ::-1], _neg(h, BAND)], axis=1)
    period = 2 * BAND + 1
    assert v.shape[1] == period
    return jnp.tile(v, (1, BAND))[:, :BAND * 2 * BAND].reshape(h, BAND, 2 * BAND)


def _sample_bias(lag_bias, dil, n_past, t_new):
    h = lag_bias.shape[0]
    n = n_past + t_new
    per_dist = jnp.repeat(lag_bias, dil, axis=1)
    per_dist = jnp.concatenate([per_dist, _neg(h, max(n - per_dist.shape[1], 0))], axis=1)[:, :n]
    per_dist = jnp.where((np.arange(n) % dil == 0)[None], per_dist, NEG_INF)
    rev = per_dist[:, ::-1]
    cached = jnp.stack([rev[:, t_new - 1 - t:t_new - 1 - t + n_past] for t in range(t_new)], axis=1)
    padded = jnp.concatenate([_neg(h, t_new - 1), per_dist[:, :t_new]], axis=1)
    new = jnp.stack([padded[:, t:t + t_new][:, ::-1] for t in range(t_new)], axis=1)
    return cached, new


def _layer(x, batch, t, weights, lag_biases, lb_logits, win_caches, hg_state0, conv_buf0, mem_kv,
           *, prompt, tm_proj, tm_merge, tm_ffn, tq_mem, chunk, n_sub):
    (w_in16, b_in, gain_pre, gain_post, hg_norm, w_att, w_hg, w_mem, w_out,
     gain_fpre, gain_fpost, w_fa, w_fb, conv_w, conv_b, w_fd) = weights
    act_dtype = BF16 if prompt else F32
    b_in = b_in.reshape(1, -1)
    if prompt:
        qkv = [proj(x, gain_pre, w_in16, b_in, QKV_BLOCKS(g), batch, t, dil, act_dtype, tm_proj)
               for g, (_, dil) in enumerate(ATT_GROUPS)]
        qkv[0] = qkv[0].reshape(batch, 1, t, QKV_COLS)
    else:
        qkv = [proj(x, gain_pre, w_in16, b_in, QKV_BLOCKS(g), 1, batch * t, 1, act_dtype, tm_proj)
               .reshape(batch, t, QKV_COLS) for g in range(len(ATT_GROUPS))]
    rec = proj(x, gain_pre, w_in16, b_in, REC_BLOCKS, 1, batch * t, 1, act_dtype, tm_proj)
    gates = proj(x, gain_pre, w_in16, b_in, GATE_BLOCKS, 1, batch * t, 1, F32, tm_proj)

    o_groups, lse_groups, new_caches = [], [], []
    for g, (win, dil) in enumerate(ATT_GROUPS):
        if prompt:
            o, lse = dilated_prompt(qkv[g], _prompt_bias(lag_biases[g]), g, dil, batch, t, 4 if dil == 1 else 1)
        else:
            cache = win_caches[g]
            n_past = cache.shape[1] // KV_ROWS
            assert n_past >= win
            bcache, bnew = _sample_bias(lag_biases[g], dil, n_past, t)
            o, lse, newc = dilated_sample(qkv[g], cache, bcache, bnew, g, batch, t)
            new_caches.append(newc)
        o_groups.append(o)
        lse_groups.append(lse)

    hg_o, hg_state = hgrn2(rec, gates, lb_logits, hg_norm, hg_state0, batch, t, chunk, n_sub, act_dtype)
    mem_o = memory_attention(rec, mem_kv, batch, t, tq_mem, act_dtype)
    x1 = merge(x, o_groups, lse_groups, hg_o, mem_o, gain_pre, w_in16, b_in, w_att, w_hg, w_mem, w_out, gain_post,
               tm_merge)
    y, tail = conv_ffn(x1, conv_buf0, gain_fpre, w_fa, w_fb, conv_w, conv_b, w_fd, gain_fpost, t, tm_ffn, 2048)
    return y, qkv, new_caches, hg_state, tail[:, 6:8, :]


def kernel(x_prompt, x_sample, mem_prompt, cache_win1_kv, cache_win2_kv, cache_win3_kv, cache_mem_kv, state_hgrn, state_ffn_conv, rel_bias, hg_lb_logits, norm_mix_pre, norm_mix_post, w_in, b_in, hg_norm, mem_norm, w_mem_kv, w_br_att, w_br_hg, w_br_mem, w_out, norm_ffn_pre, norm_ffn_post, w_ffn_a, w_ffn_b, ffn_conv_w, ffn_conv_b, w_ffn_d):
    depth = w_in.shape[0]
    assert depth == 1
    bsz, seq, d = x_prompt.shape
    dbsz, dseq, _ = x_sample.shape
    mem_tokens = mem_prompt.shape[1]
    dff = w_ffn_a.shape[2]
    layer = 0

    assert w_in.shape[2] == W_IN_BLOCKS * BRANCH_W
    weights = (w_in[layer].astype(BF16), b_in[layer],
               norm_mix_pre[layer], norm_mix_post[layer], hg_norm[layer],
               w_br_att[layer].astype(BF16), w_br_hg[layer].astype(BF16), w_br_mem[layer].astype(BF16),
               w_out[layer].astype(BF16), norm_ffn_pre[layer], norm_ffn_post[layer],
               w_ffn_a[layer].astype(BF16), w_ffn_b[layer].astype(BF16), ffn_conv_w[layer], ffn_conv_b[layer],
               w_ffn_d[layer].astype(BF16))
    lag_biases = [_lag_bias(rel_bias, g, dil) for g, (_, dil) in enumerate(ATT_GROUPS)]

    mem_kv = norm_matmul(mem_prompt.reshape(bsz * mem_tokens, d), mem_norm[layer], w_mem_kv[layer].astype(BF16),
                         jnp.zeros((2 * BRANCH_W,), F32), F32, bsz * mem_tokens, 1024)
    yp, qkv_p, _, hg_p, conv_p = _layer(
        x_prompt.reshape(bsz * seq, d), bsz, seq, weights, lag_biases, hg_lb_logits, None,
        jnp.zeros((bsz, N_HEADS, HEAD_DIM, HEAD_DIM), F32), jnp.zeros((bsz, 2, dff), F32),
        mem_kv.reshape(bsz, mem_tokens * KV_ROWS, HEAD_DIM),
        prompt=True, tm_proj=1024, tm_merge=512, tm_ffn=512, tq_mem=512, chunk=64, n_sub=4)
    p_win = []
    for g, (win, dil) in enumerate(ATT_GROUPS):
        n = min(win, seq)
        assert n % dil == 0
        tail = qkv_p[g][:, :, (seq - n) // dil:, BRANCH_W:]
        tail = jnp.swapaxes(tail, 1, 2).astype(F32)
        p_win.append(tail.reshape(1, bsz, n, 2, N_HEADS, HEAD_DIM))

    as_rows = lambda c: c[layer].reshape(dbsz, c.shape[2] * KV_ROWS, HEAD_DIM)
    ys, _, new_caches, hg_s, conv_s = _layer(
        x_sample.reshape(dbsz * dseq, d), dbsz, dseq, weights, lag_biases, hg_lb_logits,
        [as_rows(c) for c in (cache_win1_kv, cache_win2_kv, cache_win3_kv)],
        state_hgrn[layer], state_ffn_conv[layer], as_rows(cache_mem_kv),
        prompt=False, tm_proj=dbsz * dseq, tm_merge=dbsz * dseq, tm_ffn=dbsz * dseq, tq_mem=dseq, chunk=16, n_sub=1)
    s_win = [c.reshape(1, dbsz, c.shape[1] // KV_ROWS, 2, N_HEADS, HEAD_DIM) for c in new_caches]

    return (yp.reshape(bsz, seq, d), ys.reshape(dbsz, dseq, d),
            p_win[0], p_win[1], p_win[2],
            hg_p[None], conv_p[None], mem_kv.reshape(1, bsz, mem_tokens, 2, N_HEADS, HEAD_DIM),
            s_win[0], s_win[1], s_win[2],
            hg_s[None], conv_s[None])
```

```python
import functools
import math

import numpy as np
import jax
import jax.numpy as jnp
from jax import lax
from jax.experimental import pallas as pl
from jax.experimental.pallas import tpu as pltpu

F32 = jnp.float32
BF16 = jnp.bfloat16

NORM_EPS = 1e-6
NEG_INF = -1e30
HEAD_DIM = 128
N_HEADS = 4
BRANCH_W = N_HEADS * HEAD_DIM
N_LAGS = 128
BAND = 128
ATT_GROUPS = ((128, 1), (512, 4), (2048, 16))
REL_BUCKETS = 32
REL_MAX_DIST = 2048
LSE_LANES = 32
HG_BASE = 8
VMEM_LIMIT = 56 * 1024 * 1024

NT_DIMS = (((1,), (1,)), ((), ()))
TN_DIMS = (((0,), (0,)), ((), ()))

W_IN_BLOCKS = 20
QKV_BLOCKS = lambda g: (g, 3 + g, 6 + g)
REC_BLOCKS = (9, 11, 13)
GATE_BLOCKS = (10, 12)
GATE_MERGE_BLOCKS = (7, 8, 9)
QKV_COLS = 3 * BRANCH_W
REC_COLS = 3 * BRANCH_W
REC_HQ, REC_HI, REC_MQ = 0, 1, 2
G_COLS = 2 * BRANCH_W
KV_ROWS = 2 * N_HEADS


def _cparams(*sem):
    return pltpu.CompilerParams(dimension_semantics=sem, vmem_limit_bytes=VMEM_LIMIT)


def _rms(x, gain):
    return x * lax.rsqrt(jnp.mean(x * x, axis=-1, keepdims=True) + NORM_EPS) * gain


def _sigmoid(x):
    return 1.0 / (1.0 + jnp.exp(-x))


def _norm_matmul_kernel(x_ref, g_ref, w_ref, b_ref, o_ref, hn_ref):
    @pl.when(pl.program_id(1) == 0)
    def _():
        hn_ref[...] = _rms(x_ref[...], g_ref[...]).astype(BF16)

    acc = jnp.dot(hn_ref[...], w_ref[...], preferred_element_type=F32)
    o_ref[...] = (acc + b_ref[...]).astype(o_ref.dtype)


def norm_matmul(x, gain, w, bias, out_dtype, tm, tn):
    m, k = x.shape
    n = w.shape[1]
    assert m % tm == 0 and n % tn == 0
    return pl.pallas_call(
        _norm_matmul_kernel,
        grid=(m // tm, n // tn),
        in_specs=[
            pl.BlockSpec((tm, k), lambda i, j: (i, 0)),
            pl.BlockSpec((1, k), lambda i, j: (0, 0)),
            pl.BlockSpec((k, tn), lambda i, j: (0, j)),
            pl.BlockSpec((1, tn), lambda i, j: (0, j)),
        ],
        out_specs=pl.BlockSpec((tm, tn), lambda i, j: (i, j)),
        out_shape=jax.ShapeDtypeStruct((m, n), out_dtype),
        scratch_shapes=[pltpu.VMEM((tm, k), BF16)],
        compiler_params=_cparams("parallel", "arbitrary"),
        name="norm_matmul",
    )(x, gain.reshape(1, k), w, bias.reshape(1, n))


def _proj_kernel(x_ref, g_ref, *refs, n_blk, dil):
    w_refs, b_refs, o_ref = refs[:n_blk], refs[n_blk:2 * n_blk], refs[2 * n_blk]
    tm = x_ref.shape[0]
    hn = _rms(x_ref[...], g_ref[...]).astype(BF16)
    for n in range(n_blk):
        acc = jnp.dot(hn, w_refs[n][...], preferred_element_type=F32) + b_refs[n][...]
        if dil == 1:
            o_ref[:, n * BRANCH_W:(n + 1) * BRANCH_W] = acc.astype(o_ref.dtype)
            continue
        scr_ref = refs[2 * n_blk + 1]
        for c in range(N_HEADS):
            scr_ref[n * N_HEADS + c] = acc[:, c * HEAD_DIM:(c + 1) * HEAD_DIM]
        for r in range(dil):
            for c in range(N_HEADS):
                lo = n * BRANCH_W + c * HEAD_DIM
                o_ref[r, :, lo:lo + HEAD_DIM] = (
                    scr_ref[n * N_HEADS + c, pl.ds(r, tm // dil, stride=dil), :].astype(o_ref.dtype))


def proj(x, gain, w, bias, blocks, batch, seq, dil, out_dtype, tm):
    m, k = x.shape
    n_blk = len(blocks)
    cols = n_blk * BRANCH_W
    assert m == batch * seq and seq % tm == 0 and tm % dil == 0
    tiles_per_b = seq // tm
    w_specs = [pl.BlockSpec((k, BRANCH_W), lambda i, c=c: (0, c)) for c in blocks]
    b_specs = [pl.BlockSpec((1, BRANCH_W), lambda i, c=c: (0, c)) for c in blocks]
    if dil == 1:
        out_spec = pl.BlockSpec((tm, cols), lambda i: (i, 0))
        out_shape = jax.ShapeDtypeStruct((m, cols), out_dtype)
        scratch = []
    else:
        out_spec = pl.BlockSpec((None, dil, tm // dil, cols), lambda i: (i // tiles_per_b, 0, i % tiles_per_b, 0))
        out_shape = jax.ShapeDtypeStruct((batch, dil, seq // dil, cols), out_dtype)
        scratch = [pltpu.VMEM((n_blk * N_HEADS, tm, HEAD_DIM), F32)]
    return pl.pallas_call(
        functools.partial(_proj_kernel, n_blk=n_blk, dil=dil),
        grid=(m // tm,),
        in_specs=[pl.BlockSpec((tm, k), lambda i: (i, 0)), pl.BlockSpec((1, k), lambda i: (0, 0))] + w_specs + b_specs,
        out_specs=out_spec,
        out_shape=out_shape,
        scratch_shapes=scratch,
        compiler_params=_cparams("parallel"),
        name="proj",
    )(x, gain.reshape(1, k), *([w] * n_blk), *([bias] * n_blk))


def _dil_prompt_kernel(q_ref, kp_ref, kc_ref, vp_ref, vc_ref, bias_ref, o_ref, lse_ref, *scratch, dil, qb):
    scale = 1.0 / math.sqrt(HEAD_DIM)
    no_prev = (pl.program_id(1) == 0) & (lax.broadcasted_iota(jnp.int32, (1, 2 * BAND), 1) < BAND)

    def band_block(r, j):
        rows = slice(j * BAND, (j + 1) * BAND)
        both = slice((j - 1) * BAND, (j + 1) * BAND)
        outs, lses = [], []
        for h in range(N_HEADS):
            hs = slice(h * HEAD_DIM, (h + 1) * HEAD_DIM)
            if j == 0:
                k = jnp.concatenate([kp_ref[r, :, hs], kc_ref[r, rows, hs]], axis=0)
                v = jnp.concatenate([vp_ref[r, :, hs], vc_ref[r, rows, hs]], axis=0)
            else:
                k = kc_ref[r, both, hs]
                v = vc_ref[r, both, hs]
            logits = lax.dot_general(q_ref[r, rows, hs], k, NT_DIMS, preferred_element_type=F32) * scale + bias_ref[h]
            if j == 0:
                logits = jnp.where(no_prev, NEG_INF, logits)
            m = jnp.max(jnp.maximum(logits[:, :BAND], logits[:, BAND:]), axis=-1, keepdims=True)
            p = jnp.exp(logits - m)
            s = jnp.sum(p[:, :BAND] + p[:, BAND:], axis=-1, keepdims=True)
            o = jnp.dot(p.astype(BF16), v, preferred_element_type=F32)
            outs.append(o / s)
            lses.append(jnp.broadcast_to(m + jnp.log(s), (BAND, LSE_LANES)))
        return outs, jnp.concatenate(lses, axis=1)

    if dil == 1:
        for j in range(qb):
            outs, lse = band_block(0, j)
            for h in range(N_HEADS):
                o_ref[j * BAND:(j + 1) * BAND, h * HEAD_DIM:(h + 1) * HEAD_DIM] = outs[h].astype(o_ref.dtype)
            lse_ref[j * BAND:(j + 1) * BAND, :] = lse
        return

    o_scr, lse_scr = scratch

    def body(r, carry):
        for j in range(qb):
            outs, lse = band_block(r, j)
            for h in range(N_HEADS):
                o_scr[h, pl.ds(r + j * dil * BAND, BAND, stride=dil), :] = outs[h]
            lse_scr[pl.ds(r + j * dil * BAND, BAND, stride=dil), :] = lse
        return carry

    lax.fori_loop(0, dil, body, 0, unroll=min(dil, 4))
    for h in range(N_HEADS):
        o_ref[:, h * HEAD_DIM:(h + 1) * HEAD_DIM] = o_scr[h].astype(o_ref.dtype)
    lse_ref[...] = lse_scr[...]


def dilated_prompt(qkv, bias, g, dil, batch, seq, qb):
    assert seq % (dil * BAND * qb) == 0
    sub = seq // dil
    span = dil * BAND * qb
    cur = lambda part: pl.BlockSpec((None, dil, qb * BAND, BRANCH_W), lambda b, i: (b, 0, i, part))
    prev = lambda part: pl.BlockSpec((None, dil, BAND, BRANCH_W),
                                     lambda b, i: (b, 0, jnp.maximum(i * qb - 1, 0), part))
    bias_spec = pl.BlockSpec((N_HEADS, BAND, 2 * BAND), lambda b, i: (0, 0, 0))
    scratch = [] if dil == 1 else [pltpu.VMEM((N_HEADS, span, HEAD_DIM), F32),
                                   pltpu.VMEM((span, N_HEADS * LSE_LANES), F32)]
    o, lse = pl.pallas_call(
        functools.partial(_dil_prompt_kernel, dil=dil, qb=qb),
        grid=(batch, sub // (BAND * qb)),
        in_specs=[cur(0), prev(1), cur(1), prev(2), cur(2), bias_spec],
        out_specs=[pl.BlockSpec((None, span, BRANCH_W), lambda b, i: (b, i, 0)),
                   pl.BlockSpec((None, span, N_HEADS * LSE_LANES), lambda b, i: (b, i, 0))],
        out_shape=[jax.ShapeDtypeStruct((batch, seq, BRANCH_W), BF16),
                   jax.ShapeDtypeStruct((batch, seq, N_HEADS * LSE_LANES), F32)],
        scratch_shapes=scratch,
        compiler_params=_cparams("parallel", "arbitrary"),
        name=f"dilated_prompt_g{g}",
    )(qkv, qkv, qkv, qkv, qkv, bias)
    return o.reshape(batch * seq, BRANCH_W), lse.reshape(batch * seq, N_HEADS * LSE_LANES)


def _dil_sample_kernel(q_ref, kn_ref, vn_ref, cache_ref, bc_ref, bn_ref, shifted_ref, o_ref, lse_ref, newrows_ref,
                       *, win, t_new):
    del shifted_ref
    scale = 1.0 / math.sqrt(HEAD_DIM)
    for h in range(N_HEADS):
        hs = slice(h * HEAD_DIM, (h + 1) * HEAD_DIM)
        q = q_ref[:, hs]
        kn = kn_ref[:, hs]
        vn = vn_ref[:, hs]
        kc = cache_ref[pl.ds(h, win, stride=KV_ROWS), :].astype(BF16)
        vc = cache_ref[pl.ds(N_HEADS + h, win, stride=KV_ROWS), :].astype(BF16)
        lc = lax.dot_general(q.astype(BF16), kc, NT_DIMS, preferred_element_type=F32) * scale + bc_ref[h]
        ln = lax.dot_general(q, kn, NT_DIMS, preferred_element_type=F32) * scale + bn_ref[h]
        m = jnp.maximum(jnp.max(lc, axis=-1, keepdims=True), jnp.max(ln, axis=-1, keepdims=True))
        pc = jnp.exp(lc - m)
        pn = jnp.exp(ln - m)
        s = jnp.sum(pc, axis=-1, keepdims=True) + jnp.sum(pn, axis=-1, keepdims=True)
        o = jnp.dot(pc.astype(BF16), vc, preferred_element_type=F32) + jnp.dot(pn, vn, preferred_element_type=F32)
        o_ref[:, hs] = o / s
        lse_ref[:, h * LSE_LANES:(h + 1) * LSE_LANES] = jnp.broadcast_to(m + jnp.log(s), (t_new, LSE_LANES))
        newrows_ref[pl.ds(h, t_new, stride=KV_ROWS), :] = kn
        newrows_ref[pl.ds(N_HEADS + h, t_new, stride=KV_ROWS), :] = vn


def dilated_sample(qkv, cache, shifted, bias_cache, bias_new, g, batch, t_new):
    win = cache.shape[1] // KV_ROWS
    assert win % t_new == 0 and shifted.shape == cache.shape
    blk = (None, t_new, BRANCH_W)
    o, lse, newc = pl.pallas_call(
        functools.partial(_dil_sample_kernel, win=win, t_new=t_new),
        grid=(batch,),
        in_specs=[pl.BlockSpec(blk, lambda b: (b, 0, 0)),
                  pl.BlockSpec(blk, lambda b: (b, 0, 1)),
                  pl.BlockSpec(blk, lambda b: (b, 0, 2)),
                  pl.BlockSpec((None, win * KV_ROWS, HEAD_DIM), lambda b: (b, 0, 0)),
                  pl.BlockSpec((N_HEADS, t_new, win), lambda b: (0, 0, 0)),
                  pl.BlockSpec((N_HEADS, t_new, t_new), lambda b: (0, 0, 0)),
                  pl.BlockSpec(memory_space=pl.ANY)],
        out_specs=[pl.BlockSpec(blk, lambda b: (b, 0, 0)),
                   pl.BlockSpec((None, t_new, N_HEADS * LSE_LANES), lambda b: (b, 0, 0)),
                   pl.BlockSpec((None, t_new * KV_ROWS, HEAD_DIM), lambda b: (b, win // t_new - 1, 0))],
        out_shape=[jax.ShapeDtypeStruct((batch, t_new, BRANCH_W), F32),
                   jax.ShapeDtypeStruct((batch, t_new, N_HEADS * LSE_LANES), F32),
                   jax.ShapeDtypeStruct(cache.shape, F32)],
        input_output_aliases={6: 2},
        compiler_params=_cparams("parallel"),
        name=f"dilated_sample_g{g}",
    )(qkv, qkv, qkv, cache, bias_cache, bias_new, shifted)
    return o.reshape(batch * t_new, BRANCH_W), lse.reshape(batch * t_new, N_HEADS * LSE_LANES), newc


def _mem_attn_kernel(q_ref, kv_ref, o_ref, *, mem):
    scale = 1.0 / math.sqrt(HEAD_DIM)
    for h in range(N_HEADS):
        hs = slice(h * HEAD_DIM, (h + 1) * HEAD_DIM)
        k = kv_ref[pl.ds(h, mem, stride=KV_ROWS), :].astype(BF16)
        v = kv_ref[pl.ds(N_HEADS + h, mem, stride=KV_ROWS), :].astype(BF16)
        logits = lax.dot_general(q_ref[:, hs].astype(BF16), k, NT_DIMS, preferred_element_type=F32) * scale
        m = jnp.max(logits, axis=-1, keepdims=True)
        p = jnp.exp(logits - m)
        s = jnp.sum(p, axis=-1, keepdims=True)
        o = jnp.dot(p.astype(BF16), v, preferred_element_type=F32)
        o_ref[:, hs] = (o / s).astype(o_ref.dtype)


def memory_attention(rec, mem_kv, batch, t, tq, out_dtype):
    mem = mem_kv.shape[1] // KV_ROWS
    x = rec.reshape(batch, t, REC_COLS)
    o = pl.pallas_call(
        functools.partial(_mem_attn_kernel, mem=mem),
        grid=(batch, t // tq),
        in_specs=[pl.BlockSpec((None, tq, BRANCH_W), lambda b, i: (b, i, REC_MQ)),
                  pl.BlockSpec((None, mem * KV_ROWS, HEAD_DIM), lambda b, i: (b, 0, 0))],
        out_specs=pl.BlockSpec((None, tq, BRANCH_W), lambda b, i: (b, i, 0)),
        out_shape=jax.ShapeDtypeStruct((batch, t, BRANCH_W), out_dtype),
        compiler_params=_cparams("parallel", "parallel"),
        name="memory_attention",
    )(x, mem_kv)
    return o.reshape(batch * t, BRANCH_W)


def _hgrn_tables(chunk):
    c, b = chunk, HG_BASE
    t = np.arange(c)[:, None]
    u = np.arange(c)[None, :]
    masks = []
    size = 2 * b
    while size <= c:
        half = size // 2
        masks.append(((t // size) == (u // size)) & ((t % size) >= half) & ((u % size) < half))
        size *= 2
    masks.append(((t // b) == (u // b)) & (u <= t))
    erep = np.zeros((b * HEAD_DIM, c), np.float32)
    for s in range(b):
        erep[s * HEAD_DIM:(s + 1) * HEAD_DIM, np.arange(c) % b == s] = 1.0
    return (u <= t).astype(np.float32), erep, np.stack(masks).astype(np.float32)


def _shift_step(old_refs, new_refs, buf_refs, zero_ref, sem_in, sem_out, sem_zero, step, n_steps, drop):
    slot = step % 2
    other = 1 - slot

    def copies(g, chunk_idx, buf_slot):
        old, new, buf = old_refs[g], new_refs[g], buf_refs[g]
        rows_c = buf.shape[1]
        per_batch = (old.shape[1] - drop) // rows_c
        b = chunk_idx // per_batch
        lo = pl.multiple_of((chunk_idx % per_batch) * rows_c, 8)
        return (pltpu.make_async_copy(old.at[b, pl.ds(lo + drop, rows_c), :], buf.at[buf_slot], sem_in.at[g, buf_slot]),
                pltpu.make_async_copy(buf.at[buf_slot], new.at[b, pl.ds(lo, rows_c), :], sem_out.at[g, buf_slot]))

    def zero_copy(g):
        keep = old_refs[g].shape[1] - drop
        return pltpu.make_async_copy(zero_ref, new_refs[g].at[:, pl.ds(keep, drop), :], sem_zero.at[g])

    groups = range(len(old_refs))

    @pl.when(step == 0)
    def _():
        zero_ref[...] = jnp.zeros_like(zero_ref)
        for g in groups:
            zero_copy(g).start()
            copies(g, 0, 0)[0].start()

    for g in groups:
        copies(g, step, slot)[0].wait()

    @pl.when(step >= 1)
    def _():
        for g in groups:
            copies(g, step - 1, other)[1].wait()

    for g in groups:
        copies(g, step, slot)[1].start()

    @pl.when(step + 1 < n_steps)
    def _():
        for g in groups:
            copies(g, step + 1, other)[0].start()

    @pl.when(step == n_steps - 1)
    def _():
        for g in groups:
            copies(g, step, slot)[1].wait()
            zero_copy(g).wait()


def _hgrn_kernel(q_ref, v_ref, f_ref, gate_ref, lbl_ref, gn_ref, s0_ref, tri_ref, erep_ref, mask_ref, *rest,
                 chunk, rows_in, n_sub, n_levels, n_shift, shift_drop, n_steps):
    old_refs, rest = rest[:n_shift], rest[n_shift:]
    o_ref, s_out_ref = rest[:2]
    new_refs, rest = rest[2:2 + n_shift], rest[2 + n_shift:]
    st_ref = rest[0]
    c = chunk
    step = pl.program_id(1)
    if n_shift:
        buf_refs, (zero_ref, sem_in, sem_out, sem_zero) = rest[1:1 + n_shift], rest[1 + n_shift:]
        _shift_step(old_refs, new_refs, buf_refs, zero_ref, sem_in, sem_out, sem_zero,
                    pl.program_id(0) * pl.num_programs(1) + step, n_steps, shift_drop)

    @pl.when(step == 0)
    def _():
        for h in range(N_HEADS):
            st_ref[h] = s0_ref[h].T

    lbl = lbl_ref[...]
    e = jnp.exp(lbl - jnp.max(lbl, axis=0, keepdims=True))
    lb = e[0:1] / jnp.sum(e, axis=0, keepdims=True)
    tri = tri_ref[...]

    def rows(ref, ci):
        x = ref[ci * rows_in:(ci + 1) * rows_in, :].astype(F32)
        if rows_in < c:
            x = jnp.concatenate([x, jnp.zeros((c - rows_in, x.shape[1]), F32)], axis=0)
        return x

    for ci in range(n_sub):
        q = rows(q_ref, ci)
        v = rows(v_ref, ci).astype(BF16)
        f = lb + (1.0 - lb) * _sigmoid(rows(f_ref, ci))
        g = jnp.log(f)
        kk = 1.0 - f
        if rows_in < c:
            live = lax.broadcasted_iota(jnp.int32, (c, 1), 0) < rows_in
            g = jnp.where(live, g, 0.0)
            kk = jnp.where(live, kk, 0.0)

        g1 = g.astype(BF16)
        r1 = g - g1.astype(F32)
        g2 = r1.astype(BF16)
        g3 = (r1 - g2.astype(F32)).astype(BF16)
        cum = (jnp.dot(tri, g1, preferred_element_type=F32) + jnp.dot(tri, g2, preferred_element_type=F32)
               + jnp.dot(tri, g3, preferred_element_type=F32))

        def at_row(x, size, idx):
            x3 = x.reshape(c // size, size, x.shape[1])
            return jnp.broadcast_to(x3[:, idx:idx + 1, :], x3.shape).reshape(x.shape)

        q_inter = (q * jnp.exp(cum)).astype(BF16)
        g_end = cum[c - 1:c]
        k_end = (kk * jnp.exp(g_end - cum)).astype(BF16)
        q_lvl, k_lvl = [], []
        for l in range(n_levels):
            size = 2 * HG_BASE << l
            ref = at_row(cum, size, size // 2 - 1)
            q_lvl.append((q * jnp.exp(jnp.minimum(cum - ref, 0.0))).astype(BF16))
            k_lvl.append((kk * jnp.exp(jnp.minimum(ref - cum, 0.0))).astype(BF16))
        x_diag = []
        for s in range(HG_BASE):
            decay = jnp.exp(jnp.minimum(cum - at_row(cum, HG_BASE, s), 0.0))
            x_diag.append((q * at_row(kk, HG_BASE, s) * decay).astype(BF16))
        gate = _sigmoid(rows(gate_ref, ci))

        for h in range(N_HEADS):
            hs = slice(h * HEAD_DIM, (h + 1) * HEAD_DIM)
            scores = mask_ref[n_levels] * jnp.dot(jnp.concatenate([x[:, hs] for x in x_diag], axis=1),
                                                  erep_ref[...], preferred_element_type=F32)
            for l in range(n_levels):
                scores += mask_ref[l] * lax.dot_general(q_lvl[l][:, hs], k_lvl[l][:, hs], NT_DIMS,
                                                        preferred_element_type=F32)
            st = st_ref[h]
            o = (lax.dot_general(q_inter[:, hs], st.astype(BF16), NT_DIMS, preferred_element_type=F32)
                 + jnp.dot(scores.astype(BF16), v[:, hs], preferred_element_type=F32))
            st_ref[h] = (st * jnp.exp(g_end[:, hs])
                         + lax.dot_general(v[:, hs], k_end[:, hs], TN_DIMS, preferred_element_type=F32))
            y = _rms(o, gn_ref[...]) * gate[:, hs]
            o_ref[ci * rows_in:(ci + 1) * rows_in, hs] = y[0:rows_in].astype(o_ref.dtype)

    @pl.when(step == pl.num_programs(1) - 1)
    def _():
        for h in range(N_HEADS):
            s_out_ref[h] = st_ref[h].T


def hgrn2(rec, gates, lb_logits, hg_norm, state0, batch, t, chunk, n_sub, out_dtype, shift=(), shift_drop=0):
    rows_in = min(chunk, t)
    step_rows = rows_in * n_sub
    assert t % step_rows == 0 and chunk % HG_BASE == 0 and (n_sub == 1 or rows_in == chunk)
    n_steps = batch * (t // step_rows)
    n_shift = len(shift)
    any_spec = pl.BlockSpec(memory_space=pl.ANY)
    shift_scratch = []
    if n_shift:
        sb = shift[0].shape[0]
        assert n_steps % sb == 0 and all(s.shape[0] == sb and s.shape[2] == HEAD_DIM for s in shift)
        per_batch = n_steps // sb
        chunk_rows = [(s.shape[1] - shift_drop) // per_batch for s in shift]
        assert all(r % 8 == 0 and r * per_batch == s.shape[1] - shift_drop for r, s in zip(chunk_rows, shift))
        shift_scratch = ([pltpu.VMEM((2, r, HEAD_DIM), F32) for r in chunk_rows]
                         + [pltpu.VMEM((sb, shift_drop, HEAD_DIM), F32), pltpu.SemaphoreType.DMA((n_shift, 2)),
                            pltpu.SemaphoreType.DMA((n_shift, 2)), pltpu.SemaphoreType.DMA((n_shift,))])
    tri, erep, masks = _hgrn_tables(chunk)
    n_levels = masks.shape[0] - 1
    a = rec.reshape(batch, t, REC_COLS)
    gt = gates.reshape(batch, t, G_COLS)
    blk = (None, step_rows, BRANCH_W)
    full = lambda arr: pl.BlockSpec(arr.shape, lambda b, i: (0,) * arr.ndim)
    st_spec = pl.BlockSpec((None, N_HEADS, HEAD_DIM, HEAD_DIM), lambda b, i: (b, 0, 0, 0))
    tri = jnp.asarray(tri, BF16)
    erep = jnp.asarray(erep, BF16)
    masks = jnp.asarray(masks, F32)
    lbl = lb_logits.astype(F32)
    gn = hg_norm.reshape(1, HEAD_DIM).astype(F32)
    o, s_fin, *shifted = pl.pallas_call(
        functools.partial(_hgrn_kernel, chunk=chunk, rows_in=rows_in, n_sub=n_sub, n_levels=n_levels,
                          n_shift=n_shift, shift_drop=shift_drop, n_steps=n_steps),
        grid=(batch, t // step_rows),
        in_specs=[pl.BlockSpec(blk, lambda b, i: (b, i, REC_HQ)),
                  pl.BlockSpec(blk, lambda b, i: (b, i, REC_HI)),
                  pl.BlockSpec(blk, lambda b, i: (b, i, 0)),
                  pl.BlockSpec(blk, lambda b, i: (b, i, 1)),
                  full(lbl), full(gn), st_spec, full(tri), full(erep), full(masks)] + [any_spec] * n_shift,
        out_specs=[pl.BlockSpec(blk, lambda b, i: (b, i, 0)), st_spec] + [any_spec] * n_shift,
        out_shape=[jax.ShapeDtypeStruct((batch, t, BRANCH_W), out_dtype),
                   jax.ShapeDtypeStruct(state0.shape, F32)] + [jax.ShapeDtypeStruct(s.shape, F32) for s in shift],
        scratch_shapes=[pltpu.VMEM((N_HEADS, HEAD_DIM, HEAD_DIM), F32)] + shift_scratch,
        compiler_params=_cparams("arbitrary", "arbitrary"),
        name="hgrn2",
    )(a, a, gt, gt, lbl, gn, state0, tri, erep, masks, *shift)
    return o.reshape(batch * t, BRANCH_W), s_fin, shifted


def _merge_kernel(x_ref, o1_ref, o2_ref, o3_ref, l1_ref, l2_ref, l3_ref, hg_ref, mem_ref, gpre_ref,
                  wga_ref, wgh_ref, wgm_ref, bga_ref, bgh_ref, bgm_ref,
                  wa_ref, wh_ref, wm_ref, wo_ref, gain_ref, out_ref):
    tm = x_ref.shape[0]
    x = x_ref[...]
    hn = _rms(x, gpre_ref[...]).astype(BF16)
    gate = lambda w_ref, b_ref: _sigmoid(jnp.dot(hn, w_ref[...], preferred_element_type=F32) + b_ref[...])
    l1, l2, l3 = l1_ref[...], l2_ref[...], l3_ref[...]
    m = jnp.maximum(jnp.maximum(l1, l2), l3)
    e1, e2, e3 = jnp.exp(l1 - m), jnp.exp(l2 - m), jnp.exp(l3 - m)
    den = e1 + e2 + e3
    w1, w2, w3 = e1 / den, e2 / den, e3 / den
    att = []
    for h in range(N_HEADS):
        hs = slice(h * HEAD_DIM, (h + 1) * HEAD_DIM)
        col = slice(h * LSE_LANES, h * LSE_LANES + 1)
        bc = lambda w: jnp.broadcast_to(w[:, col], (tm, HEAD_DIM))
        att.append((bc(w1) * o1_ref[:, hs].astype(F32) + bc(w2) * o2_ref[:, hs].astype(F32)
                    + bc(w3) * o3_ref[:, hs].astype(F32)).astype(BF16))
    att = jnp.concatenate(att, axis=1)
    merged = (gate(wga_ref, bga_ref) * jnp.dot(att, wa_ref[...], preferred_element_type=F32)
              + gate(wgh_ref, bgh_ref) * jnp.dot(hg_ref[...].astype(BF16), wh_ref[...], preferred_element_type=F32)
              + gate(wgm_ref, bgm_ref) * jnp.dot(mem_ref[...].astype(BF16), wm_ref[...],
                                                 preferred_element_type=F32))
    y = jnp.dot(merged.astype(BF16), wo_ref[...], preferred_element_type=F32)
    out_ref[...] = x + _rms(y, gain_ref[...])


def merge(x, o_groups, lse_groups, hg_o, mem_o, gain_pre, w_in, b_in, w_att, w_hg, w_mem, w_out, gain, tm):
    m, d = x.shape
    assert m % tm == 0 and GATE_MERGE_BLOCKS[0] * d == 14 * BRANCH_W
    row = lambda w: pl.BlockSpec((tm, w), lambda i: (i, 0))
    resident = lambda shape, j=0: pl.BlockSpec(shape, lambda i: (0, j), pipeline_mode=pl.Buffered(1))
    return pl.pallas_call(
        _merge_kernel,
        grid=(m // tm,),
        in_specs=[row(d), row(BRANCH_W), row(BRANCH_W), row(BRANCH_W),
                  row(N_HEADS * LSE_LANES), row(N_HEADS * LSE_LANES), row(N_HEADS * LSE_LANES),
                  row(BRANCH_W), row(BRANCH_W), resident((1, d))]
        + [resident((d, d), j) for j in GATE_MERGE_BLOCKS] + [resident((1, d), j) for j in GATE_MERGE_BLOCKS]
        + [resident(w_att.shape), resident(w_hg.shape), resident(w_mem.shape), resident(w_out.shape),
           resident((1, d))],
        out_specs=row(d),
        out_shape=jax.ShapeDtypeStruct((m, d), F32),
        compiler_params=_cparams("parallel"),
        name="merge",
    )(x, *o_groups, *lse_groups, hg_o, mem_o, gain_pre.reshape(1, d), w_in, w_in, w_in, b_in, b_in, b_in,
      w_att, w_hg, w_mem, w_out, gain.reshape(1, d))


def _ffn_kernel(x_ref, gpre_ref, wa_ref, wb_ref, cw_ref, cb_ref, wd_ref, gpost_ref, cbuf_ref,
                out_ref, tail_ref, carry_ref, *, t_seq, tiles_per_seq, tf):
    tm = x_ref.shape[0]
    dff = wa_ref.shape[1]
    x = x_ref[...]
    hn = _rms(x, gpre_ref[...]).astype(BF16)

    if tiles_per_seq >= 1:
        @pl.when((pl.program_id(0) % tiles_per_seq) == 0)
        def _():
            carry_ref[6:8, :] = cbuf_ref[0]

        t_idx = lax.broadcasted_iota(jnp.int32, (tm, 1), 0)
    else:
        n_seq = tm // t_seq
        t_idx = lax.broadcasted_iota(jnp.int32, (tm, 1), 0) % t_seq

    acc = jnp.zeros(x.shape, F32)
    for j in range(dff // tf):
        cols = slice(j * tf, (j + 1) * tf)
        a = jnp.dot(hn, wa_ref[:, cols], preferred_element_type=F32)
        up = jnp.dot(hn, wb_ref[:, cols], preferred_element_type=F32)
        if tiles_per_seq >= 1:
            prev1 = carry_ref[7:8, cols]
            prev2 = carry_ref[6:7, cols]
            carry_ref[:, cols] = a[tm - 8:tm]
            tail_ref[0, :, cols] = a[tm - 8:tm]
        else:
            prev1 = jnp.broadcast_to(cbuf_ref[:, 1:2, cols], (n_seq, t_seq, tf)).reshape(tm, tf)
            prev2 = jnp.broadcast_to(cbuf_ref[:, 0:1, cols], (n_seq, t_seq, tf)).reshape(tm, tf)
            tail_ref[:, :, cols] = a.reshape(n_seq, t_seq, tf)
        a1 = jnp.where(t_idx >= 1, pltpu.roll(a, 1, 0), prev1)
        a2 = jnp.where(t_idx >= 2, pltpu.roll(a, 2, 0), jnp.where(t_idx == 1, prev1, prev2))
        conv = cb_ref[:, cols] + a2 * cw_ref[0:1, cols] + a1 * cw_ref[1:2, cols] + a * cw_ref[2:3, cols]
        act = conv * _sigmoid(conv) * up
        acc += jnp.dot(act.astype(BF16), wd_ref[cols, :], preferred_element_type=F32)

    out_ref[...] = x + _rms(acc, gpost_ref[...])


def conv_ffn(x, conv_buf, g_pre, w_a, w_b, conv_w, conv_b, w_d, g_post, t_seq, tm, tf):
    m, d = x.shape
    dff = w_a.shape[1]
    n_seq = m // t_seq
    assert m % tm == 0 and dff % tf == 0 and t_seq >= 8
    if tm <= t_seq:
        assert t_seq % tm == 0
        tiles_per_seq = t_seq // tm
        seq_blk = 1
        seq_idx = lambda i: (i // tiles_per_seq, 0, 0)
    else:
        assert tm % t_seq == 0 and t_seq == 8
        tiles_per_seq = 0
        seq_blk = tm // t_seq
        seq_idx = lambda i: (i, 0, 0)
    n_tail = (m // tm) * seq_blk
    resident = lambda shape: pl.BlockSpec(shape, lambda i: (0, 0), pipeline_mode=pl.Buffered(1))
    y, tail = pl.pallas_call(
        functools.partial(_ffn_kernel, t_seq=t_seq, tiles_per_seq=tiles_per_seq, tf=tf),
        grid=(m // tm,),
        in_specs=[pl.BlockSpec((tm, d), lambda i: (i, 0)), resident((1, d)),
                  resident((d, dff)), resident((d, dff)), resident((3, dff)), resident((1, dff)),
                  resident((dff, d)), resident((1, d)),
                  pl.BlockSpec((seq_blk, 2, dff), seq_idx)],
        out_specs=[pl.BlockSpec((tm, d), lambda i: (i, 0)),
                   pl.BlockSpec((seq_blk, 8, dff), lambda i: (i, 0, 0))],
        out_shape=[jax.ShapeDtypeStruct((m, d), F32), jax.ShapeDtypeStruct((n_tail, 8, dff), F32)],
        scratch_shapes=[pltpu.VMEM((8, dff), F32)],
        compiler_params=_cparams("arbitrary"),
        name="conv_ffn",
    )(x, g_pre.reshape(1, d), w_a, w_b, conv_w, conv_b.reshape(1, dff), w_d, g_post.reshape(1, d), conv_buf)
    if tiles_per_seq >= 1:
        tail = tail.reshape(n_seq, tiles_per_seq, 8, dff)[:, -1]
    return y, tail


def _rel_buckets(dil):
    max_exact = REL_BUCKETS // 2
    dist = np.arange(N_LAGS + 1, dtype=np.int32) * dil
    d = np.maximum(dist, 1).astype(np.float32)
    large = max_exact + (np.log(d / np.float32(max_exact)) / np.float32(math.log(REL_MAX_DIST / max_exact))
                         * np.float32(REL_BUCKETS - max_exact)).astype(np.int32)
    large = np.minimum(large, REL_BUCKETS - 1)
    return np.where(dist < max_exact, dist, large)


def _lag_bias(rel_bias, g, dil):
    onehot = _rel_buckets(dil)[:, None] == np.arange(REL_BUCKETS)[None, :]
    table = jnp.sum(jnp.where(onehot[:, :, None], rel_bias.astype(F32)[None], 0.0), axis=1)
    return table[:, g * N_HEADS:(g + 1) * N_HEADS].T


def _neg(h, n):
    return jnp.full((h, n), NEG_INF, F32)


def _prompt_bias(lag_bias):
    h = lag_bias.shape[0]
    v = jnp.concatenate([lag_bias[:, ::-1], _neg(h, BAND)], axis=1)
    period = 2 * BAND + 1
    assert v.shape[1] == period
    return jnp.tile(v, (1, BAND))[:, :BAND * 2 * BAND].reshape(h, BAND, 2 * BAND)


def _sample_bias(lag_bias, dil, n_past, t_new):
    h = lag_bias.shape[0]
    n = n_past + t_new
    per_dist = jnp.repeat(lag_bias, dil, axis=1)
    per_dist = jnp.concatenate([per_dist, _neg(h, max(n - per_dist.shape[1], 0))], axis=1)[:, :n]
    per_dist = jnp.where((np.arange(n) % dil == 0)[None], per_dist, NEG_INF)
    rev = per_dist[:, ::-1]
    cached = jnp.stack([rev[:, t_new - 1 - t:t_new - 1 - t + n_past] for t in range(t_new)], axis=1)
    padded = jnp.concatenate([_neg(h, t_new - 1), per_dist[:, :t_new]], axis=1)
    new = jnp.stack([padded[:, t:t + t_new][:, ::-1] for t in range(t_new)], axis=1)
    return cached, new


def _layer(x, batch, t, weights, lag_biases, lb_logits, win_caches, hg_state0, conv_buf0, mem_kv,
           *, prompt, tm_proj, tm_merge, tm_ffn, tq_mem, chunk, n_sub, shifted=None, side_shift=(), side_shift_drop=0):
    (w_in16, b_in, gain_pre, gain_post, hg_norm, w_att, w_hg, w_mem, w_out,
     gain_fpre, gain_fpost, w_fa, w_fb, conv_w, conv_b, w_fd) = weights
    act_dtype = BF16 if prompt else F32
    b_in = b_in.reshape(1, -1)
    if prompt:
        qkv = [proj(x, gain_pre, w_in16, b_in, QKV_BLOCKS(g), batch, t, dil, act_dtype, tm_proj)
               for g, (_, dil) in enumerate(ATT_GROUPS)]
        qkv[0] = qkv[0].reshape(batch, 1, t, QKV_COLS)
    else:
        qkv = [proj(x, gain_pre, w_in16, b_in, QKV_BLOCKS(g), 1, batch * t, 1, act_dtype, tm_proj)
               .reshape(batch, t, QKV_COLS) for g in range(len(ATT_GROUPS))]
    rec = proj(x, gain_pre, w_in16, b_in, REC_BLOCKS, 1, batch * t, 1, act_dtype, tm_proj)
    gates = proj(x, gain_pre, w_in16, b_in, GATE_BLOCKS, 1, batch * t, 1, F32, tm_proj)

    o_groups, lse_groups, new_caches = [], [], []
    for g, (win, dil) in enumerate(ATT_GROUPS):
        if prompt:
            o, lse = dilated_prompt(qkv[g], _prompt_bias(lag_biases[g]), g, dil, batch, t, 4 if dil == 1 else 1)
        else:
            cache = win_caches[g]
            n_past = cache.shape[1] // KV_ROWS
            assert n_past >= win
            bcache, bnew = _sample_bias(lag_biases[g], dil, n_past, t)
            o, lse, newc = dilated_sample(qkv[g], cache, shifted[g], bcache, bnew, g, batch, t)
            new_caches.append(newc)
        o_groups.append(o)
        lse_groups.append(lse)

    hg_o, hg_state, shifted_out = hgrn2(rec, gates, lb_logits, hg_norm, hg_state0, batch, t, chunk, n_sub, act_dtype,
                                        side_shift, side_shift_drop)
    mem_o = memory_attention(rec, mem_kv, batch, t, tq_mem, act_dtype)
    x1 = merge(x, o_groups, lse_groups, hg_o, mem_o, gain_pre, w_in16, b_in, w_att, w_hg, w_mem, w_out, gain_post,
               tm_merge)
    y, tail = conv_ffn(x1, conv_buf0, gain_fpre, w_fa, w_fb, conv_w, conv_b, w_fd, gain_fpost, t, tm_ffn, 2048)
    return y, qkv, new_caches, hg_state, tail[:, 6:8, :], shifted_out


def kernel(x_prompt, x_sample, mem_prompt, cache_win1_kv, cache_win2_kv, cache_win3_kv, cache_mem_kv, state_hgrn, state_ffn_conv, rel_bias, hg_lb_logits, norm_mix_pre, norm_mix_post, w_in, b_in, hg_norm, mem_norm, w_mem_kv, w_br_att, w_br_hg, w_br_mem, w_out, norm_ffn_pre, norm_ffn_post, w_ffn_a, w_ffn_b, ffn_conv_w, ffn_conv_b, w_ffn_d):
    depth = w_in.shape[0]
    assert depth == 1
    bsz, seq, d = x_prompt.shape
    dbsz, dseq, _ = x_sample.shape
    mem_tokens = mem_prompt.shape[1]
    dff = w_ffn_a.shape[2]
    layer = 0

    assert w_in.shape[2] == W_IN_BLOCKS * BRANCH_W
    weights = (w_in[layer].astype(BF16), b_in[layer],
               norm_mix_pre[layer], norm_mix_post[layer], hg_norm[layer],
               w_br_att[layer].astype(BF16), w_br_hg[layer].astype(BF16), w_br_mem[layer].astype(BF16),
               w_out[layer].astype(BF16), norm_ffn_pre[layer], norm_ffn_post[layer],
               w_ffn_a[layer].astype(BF16), w_ffn_b[layer].astype(BF16), ffn_conv_w[layer], ffn_conv_b[layer],
               w_ffn_d[layer].astype(BF16))
    lag_biases = [_lag_bias(rel_bias, g, dil) for g, (_, dil) in enumerate(ATT_GROUPS)]

    mem_kv = norm_matmul(mem_prompt.reshape(bsz * mem_tokens, d), mem_norm[layer], w_mem_kv[layer].astype(BF16),
                         jnp.zeros((2 * BRANCH_W,), F32), F32, bsz * mem_tokens, 1024)
    as_rows = lambda c: c[layer].reshape(dbsz, c.shape[2] * KV_ROWS, HEAD_DIM)
    win_caches = [as_rows(c) for c in (cache_win1_kv, cache_win2_kv, cache_win3_kv)]
    yp, qkv_p, _, hg_p, conv_p, shifted = _layer(
        x_prompt.reshape(bsz * seq, d), bsz, seq, weights, lag_biases, hg_lb_logits, None,
        jnp.zeros((bsz, N_HEADS, HEAD_DIM, HEAD_DIM), F32), jnp.zeros((bsz, 2, dff), F32),
        mem_kv.reshape(bsz, mem_tokens * KV_ROWS, HEAD_DIM),
        prompt=True, tm_proj=1024, tm_merge=512, tm_ffn=512, tq_mem=512, chunk=64, n_sub=4,
        side_shift=win_caches, side_shift_drop=dseq * KV_ROWS)
    p_win = []
    for g, (win, dil) in enumerate(ATT_GROUPS):
        n = min(win, seq)
        assert n % dil == 0
        tail = qkv_p[g][:, :, (seq - n) // dil:, BRANCH_W:]
        tail = jnp.swapaxes(tail, 1, 2).astype(F32)
        p_win.append(tail.reshape(1, bsz, n, 2, N_HEADS, HEAD_DIM))

    ys, _, new_caches, hg_s, conv_s, _ = _layer(
        x_sample.reshape(dbsz * dseq, d), dbsz, dseq, weights, lag_biases, hg_lb_logits, win_caches,
        state_hgrn[layer], state_ffn_conv[layer], as_rows(cache_mem_kv),
        prompt=False, tm_proj=dbsz * dseq, tm_merge=dbsz * dseq, tm_ffn=dbsz * dseq, tq_mem=dseq, chunk=16, n_sub=1,
        shifted=shifted)
    s_win = [c.reshape(1, dbsz, c.shape[1] // KV_ROWS, 2, N_HEADS, HEAD_DIM) for c in new_caches]

    return (yp.reshape(bsz, seq, d), ys.reshape(dbsz, dseq, d),
            p_win[0], p_win[1], p_win[2],
            hg_p[None], conv_p[None], mem_kv.reshape(1, bsz, mem_tokens, 2, N_HEADS, HEAD_DIM),
            s_win[0], s_win[1], s_win[2],
            hg_s[None], conv_s[None])
```

```python
import functools
import math

import numpy as np
import jax
import jax.numpy as jnp
from jax import lax
from jax.experimental import pallas as pl
from jax.experimental.pallas import tpu as pltpu

F32 = jnp.float32
BF16 = jnp.bfloat16

NORM_EPS = 1e-6
NEG_INF = -1e30
HEAD_DIM = 128
N_HEADS = 4
BRANCH_W = N_HEADS * HEAD_DIM
N_LAGS = 128
BAND = 128
ATT_GROUPS = ((128, 1), (512, 4), (2048, 16))
REL_BUCKETS = 32
REL_MAX_DIST = 2048
LSE_LANES = 32
HG_BASE = 8
VMEM_LIMIT = 56 * 1024 * 1024
SAMPLE_STEP_BYTES = 8 * 1024 * 1024

NT_DIMS = (((1,), (1,)), ((), ()))
TN_DIMS = (((0,), (0,)), ((), ()))

W_IN_BLOCKS = 20
QKV_BLOCKS = lambda g: (g, 3 + g, 6 + g)
REC_BLOCKS = (9, 11, 13)
GATE_BLOCKS = (10, 12)
GATE_MERGE_BLOCKS = (7, 8, 9)
QKV_COLS = 3 * BRANCH_W
REC_COLS = 3 * BRANCH_W
REC_HQ, REC_HI, REC_MQ = 0, 1, 2
G_COLS = 2 * BRANCH_W
KV_ROWS = 2 * N_HEADS


def _cparams(*sem):
    return pltpu.CompilerParams(dimension_semantics=sem, vmem_limit_bytes=VMEM_LIMIT)


def _rms(x, gain):
    return x * lax.rsqrt(jnp.mean(x * x, axis=-1, keepdims=True) + NORM_EPS) * gain


def _sigmoid(x):
    return 1.0 / (1.0 + jnp.exp(-x))


def _norm_matmul_kernel(x_ref, g_ref, w_ref, b_ref, o_ref, hn_ref):
    @pl.when(pl.program_id(1) == 0)
    def _():
        hn_ref[...] = _rms(x_ref[...], g_ref[...]).astype(BF16)

    acc = jnp.dot(hn_ref[...], w_ref[...], preferred_element_type=F32)
    o_ref[...] = (acc + b_ref[...]).astype(o_ref.dtype)


def norm_matmul(x, gain, w, bias, out_dtype, tm, tn):
    m, k = x.shape
    n = w.shape[1]
    assert m % tm == 0 and n % tn == 0
    return pl.pallas_call(
        _norm_matmul_kernel,
        grid=(m // tm, n // tn),
        in_specs=[
            pl.BlockSpec((tm, k), lambda i, j: (i, 0)),
            pl.BlockSpec((1, k), lambda i, j: (0, 0)),
            pl.BlockSpec((k, tn), lambda i, j: (0, j)),
            pl.BlockSpec((1, tn), lambda i, j: (0, j)),
        ],
        out_specs=pl.BlockSpec((tm, tn), lambda i, j: (i, j)),
        out_shape=jax.ShapeDtypeStruct((m, n), out_dtype),
        scratch_shapes=[pltpu.VMEM((tm, k), BF16)],
        compiler_params=_cparams("parallel", "arbitrary"),
        name="norm_matmul",
    )(x, gain.reshape(1, k), w, bias.reshape(1, n))


def _proj_kernel(x_ref, g_ref, *refs, n_blk, dil):
    w_refs, b_refs, o_ref = refs[:n_blk], refs[n_blk:2 * n_blk], refs[2 * n_blk]
    tm = x_ref.shape[0]
    hn = _rms(x_ref[...], g_ref[...]).astype(BF16)
    for n in range(n_blk):
        acc = jnp.dot(hn, w_refs[n][...], preferred_element_type=F32) + b_refs[n][...]
        if dil == 1:
            o_ref[:, n * BRANCH_W:(n + 1) * BRANCH_W] = acc.astype(o_ref.dtype)
            continue
        scr_ref = refs[2 * n_blk + 1]
        for c in range(N_HEADS):
            scr_ref[n * N_HEADS + c] = acc[:, c * HEAD_DIM:(c + 1) * HEAD_DIM]
        for r in range(dil):
            for c in range(N_HEADS):
                lo = n * BRANCH_W + c * HEAD_DIM
                o_ref[r, :, lo:lo + HEAD_DIM] = (
                    scr_ref[n * N_HEADS + c, pl.ds(r, tm // dil, stride=dil), :].astype(o_ref.dtype))


def proj(x, gain, w, bias, blocks, batch, seq, dil, out_dtype, tm):
    m, k = x.shape
    n_blk = len(blocks)
    cols = n_blk * BRANCH_W
    assert m == batch * seq and seq % tm == 0 and tm % dil == 0
    tiles_per_b = seq // tm
    w_specs = [pl.BlockSpec((k, BRANCH_W), lambda i, c=c: (0, c)) for c in blocks]
    b_specs = [pl.BlockSpec((1, BRANCH_W), lambda i, c=c: (0, c)) for c in blocks]
    if dil == 1:
        out_spec = pl.BlockSpec((tm, cols), lambda i: (i, 0))
        out_shape = jax.ShapeDtypeStruct((m, cols), out_dtype)
        scratch = []
    else:
        out_spec = pl.BlockSpec((None, dil, tm // dil, cols), lambda i: (i // tiles_per_b, 0, i % tiles_per_b, 0))
        out_shape = jax.ShapeDtypeStruct((batch, dil, seq // dil, cols), out_dtype)
        scratch = [pltpu.VMEM((n_blk * N_HEADS, tm, HEAD_DIM), F32)]
    return pl.pallas_call(
        functools.partial(_proj_kernel, n_blk=n_blk, dil=dil),
        grid=(m // tm,),
        in_specs=[pl.BlockSpec((tm, k), lambda i: (i, 0)), pl.BlockSpec((1, k), lambda i: (0, 0))] + w_specs + b_specs,
        out_specs=out_spec,
        out_shape=out_shape,
        scratch_shapes=scratch,
        compiler_params=_cparams("parallel"),
        name="proj",
    )(x, gain.reshape(1, k), *([w] * n_blk), *([bias] * n_blk))


def _dil_prompt_kernel(q_ref, kp_ref, kc_ref, vp_ref, vc_ref, bias_ref, o_ref, lse_ref, *scratch, dil, qb):
    scale = 1.0 / math.sqrt(HEAD_DIM)
    no_prev = (pl.program_id(1) == 0) & (lax.broadcasted_iota(jnp.int32, (1, 2 * BAND), 1) < BAND)

    def band_blocks(blocks):
        items = [(r, j, h, slice(j * BAND, (j + 1) * BAND), slice(h * HEAD_DIM, (h + 1) * HEAD_DIM))
                 for r, j in blocks for h in range(N_HEADS)]

        def keys(cur_ref, prev_ref, r, j, rows, hs):
            if j == 0:
                return jnp.concatenate([prev_ref[r, :, hs], cur_ref[r, rows, hs]], axis=0)
            return cur_ref[r, (j - 1) * BAND:(j + 1) * BAND, hs]

        logits = []
        for r, j, h, rows, hs in items:
            l = lax.dot_general(q_ref[r, rows, hs], keys(kc_ref, kp_ref, r, j, rows, hs), NT_DIMS,
                                preferred_element_type=F32) * scale + bias_ref[h]
            logits.append(jnp.where(no_prev, NEG_INF, l) if j == 0 else l)
        probs = []
        for l in logits:
            m = jnp.max(jnp.maximum(l[:, :BAND], l[:, BAND:]), axis=-1, keepdims=True)
            p = jnp.exp(l - m)
            s = jnp.sum(p[:, :BAND] + p[:, BAND:], axis=-1, keepdims=True)
            probs.append((p.astype(BF16), s, jnp.broadcast_to(m + jnp.log(s), (BAND, LSE_LANES))))
        outs = [jnp.dot(p, keys(vc_ref, vp_ref, r, j, rows, hs), preferred_element_type=F32) / s
                for (r, j, h, rows, hs), (p, s, _) in zip(items, probs)]
        return [(outs[N_HEADS * n:N_HEADS * (n + 1)],
                 jnp.concatenate([lse for _, _, lse in probs[N_HEADS * n:N_HEADS * (n + 1)]], axis=1))
                for n in range(len(blocks))]

    if dil == 1:
        for j, (outs, lse) in enumerate(band_blocks([(0, j) for j in range(qb)])):
            for h in range(N_HEADS):
                o_ref[j * BAND:(j + 1) * BAND, h * HEAD_DIM:(h + 1) * HEAD_DIM] = outs[h].astype(o_ref.dtype)
            lse_ref[j * BAND:(j + 1) * BAND, :] = lse
        return

    o_scr, lse_scr = scratch
    per_iter = min(dil, 4)

    def body(it, carry):
        blocks = [(it * per_iter + rr, j) for rr in range(per_iter) for j in range(qb)]
        for (r, j), (outs, lse) in zip(blocks, band_blocks(blocks)):
            for h in range(N_HEADS):
                o_scr[h, pl.ds(r + j * dil * BAND, BAND, stride=dil), :] = outs[h]
            lse_scr[pl.ds(r + j * dil * BAND, BAND, stride=dil), :] = lse
        return carry

    lax.fori_loop(0, dil // per_iter, body, 0)
    for h in range(N_HEADS):
        o_ref[:, h * HEAD_DIM:(h + 1) * HEAD_DIM] = o_scr[h].astype(o_ref.dtype)
    lse_ref[...] = lse_scr[...]


def dilated_prompt(qkv, bias, g, dil, batch, seq, qb):
    assert seq % (dil * BAND * qb) == 0
    sub = seq // dil
    span = dil * BAND * qb
    cur = lambda part: pl.BlockSpec((None, dil, qb * BAND, BRANCH_W), lambda b, i: (b, 0, i, part))
    prev = lambda part: pl.BlockSpec((None, dil, BAND, BRANCH_W),
                                     lambda b, i: (b, 0, jnp.maximum(i * qb - 1, 0), part))
    bias_spec = pl.BlockSpec((N_HEADS, BAND, 2 * BAND), lambda b, i: (0, 0, 0))
    scratch = [] if dil == 1 else [pltpu.VMEM((N_HEADS, span, HEAD_DIM), F32),
                                   pltpu.VMEM((span, N_HEADS * LSE_LANES), F32)]
    o, lse = pl.pallas_call(
        functools.partial(_dil_prompt_kernel, dil=dil, qb=qb),
        grid=(batch, sub // (BAND * qb)),
        in_specs=[cur(0), prev(1), cur(1), prev(2), cur(2), bias_spec],
        out_specs=[pl.BlockSpec((None, span, BRANCH_W), lambda b, i: (b, i, 0)),
                   pl.BlockSpec((None, span, N_HEADS * LSE_LANES), lambda b, i: (b, i, 0))],
        out_shape=[jax.ShapeDtypeStruct((batch, seq, BRANCH_W), BF16),
                   jax.ShapeDtypeStruct((batch, seq, N_HEADS * LSE_LANES), F32)],
        scratch_shapes=scratch,
        compiler_params=_cparams("parallel", "arbitrary"),
        name=f"dilated_prompt_g{g}",
    )(qkv, qkv, qkv, qkv, qkv, bias)
    return o.reshape(batch * seq, BRANCH_W), lse.reshape(batch * seq, N_HEADS * LSE_LANES)


def _dil_sample_kernel(q_ref, kn_ref, vn_ref, cache_ref, bc_ref, bn_ref, shifted_ref, o_ref, lse_ref, newrows_ref,
                       *, t_new):
    del shifted_ref
    scale = 1.0 / math.sqrt(HEAD_DIM)
    bb, n_grp, grp_rows, _ = cache_ref.shape
    n_keys = n_grp * grp_rows // KV_ROWS
    rows_of = lambda b, first: cache_ref[b, :, pl.ds(first, grp_rows // KV_ROWS, stride=KV_ROWS), :].reshape(
        n_keys, HEAD_DIM).astype(BF16)
    heads = [(b, h, slice(h * HEAD_DIM, (h + 1) * HEAD_DIM)) for b in range(bb) for h in range(N_HEADS)]
    logits = []
    for b, h, hs in heads:
        q = q_ref[b, :, hs]
        lc = lax.dot_general(q.astype(BF16), rows_of(b, h), NT_DIMS, preferred_element_type=F32) * scale + bc_ref[h]
        ln = lax.dot_general(q, kn_ref[b, :, hs], NT_DIMS, preferred_element_type=F32) * scale + bn_ref[h]
        logits.append((lc, ln))
    probs = []
    for lc, ln in logits:
        m = jnp.maximum(jnp.max(lc, axis=-1, keepdims=True), jnp.max(ln, axis=-1, keepdims=True))
        pc = jnp.exp(lc - m)
        pn = jnp.exp(ln - m)
        s = jnp.sum(pc, axis=-1, keepdims=True) + jnp.sum(pn, axis=-1, keepdims=True)
        probs.append((pc.astype(BF16), pn, s, m + jnp.log(s)))
    for (b, h, hs), (pc, pn, s, lse) in zip(heads, probs):
        kn = kn_ref[b, :, hs]
        vn = vn_ref[b, :, hs]
        o = (jnp.dot(pc, rows_of(b, N_HEADS + h), preferred_element_type=F32)
             + jnp.dot(pn, vn, preferred_element_type=F32))
        o_ref[b, :, hs] = o / s
        lse_ref[b, :, h * LSE_LANES:(h + 1) * LSE_LANES] = jnp.broadcast_to(lse, (t_new, LSE_LANES))
        newrows_ref[b, pl.ds(h, t_new, stride=KV_ROWS), :] = kn
        newrows_ref[b, pl.ds(N_HEADS + h, t_new, stride=KV_ROWS), :] = vn


def dilated_sample(qkv, cache, shifted, bias_cache, bias_new, g, dil, batch, t_new, bb):
    win = cache.shape[1] // KV_ROWS
    assert win % t_new == 0 and shifted.shape == cache.shape and batch % bb == 0
    period, take = (dil, t_new) if (dil > t_new and win % dil == 0) else (win, win)
    n_grp = win // period
    cache4 = cache.reshape(batch, n_grp, period * KV_ROWS, HEAD_DIM)
    bias_cache = bias_cache.reshape(N_HEADS, t_new, n_grp, period)[..., :take].reshape(N_HEADS, t_new, n_grp * take)
    blk = (bb, t_new, BRANCH_W)
    o, lse, newc = pl.pallas_call(
        functools.partial(_dil_sample_kernel, t_new=t_new),
        grid=(batch // bb,),
        in_specs=[pl.BlockSpec(blk, lambda b: (b, 0, 0)),
                  pl.BlockSpec(blk, lambda b: (b, 0, 1)),
                  pl.BlockSpec(blk, lambda b: (b, 0, 2)),
                  pl.BlockSpec((bb, n_grp, take * KV_ROWS, HEAD_DIM), lambda b: (b, 0, 0, 0)),
                  pl.BlockSpec((N_HEADS, t_new, n_grp * take), lambda b: (0, 0, 0)),
                  pl.BlockSpec((N_HEADS, t_new, t_new), lambda b: (0, 0, 0)),
                  pl.BlockSpec(memory_space=pl.ANY)],
        out_specs=[pl.BlockSpec(blk, lambda b: (b, 0, 0)),
                   pl.BlockSpec((bb, t_new, N_HEADS * LSE_LANES), lambda b: (b, 0, 0)),
                   pl.BlockSpec((bb, t_new * KV_ROWS, HEAD_DIM), lambda b: (b, win // t_new - 1, 0))],
        out_shape=[jax.ShapeDtypeStruct((batch, t_new, BRANCH_W), F32),
                   jax.ShapeDtypeStruct((batch, t_new, N_HEADS * LSE_LANES), F32),
                   jax.ShapeDtypeStruct(cache.shape, F32)],
        input_output_aliases={6: 2},
        compiler_params=_cparams("parallel"),
        name=f"dilated_sample_g{g}",
    )(qkv, qkv, qkv, cache4, bias_cache, bias_new, shifted)
    return o.reshape(batch * t_new, BRANCH_W), lse.reshape(batch * t_new, N_HEADS * LSE_LANES), newc


def _mem_attn_kernel(q_ref, kv_ref, o_ref, *, mem):
    scale = 1.0 / math.sqrt(HEAD_DIM)
    heads = [(b, h, slice(h * HEAD_DIM, (h + 1) * HEAD_DIM)) for b in range(q_ref.shape[0]) for h in range(N_HEADS)]
    rows_of = lambda b, first: kv_ref[b, pl.ds(first, mem, stride=KV_ROWS), :].astype(BF16)
    logits = [lax.dot_general(q_ref[b, :, hs].astype(BF16), rows_of(b, h), NT_DIMS, preferred_element_type=F32) * scale
              for b, h, hs in heads]
    probs = []
    for l in logits:
        half = l.shape[1] // 2
        m = jnp.max(jnp.maximum(l[:, :half], l[:, half:]), axis=-1, keepdims=True)
        p = jnp.exp(l - m)
        probs.append((p.astype(BF16), jnp.sum(p[:, :half] + p[:, half:], axis=-1, keepdims=True)))
    for (b, h, hs), (p, s) in zip(heads, probs):
        o = jnp.dot(p, rows_of(b, N_HEADS + h), preferred_element_type=F32)
        o_ref[b, :, hs] = (o / s).astype(o_ref.dtype)


def memory_attention(rec, mem_kv, batch, t, tq, bb, out_dtype):
    mem = mem_kv.shape[1] // KV_ROWS
    assert batch % bb == 0
    x = rec.reshape(batch, t, REC_COLS)
    o = pl.pallas_call(
        functools.partial(_mem_attn_kernel, mem=mem),
        grid=(batch // bb, t // tq),
        in_specs=[pl.BlockSpec((bb, tq, BRANCH_W), lambda b, i: (b, i, REC_MQ)),
                  pl.BlockSpec((bb, mem * KV_ROWS, HEAD_DIM), lambda b, i: (b, 0, 0))],
        out_specs=pl.BlockSpec((bb, tq, BRANCH_W), lambda b, i: (b, i, 0)),
        out_shape=jax.ShapeDtypeStruct((batch, t, BRANCH_W), out_dtype),
        compiler_params=_cparams("parallel", "parallel"),
        name="memory_attention",
    )(x, mem_kv)
    return o.reshape(batch * t, BRANCH_W)


def _hgrn_tables(chunk):
    c, b = chunk, HG_BASE
    t = np.arange(c)[:, None]
    u = np.arange(c)[None, :]
    masks = []
    size = 2 * b
    while size <= c:
        half = size // 2
        masks.append(((t // size) == (u // size)) & ((t % size) >= half) & ((u % size) < half))
        size *= 2
    masks.append(((t // b) == (u // b)) & (u <= t))
    erep = np.zeros((b * HEAD_DIM, c), np.float32)
    for s in range(b):
        erep[s * HEAD_DIM:(s + 1) * HEAD_DIM, np.arange(c) % b == s] = 1.0
    return (u <= t).astype(np.float32), erep, np.stack(masks).astype(np.float32)


def _shift_step(old_refs, new_refs, buf_refs, zero_ref, sem_in, sem_out, sem_zero, step, n_steps, drop):
    slot = step % 2
    other = 1 - slot

    def copies(g, chunk_idx, buf_slot):
        old, new, buf = old_refs[g], new_refs[g], buf_refs[g]
        rows_c = buf.shape[1]
        per_batch = (old.shape[1] - drop) // rows_c
        b = chunk_idx // per_batch
        lo = pl.multiple_of((chunk_idx % per_batch) * rows_c, 8)
        return (pltpu.make_async_copy(old.at[b, pl.ds(lo + drop, rows_c), :], buf.at[buf_slot], sem_in.at[g, buf_slot]),
                pltpu.make_async_copy(buf.at[buf_slot], new.at[b, pl.ds(lo, rows_c), :], sem_out.at[g, buf_slot]))

    def zero_copy(g):
        keep = old_refs[g].shape[1] - drop
        return pltpu.make_async_copy(zero_ref, new_refs[g].at[:, pl.ds(keep, drop), :], sem_zero.at[g])

    groups = range(len(old_refs))

    @pl.when(step == 0)
    def _():
        zero_ref[...] = jnp.zeros_like(zero_ref)
        for g in groups:
            zero_copy(g).start()
            copies(g, 0, 0)[0].start()

    for g in groups:
        copies(g, step, slot)[0].wait()

    @pl.when(step >= 1)
    def _():
        for g in groups:
            copies(g, step - 1, other)[1].wait()

    for g in groups:
        copies(g, step, slot)[1].start()

    @pl.when(step + 1 < n_steps)
    def _():
        for g in groups:
            copies(g, step + 1, other)[0].start()

    @pl.when(step == n_steps - 1)
    def _():
        for g in groups:
            copies(g, step, slot)[1].wait()
            zero_copy(g).wait()


def _hgrn_kernel(q_ref, v_ref, f_ref, gate_ref, lbl_ref, gn_ref, s0_ref, tri_ref, erep_ref, mask_ref, *rest,
                 chunk, rows_in, n_sub, n_levels, n_shift, shift_drop, n_steps):
    old_refs, rest = rest[:n_shift], rest[n_shift:]
    o_ref, s_out_ref = rest[:2]
    new_refs, rest = rest[2:2 + n_shift], rest[2 + n_shift:]
    st_ref = rest[0]
    c = chunk
    step = pl.program_id(1)
    if n_shift:
        buf_refs, (zero_ref, sem_in, sem_out, sem_zero) = rest[1:1 + n_shift], rest[1 + n_shift:]
        _shift_step(old_refs, new_refs, buf_refs, zero_ref, sem_in, sem_out, sem_zero,
                    pl.program_id(0) * pl.num_programs(1) + step, n_steps, shift_drop)

    @pl.when(step == 0)
    def _():
        for h in range(N_HEADS):
            st_ref[h] = s0_ref[h].T

    lbl = lbl_ref[...]
    e = jnp.exp(lbl - jnp.max(lbl, axis=0, keepdims=True))
    lb = e[0:1] / jnp.sum(e, axis=0, keepdims=True)
    tri = tri_ref[...]

    def rows(ref, ci):
        x = ref[ci * rows_in:(ci + 1) * rows_in, :].astype(F32)
        if rows_in < c:
            x = jnp.concatenate([x, jnp.zeros((c - rows_in, x.shape[1]), F32)], axis=0)
        return x

    def at_row(x, size, idx):
        x3 = x.reshape(c // size, size, x.shape[1])
        return jnp.broadcast_to(x3[:, idx:idx + 1, :], x3.shape).reshape(x.shape)

    chunks = range(n_sub)
    heads = [(h, slice(h * HEAD_DIM, (h + 1) * HEAD_DIM)) for h in range(N_HEADS)]

    q, v, kk, cum = [], [], [], []
    for ci in chunks:
        q.append(rows(q_ref, ci))
        v.append(rows(v_ref, ci).astype(BF16))
        f = lb + (1.0 - lb) * _sigmoid(rows(f_ref, ci))
        g = jnp.log(f)
        k_in = 1.0 - f
        if rows_in < c:
            live = lax.broadcasted_iota(jnp.int32, (c, 1), 0) < rows_in
            g = jnp.where(live, g, 0.0)
            k_in = jnp.where(live, k_in, 0.0)
        kk.append(k_in)
        g1 = g.astype(BF16)
        r1 = g - g1.astype(F32)
        g2 = r1.astype(BF16)
        g3 = (r1 - g2.astype(F32)).astype(BF16)
        cum.append(jnp.dot(tri, g1, preferred_element_type=F32) + jnp.dot(tri, g2, preferred_element_type=F32)
                   + jnp.dot(tri, g3, preferred_element_type=F32))

    q_inter, k_end, g_end, q_lvl, k_lvl, x_diag = [], [], [], [], [], []
    for ci in chunks:
        q_inter.append((q[ci] * jnp.exp(cum[ci])).astype(BF16))
        g_end.append(cum[ci][c - 1:c])
        k_end.append((kk[ci] * jnp.exp(g_end[ci] - cum[ci])).astype(BF16))
        ql, kl = [], []
        for l in range(n_levels):
            size = 2 * HG_BASE << l
            ref = at_row(cum[ci], size, size // 2 - 1)
            ql.append((q[ci] * jnp.exp(jnp.minimum(cum[ci] - ref, 0.0))).astype(BF16))
            kl.append((kk[ci] * jnp.exp(jnp.minimum(ref - cum[ci], 0.0))).astype(BF16))
        q_lvl.append(ql)
        k_lvl.append(kl)
        xd = []
        for s in range(HG_BASE):
            decay = jnp.exp(jnp.minimum(cum[ci] - at_row(cum[ci], HG_BASE, s), 0.0))
            xd.append((q[ci] * at_row(kk[ci], HG_BASE, s) * decay).astype(BF16))
        x_diag.append(xd)

    scores, state_in = [], []
    for ci in chunks:
        sc, si = [], []
        for h, hs in heads:
            s_h = mask_ref[n_levels] * jnp.dot(jnp.concatenate([x[:, hs] for x in x_diag[ci]], axis=1),
                                               erep_ref[...], preferred_element_type=F32)
            for l in range(n_levels):
                s_h += mask_ref[l] * lax.dot_general(q_lvl[ci][l][:, hs], k_lvl[ci][l][:, hs], NT_DIMS,
                                                     preferred_element_type=F32)
            sc.append(s_h.astype(BF16))
            si.append(lax.dot_general(v[ci][:, hs], k_end[ci][:, hs], TN_DIMS, preferred_element_type=F32))
        scores.append(sc)
        state_in.append(si)

    states = []
    for h, hs in heads:
        st = st_ref[h]
        per_chunk = []
        for ci in chunks:
            per_chunk.append(st.astype(BF16))
            st = st * jnp.exp(g_end[ci][:, hs]) + state_in[ci][h]
        st_ref[h] = st
        states.append(per_chunk)
    for ci in chunks:
        gate = _sigmoid(rows(gate_ref, ci))
        for h, hs in heads:
            o = (lax.dot_general(q_inter[ci][:, hs], states[h][ci], NT_DIMS, preferred_element_type=F32)
                 + jnp.dot(scores[ci][h], v[ci][:, hs], preferred_element_type=F32))
            y = _rms(o, gn_ref[...]) * gate[:, hs]
            o_ref[ci * rows_in:(ci + 1) * rows_in, hs] = y[0:rows_in].astype(o_ref.dtype)

    @pl.when(step == pl.num_programs(1) - 1)
    def _():
        for h in range(N_HEADS):
            s_out_ref[h] = st_ref[h].T


def hgrn2(rec, gates, lb_logits, hg_norm, state0, batch, t, chunk, n_sub, out_dtype, shift=(), shift_drop=0):
    rows_in = min(chunk, t)
    step_rows = rows_in * n_sub
    assert t % step_rows == 0 and chunk % HG_BASE == 0 and (n_sub == 1 or rows_in == chunk)
    n_steps = batch * (t // step_rows)
    n_shift = len(shift)
    any_spec = pl.BlockSpec(memory_space=pl.ANY)
    shift_scratch = []
    if n_shift:
        sb = shift[0].shape[0]
        assert n_steps % sb == 0 and all(s.shape[0] == sb and s.shape[2] == HEAD_DIM for s in shift)
        per_batch = n_steps // sb
        chunk_rows = [(s.shape[1] - shift_drop) // per_batch for s in shift]
        assert all(r % 8 == 0 and r * per_batch == s.shape[1] - shift_drop for r, s in zip(chunk_rows, shift))
        shift_scratch = ([pltpu.VMEM((2, r, HEAD_DIM), F32) for r in chunk_rows]
                         + [pltpu.VMEM((sb, shift_drop, HEAD_DIM), F32), pltpu.SemaphoreType.DMA((n_shift, 2)),
                            pltpu.SemaphoreType.DMA((n_shift, 2)), pltpu.SemaphoreType.DMA((n_shift,))])
    tri, erep, masks = _hgrn_tables(chunk)
    n_levels = masks.shape[0] - 1
    a = rec.reshape(batch, t, REC_COLS)
    gt = gates.reshape(batch, t, G_COLS)
    blk = (None, step_rows, BRANCH_W)
    full = lambda arr: pl.BlockSpec(arr.shape, lambda b, i: (0,) * arr.ndim)
    st_spec = pl.BlockSpec((None, N_HEADS, HEAD_DIM, HEAD_DIM), lambda b, i: (b, 0, 0, 0))
    tri = jnp.asarray(tri, BF16)
    erep = jnp.asarray(erep, BF16)
    masks = jnp.asarray(masks, F32)
    lbl = lb_logits.astype(F32)
    gn = hg_norm.reshape(1, HEAD_DIM).astype(F32)
    o, s_fin, *shifted = pl.pallas_call(
        functools.partial(_hgrn_kernel, chunk=chunk, rows_in=rows_in, n_sub=n_sub, n_levels=n_levels,
                          n_shift=n_shift, shift_drop=shift_drop, n_steps=n_steps),
        grid=(batch, t // step_rows),
        in_specs=[pl.BlockSpec(blk, lambda b, i: (b, i, REC_HQ)),
                  pl.BlockSpec(blk, lambda b, i: (b, i, REC_HI)),
                  pl.BlockSpec(blk, lambda b, i: (b, i, 0)),
                  pl.BlockSpec(blk, lambda b, i: (b, i, 1)),
                  full(lbl), full(gn), st_spec, full(tri), full(erep), full(masks)] + [any_spec] * n_shift,
        out_specs=[pl.BlockSpec(blk, lambda b, i: (b, i, 0)), st_spec] + [any_spec] * n_shift,
        out_shape=[jax.ShapeDtypeStruct((batch, t, BRANCH_W), out_dtype),
                   jax.ShapeDtypeStruct(state0.shape, F32)] + [jax.ShapeDtypeStruct(s.shape, F32) for s in shift],
        scratch_shapes=[pltpu.VMEM((N_HEADS, HEAD_DIM, HEAD_DIM), F32)] + shift_scratch,
        compiler_params=_cparams("arbitrary", "arbitrary"),
        name="hgrn2",
    )(a, a, gt, gt, lbl, gn, state0, tri, erep, masks, *shift)
    return o.reshape(batch * t, BRANCH_W), s_fin, shifted


def _merge_kernel(x_ref, o1_ref, o2_ref, o3_ref, l1_ref, l2_ref, l3_ref, hg_ref, mem_ref, gpre_ref,
                  wga_ref, wgh_ref, wgm_ref, bga_ref, bgh_ref, bgm_ref,
                  wa_ref, wh_ref, wm_ref, wo_ref, gain_ref, out_ref):
    tm = x_ref.shape[0]
    x = x_ref[...]
    hn = _rms(x, gpre_ref[...]).astype(BF16)
    gate_logits = [jnp.dot(hn, w_ref[...], preferred_element_type=F32) + b_ref[...]
                   for w_ref, b_ref in ((wga_ref, bga_ref), (wgh_ref, bgh_ref), (wgm_ref, bgm_ref))]
    from_hg = jnp.dot(hg_ref[...].astype(BF16), wh_ref[...], preferred_element_type=F32)
    from_mem = jnp.dot(mem_ref[...].astype(BF16), wm_ref[...], preferred_element_type=F32)
    l1, l2, l3 = l1_ref[...], l2_ref[...], l3_ref[...]
    m = jnp.maximum(jnp.maximum(l1, l2), l3)
    e1, e2, e3 = jnp.exp(l1 - m), jnp.exp(l2 - m), jnp.exp(l3 - m)
    den = e1 + e2 + e3
    w1, w2, w3 = e1 / den, e2 / den, e3 / den
    att = []
    for h in range(N_HEADS):
        hs = slice(h * HEAD_DIM, (h + 1) * HEAD_DIM)
        col = slice(h * LSE_LANES, h * LSE_LANES + 1)
        bc = lambda w: jnp.broadcast_to(w[:, col], (tm, HEAD_DIM))
        att.append((bc(w1) * o1_ref[:, hs].astype(F32) + bc(w2) * o2_ref[:, hs].astype(F32)
                    + bc(w3) * o3_ref[:, hs].astype(F32)).astype(BF16))
    att = jnp.concatenate(att, axis=1)
    ga, gh, gm = (_sigmoid(z) for z in gate_logits)
    merged = ga * jnp.dot(att, wa_ref[...], preferred_element_type=F32) + gh * from_hg + gm * from_mem
    y = jnp.dot(merged.astype(BF16), wo_ref[...], preferred_element_type=F32)
    out_ref[...] = x + _rms(y, gain_ref[...])


def merge(x, o_groups, lse_groups, hg_o, mem_o, gain_pre, w_in, b_in, w_att, w_hg, w_mem, w_out, gain, tm):
    m, d = x.shape
    assert m % tm == 0 and GATE_MERGE_BLOCKS[0] * d == 14 * BRANCH_W
    row = lambda w: pl.BlockSpec((tm, w), lambda i: (i, 0))
    resident = lambda shape, j=0: pl.BlockSpec(shape, lambda i: (0, j), pipeline_mode=pl.Buffered(1))
    return pl.pallas_call(
        _merge_kernel,
        grid=(m // tm,),
        in_specs=[row(d), row(BRANCH_W), row(BRANCH_W), row(BRANCH_W),
                  row(N_HEADS * LSE_LANES), row(N_HEADS * LSE_LANES), row(N_HEADS * LSE_LANES),
                  row(BRANCH_W), row(BRANCH_W), resident((1, d))]
        + [resident((d, d), j) for j in GATE_MERGE_BLOCKS] + [resident((1, d), j) for j in GATE_MERGE_BLOCKS]
        + [resident(w_att.shape), resident(w_hg.shape), resident(w_mem.shape), resident(w_out.shape),
           resident((1, d))],
        out_specs=row(d),
        out_shape=jax.ShapeDtypeStruct((m, d), F32),
        compiler_params=_cparams("parallel"),
        name="merge",
    )(x, *o_groups, *lse_groups, hg_o, mem_o, gain_pre.reshape(1, d), w_in, w_in, w_in, b_in, b_in, b_in,
      w_att, w_hg, w_mem, w_out, gain.reshape(1, d))


def _ffn_kernel(x_ref, gpre_ref, wa_ref, wb_ref, cw_ref, cb_ref, wd_ref, gpost_ref, cbuf_ref,
                out_ref, tail_ref, carry_ref, *, t_seq, tiles_per_seq, tf):
    tm = x_ref.shape[0]
    dff = wa_ref.shape[1]
    x = x_ref[...]
    hn = _rms(x, gpre_ref[...]).astype(BF16)

    if tiles_per_seq >= 1:
        @pl.when((pl.program_id(0) % tiles_per_seq) == 0)
        def _():
            carry_ref[6:8, :] = cbuf_ref[0]

        t_idx = lax.broadcasted_iota(jnp.int32, (tm, 1), 0)
    else:
        n_seq = tm // t_seq
        t_idx = lax.broadcasted_iota(jnp.int32, (tm, 1), 0) % t_seq

    n_chunks = dff // tf

    def up_proj(j):
        cols = slice(j * tf, (j + 1) * tf)
        return (jnp.dot(hn, wa_ref[:, cols], preferred_element_type=F32),
                jnp.dot(hn, wb_ref[:, cols], preferred_element_type=F32))

    acc = jnp.zeros(x.shape, F32)
    ahead = up_proj(0)
    for j in range(n_chunks):
        cols = slice(j * tf, (j + 1) * tf)
        a, up = ahead
        if j + 1 < n_chunks:
            ahead = up_proj(j + 1)
        if tiles_per_seq >= 1:
            prev1 = carry_ref[7:8, cols]
            prev2 = carry_ref[6:7, cols]
            carry_ref[:, cols] = a[tm - 8:tm]
            tail_ref[0, :, cols] = a[tm - 8:tm]
        else:
            prev1 = jnp.broadcast_to(cbuf_ref[:, 1:2, cols], (n_seq, t_seq, tf)).reshape(tm, tf)
            prev2 = jnp.broadcast_to(cbuf_ref[:, 0:1, cols], (n_seq, t_seq, tf)).reshape(tm, tf)
            tail_ref[:, :, cols] = a.reshape(n_seq, t_seq, tf)
        a1 = jnp.where(t_idx >= 1, pltpu.roll(a, 1, 0), prev1)
        a2 = jnp.where(t_idx >= 2, pltpu.roll(a, 2, 0), jnp.where(t_idx == 1, prev1, prev2))
        conv = cb_ref[:, cols] + a2 * cw_ref[0:1, cols] + a1 * cw_ref[1:2, cols] + a * cw_ref[2:3, cols]
        act = conv * _sigmoid(conv) * up
        acc += jnp.dot(act.astype(BF16), wd_ref[cols, :], preferred_element_type=F32)

    out_ref[...] = x + _rms(acc, gpost_ref[...])


def conv_ffn(x, conv_buf, g_pre, w_a, w_b, conv_w, conv_b, w_d, g_post, t_seq, tm, tf):
    m, d = x.shape
    dff = w_a.shape[1]
    n_seq = m // t_seq
    assert m % tm == 0 and dff % tf == 0 and t_seq >= 8
    if tm <= t_seq:
        assert t_seq % tm == 0
        tiles_per_seq = t_seq // tm
        seq_blk = 1
        seq_idx = lambda i: (i // tiles_per_seq, 0, 0)
    else:
        assert tm % t_seq == 0 and t_seq == 8
        tiles_per_seq = 0
        seq_blk = tm // t_seq
        seq_idx = lambda i: (i, 0, 0)
    n_tail = (m // tm) * seq_blk
    resident = lambda shape: pl.BlockSpec(shape, lambda i: (0, 0), pipeline_mode=pl.Buffered(1))
    y, tail = pl.pallas_call(
        functools.partial(_ffn_kernel, t_seq=t_seq, tiles_per_seq=tiles_per_seq, tf=tf),
        grid=(m // tm,),
        in_specs=[pl.BlockSpec((tm, d), lambda i: (i, 0)), resident((1, d)),
                  resident((d, dff)), resident((d, dff)), resident((3, dff)), resident((1, dff)),
                  resident((dff, d)), resident((1, d)),
                  pl.BlockSpec((seq_blk, 2, dff), seq_idx)],
        out_specs=[pl.BlockSpec((tm, d), lambda i: (i, 0)),
                   pl.BlockSpec((seq_blk, 8, dff), lambda i: (i, 0, 0))],
        out_shape=[jax.ShapeDtypeStruct((m, d), F32), jax.ShapeDtypeStruct((n_tail, 8, dff), F32)],
        scratch_shapes=[pltpu.VMEM((8, dff), F32)],
        compiler_params=_cparams("arbitrary"),
        name="conv_ffn",
    )(x, g_pre.reshape(1, d), w_a, w_b, conv_w, conv_b.reshape(1, dff), w_d, g_post.reshape(1, d), conv_buf)
    if tiles_per_seq >= 1:
        tail = tail.reshape(n_seq, tiles_per_seq, 8, dff)[:, -1]
    return y, tail


def _rel_buckets(dil):
    max_exact = REL_BUCKETS // 2
    dist = np.arange(N_LAGS + 1, dtype=np.int32) * dil
    d = np.maximum(dist, 1).astype(np.float32)
    large = max_exact + (np.log(d / np.float32(max_exact)) / np.float32(math.log(REL_MAX_DIST / max_exact))
                         * np.float32(REL_BUCKETS - max_exact)).astype(np.int32)
    large = np.minimum(large, REL_BUCKETS - 1)
    return np.where(dist < max_exact, dist, large)


def _lag_bias(rel_bias, g, dil):
    onehot = _rel_buckets(dil)[:, None] == np.arange(REL_BUCKETS)[None, :]
    table = jnp.sum(jnp.where(onehot[:, :, None], rel_bias.astype(F32)[None], 0.0), axis=1)
    return table[:, g * N_HEADS:(g + 1) * N_HEADS].T


def _neg(h, n):
    return jnp.full((h, n), NEG_INF, F32)


def _prompt_bias(lag_bias):
    h = lag_bias.shape[0]
    v = jnp.concatenate([lag_bias[:, ::-1], _neg(h, BAND)], axis=1)
    period = 2 * BAND + 1
    assert v.shape[1] == period
    return jnp.tile(v, (1, BAND))[:, :BAND * 2 * BAND].reshape(h, BAND, 2 * BAND)


def _sample_bias(lag_bias, dil, n_past, t_new):
    h = lag_bias.shape[0]
    n = n_past + t_new
    per_dist = jnp.repeat(lag_bias, dil, axis=1)
    per_dist = jnp.concatenate([per_dist, _neg(h, max(n - per_dist.shape[1], 0))], axis=1)[:, :n]
    per_dist = jnp.where((np.arange(n) % dil == 0)[None], per_dist, NEG_INF)
    rev = per_dist[:, ::-1]
    cached = jnp.stack([rev[:, t_new - 1 - t:t_new - 1 - t + n_past] for t in range(t_new)], axis=1)
    padded = jnp.concatenate([_neg(h, t_new - 1), per_dist[:, :t_new]], axis=1)
    new = jnp.stack([padded[:, t:t + t_new][:, ::-1] for t in range(t_new)], axis=1)
    return cached, new


def _layer(x, batch, t, weights, lag_biases, lb_logits, win_caches, hg_state0, conv_buf0, mem_kv,
           *, prompt, tm_proj, tm_merge, tm_ffn, tq_mem, chunk, n_sub, shifted=None, side_shift=(), side_shift_drop=0):
    (w_in16, b_in, gain_pre, gain_post, hg_norm, w_att, w_hg, w_mem, w_out,
     gain_fpre, gain_fpost, w_fa, w_fb, conv_w, conv_b, w_fd) = weights
    act_dtype = BF16 if prompt else F32
    b_in = b_in.reshape(1, -1)
    if prompt:
        qkv = [proj(x, gain_pre, w_in16, b_in, QKV_BLOCKS(g), batch, t, dil, act_dtype, tm_proj)
               for g, (_, dil) in enumerate(ATT_GROUPS)]
        qkv[0] = qkv[0].reshape(batch, 1, t, QKV_COLS)
    else:
        qkv = [proj(x, gain_pre, w_in16, b_in, QKV_BLOCKS(g), 1, batch * t, 1, act_dtype, tm_proj)
               .reshape(batch, t, QKV_COLS) for g in range(len(ATT_GROUPS))]
    rec = proj(x, gain_pre, w_in16, b_in, REC_BLOCKS, 1, batch * t, 1, act_dtype, tm_proj)
    gates = proj(x, gain_pre, w_in16, b_in, GATE_BLOCKS, 1, batch * t, 1, F32, tm_proj)

    o_groups, lse_groups, new_caches = [], [], []
    for g, (win, dil) in enumerate(ATT_GROUPS):
        if prompt:
            o, lse = dilated_prompt(qkv[g], _prompt_bias(lag_biases[g]), g, dil, batch, t, 4 if dil == 1 else 1)
        else:
            cache = win_caches[g]
            n_past = cache.shape[1] // KV_ROWS
            assert n_past >= win
            bcache, bnew = _sample_bias(lag_biases[g], dil, n_past, t)
            bb = math.gcd(batch, max(1, SAMPLE_STEP_BYTES // (cache.shape[1] * HEAD_DIM * 4)))
            o, lse, newc = dilated_sample(qkv[g], cache, shifted[g], bcache, bnew, g, dil, batch, t, bb)
            new_caches.append(newc)
        o_groups.append(o)
        lse_groups.append(lse)

    hg_o, hg_state, shifted_out = hgrn2(rec, gates, lb_logits, hg_norm, hg_state0, batch, t, chunk, n_sub, act_dtype,
                                        side_shift, side_shift_drop)
    mem_bb = 1 if prompt else math.gcd(batch, max(1, SAMPLE_STEP_BYTES // (mem_kv.shape[1] * HEAD_DIM * 4)))
    mem_o = memory_attention(rec, mem_kv, batch, t, tq_mem, mem_bb, act_dtype)
    x1 = merge(x, o_groups, lse_groups, hg_o, mem_o, gain_pre, w_in16, b_in, w_att, w_hg, w_mem, w_out, gain_post,
               tm_merge)
    y, tail = conv_ffn(x1, conv_buf0, gain_fpre, w_fa, w_fb, conv_w, conv_b, w_fd, gain_fpost, t, tm_ffn, 2048)
    return y, qkv, new_caches, hg_state, tail[:, 6:8, :], shifted_out


def kernel(x_prompt, x_sample, mem_prompt, cache_win1_kv, cache_win2_kv, cache_win3_kv, cache_mem_kv, state_hgrn, state_ffn_conv, rel_bias, hg_lb_logits, norm_mix_pre, norm_mix_post, w_in, b_in, hg_norm, mem_norm, w_mem_kv, w_br_att, w_br_hg, w_br_mem, w_out, norm_ffn_pre, norm_ffn_post, w_ffn_a, w_ffn_b, ffn_conv_w, ffn_conv_b, w_ffn_d):
    depth = w_in.shape[0]
    assert depth == 1
    bsz, seq, d = x_prompt.shape
    dbsz, dseq, _ = x_sample.shape
    mem_tokens = mem_prompt.shape[1]
    dff = w_ffn_a.shape[2]
    layer = 0

    assert w_in.shape[2] == W_IN_BLOCKS * BRANCH_W
    weights = (w_in[layer].astype(BF16), b_in[layer],
               norm_mix_pre[layer], norm_mix_post[layer], hg_norm[layer],
               w_br_att[layer].astype(BF16), w_br_hg[layer].astype(BF16), w_br_mem[layer].astype(BF16),
               w_out[layer].astype(BF16), norm_ffn_pre[layer], norm_ffn_post[layer],
               w_ffn_a[layer].astype(BF16), w_ffn_b[layer].astype(BF16), ffn_conv_w[layer], ffn_conv_b[layer],
               w_ffn_d[layer].astype(BF16))
    lag_biases = [_lag_bias(rel_bias, g, dil) for g, (_, dil) in enumerate(ATT_GROUPS)]

    mem_kv = norm_matmul(mem_prompt.reshape(bsz * mem_tokens, d), mem_norm[layer], w_mem_kv[layer].astype(BF16),
                         jnp.zeros((2 * BRANCH_W,), F32), F32, bsz * mem_tokens, 1024)
    as_rows = lambda c: c[layer].reshape(dbsz, c.shape[2] * KV_ROWS, HEAD_DIM)
    win_caches = [as_rows(c) for c in (cache_win1_kv, cache_win2_kv, cache_win3_kv)]
    yp, qkv_p, _, hg_p, conv_p, shifted = _layer(
        x_prompt.reshape(bsz * seq, d), bsz, seq, weights, lag_biases, hg_lb_logits, None,
        jnp.zeros((bsz, N_HEADS, HEAD_DIM, HEAD_DIM), F32), jnp.zeros((bsz, 2, dff), F32),
        mem_kv.reshape(bsz, mem_tokens * KV_ROWS, HEAD_DIM),
        prompt=True, tm_proj=1024, tm_merge=512, tm_ffn=512, tq_mem=512, chunk=64, n_sub=4,
        side_shift=win_caches, side_shift_drop=dseq * KV_ROWS)
    p_win = []
    for g, (win, dil) in enumerate(ATT_GROUPS):
        n = min(win, seq)
        assert n % dil == 0
        tail = qkv_p[g][:, :, (seq - n) // dil:, BRANCH_W:]
        tail = jnp.swapaxes(tail, 1, 2).astype(F32)
        p_win.append(tail.reshape(1, bsz, n, 2, N_HEADS, HEAD_DIM))

    ys, _, new_caches, hg_s, conv_s, _ = _layer(
        x_sample.reshape(dbsz * dseq, d), dbsz, dseq, weights, lag_biases, hg_lb_logits, win_caches,
        state_hgrn[layer], state_ffn_conv[layer], as_rows(cache_mem_kv),
        prompt=False, tm_proj=dbsz * dseq, tm_merge=dbsz * dseq, tm_ffn=dbsz * dseq, tq_mem=dseq, chunk=16, n_sub=1,
        shifted=shifted)
    s_win = [c.reshape(1, dbsz, c.shape[1] // KV_ROWS, 2, N_HEADS, HEAD_DIM) for c in new_caches]

    return (yp.reshape(bsz, seq, d), ys.reshape(dbsz, dseq, d),
            p_win[0], p_win[1], p_win[2],
            hg_p[None], conv_p[None], mem_kv.reshape(1, bsz, mem_tokens, 2, N_HEADS, HEAD_DIM),
            s_win[0], s_win[1], s_win[2],
            hg_s[None], conv_s[None])
```

```python
import functools
import math

import numpy as np
import jax
import jax.numpy as jnp
from jax import lax
from jax.experimental import pallas as pl
from jax.experimental.pallas import tpu as pltpu

F32 = jnp.float32
BF16 = jnp.bfloat16

NORM_EPS = 1e-6
NEG_INF = -1e30
HEAD_DIM = 128
N_HEADS = 4
BRANCH_W = N_HEADS * HEAD_DIM
N_LAGS = 128
BAND = 128
ATT_GROUPS = ((128, 1), (512, 4), (2048, 16))
REL_BUCKETS = 32
REL_MAX_DIST = 2048
LSE_LANES = 32
HG_BASE = 8
VMEM_LIMIT = 56 * 1024 * 1024
SAMPLE_STEP_BYTES = 8 * 1024 * 1024
DEINTERLEAVE_STRIDE = 4

NT_DIMS = (((1,), (1,)), ((), ()))
TN_DIMS = (((0,), (0,)), ((), ()))

W_IN_BLOCKS = 20
QKV_BLOCKS = lambda g: (g, 3 + g, 6 + g)
REC_BLOCKS = (9, 11, 13)
GATE_BLOCKS = (10, 12)
GATE_MERGE_BLOCKS = (7, 8, 9)
QKV_COLS = 3 * BRANCH_W
REC_COLS = 3 * BRANCH_W
REC_HQ, REC_HI, REC_MQ = 0, 1, 2
G_COLS = 2 * BRANCH_W
KV_ROWS = 2 * N_HEADS


def _cparams(*sem):
    return pltpu.CompilerParams(dimension_semantics=sem, vmem_limit_bytes=VMEM_LIMIT)


def _rms(x, gain):
    return x * lax.rsqrt(jnp.mean(x * x, axis=-1, keepdims=True) + NORM_EPS) * gain


def _sigmoid(x):
    return 1.0 / (1.0 + jnp.exp(-x))


def _norm_matmul_kernel(x_ref, g_ref, w_ref, b_ref, o_ref, hn_ref):
    @pl.when(pl.program_id(1) == 0)
    def _():
        hn_ref[...] = _rms(x_ref[...], g_ref[...]).astype(BF16)

    acc = jnp.dot(hn_ref[...], w_ref[...], preferred_element_type=F32)
    o_ref[...] = (acc + b_ref[...]).astype(o_ref.dtype)


def norm_matmul(x, gain, w, bias, out_dtype, tm, tn):
    m, k = x.shape
    n = w.shape[1]
    assert m % tm == 0 and n % tn == 0
    return pl.pallas_call(
        _norm_matmul_kernel,
        grid=(m // tm, n // tn),
        in_specs=[
            pl.BlockSpec((tm, k), lambda i, j: (i, 0)),
            pl.BlockSpec((1, k), lambda i, j: (0, 0)),
            pl.BlockSpec((k, tn), lambda i, j: (0, j)),
            pl.BlockSpec((1, tn), lambda i, j: (0, j)),
        ],
        out_specs=pl.BlockSpec((tm, tn), lambda i, j: (i, j)),
        out_shape=jax.ShapeDtypeStruct((m, n), out_dtype),
        scratch_shapes=[pltpu.VMEM((tm, k), BF16)],
        compiler_params=_cparams("parallel", "arbitrary"),
        name="norm_matmul",
    )(x, gain.reshape(1, k), w, bias.reshape(1, n))


def _proj_kernel(x_ref, g_ref, *refs, n_blk, dil):
    w_refs, b_refs, o_ref = refs[:n_blk], refs[n_blk:2 * n_blk], refs[2 * n_blk]
    tm = x_ref.shape[0]
    hn = _rms(x_ref[...], g_ref[...]).astype(BF16)
    for n in range(n_blk):
        acc = jnp.dot(hn, w_refs[n][...], preferred_element_type=F32) + b_refs[n][...]
        if dil == 1:
            o_ref[:, n * BRANCH_W:(n + 1) * BRANCH_W] = acc.astype(o_ref.dtype)
            continue
        scr_ref = refs[2 * n_blk + 1]
        for c in range(N_HEADS):
            scr_ref[n * N_HEADS + c] = acc[:, c * HEAD_DIM:(c + 1) * HEAD_DIM]
        for c in range(N_HEADS):
            slab = n * N_HEADS + c
            lo = n * BRANCH_W + c * HEAD_DIM
            if dil <= DEINTERLEAVE_STRIDE:
                for r in range(dil):
                    o_ref[r, :, lo:lo + HEAD_DIM] = (
                        scr_ref[slab, pl.ds(r, tm // dil, stride=dil), :].astype(o_ref.dtype))
                continue
            s1, s2 = DEINTERLEAVE_STRIDE, dil // DEINTERLEAVE_STRIDE
            mid_ref = refs[2 * n_blk + 2]
            for a in range(s1):
                mid_ref[slab, a] = scr_ref[slab, pl.ds(a, tm // s1, stride=s1), :]
            for a in range(s1):
                for b in range(s2):
                    o_ref[a + s1 * b, :, lo:lo + HEAD_DIM] = (
                        mid_ref[slab, a, pl.ds(b, tm // dil, stride=s2), :].astype(o_ref.dtype))


def proj(x, gain, w, bias, blocks, batch, seq, dil, out_dtype, tm):
    m, k = x.shape
    n_blk = len(blocks)
    cols = n_blk * BRANCH_W
    assert m == batch * seq and seq % tm == 0 and tm % dil == 0
    tiles_per_b = seq // tm
    w_specs = [pl.BlockSpec((k, BRANCH_W), lambda i, c=c: (0, c)) for c in blocks]
    b_specs = [pl.BlockSpec((1, BRANCH_W), lambda i, c=c: (0, c)) for c in blocks]
    if dil == 1:
        out_spec = pl.BlockSpec((tm, cols), lambda i: (i, 0))
        out_shape = jax.ShapeDtypeStruct((m, cols), out_dtype)
        scratch = []
    else:
        out_spec = pl.BlockSpec((None, dil, tm // dil, cols), lambda i: (i // tiles_per_b, 0, i % tiles_per_b, 0))
        out_shape = jax.ShapeDtypeStruct((batch, dil, seq // dil, cols), out_dtype)
        scratch = [pltpu.VMEM((n_blk * N_HEADS, tm, HEAD_DIM), F32)]
        if dil > DEINTERLEAVE_STRIDE:
            assert dil % DEINTERLEAVE_STRIDE == 0 and dil // DEINTERLEAVE_STRIDE <= DEINTERLEAVE_STRIDE
            scratch.append(pltpu.VMEM((n_blk * N_HEADS, DEINTERLEAVE_STRIDE, tm // DEINTERLEAVE_STRIDE, HEAD_DIM), F32))
    return pl.pallas_call(
        functools.partial(_proj_kernel, n_blk=n_blk, dil=dil),
        grid=(m // tm,),
        in_specs=[pl.BlockSpec((tm, k), lambda i: (i, 0)), pl.BlockSpec((1, k), lambda i: (0, 0))] + w_specs + b_specs,
        out_specs=out_spec,
        out_shape=out_shape,
        scratch_shapes=scratch,
        compiler_params=_cparams("parallel"),
        name="proj",
    )(x, gain.reshape(1, k), *([w] * n_blk), *([bias] * n_blk))


def _dil_prompt_kernel(q_ref, kp_ref, kc_ref, vp_ref, vc_ref, bias_ref, o_ref, lse_ref, *scratch, dil, qb):
    scale = 1.0 / math.sqrt(HEAD_DIM)
    no_prev = (pl.program_id(1) == 0) & (lax.broadcasted_iota(jnp.int32, (1, 2 * BAND), 1) < BAND)

    def band_blocks(blocks):
        items = [(r, j, h, slice(j * BAND, (j + 1) * BAND), slice(h * HEAD_DIM, (h + 1) * HEAD_DIM))
                 for r, j in blocks for h in range(N_HEADS)]

        def keys(cur_ref, prev_ref, r, j, rows, hs):
            if j == 0:
                return jnp.concatenate([prev_ref[r, :, hs], cur_ref[r, rows, hs]], axis=0)
            return cur_ref[r, (j - 1) * BAND:(j + 1) * BAND, hs]

        logits = []
        for r, j, h, rows, hs in items:
            l = lax.dot_general(q_ref[r, rows, hs], keys(kc_ref, kp_ref, r, j, rows, hs), NT_DIMS,
                                preferred_element_type=F32) * scale + bias_ref[h]
            logits.append(jnp.where(no_prev, NEG_INF, l) if j == 0 else l)
        probs = []
        for l in logits:
            m = jnp.max(jnp.maximum(l[:, :BAND], l[:, BAND:]), axis=-1, keepdims=True)
            p = jnp.exp(l - m)
            s = jnp.sum(p[:, :BAND] + p[:, BAND:], axis=-1, keepdims=True)
            probs.append((p.astype(BF16), s, jnp.broadcast_to(m + jnp.log(s), (BAND, LSE_LANES))))
        outs = [jnp.dot(p, keys(vc_ref, vp_ref, r, j, rows, hs), preferred_element_type=F32) / s
                for (r, j, h, rows, hs), (p, s, _) in zip(items, probs)]
        return [(outs[N_HEADS * n:N_HEADS * (n + 1)],
                 jnp.concatenate([lse for _, _, lse in probs[N_HEADS * n:N_HEADS * (n + 1)]], axis=1))
                for n in range(len(blocks))]

    if dil == 1:
        for j, (outs, lse) in enumerate(band_blocks([(0, j) for j in range(qb)])):
            for h in range(N_HEADS):
                o_ref[j * BAND:(j + 1) * BAND, h * HEAD_DIM:(h + 1) * HEAD_DIM] = outs[h].astype(o_ref.dtype)
            lse_ref[j * BAND:(j + 1) * BAND, :] = lse
        return

    o_scr, lse_scr = scratch
    per_iter = min(dil, 4)

    def body(it, carry):
        blocks = [(it * per_iter + rr, j) for rr in range(per_iter) for j in range(qb)]
        for (r, j), (outs, lse) in zip(blocks, band_blocks(blocks)):
            for h in range(N_HEADS):
                o_scr[h, pl.ds(r + j * dil * BAND, BAND, stride=dil), :] = outs[h]
            lse_scr[pl.ds(r + j * dil * BAND, BAND, stride=dil), :] = lse
        return carry

    lax.fori_loop(0, dil // per_iter, body, 0)
    for h in range(N_HEADS):
        o_ref[:, h * HEAD_DIM:(h + 1) * HEAD_DIM] = o_scr[h].astype(o_ref.dtype)
    lse_ref[...] = lse_scr[...]


def dilated_prompt(qkv, bias, g, dil, batch, seq, qb):
    assert seq % (dil * BAND * qb) == 0
    sub = seq // dil
    span = dil * BAND * qb
    cur = lambda part: pl.BlockSpec((None, dil, qb * BAND, BRANCH_W), lambda b, i: (b, 0, i, part))
    prev = lambda part: pl.BlockSpec((None, dil, BAND, BRANCH_W),
                                     lambda b, i: (b, 0, jnp.maximum(i * qb - 1, 0), part))
    bias_spec = pl.BlockSpec((N_HEADS, BAND, 2 * BAND), lambda b, i: (0, 0, 0))
    scratch = [] if dil == 1 else [pltpu.VMEM((N_HEADS, span, HEAD_DIM), F32),
                                   pltpu.VMEM((span, N_HEADS * LSE_LANES), F32)]
    o, lse = pl.pallas_call(
        functools.partial(_dil_prompt_kernel, dil=dil, qb=qb),
        grid=(batch, sub // (BAND * qb)),
        in_specs=[cur(0), prev(1), cur(1), prev(2), cur(2), bias_spec],
        out_specs=[pl.BlockSpec((None, span, BRANCH_W), lambda b, i: (b, i, 0)),
                   pl.BlockSpec((None, span, N_HEADS * LSE_LANES), lambda b, i: (b, i, 0))],
        out_shape=[jax.ShapeDtypeStruct((batch, seq, BRANCH_W), BF16),
                   jax.ShapeDtypeStruct((batch, seq, N_HEADS * LSE_LANES), F32)],
        scratch_shapes=scratch,
        compiler_params=_cparams("parallel", "arbitrary"),
        name=f"dilated_prompt_g{g}",
    )(qkv, qkv, qkv, qkv, qkv, bias)
    return o.reshape(batch * seq, BRANCH_W), lse.reshape(batch * seq, N_HEADS * LSE_LANES)


def _dil_sample_kernel(q_ref, kn_ref, vn_ref, cache_ref, bc_ref, bn_ref, shifted_ref, o_ref, lse_ref, newrows_ref,
                       *, t_new):
    del shifted_ref
    scale = 1.0 / math.sqrt(HEAD_DIM)
    bb, n_grp, grp_rows, _ = cache_ref.shape
    n_keys = n_grp * grp_rows // KV_ROWS
    rows_of = lambda b, first: cache_ref[b, :, pl.ds(first, grp_rows // KV_ROWS, stride=KV_ROWS), :].reshape(
        n_keys, HEAD_DIM).astype(BF16)
    heads = [(b, h, slice(h * HEAD_DIM, (h + 1) * HEAD_DIM)) for b in range(bb) for h in range(N_HEADS)]
    logits = []
    for b, h, hs in heads:
        q = q_ref[b, :, hs]
        lc = lax.dot_general(q.astype(BF16), rows_of(b, h), NT_DIMS, preferred_element_type=F32) * scale + bc_ref[h]
        ln = lax.dot_general(q, kn_ref[b, :, hs], NT_DIMS, preferred_element_type=F32) * scale + bn_ref[h]
        logits.append((lc, ln))
    probs = []
    for lc, ln in logits:
        m = jnp.maximum(jnp.max(lc, axis=-1, keepdims=True), jnp.max(ln, axis=-1, keepdims=True))
        pc = jnp.exp(lc - m)
        pn = jnp.exp(ln - m)
        s = jnp.sum(pc, axis=-1, keepdims=True) + jnp.sum(pn, axis=-1, keepdims=True)
        probs.append((pc.astype(BF16), pn, s, m + jnp.log(s)))
    for (b, h, hs), (pc, pn, s, lse) in zip(heads, probs):
        kn = kn_ref[b, :, hs]
        vn = vn_ref[b, :, hs]
        o = (jnp.dot(pc, rows_of(b, N_HEADS + h), preferred_element_type=F32)
             + jnp.dot(pn, vn, preferred_element_type=F32))
        o_ref[b, :, hs] = o / s
        lse_ref[b, :, h * LSE_LANES:(h + 1) * LSE_LANES] = jnp.broadcast_to(lse, (t_new, LSE_LANES))
        newrows_ref[b, pl.ds(h, t_new, stride=KV_ROWS), :] = kn
        newrows_ref[b, pl.ds(N_HEADS + h, t_new, stride=KV_ROWS), :] = vn


def dilated_sample(qkv, cache, shifted, bias_cache, bias_new, g, dil, batch, t_new, bb):
    win = cache.shape[1] // KV_ROWS
    assert win % t_new == 0 and shifted.shape == cache.shape and batch % bb == 0
    period, take = (dil, t_new) if (dil > t_new and win % dil == 0) else (win, win)
    n_grp = win // period
    cache4 = cache.reshape(batch, n_grp, period * KV_ROWS, HEAD_DIM)
    bias_cache = bias_cache.reshape(N_HEADS, t_new, n_grp, period)[..., :take].reshape(N_HEADS, t_new, n_grp * take)
    blk = (bb, t_new, BRANCH_W)
    o, lse, newc = pl.pallas_call(
        functools.partial(_dil_sample_kernel, t_new=t_new),
        grid=(batch // bb,),
        in_specs=[pl.BlockSpec(blk, lambda b: (b, 0, 0)),
                  pl.BlockSpec(blk, lambda b: (b, 0, 1)),
                  pl.BlockSpec(blk, lambda b: (b, 0, 2)),
                  pl.BlockSpec((bb, n_grp, take * KV_ROWS, HEAD_DIM), lambda b: (b, 0, 0, 0)),
                  pl.BlockSpec((N_HEADS, t_new, n_grp * take), lambda b: (0, 0, 0)),
                  pl.BlockSpec((N_HEADS, t_new, t_new), lambda b: (0, 0, 0)),
                  pl.BlockSpec(memory_space=pl.ANY)],
        out_specs=[pl.BlockSpec(blk, lambda b: (b, 0, 0)),
                   pl.BlockSpec((bb, t_new, N_HEADS * LSE_LANES), lambda b: (b, 0, 0)),
                   pl.BlockSpec((bb, t_new * KV_ROWS, HEAD_DIM), lambda b: (b, win // t_new - 1, 0))],
        out_shape=[jax.ShapeDtypeStruct((batch, t_new, BRANCH_W), F32),
                   jax.ShapeDtypeStruct((batch, t_new, N_HEADS * LSE_LANES), F32),
                   jax.ShapeDtypeStruct(cache.shape, F32)],
        input_output_aliases={6: 2},
        compiler_params=_cparams("parallel"),
        name=f"dilated_sample_g{g}",
    )(qkv, qkv, qkv, cache4, bias_cache, bias_new, shifted)
    return o.reshape(batch * t_new, BRANCH_W), lse.reshape(batch * t_new, N_HEADS * LSE_LANES), newc


def _mem_attn_kernel(q_ref, kv_ref, o_ref, *, mem):
    scale = 1.0 / math.sqrt(HEAD_DIM)
    heads = [(b, h, slice(h * HEAD_DIM, (h + 1) * HEAD_DIM)) for b in range(q_ref.shape[0]) for h in range(N_HEADS)]
    rows_of = lambda b, first: kv_ref[b, pl.ds(first, mem, stride=KV_ROWS), :].astype(BF16)
    logits = [lax.dot_general(q_ref[b, :, hs].astype(BF16), rows_of(b, h), NT_DIMS, preferred_element_type=F32) * scale
              for b, h, hs in heads]
    probs = []
    for l in logits:
        half = l.shape[1] // 2
        m = jnp.max(jnp.maximum(l[:, :half], l[:, half:]), axis=-1, keepdims=True)
        p = jnp.exp(l - m)
        probs.append((p.astype(BF16), jnp.sum(p[:, :half] + p[:, half:], axis=-1, keepdims=True)))
    for (b, h, hs), (p, s) in zip(heads, probs):
        o = jnp.dot(p, rows_of(b, N_HEADS + h), preferred_element_type=F32)
        o_ref[b, :, hs] = (o / s).astype(o_ref.dtype)


def memory_attention(rec, mem_kv, batch, t, tq, bb, out_dtype):
    mem = mem_kv.shape[1] // KV_ROWS
    assert batch % bb == 0
    x = rec.reshape(batch, t, REC_COLS)
    o = pl.pallas_call(
        functools.partial(_mem_attn_kernel, mem=mem),
        grid=(batch // bb, t // tq),
        in_specs=[pl.BlockSpec((bb, tq, BRANCH_W), lambda b, i: (b, i, REC_MQ)),
                  pl.BlockSpec((bb, mem * KV_ROWS, HEAD_DIM), lambda b, i: (b, 0, 0))],
        out_specs=pl.BlockSpec((bb, tq, BRANCH_W), lambda b, i: (b, i, 0)),
        out_shape=jax.ShapeDtypeStruct((batch, t, BRANCH_W), out_dtype),
        compiler_params=_cparams("parallel", "parallel"),
        name="memory_attention",
    )(x, mem_kv)
    return o.reshape(batch * t, BRANCH_W)


def _hgrn_tables(chunk):
    c, b = chunk, HG_BASE
    t = np.arange(c)[:, None]
    u = np.arange(c)[None, :]
    masks = []
    size = 2 * b
    while size <= c:
        half = size // 2
        masks.append(((t // size) == (u // size)) & ((t % size) >= half) & ((u % size) < half))
        size *= 2
    masks.append(((t // b) == (u // b)) & (u <= t))
    erep = np.zeros((b * HEAD_DIM, c), np.float32)
    for s in range(b):
        erep[s * HEAD_DIM:(s + 1) * HEAD_DIM, np.arange(c) % b == s] = 1.0
    return (u <= t).astype(np.float32), erep, np.stack(masks).astype(np.float32)


def _shift_step(old_refs, new_refs, buf_refs, zero_ref, sem_in, sem_out, sem_zero, step, n_steps, drop):
    slot = step % 2
    other = 1 - slot

    def copies(g, chunk_idx, buf_slot):
        old, new, buf = old_refs[g], new_refs[g], buf_refs[g]
        rows_c = buf.shape[1]
        per_batch = (old.shape[1] - drop) // rows_c
        b = chunk_idx // per_batch
        lo = pl.multiple_of((chunk_idx % per_batch) * rows_c, 8)
        return (pltpu.make_async_copy(old.at[b, pl.ds(lo + drop, rows_c), :], buf.at[buf_slot], sem_in.at[g, buf_slot]),
                pltpu.make_async_copy(buf.at[buf_slot], new.at[b, pl.ds(lo, rows_c), :], sem_out.at[g, buf_slot]))

    def zero_copy(g):
        keep = old_refs[g].shape[1] - drop
        return pltpu.make_async_copy(zero_ref, new_refs[g].at[:, pl.ds(keep, drop), :], sem_zero.at[g])

    groups = range(len(old_refs))

    @pl.when(step == 0)
    def _():
        zero_ref[...] = jnp.zeros_like(zero_ref)
        for g in groups:
            zero_copy(g).start()
            copies(g, 0, 0)[0].start()

    for g in groups:
        copies(g, step, slot)[0].wait()

    @pl.when(step >= 1)
    def _():
        for g in groups:
            copies(g, step - 1, other)[1].wait()

    for g in groups:
        copies(g, step, slot)[1].start()

    @pl.when(step + 1 < n_steps)
    def _():
        for g in groups:
            copies(g, step + 1, other)[0].start()

    @pl.when(step == n_steps - 1)
    def _():
        for g in groups:
            copies(g, step, slot)[1].wait()
            zero_copy(g).wait()


def _shift_scratch(shift, drop, n_steps):
    if not shift:
        return []
    sb = shift[0].shape[0]
    assert n_steps % sb == 0 and all(s.shape[0] == sb and s.shape[2] == HEAD_DIM for s in shift)
    per_batch = n_steps // sb
    chunk_rows = [(s.shape[1] - drop) // per_batch for s in shift]
    assert all(r % 8 == 0 and r * per_batch == s.shape[1] - drop for r, s in zip(chunk_rows, shift))
    n = len(shift)
    return ([pltpu.VMEM((2, r, HEAD_DIM), F32) for r in chunk_rows]
            + [pltpu.VMEM((sb, drop, HEAD_DIM), F32), pltpu.SemaphoreType.DMA((n, 2)),
               pltpu.SemaphoreType.DMA((n, 2)), pltpu.SemaphoreType.DMA((n,))])


def _hgrn_kernel(q_ref, v_ref, f_ref, gate_ref, lbl_ref, gn_ref, s0_ref, tri_ref, erep_ref, mask_ref, *rest,
                 chunk, rows_in, n_sub, n_levels, n_shift, shift_drop, n_steps):
    old_refs, rest = rest[:n_shift], rest[n_shift:]
    o_ref, s_out_ref = rest[:2]
    new_refs, rest = rest[2:2 + n_shift], rest[2 + n_shift:]
    st_ref = rest[0]
    c = chunk
    step = pl.program_id(1)
    if n_shift:
        buf_refs, (zero_ref, sem_in, sem_out, sem_zero) = rest[1:1 + n_shift], rest[1 + n_shift:]
        _shift_step(old_refs, new_refs, buf_refs, zero_ref, sem_in, sem_out, sem_zero,
                    pl.program_id(0) * pl.num_programs(1) + step, n_steps, shift_drop)

    @pl.when(step == 0)
    def _():
        for h in range(N_HEADS):
            st_ref[h] = s0_ref[h].T

    lbl = lbl_ref[...]
    e = jnp.exp(lbl - jnp.max(lbl, axis=0, keepdims=True))
    lb = e[0:1] / jnp.sum(e, axis=0, keepdims=True)
    tri = tri_ref[...]

    def rows(ref, ci):
        x = ref[ci * rows_in:(ci + 1) * rows_in, :].astype(F32)
        if rows_in < c:
            x = jnp.concatenate([x, jnp.zeros((c - rows_in, x.shape[1]), F32)], axis=0)
        return x

    def at_row(x, size, idx):
        x3 = x.reshape(c // size, size, x.shape[1])
        return jnp.broadcast_to(x3[:, idx:idx + 1, :], x3.shape).reshape(x.shape)

    chunks = range(n_sub)
    heads = [(h, slice(h * HEAD_DIM, (h + 1) * HEAD_DIM)) for h in range(N_HEADS)]

    q, v, kk, cum = [], [], [], []
    for ci in chunks:
        q.append(rows(q_ref, ci))
        v.append(rows(v_ref, ci).astype(BF16))
        f = lb + (1.0 - lb) * _sigmoid(rows(f_ref, ci))
        g = jnp.log(f)
        k_in = 1.0 - f
        if rows_in < c:
            live = lax.broadcasted_iota(jnp.int32, (c, 1), 0) < rows_in
            g = jnp.where(live, g, 0.0)
            k_in = jnp.where(live, k_in, 0.0)
        kk.append(k_in)
        g1 = g.astype(BF16)
        r1 = g - g1.astype(F32)
        g2 = r1.astype(BF16)
        g3 = (r1 - g2.astype(F32)).astype(BF16)
        cum.append(jnp.dot(tri, g1, preferred_element_type=F32) + jnp.dot(tri, g2, preferred_element_type=F32)
                   + jnp.dot(tri, g3, preferred_element_type=F32))

    q_inter, k_end, g_end, q_lvl, k_lvl, x_diag = [], [], [], [], [], []
    for ci in chunks:
        q_inter.append((q[ci] * jnp.exp(cum[ci])).astype(BF16))
        g_end.append(cum[ci][c - 1:c])
        k_end.append((kk[ci] * jnp.exp(g_end[ci] - cum[ci])).astype(BF16))
        ql, kl = [], []
        for l in range(n_levels):
            size = 2 * HG_BASE << l
            ref = at_row(cum[ci], size, size // 2 - 1)
            ql.append((q[ci] * jnp.exp(jnp.minimum(cum[ci] - ref, 0.0))).astype(BF16))
            kl.append((kk[ci] * jnp.exp(jnp.minimum(ref - cum[ci], 0.0))).astype(BF16))
        q_lvl.append(ql)
        k_lvl.append(kl)
        xd = []
        for s in range(HG_BASE):
            decay = jnp.exp(jnp.minimum(cum[ci] - at_row(cum[ci], HG_BASE, s), 0.0))
            xd.append((q[ci] * at_row(kk[ci], HG_BASE, s) * decay).astype(BF16))
        x_diag.append(xd)

    scores, state_in = [], []
    for ci in chunks:
        sc, si = [], []
        for h, hs in heads:
            s_h = mask_ref[n_levels] * jnp.dot(jnp.concatenate([x[:, hs] for x in x_diag[ci]], axis=1),
                                               erep_ref[...], preferred_element_type=F32)
            for l in range(n_levels):
                s_h += mask_ref[l] * lax.dot_general(q_lvl[ci][l][:, hs], k_lvl[ci][l][:, hs], NT_DIMS,
                                                     preferred_element_type=F32)
            sc.append(s_h.astype(BF16))
            si.append(lax.dot_general(v[ci][:, hs], k_end[ci][:, hs], TN_DIMS, preferred_element_type=F32))
        scores.append(sc)
        state_in.append(si)

    states = []
    for h, hs in heads:
        st = st_ref[h]
        per_chunk = []
        for ci in chunks:
            per_chunk.append(st.astype(BF16))
            st = st * jnp.exp(g_end[ci][:, hs]) + state_in[ci][h]
        st_ref[h] = st
        states.append(per_chunk)
    for ci in chunks:
        gate = _sigmoid(rows(gate_ref, ci))
        for h, hs in heads:
            o = (lax.dot_general(q_inter[ci][:, hs], states[h][ci], NT_DIMS, preferred_element_type=F32)
                 + jnp.dot(scores[ci][h], v[ci][:, hs], preferred_element_type=F32))
            y = _rms(o, gn_ref[...]) * gate[:, hs]
            o_ref[ci * rows_in:(ci + 1) * rows_in, hs] = y[0:rows_in].astype(o_ref.dtype)

    @pl.when(step == pl.num_programs(1) - 1)
    def _():
        for h in range(N_HEADS):
            s_out_ref[h] = st_ref[h].T


def hgrn2(rec, gates, lb_logits, hg_norm, state0, batch, t, chunk, n_sub, out_dtype, shift=(), shift_drop=0):
    rows_in = min(chunk, t)
    step_rows = rows_in * n_sub
    assert t % step_rows == 0 and chunk % HG_BASE == 0 and (n_sub == 1 or rows_in == chunk)
    n_steps = batch * (t // step_rows)
    n_shift = len(shift)
    any_spec = pl.BlockSpec(memory_space=pl.ANY)
    shift_scratch = _shift_scratch(shift, shift_drop, n_steps)
    tri, erep, masks = _hgrn_tables(chunk)
    n_levels = masks.shape[0] - 1
    a = rec.reshape(batch, t, REC_COLS)
    gt = gates.reshape(batch, t, G_COLS)
    blk = (None, step_rows, BRANCH_W)
    full = lambda arr: pl.BlockSpec(arr.shape, lambda b, i: (0,) * arr.ndim)
    st_spec = pl.BlockSpec((None, N_HEADS, HEAD_DIM, HEAD_DIM), lambda b, i: (b, 0, 0, 0))
    tri = jnp.asarray(tri, BF16)
    erep = jnp.asarray(erep, BF16)
    masks = jnp.asarray(masks, F32)
    lbl = lb_logits.astype(F32)
    gn = hg_norm.reshape(1, HEAD_DIM).astype(F32)
    o, s_fin, *shifted = pl.pallas_call(
        functools.partial(_hgrn_kernel, chunk=chunk, rows_in=rows_in, n_sub=n_sub, n_levels=n_levels,
                          n_shift=n_shift, shift_drop=shift_drop, n_steps=n_steps),
        grid=(batch, t // step_rows),
        in_specs=[pl.BlockSpec(blk, lambda b, i: (b, i, REC_HQ)),
                  pl.BlockSpec(blk, lambda b, i: (b, i, REC_HI)),
                  pl.BlockSpec(blk, lambda b, i: (b, i, 0)),
                  pl.BlockSpec(blk, lambda b, i: (b, i, 1)),
                  full(lbl), full(gn), st_spec, full(tri), full(erep), full(masks)] + [any_spec] * n_shift,
        out_specs=[pl.BlockSpec(blk, lambda b, i: (b, i, 0)), st_spec] + [any_spec] * n_shift,
        out_shape=[jax.ShapeDtypeStruct((batch, t, BRANCH_W), out_dtype),
                   jax.ShapeDtypeStruct(state0.shape, F32)] + [jax.ShapeDtypeStruct(s.shape, F32) for s in shift],
        scratch_shapes=[pltpu.VMEM((N_HEADS, HEAD_DIM, HEAD_DIM), F32)] + shift_scratch,
        compiler_params=_cparams("arbitrary", "arbitrary"),
        name="hgrn2",
    )(a, a, gt, gt, lbl, gn, state0, tri, erep, masks, *shift)
    return o.reshape(batch * t, BRANCH_W), s_fin, shifted


def _merge_kernel(x_ref, o1_ref, o2_ref, o3_ref, l1_ref, l2_ref, l3_ref, hg_ref, mem_ref, gpre_ref,
                  wga_ref, wgh_ref, wgm_ref, bga_ref, bgh_ref, bgm_ref,
                  wa_ref, wh_ref, wm_ref, wo_ref, gain_ref, *rest, n_shift, shift_drop, n_steps):
    old_refs, out_ref, new_refs, scratch = rest[:n_shift], rest[n_shift], rest[n_shift + 1:2 * n_shift + 1], \
        rest[2 * n_shift + 1:]
    if n_shift:
        _shift_step(old_refs, new_refs, scratch[:n_shift], *scratch[n_shift:], pl.program_id(0), n_steps, shift_drop)
    tm = x_ref.shape[0]
    x = x_ref[...]
    hn = _rms(x, gpre_ref[...]).astype(BF16)
    gate_logits = [jnp.dot(hn, w_ref[...], preferred_element_type=F32) + b_ref[...]
                   for w_ref, b_ref in ((wga_ref, bga_ref), (wgh_ref, bgh_ref), (wgm_ref, bgm_ref))]
    from_hg = jnp.dot(hg_ref[...].astype(BF16), wh_ref[...], preferred_element_type=F32)
    from_mem = jnp.dot(mem_ref[...].astype(BF16), wm_ref[...], preferred_element_type=F32)
    l1, l2, l3 = l1_ref[...], l2_ref[...], l3_ref[...]
    m = jnp.maximum(jnp.maximum(l1, l2), l3)
    e1, e2, e3 = jnp.exp(l1 - m), jnp.exp(l2 - m), jnp.exp(l3 - m)
    den = e1 + e2 + e3
    w1, w2, w3 = e1 / den, e2 / den, e3 / den
    att = []
    for h in range(N_HEADS):
        hs = slice(h * HEAD_DIM, (h + 1) * HEAD_DIM)
        col = slice(h * LSE_LANES, h * LSE_LANES + 1)
        bc = lambda w: jnp.broadcast_to(w[:, col], (tm, HEAD_DIM))
        att.append((bc(w1) * o1_ref[:, hs].astype(F32) + bc(w2) * o2_ref[:, hs].astype(F32)
                    + bc(w3) * o3_ref[:, hs].astype(F32)).astype(BF16))
    att = jnp.concatenate(att, axis=1)
    ga, gh, gm = (_sigmoid(z) for z in gate_logits)
    merged = ga * jnp.dot(att, wa_ref[...], preferred_element_type=F32) + gh * from_hg + gm * from_mem
    y = jnp.dot(merged.astype(BF16), wo_ref[...], preferred_element_type=F32)
    out_ref[...] = x + _rms(y, gain_ref[...])


def merge(x, o_groups, lse_groups, hg_o, mem_o, gain_pre, w_in, b_in, w_att, w_hg, w_mem, w_out, gain, tm,
          shift=(), shift_drop=0):
    m, d = x.shape
    assert m % tm == 0 and GATE_MERGE_BLOCKS[0] * d == 14 * BRANCH_W
    n_steps = m // tm
    n_shift = len(shift)
    any_spec = pl.BlockSpec(memory_space=pl.ANY)
    row = lambda w: pl.BlockSpec((tm, w), lambda i: (i, 0))
    resident = lambda shape, j=0: pl.BlockSpec(shape, lambda i: (0, j), pipeline_mode=pl.Buffered(1))
    x1, *shifted = pl.pallas_call(
        functools.partial(_merge_kernel, n_shift=n_shift, shift_drop=shift_drop, n_steps=n_steps),
        grid=(n_steps,),
        in_specs=[row(d), row(BRANCH_W), row(BRANCH_W), row(BRANCH_W),
                  row(N_HEADS * LSE_LANES), row(N_HEADS * LSE_LANES), row(N_HEADS * LSE_LANES),
                  row(BRANCH_W), row(BRANCH_W), resident((1, d))]
        + [resident((d, d), j) for j in GATE_MERGE_BLOCKS] + [resident((1, d), j) for j in GATE_MERGE_BLOCKS]
        + [resident(w_att.shape), resident(w_hg.shape), resident(w_mem.shape), resident(w_out.shape),
           resident((1, d))] + [any_spec] * n_shift,
        out_specs=[row(d)] + [any_spec] * n_shift,
        out_shape=[jax.ShapeDtypeStruct((m, d), F32)] + [jax.ShapeDtypeStruct(s.shape, F32) for s in shift],
        scratch_shapes=_shift_scratch(shift, shift_drop, n_steps),
        compiler_params=_cparams("arbitrary"),
        name="merge",
    )(x, *o_groups, *lse_groups, hg_o, mem_o, gain_pre.reshape(1, d), w_in, w_in, w_in, b_in, b_in, b_in,
      w_att, w_hg, w_mem, w_out, gain.reshape(1, d), *shift)
    return x1, shifted


def _ffn_kernel(x_ref, gpre_ref, wa_ref, wb_ref, cw_ref, cb_ref, wd_ref, gpost_ref, cbuf_ref,
                out_ref, tail_ref, carry_ref, *, t_seq, tiles_per_seq, tf):
    tm = x_ref.shape[0]
    dff = wa_ref.shape[1]
    x = x_ref[...]
    hn = _rms(x, gpre_ref[...]).astype(BF16)

    if tiles_per_seq >= 1:
        @pl.when((pl.program_id(0) % tiles_per_seq) == 0)
        def _():
            carry_ref[6:8, :] = cbuf_ref[0]

        t_idx = lax.broadcasted_iota(jnp.int32, (tm, 1), 0)
    else:
        n_seq = tm // t_seq
        t_idx = lax.broadcasted_iota(jnp.int32, (tm, 1), 0) % t_seq

    n_chunks = dff // tf

    def up_proj(j):
        cols = slice(j * tf, (j + 1) * tf)
        return (jnp.dot(hn, wa_ref[:, cols], preferred_element_type=F32),
                jnp.dot(hn, wb_ref[:, cols], preferred_element_type=F32))

    acc = jnp.zeros(x.shape, F32)
    ahead = up_proj(0)
    for j in range(n_chunks):
        cols = slice(j * tf, (j + 1) * tf)
        a, up = ahead
        if j + 1 < n_chunks:
            ahead = up_proj(j + 1)
        if tiles_per_seq >= 1:
            prev1 = carry_ref[7:8, cols]
            prev2 = carry_ref[6:7, cols]
            carry_ref[:, cols] = a[tm - 8:tm]
            tail_ref[0, :, cols] = a[tm - 8:tm]
        else:
            prev1 = jnp.broadcast_to(cbuf_ref[:, 1:2, cols], (n_seq, t_seq, tf)).reshape(tm, tf)
            prev2 = jnp.broadcast_to(cbuf_ref[:, 0:1, cols], (n_seq, t_seq, tf)).reshape(tm, tf)
            tail_ref[:, :, cols] = a.reshape(n_seq, t_seq, tf)
        a1 = jnp.where(t_idx >= 1, pltpu.roll(a, 1, 0), prev1)
        a2 = jnp.where(t_idx >= 2, pltpu.roll(a, 2, 0), jnp.where(t_idx == 1, prev1, prev2))
        conv = cb_ref[:, cols] + a2 * cw_ref[0:1, cols] + a1 * cw_ref[1:2, cols] + a * cw_ref[2:3, cols]
        act = conv * _sigmoid(conv) * up
        acc += jnp.dot(act.astype(BF16), wd_ref[cols, :], preferred_element_type=F32)

    out_ref[...] = x + _rms(acc, gpost_ref[...])


def conv_ffn(x, conv_buf, g_pre, w_a, w_b, conv_w, conv_b, w_d, g_post, t_seq, tm, tf):
    m, d = x.shape
    dff = w_a.shape[1]
    n_seq = m // t_seq
    assert m % tm == 0 and dff % tf == 0 and t_seq >= 8
    if tm <= t_seq:
        assert t_seq % tm == 0
        tiles_per_seq = t_seq // tm
        seq_blk = 1
        seq_idx = lambda i: (i // tiles_per_seq, 0, 0)
    else:
        assert tm % t_seq == 0 and t_seq == 8
        tiles_per_seq = 0
        seq_blk = tm // t_seq
        seq_idx = lambda i: (i, 0, 0)
    n_tail = (m // tm) * seq_blk
    resident = lambda shape: pl.BlockSpec(shape, lambda i: (0, 0), pipeline_mode=pl.Buffered(1))
    y, tail = pl.pallas_call(
        functools.partial(_ffn_kernel, t_seq=t_seq, tiles_per_seq=tiles_per_seq, tf=tf),
        grid=(m // tm,),
        in_specs=[pl.BlockSpec((tm, d), lambda i: (i, 0)), resident((1, d)),
                  resident((d, dff)), resident((d, dff)), resident((3, dff)), resident((1, dff)),
                  resident((dff, d)), resident((1, d)),
                  pl.BlockSpec((seq_blk, 2, dff), seq_idx)],
        out_specs=[pl.BlockSpec((tm, d), lambda i: (i, 0)),
                   pl.BlockSpec((seq_blk, 8, dff), lambda i: (i, 0, 0))],
        out_shape=[jax.ShapeDtypeStruct((m, d), F32), jax.ShapeDtypeStruct((n_tail, 8, dff), F32)],
        scratch_shapes=[pltpu.VMEM((8, dff), F32)],
        compiler_params=_cparams("arbitrary"),
        name="conv_ffn",
    )(x, g_pre.reshape(1, d), w_a, w_b, conv_w, conv_b.reshape(1, dff), w_d, g_post.reshape(1, d), conv_buf)
    if tiles_per_seq >= 1:
        tail = tail.reshape(n_seq, tiles_per_seq, 8, dff)[:, -1]
    return y, tail


def _rel_buckets(dil):
    max_exact = REL_BUCKETS // 2
    dist = np.arange(N_LAGS + 1, dtype=np.int32) * dil
    d = np.maximum(dist, 1).astype(np.float32)
    large = max_exact + (np.log(d / np.float32(max_exact)) / np.float32(math.log(REL_MAX_DIST / max_exact))
                         * np.float32(REL_BUCKETS - max_exact)).astype(np.int32)
    large = np.minimum(large, REL_BUCKETS - 1)
    return np.where(dist < max_exact, dist, large)


def _lag_bias(rel_bias, g, dil):
    onehot = _rel_buckets(dil)[:, None] == np.arange(REL_BUCKETS)[None, :]
    table = jnp.sum(jnp.where(onehot[:, :, None], rel_bias.astype(F32)[None], 0.0), axis=1)
    return table[:, g * N_HEADS:(g + 1) * N_HEADS].T


def _neg(h, n):
    return jnp.full((h, n), NEG_INF, F32)


def _prompt_bias(lag_bias):
    h = lag_bias.shape[0]
    v = jnp.concatenate([lag_bias[:, ::-1], _neg(h, BAND)], axis=1)
    period = 2 * BAND + 1
    assert v.shape[1] == period
    return jnp.tile(v, (1, BAND))[:, :BAND * 2 * BAND].reshape(h, BAND, 2 * BAND)


def _sample_bias(lag_bias, dil, n_past, t_new):
    h = lag_bias.shape[0]
    n = n_past + t_new
    per_dist = jnp.repeat(lag_bias, dil, axis=1)
    per_dist = jnp.concatenate([per_dist, _neg(h, max(n - per_dist.shape[1], 0))], axis=1)[:, :n]
    per_dist = jnp.where((np.arange(n) % dil == 0)[None], per_dist, NEG_INF)
    rev = per_dist[:, ::-1]
    cached = jnp.stack([rev[:, t_new - 1 - t:t_new - 1 - t + n_past] for t in range(t_new)], axis=1)
    padded = jnp.concatenate([_neg(h, t_new - 1), per_dist[:, :t_new]], axis=1)
    new = jnp.stack([padded[:, t:t + t_new][:, ::-1] for t in range(t_new)], axis=1)
    return cached, new


def _layer(x, batch, t, weights, lag_biases, lb_logits, win_caches, hg_state0, conv_buf0, mem_kv,
           *, prompt, tm_proj, tm_merge, tm_ffn, tq_mem, chunk, n_sub, shifted=None, side_shift=(), side_shift_drop=0):
    (w_in16, b_in, gain_pre, gain_post, hg_norm, w_att, w_hg, w_mem, w_out,
     gain_fpre, gain_fpost, w_fa, w_fb, conv_w, conv_b, w_fd) = weights
    act_dtype = BF16 if prompt else F32
    b_in = b_in.reshape(1, -1)
    if prompt:
        qkv = [proj(x, gain_pre, w_in16, b_in, QKV_BLOCKS(g), batch, t, dil, act_dtype, tm_proj)
               for g, (_, dil) in enumerate(ATT_GROUPS)]
        qkv[0] = qkv[0].reshape(batch, 1, t, QKV_COLS)
    else:
        qkv = [proj(x, gain_pre, w_in16, b_in, QKV_BLOCKS(g), 1, batch * t, 1, act_dtype, tm_proj)
               .reshape(batch, t, QKV_COLS) for g in range(len(ATT_GROUPS))]
    rec = proj(x, gain_pre, w_in16, b_in, REC_BLOCKS, 1, batch * t, 1, act_dtype, tm_proj)
    gates = proj(x, gain_pre, w_in16, b_in, GATE_BLOCKS, 1, batch * t, 1, F32, tm_proj)

    o_groups, lse_groups, new_caches = [], [], []
    for g, (win, dil) in enumerate(ATT_GROUPS):
        if prompt:
            o, lse = dilated_prompt(qkv[g], _prompt_bias(lag_biases[g]), g, dil, batch, t, 4 if dil == 1 else 1)
        else:
            cache = win_caches[g]
            n_past = cache.shape[1] // KV_ROWS
            assert n_past >= win
            bcache, bnew = _sample_bias(lag_biases[g], dil, n_past, t)
            bb = math.gcd(batch, max(1, SAMPLE_STEP_BYTES // (cache.shape[1] * HEAD_DIM * 4)))
            o, lse, newc = dilated_sample(qkv[g], cache, shifted[g], bcache, bnew, g, dil, batch, t, bb)
            new_caches.append(newc)
        o_groups.append(o)
        lse_groups.append(lse)

    largest = max(range(len(side_shift)), key=lambda i: side_shift[i].size) if side_shift else None
    with_hgrn = [s for i, s in enumerate(side_shift) if i == largest]
    with_merge = [s for i, s in enumerate(side_shift) if i != largest]
    hg_o, hg_state, from_hgrn = hgrn2(rec, gates, lb_logits, hg_norm, hg_state0, batch, t, chunk, n_sub, act_dtype,
                                      with_hgrn, side_shift_drop)
    mem_bb = 1 if prompt else math.gcd(batch, max(1, SAMPLE_STEP_BYTES // (mem_kv.shape[1] * HEAD_DIM * 4)))
    mem_o = memory_attention(rec, mem_kv, batch, t, tq_mem, mem_bb, act_dtype)
    x1, from_merge = merge(x, o_groups, lse_groups, hg_o, mem_o, gain_pre, w_in16, b_in, w_att, w_hg, w_mem, w_out,
                           gain_post, tm_merge, with_merge, side_shift_drop)
    from_hgrn, from_merge = list(from_hgrn), list(from_merge)
    shifted_out = [from_hgrn.pop(0) if i == largest else from_merge.pop(0) for i in range(len(side_shift))]
    y, tail = conv_ffn(x1, conv_buf0, gain_fpre, w_fa, w_fb, conv_w, conv_b, w_fd, gain_fpost, t, tm_ffn, 2048)
    return y, qkv, new_caches, hg_state, tail[:, 6:8, :], shifted_out


def kernel(x_prompt, x_sample, mem_prompt, cache_win1_kv, cache_win2_kv, cache_win3_kv, cache_mem_kv, state_hgrn, state_ffn_conv, rel_bias, hg_lb_logits, norm_mix_pre, norm_mix_post, w_in, b_in, hg_norm, mem_norm, w_mem_kv, w_br_att, w_br_hg, w_br_mem, w_out, norm_ffn_pre, norm_ffn_post, w_ffn_a, w_ffn_b, ffn_conv_w, ffn_conv_b, w_ffn_d):
    depth = w_in.shape[0]
    assert depth == 1
    bsz, seq, d = x_prompt.shape
    dbsz, dseq, _ = x_sample.shape
    mem_tokens = mem_prompt.shape[1]
    dff = w_ffn_a.shape[2]
    layer = 0

    assert w_in.shape[2] == W_IN_BLOCKS * BRANCH_W
    weights = (w_in[layer].astype(BF16), b_in[layer],
               norm_mix_pre[layer], norm_mix_post[layer], hg_norm[layer],
               w_br_att[layer].astype(BF16), w_br_hg[layer].astype(BF16), w_br_mem[layer].astype(BF16),
               w_out[layer].astype(BF16), norm_ffn_pre[layer], norm_ffn_post[layer],
               w_ffn_a[layer].astype(BF16), w_ffn_b[layer].astype(BF16), ffn_conv_w[layer], ffn_conv_b[layer],
               w_ffn_d[layer].astype(BF16))
    lag_biases = [_lag_bias(rel_bias, g, dil) for g, (_, dil) in enumerate(ATT_GROUPS)]

    mem_kv = norm_matmul(mem_prompt.reshape(bsz * mem_tokens, d), mem_norm[layer], w_mem_kv[layer].astype(BF16),
                         jnp.zeros((2 * BRANCH_W,), F32), F32, bsz * mem_tokens, 1024)
    as_rows = lambda c: c[layer].reshape(dbsz, c.shape[2] * KV_ROWS, HEAD_DIM)
    win_caches = [as_rows(c) for c in (cache_win1_kv, cache_win2_kv, cache_win3_kv)]
    yp, qkv_p, _, hg_p, conv_p, shifted = _layer(
        x_prompt.reshape(bsz * seq, d), bsz, seq, weights, lag_biases, hg_lb_logits, None,
        jnp.zeros((bsz, N_HEADS, HEAD_DIM, HEAD_DIM), F32), jnp.zeros((bsz, 2, dff), F32),
        mem_kv.reshape(bsz, mem_tokens * KV_ROWS, HEAD_DIM),
        prompt=True, tm_proj=1024, tm_merge=512, tm_ffn=512, tq_mem=512, chunk=64, n_sub=4,
        side_shift=win_caches, side_shift_drop=dseq * KV_ROWS)
    p_win = []
    for g, (win, dil) in enumerate(ATT_GROUPS):
        n = min(win, seq)
        assert n % dil == 0
        tail = qkv_p[g][:, :, (seq - n) // dil:, BRANCH_W:]
        tail = jnp.swapaxes(tail, 1, 2).astype(F32)
        p_win.append(tail.reshape(1, bsz, n, 2, N_HEADS, HEAD_DIM))

    ys, _, new_caches, hg_s, conv_s, _ = _layer(
        x_sample.reshape(dbsz * dseq, d), dbsz, dseq, weights, lag_biases, hg_lb_logits, win_caches,
        state_hgrn[layer], state_ffn_conv[layer], as_rows(cache_mem_kv),
        prompt=False, tm_proj=dbsz * dseq, tm_merge=dbsz * dseq, tm_ffn=dbsz * dseq, tq_mem=dseq, chunk=16, n_sub=1,
        shifted=shifted)
    s_win = [c.reshape(1, dbsz, c.shape[1] // KV_ROWS, 2, N_HEADS, HEAD_DIM) for c in new_caches]

    return (yp.reshape(bsz, seq, d), ys.reshape(dbsz, dseq, d),
            p_win[0], p_win[1], p_win[2],
            hg_p[None], conv_p[None], mem_kv.reshape(1, bsz, mem_tokens, 2, N_HEADS, HEAD_DIM),
            s_win[0], s_win[1], s_win[2],
            hg_s[None], conv_s[None])
```

```python
import functools
import math

import numpy as np
import jax
import jax.numpy as jnp
from jax import lax
from jax.experimental import pallas as pl
from jax.experimental.pallas import tpu as pltpu

F32 = jnp.float32
BF16 = jnp.bfloat16

NORM_EPS = 1e-6
NEG_INF = -1e30
HEAD_DIM = 128
N_HEADS = 4
BRANCH_W = N_HEADS * HEAD_DIM
N_LAGS = 128
BAND = 128
ATT_GROUPS = ((128, 1), (512, 4), (2048, 16))
REL_BUCKETS = 32
REL_MAX_DIST = 2048
LSE_LANES = 32
HG_BASE = 8
VMEM_LIMIT = 56 * 1024 * 1024
SAMPLE_STEP_BYTES = 8 * 1024 * 1024
DEINTERLEAVE_STRIDE = 4
HGRN_SAMPLE_ROWS = 4

NT_DIMS = (((1,), (1,)), ((), ()))
TN_DIMS = (((0,), (0,)), ((), ()))

W_IN_BLOCKS = 20
QKV_BLOCKS = lambda g: (g, 3 + g, 6 + g)
REC_BLOCKS = (9, 11, 13)
GATE_BLOCKS = (10, 12)
GATE_MERGE_BLOCKS = (7, 8, 9)
QKV_COLS = 3 * BRANCH_W
REC_COLS = 3 * BRANCH_W
REC_HQ, REC_HI, REC_MQ = 0, 1, 2
G_COLS = 2 * BRANCH_W
KV_ROWS = 2 * N_HEADS


def _cparams(*sem):
    return pltpu.CompilerParams(dimension_semantics=sem, vmem_limit_bytes=VMEM_LIMIT)


def _rms(x, gain):
    return x * lax.rsqrt(jnp.mean(x * x, axis=-1, keepdims=True) + NORM_EPS) * gain


def _sigmoid(x):
    return 1.0 / (1.0 + jnp.exp(-x))


def _norm_matmul_kernel(x_ref, g_ref, w_ref, b_ref, o_ref, hn_ref):
    @pl.when(pl.program_id(1) == 0)
    def _():
        hn_ref[...] = _rms(x_ref[...], g_ref[...]).astype(BF16)

    acc = jnp.dot(hn_ref[...], w_ref[...], preferred_element_type=F32)
    o_ref[...] = (acc + b_ref[...]).astype(o_ref.dtype)


def norm_matmul(x, gain, w, bias, out_dtype, tm, tn):
    m, k = x.shape
    n = w.shape[1]
    assert m % tm == 0 and n % tn == 0
    return pl.pallas_call(
        _norm_matmul_kernel,
        grid=(m // tm, n // tn),
        in_specs=[
            pl.BlockSpec((tm, k), lambda i, j: (i, 0)),
            pl.BlockSpec((1, k), lambda i, j: (0, 0)),
            pl.BlockSpec((k, tn), lambda i, j: (0, j)),
            pl.BlockSpec((1, tn), lambda i, j: (0, j)),
        ],
        out_specs=pl.BlockSpec((tm, tn), lambda i, j: (i, j)),
        out_shape=jax.ShapeDtypeStruct((m, n), out_dtype),
        scratch_shapes=[pltpu.VMEM((tm, k), BF16)],
        compiler_params=_cparams("parallel", "arbitrary"),
        name="norm_matmul",
    )(x, gain.reshape(1, k), w, bias.reshape(1, n))


def _proj_kernel(x_ref, g_ref, *refs, n_blk, dil):
    w_refs, b_refs, o_ref = refs[:n_blk], refs[n_blk:2 * n_blk], refs[2 * n_blk]
    tm = x_ref.shape[0]
    hn = _rms(x_ref[...], g_ref[...]).astype(BF16)
    for n in range(n_blk):
        acc = jnp.dot(hn, w_refs[n][...], preferred_element_type=F32) + b_refs[n][...]
        if dil == 1:
            o_ref[:, n * BRANCH_W:(n + 1) * BRANCH_W] = acc.astype(o_ref.dtype)
            continue
        scr_ref = refs[2 * n_blk + 1]
        for c in range(N_HEADS):
            scr_ref[n * N_HEADS + c] = acc[:, c * HEAD_DIM:(c + 1) * HEAD_DIM]
        for c in range(N_HEADS):
            slab = n * N_HEADS + c
            lo = n * BRANCH_W + c * HEAD_DIM
            if dil <= DEINTERLEAVE_STRIDE:
                for r in range(dil):
                    o_ref[r, :, lo:lo + HEAD_DIM] = (
                        scr_ref[slab, pl.ds(r, tm // dil, stride=dil), :].astype(o_ref.dtype))
                continue
            s1, s2 = DEINTERLEAVE_STRIDE, dil // DEINTERLEAVE_STRIDE
            mid_ref = refs[2 * n_blk + 2]
            for a in range(s1):
                mid_ref[slab, a] = scr_ref[slab, pl.ds(a, tm // s1, stride=s1), :]
            for a in range(s1):
                for b in range(s2):
                    o_ref[a + s1 * b, :, lo:lo + HEAD_DIM] = (
                        mid_ref[slab, a, pl.ds(b, tm // dil, stride=s2), :].astype(o_ref.dtype))


def proj(x, gain, w, bias, blocks, batch, seq, dil, out_dtype, tm):
    m, k = x.shape
    n_blk = len(blocks)
    cols = n_blk * BRANCH_W
    assert m == batch * seq and seq % tm == 0 and tm % dil == 0
    tiles_per_b = seq // tm
    w_specs = [pl.BlockSpec((k, BRANCH_W), lambda i, c=c: (0, c)) for c in blocks]
    b_specs = [pl.BlockSpec((1, BRANCH_W), lambda i, c=c: (0, c)) for c in blocks]
    if dil == 1:
        out_spec = pl.BlockSpec((tm, cols), lambda i: (i, 0))
        out_shape = jax.ShapeDtypeStruct((m, cols), out_dtype)
        scratch = []
    else:
        out_spec = pl.BlockSpec((None, dil, tm // dil, cols), lambda i: (i // tiles_per_b, 0, i % tiles_per_b, 0))
        out_shape = jax.ShapeDtypeStruct((batch, dil, seq // dil, cols), out_dtype)
        scratch = [pltpu.VMEM((n_blk * N_HEADS, tm, HEAD_DIM), F32)]
        if dil > DEINTERLEAVE_STRIDE:
            assert dil % DEINTERLEAVE_STRIDE == 0 and dil // DEINTERLEAVE_STRIDE <= DEINTERLEAVE_STRIDE
            scratch.append(pltpu.VMEM((n_blk * N_HEADS, DEINTERLEAVE_STRIDE, tm // DEINTERLEAVE_STRIDE, HEAD_DIM), F32))
    return pl.pallas_call(
        functools.partial(_proj_kernel, n_blk=n_blk, dil=dil),
        grid=(m // tm,),
        in_specs=[pl.BlockSpec((tm, k), lambda i: (i, 0)), pl.BlockSpec((1, k), lambda i: (0, 0))] + w_specs + b_specs,
        out_specs=out_spec,
        out_shape=out_shape,
        scratch_shapes=scratch,
        compiler_params=_cparams("parallel"),
        name="proj",
    )(x, gain.reshape(1, k), *([w] * n_blk), *([bias] * n_blk))


def _proj_groups_kernel(x_ref, g_ref, *refs, sizes):
    n = sum(sizes)
    w_refs, b_refs, o_refs = refs[:n], refs[n:2 * n], refs[2 * n:]
    hn = _rms(x_ref[...], g_ref[...]).astype(BF16)
    k = 0
    for o_ref, size in zip(o_refs, sizes):
        for j in range(size):
            acc = jnp.dot(hn, w_refs[k][...], preferred_element_type=F32) + b_refs[k][...]
            o_ref[:, j * BRANCH_W:(j + 1) * BRANCH_W] = acc.astype(o_ref.dtype)
            k += 1


def proj_groups(x, gain, w, bias, groups, out_dtypes, tm):
    m, k = x.shape
    assert m % tm == 0
    blocks = [c for grp in groups for c in grp]
    once = lambda shape, c: pl.BlockSpec(shape, lambda i, c=c: (0, c), pipeline_mode=pl.Buffered(1))
    return pl.pallas_call(
        functools.partial(_proj_groups_kernel, sizes=tuple(len(grp) for grp in groups)),
        grid=(m // tm,),
        in_specs=[pl.BlockSpec((tm, k), lambda i: (i, 0)), pl.BlockSpec((1, k), lambda i: (0, 0))]
        + [once((k, BRANCH_W), c) for c in blocks] + [once((1, BRANCH_W), c) for c in blocks],
        out_specs=[pl.BlockSpec((tm, len(grp) * BRANCH_W), lambda i: (i, 0)) for grp in groups],
        out_shape=[jax.ShapeDtypeStruct((m, len(grp) * BRANCH_W), dt) for grp, dt in zip(groups, out_dtypes)],
        compiler_params=_cparams("parallel"),
        name="proj_groups",
    )(x, gain.reshape(1, k), *([w] * len(blocks)), *([bias] * len(blocks)))


def _dil_prompt_kernel(q_ref, kp_ref, kc_ref, vp_ref, vc_ref, bias_ref, o_ref, lse_ref, *scratch, dil, qb):
    scale = 1.0 / math.sqrt(HEAD_DIM)
    no_prev = (pl.program_id(1) == 0) & (lax.broadcasted_iota(jnp.int32, (1, 2 * BAND), 1) < BAND)

    def band_blocks(blocks):
        items = [(r, j, h, slice(j * BAND, (j + 1) * BAND), slice(h * HEAD_DIM, (h + 1) * HEAD_DIM))
                 for r, j in blocks for h in range(N_HEADS)]

        def keys(cur_ref, prev_ref, r, j, rows, hs):
            if j == 0:
                return jnp.concatenate([prev_ref[r, :, hs], cur_ref[r, rows, hs]], axis=0)
            return cur_ref[r, (j - 1) * BAND:(j + 1) * BAND, hs]

        logits = []
        for r, j, h, rows, hs in items:
            l = lax.dot_general(q_ref[r, rows, hs], keys(kc_ref, kp_ref, r, j, rows, hs), NT_DIMS,
                                preferred_element_type=F32) * scale + bias_ref[h]
            logits.append(jnp.where(no_prev, NEG_INF, l) if j == 0 else l)
        probs = []
        for l in logits:
            m = jnp.max(jnp.maximum(l[:, :BAND], l[:, BAND:]), axis=-1, keepdims=True)
            p = jnp.exp(l - m)
            s = jnp.sum(p[:, :BAND] + p[:, BAND:], axis=-1, keepdims=True)
            probs.append((p.astype(BF16), s, jnp.broadcast_to(m + jnp.log(s), (BAND, LSE_LANES))))
        outs = [jnp.dot(p, keys(vc_ref, vp_ref, r, j, rows, hs), preferred_element_type=F32) / s
                for (r, j, h, rows, hs), (p, s, _) in zip(items, probs)]
        return [(outs[N_HEADS * n:N_HEADS * (n + 1)],
                 jnp.concatenate([lse for _, _, lse in probs[N_HEADS * n:N_HEADS * (n + 1)]], axis=1))
                for n in range(len(blocks))]

    if dil == 1:
        for j, (outs, lse) in enumerate(band_blocks([(0, j) for j in range(qb)])):
            for h in range(N_HEADS):
                o_ref[j * BAND:(j + 1) * BAND, h * HEAD_DIM:(h + 1) * HEAD_DIM] = outs[h].astype(o_ref.dtype)
            lse_ref[j * BAND:(j + 1) * BAND, :] = lse
        return

    o_scr, lse_scr = scratch
    per_iter = min(dil, 4)

    def body(it, carry):
        blocks = [(it * per_iter + rr, j) for rr in range(per_iter) for j in range(qb)]
        for (r, j), (outs, lse) in zip(blocks, band_blocks(blocks)):
            for h in range(N_HEADS):
                o_scr[h, pl.ds(r + j * dil * BAND, BAND, stride=dil), :] = outs[h]
            lse_scr[pl.ds(r + j * dil * BAND, BAND, stride=dil), :] = lse
        return carry

    lax.fori_loop(0, dil // per_iter, body, 0)
    for h in range(N_HEADS):
        o_ref[:, h * HEAD_DIM:(h + 1) * HEAD_DIM] = o_scr[h].astype(o_ref.dtype)
    lse_ref[...] = lse_scr[...]


def dilated_prompt(qkv, bias, g, dil, batch, seq, qb):
    assert seq % (dil * BAND * qb) == 0
    sub = seq // dil
    span = dil * BAND * qb
    cur = lambda part: pl.BlockSpec((None, dil, qb * BAND, BRANCH_W), lambda b, i: (b, 0, i, part))
    prev = lambda part: pl.BlockSpec((None, dil, BAND, BRANCH_W),
                                     lambda b, i: (b, 0, jnp.maximum(i * qb - 1, 0), part))
    bias_spec = pl.BlockSpec((N_HEADS, BAND, 2 * BAND), lambda b, i: (0, 0, 0))
    scratch = [] if dil == 1 else [pltpu.VMEM((N_HEADS, span, HEAD_DIM), F32),
                                   pltpu.VMEM((span, N_HEADS * LSE_LANES), F32)]
    o, lse = pl.pallas_call(
        functools.partial(_dil_prompt_kernel, dil=dil, qb=qb),
        grid=(batch, sub // (BAND * qb)),
        in_specs=[cur(0), prev(1), cur(1), prev(2), cur(2), bias_spec],
        out_specs=[pl.BlockSpec((None, span, BRANCH_W), lambda b, i: (b, i, 0)),
                   pl.BlockSpec((None, span, N_HEADS * LSE_LANES), lambda b, i: (b, i, 0))],
        out_shape=[jax.ShapeDtypeStruct((batch, seq, BRANCH_W), BF16),
                   jax.ShapeDtypeStruct((batch, seq, N_HEADS * LSE_LANES), F32)],
        scratch_shapes=scratch,
        compiler_params=_cparams("parallel", "arbitrary"),
        name=f"dilated_prompt_g{g}",
    )(qkv, qkv, qkv, qkv, qkv, bias)
    return o.reshape(batch * seq, BRANCH_W), lse.reshape(batch * seq, N_HEADS * LSE_LANES)


def _dil_sample_kernel(q_ref, kn_ref, vn_ref, cache_ref, bc_ref, bn_ref, shifted_ref, o_ref, lse_ref, newrows_ref,
                       *, t_new):
    del shifted_ref
    scale = 1.0 / math.sqrt(HEAD_DIM)
    bb, n_grp, grp_rows, _ = cache_ref.shape
    n_keys = n_grp * grp_rows // KV_ROWS
    rows_of = lambda b, first: cache_ref[b, :, pl.ds(first, grp_rows // KV_ROWS, stride=KV_ROWS), :].reshape(
        n_keys, HEAD_DIM).astype(BF16)
    heads = [(b, h, slice(h * HEAD_DIM, (h + 1) * HEAD_DIM)) for b in range(bb) for h in range(N_HEADS)]
    logits = []
    for b, h, hs in heads:
        q = q_ref[b, :, hs]
        lc = lax.dot_general(q.astype(BF16), rows_of(b, h), NT_DIMS, preferred_element_type=F32) * scale + bc_ref[h]
        ln = lax.dot_general(q, kn_ref[b, :, hs], NT_DIMS, preferred_element_type=F32) * scale + bn_ref[h]
        logits.append((lc, ln))
    probs = []
    for lc, ln in logits:
        m = jnp.maximum(jnp.max(lc, axis=-1, keepdims=True), jnp.max(ln, axis=-1, keepdims=True))
        pc = jnp.exp(lc - m)
        pn = jnp.exp(ln - m)
        s = jnp.sum(pc, axis=-1, keepdims=True) + jnp.sum(pn, axis=-1, keepdims=True)
        probs.append((pc.astype(BF16), pn, s, m + jnp.log(s)))
    for (b, h, hs), (pc, pn, s, lse) in zip(heads, probs):
        kn = kn_ref[b, :, hs]
        vn = vn_ref[b, :, hs]
        o = (jnp.dot(pc, rows_of(b, N_HEADS + h), preferred_element_type=F32)
             + jnp.dot(pn, vn, preferred_element_type=F32))
        o_ref[b, :, hs] = o / s
        lse_ref[b, :, h * LSE_LANES:(h + 1) * LSE_LANES] = jnp.broadcast_to(lse, (t_new, LSE_LANES))
        newrows_ref[b, pl.ds(h, t_new, stride=KV_ROWS), :] = kn
        newrows_ref[b, pl.ds(N_HEADS + h, t_new, stride=KV_ROWS), :] = vn


def _sample_period(win, dil, t_new):
    return (dil, t_new) if (dil > t_new and win % dil == 0) else (win, win)


def _sample_positions(win, dil, t_new):
    period, take = _sample_period(win, dil, t_new)
    return (np.arange(win // period)[:, None] * period + np.arange(take)[None, :]).reshape(-1)


def dilated_sample(qkv, cache, shifted, bias_cache, bias_new, g, dil, batch, t_new, bb):
    win = cache.shape[1] // KV_ROWS
    assert win % t_new == 0 and shifted.shape == cache.shape and batch % bb == 0
    period, take = _sample_period(win, dil, t_new)
    n_grp = win // period
    cache4 = cache.reshape(batch, n_grp, period * KV_ROWS, HEAD_DIM)
    assert bias_cache.shape == (N_HEADS, t_new, n_grp * take)
    blk = (bb, t_new, BRANCH_W)
    o, lse, newc = pl.pallas_call(
        functools.partial(_dil_sample_kernel, t_new=t_new),
        grid=(batch // bb,),
        in_specs=[pl.BlockSpec(blk, lambda b: (b, 0, 0)),
                  pl.BlockSpec(blk, lambda b: (b, 0, 1)),
                  pl.BlockSpec(blk, lambda b: (b, 0, 2)),
                  pl.BlockSpec((bb, n_grp, take * KV_ROWS, HEAD_DIM), lambda b: (b, 0, 0, 0)),
                  pl.BlockSpec((N_HEADS, t_new, n_grp * take), lambda b: (0, 0, 0)),
                  pl.BlockSpec((N_HEADS, t_new, t_new), lambda b: (0, 0, 0)),
                  pl.BlockSpec(memory_space=pl.ANY)],
        out_specs=[pl.BlockSpec(blk, lambda b: (b, 0, 0)),
                   pl.BlockSpec((bb, t_new, N_HEADS * LSE_LANES), lambda b: (b, 0, 0)),
                   pl.BlockSpec((bb, t_new * KV_ROWS, HEAD_DIM), lambda b: (b, win // t_new - 1, 0))],
        out_shape=[jax.ShapeDtypeStruct((batch, t_new, BRANCH_W), F32),
                   jax.ShapeDtypeStruct((batch, t_new, N_HEADS * LSE_LANES), F32),
                   jax.ShapeDtypeStruct(cache.shape, F32)],
        input_output_aliases={6: 2},
        compiler_params=_cparams("parallel"),
        name=f"dilated_sample_g{g}",
    )(qkv, qkv, qkv, cache4, bias_cache, bias_new, shifted)
    return o.reshape(batch * t_new, BRANCH_W), lse.reshape(batch * t_new, N_HEADS * LSE_LANES), newc


def _mem_attn_kernel(q_ref, kv_ref, o_ref, *, mem):
    scale = 1.0 / math.sqrt(HEAD_DIM)
    heads = [(b, h, slice(h * HEAD_DIM, (h + 1) * HEAD_DIM)) for b in range(q_ref.shape[0]) for h in range(N_HEADS)]
    rows_of = lambda b, first: kv_ref[b, pl.ds(first, mem, stride=KV_ROWS), :].astype(BF16)
    logits = [lax.dot_general(q_ref[b, :, hs].astype(BF16), rows_of(b, h), NT_DIMS, preferred_element_type=F32) * scale
              for b, h, hs in heads]
    probs = []
    for l in logits:
        half = l.shape[1] // 2
        m = jnp.max(jnp.maximum(l[:, :half], l[:, half:]), axis=-1, keepdims=True)
        p = jnp.exp(l - m)
        probs.append((p.astype(BF16), jnp.sum(p[:, :half] + p[:, half:], axis=-1, keepdims=True)))
    for (b, h, hs), (p, s) in zip(heads, probs):
        o = jnp.dot(p, rows_of(b, N_HEADS + h), preferred_element_type=F32)
        o_ref[b, :, hs] = (o / s).astype(o_ref.dtype)


def memory_attention(rec, mem_kv, batch, t, tq, bb, out_dtype):
    mem = mem_kv.shape[1] // KV_ROWS
    assert batch % bb == 0
    x = rec.reshape(batch, t, REC_COLS)
    o = pl.pallas_call(
        functools.partial(_mem_attn_kernel, mem=mem),
        grid=(batch // bb, t // tq),
        in_specs=[pl.BlockSpec((bb, tq, BRANCH_W), lambda b, i: (b, i, REC_MQ)),
                  pl.BlockSpec((bb, mem * KV_ROWS, HEAD_DIM), lambda b, i: (b, 0, 0))],
        out_specs=pl.BlockSpec((bb, tq, BRANCH_W), lambda b, i: (b, i, 0)),
        out_shape=jax.ShapeDtypeStruct((batch, t, BRANCH_W), out_dtype),
        compiler_params=_cparams("parallel", "parallel"),
        name="memory_attention",
    )(x, mem_kv)
    return o.reshape(batch * t, BRANCH_W)


def _hgrn_tables(chunk):
    c, b = chunk, HG_BASE
    t = np.arange(c)[:, None]
    u = np.arange(c)[None, :]
    masks = []
    size = 2 * b
    while size <= c:
        half = size // 2
        masks.append(((t // size) == (u // size)) & ((t % size) >= half) & ((u % size) < half))
        size *= 2
    masks.append(((t // b) == (u // b)) & (u <= t))
    erep = np.zeros((b * HEAD_DIM, c), np.float32)
    for s in range(b):
        erep[s * HEAD_DIM:(s + 1) * HEAD_DIM, np.arange(c) % b == s] = 1.0
    return (u <= t).astype(np.float32), erep, np.stack(masks).astype(np.float32)


def _shift_step(old_refs, new_refs, buf_refs, zero_ref, sem_in, sem_out, sem_zero, step, n_steps, drop):
    slot = step % 2
    other = 1 - slot

    def copies(g, chunk_idx, buf_slot):
        old, new, buf = old_refs[g], new_refs[g], buf_refs[g]
        rows_c = buf.shape[1]
        per_batch = (old.shape[1] - drop) // rows_c
        b = chunk_idx // per_batch
        lo = pl.multiple_of((chunk_idx % per_batch) * rows_c, 8)
        return (pltpu.make_async_copy(old.at[b, pl.ds(lo + drop, rows_c), :], buf.at[buf_slot], sem_in.at[g, buf_slot]),
                pltpu.make_async_copy(buf.at[buf_slot], new.at[b, pl.ds(lo, rows_c), :], sem_out.at[g, buf_slot]))

    def zero_copy(g):
        keep = old_refs[g].shape[1] - drop
        return pltpu.make_async_copy(zero_ref, new_refs[g].at[:, pl.ds(keep, drop), :], sem_zero.at[g])

    groups = range(len(old_refs))

    @pl.when(step == 0)
    def _():
        zero_ref[...] = jnp.zeros_like(zero_ref)
        for g in groups:
            zero_copy(g).start()
            copies(g, 0, 0)[0].start()

    for g in groups:
        copies(g, step, slot)[0].wait()

    @pl.when(step >= 1)
    def _():
        for g in groups:
            copies(g, step - 1, other)[1].wait()

    for g in groups:
        copies(g, step, slot)[1].start()

    @pl.when(step + 1 < n_steps)
    def _():
        for g in groups:
            copies(g, step + 1, other)[0].start()

    @pl.when(step == n_steps - 1)
    def _():
        for g in groups:
            copies(g, step, slot)[1].wait()
            zero_copy(g).wait()


def _shift_scratch(shift, drop, n_steps):
    if not shift:
        return []
    sb = shift[0].shape[0]
    assert n_steps % sb == 0 and all(s.shape[0] == sb and s.shape[2] == HEAD_DIM for s in shift)
    per_batch = n_steps // sb
    chunk_rows = [(s.shape[1] - drop) // per_batch for s in shift]
    assert all(r % 8 == 0 and r * per_batch == s.shape[1] - drop for r, s in zip(chunk_rows, shift))
    n = len(shift)
    return ([pltpu.VMEM((2, r, HEAD_DIM), F32) for r in chunk_rows]
            + [pltpu.VMEM((sb, drop, HEAD_DIM), F32), pltpu.SemaphoreType.DMA((n, 2)),
               pltpu.SemaphoreType.DMA((n, 2)), pltpu.SemaphoreType.DMA((n,))])


def _hgrn_kernel(q_ref, v_ref, f_ref, gate_ref, lbl_ref, gn_ref, s0_ref, tri_ref, erep_ref, mask_ref, *rest,
                 chunk, rows_in, n_sub, bb, n_levels, n_shift, shift_drop, n_steps):
    old_refs, rest = rest[:n_shift], rest[n_shift:]
    o_ref, s_out_ref = rest[:2]
    new_refs, rest = rest[2:2 + n_shift], rest[2 + n_shift:]
    st_ref = rest[0]
    c = chunk
    step = pl.program_id(1)
    if n_shift:
        buf_refs, (zero_ref, sem_in, sem_out, sem_zero) = rest[1:1 + n_shift], rest[1 + n_shift:]
        _shift_step(old_refs, new_refs, buf_refs, zero_ref, sem_in, sem_out, sem_zero,
                    pl.program_id(0) * pl.num_programs(1) + step, n_steps, shift_drop)

    @pl.when(step == 0)
    def _():
        for b in range(bb):
            for h in range(N_HEADS):
                st_ref[b * N_HEADS + h] = s0_ref[b, h].T

    lbl = lbl_ref[...]
    e = jnp.exp(lbl - jnp.max(lbl, axis=0, keepdims=True))
    lb = e[0:1] / jnp.sum(e, axis=0, keepdims=True)
    tri = tri_ref[...]

    def rows(ref, b, ci):
        x = ref[b, ci * rows_in:(ci + 1) * rows_in, :].astype(F32)
        if rows_in < c:
            x = jnp.concatenate([x, jnp.zeros((c - rows_in, x.shape[1]), F32)], axis=0)
        return x

    def at_row(x, size, idx):
        x3 = x.reshape(c // size, size, x.shape[1])
        return jnp.broadcast_to(x3[:, idx:idx + 1, :], x3.shape).reshape(x.shape)

    chunk_ids = [(b, ci) for b in range(bb) for ci in range(n_sub)]
    chunks = range(len(chunk_ids))
    heads = [(h, slice(h * HEAD_DIM, (h + 1) * HEAD_DIM)) for h in range(N_HEADS)]

    q, v, kk, cum = [], [], [], []
    for b, ci in chunk_ids:
        q.append(rows(q_ref, b, ci))
        v.append(rows(v_ref, b, ci).astype(BF16))
        f = lb + (1.0 - lb) * _sigmoid(rows(f_ref, b, ci))
        g = jnp.log(f)
        k_in = 1.0 - f
        if rows_in < c:
            live = lax.broadcasted_iota(jnp.int32, (c, 1), 0) < rows_in
            g = jnp.where(live, g, 0.0)
            k_in = jnp.where(live, k_in, 0.0)
        kk.append(k_in)
        g1 = g.astype(BF16)
        r1 = g - g1.astype(F32)
        g2 = r1.astype(BF16)
        g3 = (r1 - g2.astype(F32)).astype(BF16)
        cum.append(jnp.dot(tri, g1, preferred_element_type=F32) + jnp.dot(tri, g2, preferred_element_type=F32)
                   + jnp.dot(tri, g3, preferred_element_type=F32))

    q_inter, k_end, g_end, q_lvl, k_lvl, x_diag = [], [], [], [], [], []
    for ci in chunks:
        q_inter.append((q[ci] * jnp.exp(cum[ci])).astype(BF16))
        g_end.append(cum[ci][c - 1:c])
        k_end.append((kk[ci] * jnp.exp(g_end[ci] - cum[ci])).astype(BF16))
        ql, kl = [], []
        for l in range(n_levels):
            size = 2 * HG_BASE << l
            ref = at_row(cum[ci], size, size // 2 - 1)
            ql.append((q[ci] * jnp.exp(jnp.minimum(cum[ci] - ref, 0.0))).astype(BF16))
            kl.append((kk[ci] * jnp.exp(jnp.minimum(ref - cum[ci], 0.0))).astype(BF16))
        q_lvl.append(ql)
        k_lvl.append(kl)
        xd = []
        for s in range(HG_BASE):
            decay = jnp.exp(jnp.minimum(cum[ci] - at_row(cum[ci], HG_BASE, s), 0.0))
            xd.append((q[ci] * at_row(kk[ci], HG_BASE, s) * decay).astype(BF16))
        x_diag.append(xd)

    scores, state_in = [], []
    for ci in chunks:
        sc, si = [], []
        for h, hs in heads:
            s_h = mask_ref[n_levels] * jnp.dot(jnp.concatenate([x[:, hs] for x in x_diag[ci]], axis=1),
                                               erep_ref[...], preferred_element_type=F32)
            for l in range(n_levels):
                s_h += mask_ref[l] * lax.dot_general(q_lvl[ci][l][:, hs], k_lvl[ci][l][:, hs], NT_DIMS,
                                                     preferred_element_type=F32)
            sc.append(s_h.astype(BF16))
            si.append(lax.dot_general(v[ci][:, hs], k_end[ci][:, hs], TN_DIMS, preferred_element_type=F32))
        scores.append(sc)
        state_in.append(si)

    states = {}
    for b in range(bb):
        for h, hs in heads:
            st = st_ref[b * N_HEADS + h]
            for ci in range(n_sub):
                idx = b * n_sub + ci
                states[idx, h] = st.astype(BF16)
                st = st * jnp.exp(g_end[idx][:, hs]) + state_in[idx][h]
            st_ref[b * N_HEADS + h] = st
    for idx, (b, ci) in enumerate(chunk_ids):
        gate = _sigmoid(rows(gate_ref, b, ci))
        for h, hs in heads:
            o = (lax.dot_general(q_inter[idx][:, hs], states[idx, h], NT_DIMS, preferred_element_type=F32)
                 + jnp.dot(scores[idx][h], v[idx][:, hs], preferred_element_type=F32))
            y = _rms(o, gn_ref[...]) * gate[:, hs]
            o_ref[b, ci * rows_in:(ci + 1) * rows_in, hs] = y[0:rows_in].astype(o_ref.dtype)

    @pl.when(step == pl.num_programs(1) - 1)
    def _():
        for b in range(bb):
            for h in range(N_HEADS):
                s_out_ref[b, h] = st_ref[b * N_HEADS + h].T


def hgrn2(rec, gates, lb_logits, hg_norm, state0, batch, t, chunk, n_sub, bb, out_dtype, shift=(), shift_drop=0):
    rows_in = min(chunk, t)
    step_rows = rows_in * n_sub
    assert t % step_rows == 0 and chunk % HG_BASE == 0 and (n_sub == 1 or rows_in == chunk) and batch % bb == 0
    n_steps = (batch // bb) * (t // step_rows)
    n_shift = len(shift)
    any_spec = pl.BlockSpec(memory_space=pl.ANY)
    shift_scratch = _shift_scratch(shift, shift_drop, n_steps)
    tri, erep, masks = _hgrn_tables(chunk)
    n_levels = masks.shape[0] - 1
    a = rec.reshape(batch, t, REC_COLS)
    gt = gates.reshape(batch, t, G_COLS)
    blk = (bb, step_rows, BRANCH_W)
    full = lambda arr: pl.BlockSpec(arr.shape, lambda b, i: (0,) * arr.ndim)
    st_spec = pl.BlockSpec((bb, N_HEADS, HEAD_DIM, HEAD_DIM), lambda b, i: (b, 0, 0, 0))
    tri = jnp.asarray(tri, BF16)
    erep = jnp.asarray(erep, BF16)
    masks = jnp.asarray(masks, F32)
    lbl = lb_logits.astype(F32)
    gn = hg_norm.reshape(1, HEAD_DIM).astype(F32)
    o, s_fin, *shifted = pl.pallas_call(
        functools.partial(_hgrn_kernel, chunk=chunk, rows_in=rows_in, n_sub=n_sub, bb=bb, n_levels=n_levels,
                          n_shift=n_shift, shift_drop=shift_drop, n_steps=n_steps),
        grid=(batch // bb, t // step_rows),
        in_specs=[pl.BlockSpec(blk, lambda b, i: (b, i, REC_HQ)),
                  pl.BlockSpec(blk, lambda b, i: (b, i, REC_HI)),
                  pl.BlockSpec(blk, lambda b, i: (b, i, 0)),
                  pl.BlockSpec(blk, lambda b, i: (b, i, 1)),
                  full(lbl), full(gn), st_spec, full(tri), full(erep), full(masks)] + [any_spec] * n_shift,
        out_specs=[pl.BlockSpec(blk, lambda b, i: (b, i, 0)), st_spec] + [any_spec] * n_shift,
        out_shape=[jax.ShapeDtypeStruct((batch, t, BRANCH_W), out_dtype),
                   jax.ShapeDtypeStruct(state0.shape, F32)] + [jax.ShapeDtypeStruct(s.shape, F32) for s in shift],
        scratch_shapes=[pltpu.VMEM((bb * N_HEADS, HEAD_DIM, HEAD_DIM), F32)] + shift_scratch,
        compiler_params=_cparams("arbitrary", "arbitrary"),
        name="hgrn2",
    )(a, a, gt, gt, lbl, gn, state0, tri, erep, masks, *shift)
    return o.reshape(batch * t, BRANCH_W), s_fin, shifted


def _merge_kernel(x_ref, o1_ref, o2_ref, o3_ref, l1_ref, l2_ref, l3_ref, hg_ref, mem_ref, gpre_ref,
                  wga_ref, wgh_ref, wgm_ref, bga_ref, bgh_ref, bgm_ref,
                  wa_ref, wh_ref, wm_ref, wo_ref, gain_ref, *rest, n_shift, shift_drop, n_steps):
    old_refs, out_ref, new_refs, scratch = rest[:n_shift], rest[n_shift], rest[n_shift + 1:2 * n_shift + 1], \
        rest[2 * n_shift + 1:]
    if n_shift:
        _shift_step(old_refs, new_refs, scratch[:n_shift], *scratch[n_shift:], pl.program_id(0), n_steps, shift_drop)
    tm = x_ref.shape[0]
    x = x_ref[...]
    hn = _rms(x, gpre_ref[...]).astype(BF16)
    gate_logits = [jnp.dot(hn, w_ref[...], preferred_element_type=F32) + b_ref[...]
                   for w_ref, b_ref in ((wga_ref, bga_ref), (wgh_ref, bgh_ref), (wgm_ref, bgm_ref))]
    from_hg = jnp.dot(hg_ref[...].astype(BF16), wh_ref[...], preferred_element_type=F32)
    from_mem = jnp.dot(mem_ref[...].astype(BF16), wm_ref[...], preferred_element_type=F32)
    l1, l2, l3 = l1_ref[...], l2_ref[...], l3_ref[...]
    m = jnp.maximum(jnp.maximum(l1, l2), l3)
    e1, e2, e3 = jnp.exp(l1 - m), jnp.exp(l2 - m), jnp.exp(l3 - m)
    den = e1 + e2 + e3
    w1, w2, w3 = e1 / den, e2 / den, e3 / den
    att = []
    for h in range(N_HEADS):
        hs = slice(h * HEAD_DIM, (h + 1) * HEAD_DIM)
        col = slice(h * LSE_LANES, h * LSE_LANES + 1)
        bc = lambda w: jnp.broadcast_to(w[:, col], (tm, HEAD_DIM))
        att.append((bc(w1) * o1_ref[:, hs].astype(F32) + bc(w2) * o2_ref[:, hs].astype(F32)
                    + bc(w3) * o3_ref[:, hs].astype(F32)).astype(BF16))
    att = jnp.concatenate(att, axis=1)
    ga, gh, gm = (_sigmoid(z) for z in gate_logits)
    merged = ga * jnp.dot(att, wa_ref[...], preferred_element_type=F32) + gh * from_hg + gm * from_mem
    y = jnp.dot(merged.astype(BF16), wo_ref[...], preferred_element_type=F32)
    out_ref[...] = x + _rms(y, gain_ref[...])


def merge(x, o_groups, lse_groups, hg_o, mem_o, gain_pre, w_in, b_in, w_att, w_hg, w_mem, w_out, gain, tm,
          shift=(), shift_drop=0):
    m, d = x.shape
    assert m % tm == 0 and GATE_MERGE_BLOCKS[0] * d == 14 * BRANCH_W
    n_steps = m // tm
    n_shift = len(shift)
    any_spec = pl.BlockSpec(memory_space=pl.ANY)
    row = lambda w: pl.BlockSpec((tm, w), lambda i: (i, 0))
    resident = lambda shape, j=0: pl.BlockSpec(shape, lambda i: (0, j), pipeline_mode=pl.Buffered(1))
    x1, *shifted = pl.pallas_call(
        functools.partial(_merge_kernel, n_shift=n_shift, shift_drop=shift_drop, n_steps=n_steps),
        grid=(n_steps,),
        in_specs=[row(d), row(BRANCH_W), row(BRANCH_W), row(BRANCH_W),
                  row(N_HEADS * LSE_LANES), row(N_HEADS * LSE_LANES), row(N_HEADS * LSE_LANES),
                  row(BRANCH_W), row(BRANCH_W), resident((1, d))]
        + [resident((d, d), j) for j in GATE_MERGE_BLOCKS] + [resident((1, d), j) for j in GATE_MERGE_BLOCKS]
        + [resident(w_att.shape), resident(w_hg.shape), resident(w_mem.shape), resident(w_out.shape),
           resident((1, d))] + [any_spec] * n_shift,
        out_specs=[row(d)] + [any_spec] * n_shift,
        out_shape=[jax.ShapeDtypeStruct((m, d), F32)] + [jax.ShapeDtypeStruct(s.shape, F32) for s in shift],
        scratch_shapes=_shift_scratch(shift, shift_drop, n_steps),
        compiler_params=_cparams("arbitrary"),
        name="merge",
    )(x, *o_groups, *lse_groups, hg_o, mem_o, gain_pre.reshape(1, d), w_in, w_in, w_in, b_in, b_in, b_in,
      w_att, w_hg, w_mem, w_out, gain.reshape(1, d), *shift)
    return x1, shifted


def _ffn_kernel(x_ref, gpre_ref, wa_ref, wb_ref, cw_ref, cb_ref, wd_ref, gpost_ref, cbuf_ref,
                out_ref, tail_ref, carry_ref, *, t_seq, tiles_per_seq, tf):
    tm = x_ref.shape[0]
    dff = wa_ref.shape[1]
    x = x_ref[...]
    hn = _rms(x, gpre_ref[...]).astype(BF16)

    if tiles_per_seq >= 1:
        @pl.when((pl.program_id(0) % tiles_per_seq) == 0)
        def _():
            carry_ref[6:8, :] = cbuf_ref[0]

        t_idx = lax.broadcasted_iota(jnp.int32, (tm, 1), 0)
    else:
        n_seq = tm // t_seq
        t_idx = lax.broadcasted_iota(jnp.int32, (tm, 1), 0) % t_seq

    n_chunks = dff // tf

    def up_proj(j):
        cols = slice(j * tf, (j + 1) * tf)
        return (jnp.dot(hn, wa_ref[:, cols], preferred_element_type=F32),
                jnp.dot(hn, wb_ref[:, cols], preferred_element_type=F32))

    acc = jnp.zeros(x.shape, F32)
    ahead = up_proj(0)
    for j in range(n_chunks):
        cols = slice(j * tf, (j + 1) * tf)
        a, up = ahead
        if j + 1 < n_chunks:
            ahead = up_proj(j + 1)
        if tiles_per_seq >= 1:
            prev1 = carry_ref[7:8, cols]
            prev2 = carry_ref[6:7, cols]
            carry_ref[:, cols] = a[tm - 8:tm]
            tail_ref[0, :, cols] = a[tm - 8:tm]
        else:
            prev1 = jnp.broadcast_to(cbuf_ref[:, 1:2, cols], (n_seq, t_seq, tf)).reshape(tm, tf)
            prev2 = jnp.broadcast_to(cbuf_ref[:, 0:1, cols], (n_seq, t_seq, tf)).reshape(tm, tf)
            tail_ref[:, :, cols] = a.reshape(n_seq, t_seq, tf)
        a1 = jnp.where(t_idx >= 1, pltpu.roll(a, 1, 0), prev1)
        a2 = jnp.where(t_idx >= 2, pltpu.roll(a, 2, 0), jnp.where(t_idx == 1, prev1, prev2))
        conv = cb_ref[:, cols] + a2 * cw_ref[0:1, cols] + a1 * cw_ref[1:2, cols] + a * cw_ref[2:3, cols]
        act = conv * _sigmoid(conv) * up
        acc += jnp.dot(act.astype(BF16), wd_ref[cols, :], preferred_element_type=F32)

    out_ref[...] = x + _rms(acc, gpost_ref[...])


def conv_ffn(x, conv_buf, g_pre, w_a, w_b, conv_w, conv_b, w_d, g_post, t_seq, tm, tf):
    m, d = x.shape
    dff = w_a.shape[1]
    n_seq = m // t_seq
    assert m % tm == 0 and dff % tf == 0 and t_seq >= 8
    if tm <= t_seq:
        assert t_seq % tm == 0
        tiles_per_seq = t_seq // tm
        seq_blk = 1
        seq_idx = lambda i: (i // tiles_per_seq, 0, 0)
    else:
        assert tm % t_seq == 0 and t_seq == 8
        tiles_per_seq = 0
        seq_blk = tm // t_seq
        seq_idx = lambda i: (i, 0, 0)
    n_tail = (m // tm) * seq_blk
    resident = lambda shape: pl.BlockSpec(shape, lambda i: (0, 0), pipeline_mode=pl.Buffered(1))
    y, tail = pl.pallas_call(
        functools.partial(_ffn_kernel, t_seq=t_seq, tiles_per_seq=tiles_per_seq, tf=tf),
        grid=(m // tm,),
        in_specs=[pl.BlockSpec((tm, d), lambda i: (i, 0)), resident((1, d)),
                  resident((d, dff)), resident((d, dff)), resident((3, dff)), resident((1, dff)),
                  resident((dff, d)), resident((1, d)),
                  pl.BlockSpec((seq_blk, 2, dff), seq_idx)],
        out_specs=[pl.BlockSpec((tm, d), lambda i: (i, 0)),
                   pl.BlockSpec((seq_blk, 8, dff), lambda i: (i, 0, 0))],
        out_shape=[jax.ShapeDtypeStruct((m, d), F32), jax.ShapeDtypeStruct((n_tail, 8, dff), F32)],
        scratch_shapes=[pltpu.VMEM((8, dff), F32)],
        compiler_params=_cparams("arbitrary"),
        name="conv_ffn",
    )(x, g_pre.reshape(1, d), w_a, w_b, conv_w, conv_b.reshape(1, dff), w_d, g_post.reshape(1, d), conv_buf)
    if tiles_per_seq >= 1:
        tail = tail.reshape(n_seq, tiles_per_seq, 8, dff)[:, -1]
    return y, tail


def _rel_buckets(dil):
    max_exact = REL_BUCKETS // 2
    dist = np.arange(N_LAGS + 1, dtype=np.int32) * dil
    d = np.maximum(dist, 1).astype(np.float32)
    large = max_exact + (np.log(d / np.float32(max_exact)) / np.float32(math.log(REL_MAX_DIST / max_exact))
                         * np.float32(REL_BUCKETS - max_exact)).astype(np.int32)
    large = np.minimum(large, REL_BUCKETS - 1)
    return np.where(dist < max_exact, dist, large)


def _bias_table(rel_bias, g, dil, dist):
    dist = np.asarray(dist)
    ok = (dist >= 0) & (dist % dil == 0) & (dist <= N_LAGS * dil)
    bucket = np.where(ok, _rel_buckets(dil)[np.clip(dist // dil, 0, N_LAGS)], -1)
    onehot = bucket[..., None] == np.arange(REL_BUCKETS)
    heads = rel_bias.astype(F32)[:, g * N_HEADS:(g + 1) * N_HEADS].T
    picked = jnp.sum(jnp.where(onehot[None], heads.reshape((N_HEADS,) + (1,) * dist.ndim + (REL_BUCKETS,)), 0.0), -1)
    return jnp.where(ok[None], picked, NEG_INF)


def _prompt_bias(rel_bias, g, dil):
    return _bias_table(rel_bias, g, dil, (BAND + np.arange(BAND)[:, None] - np.arange(2 * BAND)[None, :]) * dil)


def _sample_bias(rel_bias, g, dil, cached_pos, n_past, t_new):
    t = np.arange(t_new)[:, None]
    return (_bias_table(rel_bias, g, dil, n_past + t - np.asarray(cached_pos)[None, :]),
            _bias_table(rel_bias, g, dil, t - np.arange(t_new)[None, :]))


def _layer(x, batch, t, weights, rel_bias, lb_logits, win_caches, hg_state0, conv_buf0, mem_kv,
           *, prompt, tm_proj, tm_merge, tm_ffn, tq_mem, chunk, n_sub, shifted=None, side_shift=(), side_shift_drop=0):
    (w_in16, b_in, gain_pre, gain_post, hg_norm, w_att, w_hg, w_mem, w_out,
     gain_fpre, gain_fpost, w_fa, w_fb, conv_w, conv_b, w_fd) = weights
    act_dtype = BF16 if prompt else F32
    b_in = b_in.reshape(1, -1)
    if prompt:
        qkv = [proj(x, gain_pre, w_in16, b_in, QKV_BLOCKS(g), batch, t, dil, act_dtype, tm_proj)
               for g, (_, dil) in enumerate(ATT_GROUPS)]
        qkv[0] = qkv[0].reshape(batch, 1, t, QKV_COLS)
        rec = proj(x, gain_pre, w_in16, b_in, REC_BLOCKS, 1, batch * t, 1, act_dtype, tm_proj)
        gates = proj(x, gain_pre, w_in16, b_in, GATE_BLOCKS, 1, batch * t, 1, F32, tm_proj)
    else:
        groups = [QKV_BLOCKS(g) for g in range(len(ATT_GROUPS))] + [REC_BLOCKS, GATE_BLOCKS]
        *qkv, rec, gates = proj_groups(x, gain_pre, w_in16, b_in, groups, [act_dtype] * 4 + [F32], tm_proj)
        qkv = [z.reshape(batch, t, QKV_COLS) for z in qkv]

    o_groups, lse_groups, new_caches = [], [], []
    for g, (win, dil) in enumerate(ATT_GROUPS):
        if prompt:
            o, lse = dilated_prompt(qkv[g], _prompt_bias(rel_bias, g, dil), g, dil, batch, t, 4 if dil == 1 else 1)
        else:
            cache = win_caches[g]
            n_past = cache.shape[1] // KV_ROWS
            assert n_past >= win
            bcache, bnew = _sample_bias(rel_bias, g, dil, _sample_positions(n_past, dil, t), n_past, t)
            bb = math.gcd(batch, max(1, SAMPLE_STEP_BYTES // (cache.shape[1] * HEAD_DIM * 4)))
            o, lse, newc = dilated_sample(qkv[g], cache, shifted[g], bcache, bnew, g, dil, batch, t, bb)
            new_caches.append(newc)
        o_groups.append(o)
        lse_groups.append(lse)

    largest = max(range(len(side_shift)), key=lambda i: side_shift[i].size) if side_shift else None
    with_hgrn = [s for i, s in enumerate(side_shift) if i == largest]
    with_merge = [s for i, s in enumerate(side_shift) if i != largest]
    hg_bb = 1 if prompt else math.gcd(batch, HGRN_SAMPLE_ROWS)
    hg_o, hg_state, from_hgrn = hgrn2(rec, gates, lb_logits, hg_norm, hg_state0, batch, t, chunk, n_sub, hg_bb,
                                      act_dtype, with_hgrn, side_shift_drop)
    mem_bb = 1 if prompt else math.gcd(batch, max(1, SAMPLE_STEP_BYTES // (mem_kv.shape[1] * HEAD_DIM * 4)))
    mem_o = memory_attention(rec, mem_kv, batch, t, tq_mem, mem_bb, act_dtype)
    x1, from_merge = merge(x, o_groups, lse_groups, hg_o, mem_o, gain_pre, w_in16, b_in, w_att, w_hg, w_mem, w_out,
                           gain_post, tm_merge, with_merge, side_shift_drop)
    from_hgrn, from_merge = list(from_hgrn), list(from_merge)
    shifted_out = [from_hgrn.pop(0) if i == largest else from_merge.pop(0) for i in range(len(side_shift))]
    y, tail = conv_ffn(x1, conv_buf0, gain_fpre, w_fa, w_fb, conv_w, conv_b, w_fd, gain_fpost, t, tm_ffn, 2048)
    return y, qkv, new_caches, hg_state, tail[:, 6:8, :], shifted_out


def kernel(x_prompt, x_sample, mem_prompt, cache_win1_kv, cache_win2_kv, cache_win3_kv, cache_mem_kv, state_hgrn, state_ffn_conv, rel_bias, hg_lb_logits, norm_mix_pre, norm_mix_post, w_in, b_in, hg_norm, mem_norm, w_mem_kv, w_br_att, w_br_hg, w_br_mem, w_out, norm_ffn_pre, norm_ffn_post, w_ffn_a, w_ffn_b, ffn_conv_w, ffn_conv_b, w_ffn_d):
    depth = w_in.shape[0]
    assert depth == 1
    bsz, seq, d = x_prompt.shape
    dbsz, dseq, _ = x_sample.shape
    mem_tokens = mem_prompt.shape[1]
    dff = w_ffn_a.shape[2]
    layer = 0

    assert w_in.shape[2] == W_IN_BLOCKS * BRANCH_W
    weights = (w_in[layer].astype(BF16), b_in[layer],
               norm_mix_pre[layer], norm_mix_post[layer], hg_norm[layer],
               w_br_att[layer].astype(BF16), w_br_hg[layer].astype(BF16), w_br_mem[layer].astype(BF16),
               w_out[layer].astype(BF16), norm_ffn_pre[layer], norm_ffn_post[layer],
               w_ffn_a[layer].astype(BF16), w_ffn_b[layer].astype(BF16), ffn_conv_w[layer], ffn_conv_b[layer],
               w_ffn_d[layer].astype(BF16))

    mem_kv = norm_matmul(mem_prompt.reshape(bsz * mem_tokens, d), mem_norm[layer], w_mem_kv[layer].astype(BF16),
                         jnp.zeros((2 * BRANCH_W,), F32), F32, bsz * mem_tokens, 1024)
    as_rows = lambda c: c[layer].reshape(dbsz, c.shape[2] * KV_ROWS, HEAD_DIM)
    win_caches = [as_rows(c) for c in (cache_win1_kv, cache_win2_kv, cache_win3_kv)]
    yp, qkv_p, _, hg_p, conv_p, shifted = _layer(
        x_prompt.reshape(bsz * seq, d), bsz, seq, weights, rel_bias, hg_lb_logits, None,
        jnp.zeros((bsz, N_HEADS, HEAD_DIM, HEAD_DIM), F32), jnp.zeros((bsz, 2, dff), F32),
        mem_kv.reshape(bsz, mem_tokens * KV_ROWS, HEAD_DIM),
        prompt=True, tm_proj=1024, tm_merge=512, tm_ffn=512, tq_mem=1024, chunk=64, n_sub=4,
        side_shift=win_caches, side_shift_drop=dseq * KV_ROWS)
    p_win = []
    for g, (win, dil) in enumerate(ATT_GROUPS):
        n = min(win, seq)
        assert n % dil == 0
        tail = qkv_p[g][:, :, (seq - n) // dil:, BRANCH_W:]
        tail = jnp.swapaxes(tail, 1, 2).astype(F32)
        p_win.append(tail.reshape(1, bsz, n, 2, N_HEADS, HEAD_DIM))

    ys, _, new_caches, hg_s, conv_s, _ = _layer(
        x_sample.reshape(dbsz * dseq, d), dbsz, dseq, weights, rel_bias, hg_lb_logits, win_caches,
        state_hgrn[layer], state_ffn_conv[layer], as_rows(cache_mem_kv),
        prompt=False, tm_proj=dbsz * dseq, tm_merge=dbsz * dseq, tm_ffn=dbsz * dseq, tq_mem=dseq, chunk=16, n_sub=1,
        shifted=shifted)
    s_win = [c.reshape(1, dbsz, c.shape[1] // KV_ROWS, 2, N_HEADS, HEAD_DIM) for c in new_caches]

    return (yp.reshape(bsz, seq, d), ys.reshape(dbsz, dseq, d),
            p_win[0], p_win[1], p_win[2],
            hg_p[None], conv_p[None], mem_kv.reshape(1, bsz, mem_tokens, 2, N_HEADS, HEAD_DIM),
            s_win[0], s_win[1], s_win[2],
            hg_s[None], conv_s[None])
```

```python
import functools
import math

import numpy as np
import jax
import jax.numpy as jnp
from jax import lax
from jax.experimental import pallas as pl
from jax.experimental.pallas import tpu as pltpu

F32 = jnp.float32
BF16 = jnp.bfloat16

NORM_EPS = 1e-6
NEG_INF = -1e30
HEAD_DIM = 128
N_HEADS = 4
BRANCH_W = N_HEADS * HEAD_DIM
N_LAGS = 128
BAND = 128
ATT_GROUPS = ((128, 1), (512, 4), (2048, 16))
REL_BUCKETS = 32
REL_MAX_DIST = 2048
LSE_LANES = 32
HG_BASE = 8
VMEM_LIMIT = 56 * 1024 * 1024
SAMPLE_STEP_BYTES = 8 * 1024 * 1024
DEINTERLEAVE_STRIDE = 4
HGRN_SAMPLE_ROWS = 4

NT_DIMS = (((1,), (1,)), ((), ()))
TN_DIMS = (((0,), (0,)), ((), ()))

W_IN_BLOCKS = 20
QKV_BLOCKS = lambda g: (g, 3 + g, 6 + g)
REC_BLOCKS = (9, 11, 13)
GATE_BLOCKS = (10, 12)
GATE_MERGE_BLOCKS = (7, 8, 9)
QKV_COLS = 3 * BRANCH_W
REC_COLS = 3 * BRANCH_W
REC_HQ, REC_HI, REC_MQ = 0, 1, 2
G_COLS = 2 * BRANCH_W
KV_ROWS = 2 * N_HEADS


def _cparams(*sem):
    return pltpu.CompilerParams(dimension_semantics=sem, vmem_limit_bytes=VMEM_LIMIT)


def _rms(x, gain):
    return x * lax.rsqrt(jnp.mean(x * x, axis=-1, keepdims=True) + NORM_EPS) * gain


def _sigmoid(x):
    return 0.5 * jnp.tanh(0.5 * x) + 0.5


def _norm_matmul_kernel(x_ref, g_ref, w_ref, b_ref, o_ref, hn_ref):
    @pl.when(pl.program_id(1) == 0)
    def _():
        hn_ref[...] = _rms(x_ref[...], g_ref[...]).astype(BF16)

    acc = jnp.dot(hn_ref[...], w_ref[...], preferred_element_type=F32)
    o_ref[...] = (acc + b_ref[...]).astype(o_ref.dtype)


def norm_matmul(x, gain, w, bias, out_dtype, tm, tn):
    m, k = x.shape
    n = w.shape[1]
    assert m % tm == 0 and n % tn == 0
    return pl.pallas_call(
        _norm_matmul_kernel,
        grid=(m // tm, n // tn),
        in_specs=[
            pl.BlockSpec((tm, k), lambda i, j: (i, 0)),
            pl.BlockSpec((1, k), lambda i, j: (0, 0)),
            pl.BlockSpec((k, tn), lambda i, j: (0, j)),
            pl.BlockSpec((1, tn), lambda i, j: (0, j)),
        ],
        out_specs=pl.BlockSpec((tm, tn), lambda i, j: (i, j)),
        out_shape=jax.ShapeDtypeStruct((m, n), out_dtype),
        scratch_shapes=[pltpu.VMEM((tm, k), BF16)],
        compiler_params=_cparams("parallel", "arbitrary"),
        name="norm_matmul",
    )(x, gain.reshape(1, k), w, bias.reshape(1, n))


def _proj_kernel(x_ref, g_ref, *refs, n_blk, dil):
    w_refs, b_refs, o_ref = refs[:n_blk], refs[n_blk:2 * n_blk], refs[2 * n_blk]
    tm = x_ref.shape[0]
    hn = _rms(x_ref[...], g_ref[...]).astype(BF16)
    for n in range(n_blk):
        acc = jnp.dot(hn, w_refs[n][...], preferred_element_type=F32) + b_refs[n][...]
        if dil == 1:
            o_ref[:, n * BRANCH_W:(n + 1) * BRANCH_W] = acc.astype(o_ref.dtype)
            continue
        scr_ref = refs[2 * n_blk + 1]
        for c in range(N_HEADS):
            scr_ref[n * N_HEADS + c] = acc[:, c * HEAD_DIM:(c + 1) * HEAD_DIM]
        for c in range(N_HEADS):
            slab = n * N_HEADS + c
            lo = n * BRANCH_W + c * HEAD_DIM
            if dil <= DEINTERLEAVE_STRIDE:
                for r in range(dil):
                    o_ref[r, :, lo:lo + HEAD_DIM] = (
                        scr_ref[slab, pl.ds(r, tm // dil, stride=dil), :].astype(o_ref.dtype))
                continue
            s1, s2 = DEINTERLEAVE_STRIDE, dil // DEINTERLEAVE_STRIDE
            mid_ref = refs[2 * n_blk + 2]
            for a in range(s1):
                mid_ref[slab, a] = scr_ref[slab, pl.ds(a, tm // s1, stride=s1), :]
            for a in range(s1):
                for b in range(s2):
                    o_ref[a + s1 * b, :, lo:lo + HEAD_DIM] = (
                        mid_ref[slab, a, pl.ds(b, tm // dil, stride=s2), :].astype(o_ref.dtype))


def proj(x, gain, w, bias, blocks, batch, seq, dil, out_dtype, tm):
    m, k = x.shape
    n_blk = len(blocks)
    cols = n_blk * BRANCH_W
    assert m == batch * seq and seq % tm == 0 and tm % dil == 0
    tiles_per_b = seq // tm
    w_specs = [pl.BlockSpec((k, BRANCH_W), lambda i, c=c: (0, c)) for c in blocks]
    b_specs = [pl.BlockSpec((1, BRANCH_W), lambda i, c=c: (0, c)) for c in blocks]
    if dil == 1:
        out_spec = pl.BlockSpec((tm, cols), lambda i: (i, 0))
        out_shape = jax.ShapeDtypeStruct((m, cols), out_dtype)
        scratch = []
    else:
        out_spec = pl.BlockSpec((None, dil, tm // dil, cols), lambda i: (i // tiles_per_b, 0, i % tiles_per_b, 0))
        out_shape = jax.ShapeDtypeStruct((batch, dil, seq // dil, cols), out_dtype)
        scratch = [pltpu.VMEM((n_blk * N_HEADS, tm, HEAD_DIM), F32)]
        if dil > DEINTERLEAVE_STRIDE:
            assert dil % DEINTERLEAVE_STRIDE == 0 and dil // DEINTERLEAVE_STRIDE <= DEINTERLEAVE_STRIDE
            scratch.append(pltpu.VMEM((n_blk * N_HEADS, DEINTERLEAVE_STRIDE, tm // DEINTERLEAVE_STRIDE, HEAD_DIM), F32))
    return pl.pallas_call(
        functools.partial(_proj_kernel, n_blk=n_blk, dil=dil),
        grid=(m // tm,),
        in_specs=[pl.BlockSpec((tm, k), lambda i: (i, 0)), pl.BlockSpec((1, k), lambda i: (0, 0))] + w_specs + b_specs,
        out_specs=out_spec,
        out_shape=out_shape,
        scratch_shapes=scratch,
        compiler_params=_cparams("parallel"),
        name="proj",
    )(x, gain.reshape(1, k), *([w] * n_blk), *([bias] * n_blk))


def _proj_groups_kernel(x_ref, g_ref, *refs, sizes):
    n = sum(sizes)
    w_refs, b_refs, o_refs = refs[:n], refs[n:2 * n], refs[2 * n:]
    hn = _rms(x_ref[...], g_ref[...]).astype(BF16)
    k = 0
    for o_ref, size in zip(o_refs, sizes):
        for j in range(size):
            acc = jnp.dot(hn, w_refs[k][...], preferred_element_type=F32) + b_refs[k][...]
            o_ref[:, j * BRANCH_W:(j + 1) * BRANCH_W] = acc.astype(o_ref.dtype)
            k += 1


def proj_groups(x, gain, w, bias, groups, out_dtypes, tm):
    m, k = x.shape
    assert m % tm == 0
    blocks = [c for grp in groups for c in grp]
    once = lambda shape, c: pl.BlockSpec(shape, lambda i, c=c: (0, c), pipeline_mode=pl.Buffered(1))
    return pl.pallas_call(
        functools.partial(_proj_groups_kernel, sizes=tuple(len(grp) for grp in groups)),
        grid=(m // tm,),
        in_specs=[pl.BlockSpec((tm, k), lambda i: (i, 0)), pl.BlockSpec((1, k), lambda i: (0, 0))]
        + [once((k, BRANCH_W), c) for c in blocks] + [once((1, BRANCH_W), c) for c in blocks],
        out_specs=[pl.BlockSpec((tm, len(grp) * BRANCH_W), lambda i: (i, 0)) for grp in groups],
        out_shape=[jax.ShapeDtypeStruct((m, len(grp) * BRANCH_W), dt) for grp, dt in zip(groups, out_dtypes)],
        compiler_params=_cparams("parallel"),
        name="proj_groups",
    )(x, gain.reshape(1, k), *([w] * len(blocks)), *([bias] * len(blocks)))


def _dil_prompt_kernel(q_ref, kp_ref, kc_ref, vp_ref, vc_ref, bias_ref, o_ref, lse_ref, *scratch, dil, qb):
    scale = 1.0 / math.sqrt(HEAD_DIM)
    no_prev = (pl.program_id(1) == 0) & (lax.broadcasted_iota(jnp.int32, (1, 2 * BAND), 1) < BAND)

    def band_blocks(blocks):
        items = [(r, j, h, slice(j * BAND, (j + 1) * BAND), slice(h * HEAD_DIM, (h + 1) * HEAD_DIM))
                 for r, j in blocks for h in range(N_HEADS)]

        def keys(cur_ref, prev_ref, r, j, rows, hs):
            if j == 0:
                return jnp.concatenate([prev_ref[r, :, hs], cur_ref[r, rows, hs]], axis=0)
            return cur_ref[r, (j - 1) * BAND:(j + 1) * BAND, hs]

        logits = []
        for r, j, h, rows, hs in items:
            l = lax.dot_general(q_ref[r, rows, hs], keys(kc_ref, kp_ref, r, j, rows, hs), NT_DIMS,
                                preferred_element_type=F32) * scale + bias_ref[h]
            logits.append(jnp.where(no_prev, NEG_INF, l) if j == 0 else l)
        probs = []
        for l in logits:
            m = jnp.max(jnp.maximum(l[:, :BAND], l[:, BAND:]), axis=-1, keepdims=True)
            p = jnp.exp(l - m)
            s = jnp.sum(p[:, :BAND] + p[:, BAND:], axis=-1, keepdims=True)
            probs.append((p.astype(BF16), s, jnp.broadcast_to(m + jnp.log(s), (BAND, LSE_LANES))))
        outs = [jnp.dot(p, keys(vc_ref, vp_ref, r, j, rows, hs), preferred_element_type=F32) / s
                for (r, j, h, rows, hs), (p, s, _) in zip(items, probs)]
        return [(outs[N_HEADS * n:N_HEADS * (n + 1)],
                 jnp.concatenate([lse for _, _, lse in probs[N_HEADS * n:N_HEADS * (n + 1)]], axis=1))
                for n in range(len(blocks))]

    if dil == 1:
        for j, (outs, lse) in enumerate(band_blocks([(0, j) for j in range(qb)])):
            for h in range(N_HEADS):
                o_ref[j * BAND:(j + 1) * BAND, h * HEAD_DIM:(h + 1) * HEAD_DIM] = outs[h].astype(o_ref.dtype)
            lse_ref[j * BAND:(j + 1) * BAND, :] = lse
        return

    o_scr, lse_scr = scratch
    per_iter = min(dil, 4)

    def body(it, carry):
        blocks = [(it * per_iter + rr, j) for rr in range(per_iter) for j in range(qb)]
        for (r, j), (outs, lse) in zip(blocks, band_blocks(blocks)):
            for h in range(N_HEADS):
                o_scr[h, pl.ds(r + j * dil * BAND, BAND, stride=dil), :] = outs[h]
            lse_scr[pl.ds(r + j * dil * BAND, BAND, stride=dil), :] = lse
        return carry

    lax.fori_loop(0, dil // per_iter, body, 0)
    for h in range(N_HEADS):
        o_ref[:, h * HEAD_DIM:(h + 1) * HEAD_DIM] = o_scr[h].astype(o_ref.dtype)
    lse_ref[...] = lse_scr[...]


def dilated_prompt(qkv, bias, g, dil, batch, seq, qb):
    assert seq % (dil * BAND * qb) == 0
    sub = seq // dil
    span = dil * BAND * qb
    cur = lambda part: pl.BlockSpec((None, dil, qb * BAND, BRANCH_W), lambda b, i: (b, 0, i, part))
    prev = lambda part: pl.BlockSpec((None, dil, BAND, BRANCH_W),
                                     lambda b, i: (b, 0, jnp.maximum(i * qb - 1, 0), part))
    bias_spec = pl.BlockSpec((N_HEADS, BAND, 2 * BAND), lambda b, i: (0, 0, 0))
    scratch = [] if dil == 1 else [pltpu.VMEM((N_HEADS, span, HEAD_DIM), F32),
                                   pltpu.VMEM((span, N_HEADS * LSE_LANES), F32)]
    o, lse = pl.pallas_call(
        functools.partial(_dil_prompt_kernel, dil=dil, qb=qb),
        grid=(batch, sub // (BAND * qb)),
        in_specs=[cur(0), prev(1), cur(1), prev(2), cur(2), bias_spec],
        out_specs=[pl.BlockSpec((None, span, BRANCH_W), lambda b, i: (b, i, 0)),
                   pl.BlockSpec((None, span, N_HEADS * LSE_LANES), lambda b, i: (b, i, 0))],
        out_shape=[jax.ShapeDtypeStruct((batch, seq, BRANCH_W), BF16),
                   jax.ShapeDtypeStruct((batch, seq, N_HEADS * LSE_LANES), F32)],
        scratch_shapes=scratch,
        compiler_params=_cparams("parallel", "arbitrary"),
        name=f"dilated_prompt_g{g}",
    )(qkv, qkv, qkv, qkv, qkv, bias)
    return o.reshape(batch * seq, BRANCH_W), lse.reshape(batch * seq, N_HEADS * LSE_LANES)


def _dil_sample_kernel(q_ref, kn_ref, vn_ref, cache_ref, bc_ref, bn_ref, shifted_ref, o_ref, lse_ref, newrows_ref,
                       *, t_new):
    del shifted_ref
    scale = 1.0 / math.sqrt(HEAD_DIM)
    bb, n_grp, grp_rows, _ = cache_ref.shape
    n_keys = n_grp * grp_rows // KV_ROWS
    rows_of = lambda b, first: cache_ref[b, :, pl.ds(first, grp_rows // KV_ROWS, stride=KV_ROWS), :].reshape(
        n_keys, HEAD_DIM).astype(BF16)
    heads = [(b, h, slice(h * HEAD_DIM, (h + 1) * HEAD_DIM)) for b in range(bb) for h in range(N_HEADS)]
    logits = []
    for b, h, hs in heads:
        q = q_ref[b, :, hs]
        lc = lax.dot_general(q.astype(BF16), rows_of(b, h), NT_DIMS, preferred_element_type=F32) * scale + bc_ref[h]
        ln = lax.dot_general(q, kn_ref[b, :, hs], NT_DIMS, preferred_element_type=F32) * scale + bn_ref[h]
        logits.append((lc, ln))
    probs = []
    for lc, ln in logits:
        m = jnp.maximum(jnp.max(lc, axis=-1, keepdims=True), jnp.max(ln, axis=-1, keepdims=True))
        pc = jnp.exp(lc - m)
        pn = jnp.exp(ln - m)
        s = jnp.sum(pc, axis=-1, keepdims=True) + jnp.sum(pn, axis=-1, keepdims=True)
        probs.append((pc.astype(BF16), pn, s, m + jnp.log(s)))
    for (b, h, hs), (pc, pn, s, lse) in zip(heads, probs):
        kn = kn_ref[b, :, hs]
        vn = vn_ref[b, :, hs]
        o = (jnp.dot(pc, rows_of(b, N_HEADS + h), preferred_element_type=F32)
             + jnp.dot(pn, vn, preferred_element_type=F32))
        o_ref[b, :, hs] = o / s
        lse_ref[b, :, h * LSE_LANES:(h + 1) * LSE_LANES] = jnp.broadcast_to(lse, (t_new, LSE_LANES))
        newrows_ref[b, pl.ds(h, t_new, stride=KV_ROWS), :] = kn
        newrows_ref[b, pl.ds(N_HEADS + h, t_new, stride=KV_ROWS), :] = vn


def _sample_period(win, dil, t_new):
    return (dil, t_new) if (dil > t_new and win % dil == 0) else (win, win)


def _sample_positions(win, dil, t_new):
    period, take = _sample_period(win, dil, t_new)
    return (np.arange(win // period)[:, None] * period + np.arange(take)[None, :]).reshape(-1)


def dilated_sample(qkv, cache, shifted, bias_cache, bias_new, g, dil, batch, t_new, bb):
    win = cache.shape[1] // KV_ROWS
    assert win % t_new == 0 and shifted.shape == cache.shape and batch % bb == 0
    period, take = _sample_period(win, dil, t_new)
    n_grp = win // period
    cache4 = cache.reshape(batch, n_grp, period * KV_ROWS, HEAD_DIM)
    assert bias_cache.shape == (N_HEADS, t_new, n_grp * take)
    blk = (bb, t_new, BRANCH_W)
    o, lse, newc = pl.pallas_call(
        functools.partial(_dil_sample_kernel, t_new=t_new),
        grid=(batch // bb,),
        in_specs=[pl.BlockSpec(blk, lambda b: (b, 0, 0)),
                  pl.BlockSpec(blk, lambda b: (b, 0, 1)),
                  pl.BlockSpec(blk, lambda b: (b, 0, 2)),
                  pl.BlockSpec((bb, n_grp, take * KV_ROWS, HEAD_DIM), lambda b: (b, 0, 0, 0)),
                  pl.BlockSpec((N_HEADS, t_new, n_grp * take), lambda b: (0, 0, 0)),
                  pl.BlockSpec((N_HEADS, t_new, t_new), lambda b: (0, 0, 0)),
                  pl.BlockSpec(memory_space=pl.ANY)],
        out_specs=[pl.BlockSpec(blk, lambda b: (b, 0, 0)),
                   pl.BlockSpec((bb, t_new, N_HEADS * LSE_LANES), lambda b: (b, 0, 0)),
                   pl.BlockSpec((bb, t_new * KV_ROWS, HEAD_DIM), lambda b: (b, win // t_new - 1, 0))],
        out_shape=[jax.ShapeDtypeStruct((batch, t_new, BRANCH_W), F32),
                   jax.ShapeDtypeStruct((batch, t_new, N_HEADS * LSE_LANES), F32),
                   jax.ShapeDtypeStruct(cache.shape, F32)],
        input_output_aliases={6: 2},
        compiler_params=_cparams("parallel"),
        name=f"dilated_sample_g{g}",
    )(qkv, qkv, qkv, cache4, bias_cache, bias_new, shifted)
    return o.reshape(batch * t_new, BRANCH_W), lse.reshape(batch * t_new, N_HEADS * LSE_LANES), newc


def _mem_attn_kernel(q_ref, kv_ref, o_ref, *, mem):
    scale = 1.0 / math.sqrt(HEAD_DIM)
    heads = [(b, h, slice(h * HEAD_DIM, (h + 1) * HEAD_DIM)) for b in range(q_ref.shape[0]) for h in range(N_HEADS)]
    rows_of = lambda b, first: kv_ref[b, pl.ds(first, mem, stride=KV_ROWS), :].astype(BF16)
    logits = [lax.dot_general(q_ref[b, :, hs].astype(BF16), rows_of(b, h), NT_DIMS, preferred_element_type=F32) * scale
              for b, h, hs in heads]
    probs = []
    for l in logits:
        half = l.shape[1] // 2
        m = jnp.max(jnp.maximum(l[:, :half], l[:, half:]), axis=-1, keepdims=True)
        p = jnp.exp(l - m)
        probs.append((p.astype(BF16), jnp.sum(p[:, :half] + p[:, half:], axis=-1, keepdims=True)))
    for (b, h, hs), (p, s) in zip(heads, probs):
        o = jnp.dot(p, rows_of(b, N_HEADS + h), preferred_element_type=F32)
        o_ref[b, :, hs] = (o / s).astype(o_ref.dtype)


def memory_attention(rec, mem_kv, batch, t, tq, bb, out_dtype):
    mem = mem_kv.shape[1] // KV_ROWS
    assert batch % bb == 0
    x = rec.reshape(batch, t, REC_COLS)
    o = pl.pallas_call(
        functools.partial(_mem_attn_kernel, mem=mem),
        grid=(batch // bb, t // tq),
        in_specs=[pl.BlockSpec((bb, tq, BRANCH_W), lambda b, i: (b, i, REC_MQ)),
                  pl.BlockSpec((bb, mem * KV_ROWS, HEAD_DIM), lambda b, i: (b, 0, 0))],
        out_specs=pl.BlockSpec((bb, tq, BRANCH_W), lambda b, i: (b, i, 0)),
        out_shape=jax.ShapeDtypeStruct((batch, t, BRANCH_W), out_dtype),
        compiler_params=_cparams("parallel", "parallel"),
        name="memory_attention",
    )(x, mem_kv)
    return o.reshape(batch * t, BRANCH_W)


def _hgrn_tables(chunk):
    c, b = chunk, HG_BASE
    t = np.arange(c)[:, None]
    u = np.arange(c)[None, :]
    masks = []
    size = 2 * b
    while size <= c:
        half = size // 2
        masks.append(((t // size) == (u // size)) & ((t % size) >= half) & ((u % size) < half))
        size *= 2
    masks.append(((t // b) == (u // b)) & (u <= t))
    erep = np.zeros((b * HEAD_DIM, c), np.float32)
    for s in range(b):
        erep[s * HEAD_DIM:(s + 1) * HEAD_DIM, np.arange(c) % b == s] = 1.0
    return (u <= t).astype(np.float32), erep, np.stack(masks).astype(np.float32)


def _shift_step(old_refs, new_refs, buf_refs, zero_ref, sem_in, sem_out, sem_zero, step, n_steps, drop):
    slot = step % 2
    other = 1 - slot

    def copies(g, chunk_idx, buf_slot):
        old, new, buf = old_refs[g], new_refs[g], buf_refs[g]
        rows_c = buf.shape[1]
        per_batch = (old.shape[1] - drop) // rows_c
        b = chunk_idx // per_batch
        lo = pl.multiple_of((chunk_idx % per_batch) * rows_c, 8)
        return (pltpu.make_async_copy(old.at[b, pl.ds(lo + drop, rows_c), :], buf.at[buf_slot], sem_in.at[g, buf_slot]),
                pltpu.make_async_copy(buf.at[buf_slot], new.at[b, pl.ds(lo, rows_c), :], sem_out.at[g, buf_slot]))

    def zero_copy(g):
        keep = old_refs[g].shape[1] - drop
        return pltpu.make_async_copy(zero_ref, new_refs[g].at[:, pl.ds(keep, drop), :], sem_zero.at[g])

    groups = range(len(old_refs))

    @pl.when(step == 0)
    def _():
        zero_ref[...] = jnp.zeros_like(zero_ref)
        for g in groups:
            zero_copy(g).start()
            copies(g, 0, 0)[0].start()

    for g in groups:
        copies(g, step, slot)[0].wait()

    @pl.when(step >= 1)
    def _():
        for g in groups:
            copies(g, step - 1, other)[1].wait()

    for g in groups:
        copies(g, step, slot)[1].start()

    @pl.when(step + 1 < n_steps)
    def _():
        for g in groups:
            copies(g, step + 1, other)[0].start()

    @pl.when(step == n_steps - 1)
    def _():
        for g in groups:
            copies(g, step, slot)[1].wait()
            zero_copy(g).wait()


def _shift_scratch(shift, drop, n_steps):
    if not shift:
        return []
    sb = shift[0].shape[0]
    assert n_steps % sb == 0 and all(s.shape[0] == sb and s.shape[2] == HEAD_DIM for s in shift)
    per_batch = n_steps // sb
    chunk_rows = [(s.shape[1] - drop) // per_batch for s in shift]
    assert all(r % 8 == 0 and r * per_batch == s.shape[1] - drop for r, s in zip(chunk_rows, shift))
    n = len(shift)
    return ([pltpu.VMEM((2, r, HEAD_DIM), F32) for r in chunk_rows]
            + [pltpu.VMEM((sb, drop, HEAD_DIM), F32), pltpu.SemaphoreType.DMA((n, 2)),
               pltpu.SemaphoreType.DMA((n, 2)), pltpu.SemaphoreType.DMA((n,))])


def _hgrn_kernel(q_ref, v_ref, f_ref, gate_ref, lbl_ref, gn_ref, s0_ref, tri_ref, erep_ref, mask_ref, *rest,
                 chunk, rows_in, n_sub, bb, n_levels, n_shift, shift_drop, n_steps):
    old_refs, rest = rest[:n_shift], rest[n_shift:]
    o_ref, s_out_ref = rest[:2]
    new_refs, rest = rest[2:2 + n_shift], rest[2 + n_shift:]
    st_ref = rest[0]
    c = chunk
    step = pl.program_id(1)
    if n_shift:
        buf_refs, (zero_ref, sem_in, sem_out, sem_zero) = rest[1:1 + n_shift], rest[1 + n_shift:]
        _shift_step(old_refs, new_refs, buf_refs, zero_ref, sem_in, sem_out, sem_zero,
                    pl.program_id(0) * pl.num_programs(1) + step, n_steps, shift_drop)

    @pl.when(step == 0)
    def _():
        for b in range(bb):
            for h in range(N_HEADS):
                st_ref[b * N_HEADS + h] = s0_ref[b, h].T

    lbl = lbl_ref[...]
    e = jnp.exp(lbl - jnp.max(lbl, axis=0, keepdims=True))
    lb = e[0:1] / jnp.sum(e, axis=0, keepdims=True)
    tri = tri_ref[...]

    def rows(ref, b, ci):
        x = ref[b, ci * rows_in:(ci + 1) * rows_in, :].astype(F32)
        if rows_in < c:
            x = jnp.concatenate([x, jnp.zeros((c - rows_in, x.shape[1]), F32)], axis=0)
        return x

    def at_row(x, size, idx):
        x3 = x.reshape(c // size, size, x.shape[1])
        return jnp.broadcast_to(x3[:, idx:idx + 1, :], x3.shape).reshape(x.shape)

    chunk_ids = [(b, ci) for b in range(bb) for ci in range(n_sub)]
    chunks = range(len(chunk_ids))
    heads = [(h, slice(h * HEAD_DIM, (h + 1) * HEAD_DIM)) for h in range(N_HEADS)]

    q, v, kk, cum = [], [], [], []
    for b, ci in chunk_ids:
        q.append(rows(q_ref, b, ci))
        v.append(rows(v_ref, b, ci).astype(BF16))
        f = lb + (1.0 - lb) * _sigmoid(rows(f_ref, b, ci))
        g = jnp.log(f)
        k_in = 1.0 - f
        if rows_in < c:
            live = lax.broadcasted_iota(jnp.int32, (c, 1), 0) < rows_in
            g = jnp.where(live, g, 0.0)
            k_in = jnp.where(live, k_in, 0.0)
        kk.append(k_in)
        g1 = g.astype(BF16)
        r1 = g - g1.astype(F32)
        g2 = r1.astype(BF16)
        g3 = (r1 - g2.astype(F32)).astype(BF16)
        cum.append(jnp.dot(tri, g1, preferred_element_type=F32) + jnp.dot(tri, g2, preferred_element_type=F32)
                   + jnp.dot(tri, g3, preferred_element_type=F32))

    q_inter, k_end, g_end, q_lvl, k_lvl, x_diag = [], [], [], [], [], []
    for ci in chunks:
        q_inter.append((q[ci] * jnp.exp(cum[ci])).astype(BF16))
        g_end.append(cum[ci][c - 1:c])
        k_end.append((kk[ci] * jnp.exp(g_end[ci] - cum[ci])).astype(BF16))
        ql, kl = [], []
        for l in range(n_levels):
            size = 2 * HG_BASE << l
            ref = at_row(cum[ci], size, size // 2 - 1)
            ql.append((q[ci] * jnp.exp(cum[ci] - ref)).astype(BF16))
            kl.append((kk[ci] * jnp.exp(ref - cum[ci])).astype(BF16))
        q_lvl.append(ql)
        k_lvl.append(kl)
        xd = []
        for s in range(HG_BASE):
            decay = jnp.exp(jnp.minimum(cum[ci] - at_row(cum[ci], HG_BASE, s), 0.0))
            xd.append((q[ci] * at_row(kk[ci], HG_BASE, s) * decay).astype(BF16))
        x_diag.append(xd)

    level_mask = [mask_ref[l] > 0.5 for l in range(n_levels)]
    scores, state_in = [], []
    for ci in chunks:
        sc, si = [], []
        for h, hs in heads:
            s_h = mask_ref[n_levels] * jnp.dot(jnp.concatenate([x[:, hs] for x in x_diag[ci]], axis=1),
                                               erep_ref[...], preferred_element_type=F32)
            for l in range(n_levels):
                s_h += jnp.where(level_mask[l], lax.dot_general(q_lvl[ci][l][:, hs], k_lvl[ci][l][:, hs], NT_DIMS,
                                                                preferred_element_type=F32), 0.0)
            sc.append(s_h.astype(BF16))
            si.append(lax.dot_general(v[ci][:, hs], k_end[ci][:, hs], TN_DIMS, preferred_element_type=F32))
        scores.append(sc)
        state_in.append(si)

    states = {}
    for b in range(bb):
        for h, hs in heads:
            st = st_ref[b * N_HEADS + h]
            for ci in range(n_sub):
                idx = b * n_sub + ci
                states[idx, h] = st.astype(BF16)
                st = st * jnp.exp(g_end[idx][:, hs]) + state_in[idx][h]
            st_ref[b * N_HEADS + h] = st
    for idx, (b, ci) in enumerate(chunk_ids):
        gate = _sigmoid(rows(gate_ref, b, ci))
        for h, hs in heads:
            o = (lax.dot_general(q_inter[idx][:, hs], states[idx, h], NT_DIMS, preferred_element_type=F32)
                 + jnp.dot(scores[idx][h], v[idx][:, hs], preferred_element_type=F32))
            y = _rms(o, gn_ref[...]) * gate[:, hs]
            o_ref[b, ci * rows_in:(ci + 1) * rows_in, hs] = y[0:rows_in].astype(o_ref.dtype)

    @pl.when(step == pl.num_programs(1) - 1)
    def _():
        for b in range(bb):
            for h in range(N_HEADS):
                s_out_ref[b, h] = st_ref[b * N_HEADS + h].T


def hgrn2(rec, gates, lb_logits, hg_norm, state0, batch, t, chunk, n_sub, bb, out_dtype, shift=(), shift_drop=0):
    rows_in = min(chunk, t)
    step_rows = rows_in * n_sub
    assert t % step_rows == 0 and chunk % HG_BASE == 0 and (n_sub == 1 or rows_in == chunk) and batch % bb == 0
    n_steps = (batch // bb) * (t // step_rows)
    n_shift = len(shift)
    any_spec = pl.BlockSpec(memory_space=pl.ANY)
    shift_scratch = _shift_scratch(shift, shift_drop, n_steps)
    tri, erep, masks = _hgrn_tables(chunk)
    n_levels = masks.shape[0] - 1
    a = rec.reshape(batch, t, REC_COLS)
    gt = gates.reshape(batch, t, G_COLS)
    blk = (bb, step_rows, BRANCH_W)
    full = lambda arr: pl.BlockSpec(arr.shape, lambda b, i: (0,) * arr.ndim)
    st_spec = pl.BlockSpec((bb, N_HEADS, HEAD_DIM, HEAD_DIM), lambda b, i: (b, 0, 0, 0))
    tri = jnp.asarray(tri, BF16)
    erep = jnp.asarray(erep, BF16)
    masks = jnp.asarray(masks, F32)
    lbl = lb_logits.astype(F32)
    gn = hg_norm.reshape(1, HEAD_DIM).astype(F32)
    o, s_fin, *shifted = pl.pallas_call(
        functools.partial(_hgrn_kernel, chunk=chunk, rows_in=rows_in, n_sub=n_sub, bb=bb, n_levels=n_levels,
                          n_shift=n_shift, shift_drop=shift_drop, n_steps=n_steps),
        grid=(batch // bb, t // step_rows),
        in_specs=[pl.BlockSpec(blk, lambda b, i: (b, i, REC_HQ)),
                  pl.BlockSpec(blk, lambda b, i: (b, i, REC_HI)),
                  pl.BlockSpec(blk, lambda b, i: (b, i, 0)),
                  pl.BlockSpec(blk, lambda b, i: (b, i, 1)),
                  full(lbl), full(gn), st_spec, full(tri), full(erep), full(masks)] + [any_spec] * n_shift,
        out_specs=[pl.BlockSpec(blk, lambda b, i: (b, i, 0)), st_spec] + [any_spec] * n_shift,
        out_shape=[jax.ShapeDtypeStruct((batch, t, BRANCH_W), out_dtype),
                   jax.ShapeDtypeStruct(state0.shape, F32)] + [jax.ShapeDtypeStruct(s.shape, F32) for s in shift],
        scratch_shapes=[pltpu.VMEM((bb * N_HEADS, HEAD_DIM, HEAD_DIM), F32)] + shift_scratch,
        compiler_params=_cparams("arbitrary", "arbitrary"),
        name="hgrn2",
    )(a, a, gt, gt, lbl, gn, state0, tri, erep, masks, *shift)
    return o.reshape(batch * t, BRANCH_W), s_fin, shifted


def _merge_kernel(x_ref, o1_ref, o2_ref, o3_ref, l1_ref, l2_ref, l3_ref, hg_ref, mem_ref, gpre_ref,
                  wga_ref, wgh_ref, wgm_ref, bga_ref, bgh_ref, bgm_ref,
                  wa_ref, wh_ref, wm_ref, wo_ref, gain_ref, *rest, n_shift, shift_drop, n_steps):
    old_refs, out_ref, new_refs, scratch = rest[:n_shift], rest[n_shift], rest[n_shift + 1:2 * n_shift + 1], \
        rest[2 * n_shift + 1:]
    if n_shift:
        _shift_step(old_refs, new_refs, scratch[:n_shift], *scratch[n_shift:], pl.program_id(0), n_steps, shift_drop)
    tm = x_ref.shape[0]
    x = x_ref[...]
    hn = _rms(x, gpre_ref[...]).astype(BF16)
    gate_logits = [jnp.dot(hn, w_ref[...], preferred_element_type=F32) + b_ref[...]
                   for w_ref, b_ref in ((wga_ref, bga_ref), (wgh_ref, bgh_ref), (wgm_ref, bgm_ref))]
    from_hg = jnp.dot(hg_ref[...].astype(BF16), wh_ref[...], preferred_element_type=F32)
    from_mem = jnp.dot(mem_ref[...].astype(BF16), wm_ref[...], preferred_element_type=F32)
    l1, l2, l3 = l1_ref[...], l2_ref[...], l3_ref[...]
    m = jnp.maximum(jnp.maximum(l1, l2), l3)
    e1, e2, e3 = jnp.exp(l1 - m), jnp.exp(l2 - m), jnp.exp(l3 - m)
    den = e1 + e2 + e3
    w1, w2, w3 = e1 / den, e2 / den, e3 / den
    att = []
    for h in range(N_HEADS):
        hs = slice(h * HEAD_DIM, (h + 1) * HEAD_DIM)
        col = slice(h * LSE_LANES, h * LSE_LANES + 1)
        bc = lambda w: jnp.broadcast_to(w[:, col], (tm, HEAD_DIM))
        att.append((bc(w1) * o1_ref[:, hs].astype(F32) + bc(w2) * o2_ref[:, hs].astype(F32)
                    + bc(w3) * o3_ref[:, hs].astype(F32)).astype(BF16))
    att = jnp.concatenate(att, axis=1)
    ga, gh, gm = (_sigmoid(z) for z in gate_logits)
    merged = ga * jnp.dot(att, wa_ref[...], preferred_element_type=F32) + gh * from_hg + gm * from_mem
    y = jnp.dot(merged.astype(BF16), wo_ref[...], preferred_element_type=F32)
    out_ref[...] = x + _rms(y, gain_ref[...])


def merge(x, o_groups, lse_groups, hg_o, mem_o, gain_pre, w_in, b_in, w_att, w_hg, w_mem, w_out, gain, tm,
          shift=(), shift_drop=0):
    m, d = x.shape
    assert m % tm == 0 and GATE_MERGE_BLOCKS[0] * d == 14 * BRANCH_W
    n_steps = m // tm
    n_shift = len(shift)
    any_spec = pl.BlockSpec(memory_space=pl.ANY)
    row = lambda w: pl.BlockSpec((tm, w), lambda i: (i, 0))
    resident = lambda shape, j=0: pl.BlockSpec(shape, lambda i: (0, j), pipeline_mode=pl.Buffered(1))
    x1, *shifted = pl.pallas_call(
        functools.partial(_merge_kernel, n_shift=n_shift, shift_drop=shift_drop, n_steps=n_steps),
        grid=(n_steps,),
        in_specs=[row(d), row(BRANCH_W), row(BRANCH_W), row(BRANCH_W),
                  row(N_HEADS * LSE_LANES), row(N_HEADS * LSE_LANES), row(N_HEADS * LSE_LANES),
                  row(BRANCH_W), row(BRANCH_W), resident((1, d))]
        + [resident((d, d), j) for j in GATE_MERGE_BLOCKS] + [resident((1, d), j) for j in GATE_MERGE_BLOCKS]
        + [resident(w_att.shape), resident(w_hg.shape), resident(w_mem.shape), resident(w_out.shape),
           resident((1, d))] + [any_spec] * n_shift,
        out_specs=[row(d)] + [any_spec] * n_shift,
        out_shape=[jax.ShapeDtypeStruct((m, d), F32)] + [jax.ShapeDtypeStruct(s.shape, F32) for s in shift],
        scratch_shapes=_shift_scratch(shift, shift_drop, n_steps),
        compiler_params=_cparams("arbitrary"),
        name="merge",
    )(x, *o_groups, *lse_groups, hg_o, mem_o, gain_pre.reshape(1, d), w_in, w_in, w_in, b_in, b_in, b_in,
      w_att, w_hg, w_mem, w_out, gain.reshape(1, d), *shift)
    return x1, shifted


def _ffn_kernel(x_ref, gpre_ref, wa_ref, wb_ref, cw_ref, cb_ref, wd_ref, gpost_ref, cbuf_ref,
                out_ref, tail_ref, carry_ref, *, t_seq, tiles_per_seq, tf):
    tm = x_ref.shape[0]
    dff = wa_ref.shape[1]
    x = x_ref[...]
    hn = _rms(x, gpre_ref[...]).astype(BF16)

    if tiles_per_seq >= 1:
        @pl.when((pl.program_id(0) % tiles_per_seq) == 0)
        def _():
            carry_ref[6:8, :] = cbuf_ref[0]

        t_idx = lax.broadcasted_iota(jnp.int32, (tm, 1), 0)
    else:
        n_seq = tm // t_seq
        t_idx = lax.broadcasted_iota(jnp.int32, (tm, 1), 0) % t_seq

    n_chunks = dff // tf

    def up_proj(j):
        cols = slice(j * tf, (j + 1) * tf)
        return (jnp.dot(hn, wa_ref[:, cols], preferred_element_type=F32),
                jnp.dot(hn, wb_ref[:, cols], preferred_element_type=F32))

    acc = jnp.zeros(x.shape, F32)
    ahead = up_proj(0)
    for j in range(n_chunks):
        cols = slice(j * tf, (j + 1) * tf)
        a, up = ahead
        if j + 1 < n_chunks:
            ahead = up_proj(j + 1)
        if tiles_per_seq >= 1:
            prev1 = carry_ref[7:8, cols]
            prev2 = carry_ref[6:7, cols]
            carry_ref[:, cols] = a[tm - 8:tm]
            tail_ref[0, :, cols] = a[tm - 8:tm]
        else:
            prev1 = jnp.broadcast_to(cbuf_ref[:, 1:2, cols], (n_seq, t_seq, tf)).reshape(tm, tf)
            prev2 = jnp.broadcast_to(cbuf_ref[:, 0:1, cols], (n_seq, t_seq, tf)).reshape(tm, tf)
            tail_ref[:, :, cols] = a.reshape(n_seq, t_seq, tf)
        a1 = jnp.where(t_idx >= 1, pltpu.roll(a, 1, 0), prev1)
        a2 = jnp.where(t_idx >= 2, pltpu.roll(a, 2, 0), jnp.where(t_idx == 1, prev1, prev2))
        conv = cb_ref[:, cols] + a2 * cw_ref[0:1, cols] + a1 * cw_ref[1:2, cols] + a * cw_ref[2:3, cols]
        act = conv * _sigmoid(conv) * up
        acc += jnp.dot(act.astype(BF16), wd_ref[cols, :], preferred_element_type=F32)

    out_ref[...] = x + _rms(acc, gpost_ref[...])


def conv_ffn(x, conv_buf, g_pre, w_a, w_b, conv_w, conv_b, w_d, g_post, t_seq, tm, tf):
    m, d = x.shape
    dff = w_a.shape[1]
    n_seq = m // t_seq
    assert m % tm == 0 and dff % tf == 0 and t_seq >= 8
    if tm <= t_seq:
        assert t_seq % tm == 0
        tiles_per_seq = t_seq // tm
        seq_blk = 1
        seq_idx = lambda i: (i // tiles_per_seq, 0, 0)
    else:
        assert tm % t_seq == 0 and t_seq == 8
        tiles_per_seq = 0
        seq_blk = tm // t_seq
        seq_idx = lambda i: (i, 0, 0)
    n_tail = (m // tm) * seq_blk
    resident = lambda shape: pl.BlockSpec(shape, lambda i: (0, 0), pipeline_mode=pl.Buffered(1))
    y, tail = pl.pallas_call(
        functools.partial(_ffn_kernel, t_seq=t_seq, tiles_per_seq=tiles_per_seq, tf=tf),
        grid=(m // tm,),
        in_specs=[pl.BlockSpec((tm, d), lambda i: (i, 0)), resident((1, d)),
                  resident((d, dff)), resident((d, dff)), resident((3, dff)), resident((1, dff)),
                  resident((dff, d)), resident((1, d)),
                  pl.BlockSpec((seq_blk, 2, dff), seq_idx)],
        out_specs=[pl.BlockSpec((tm, d), lambda i: (i, 0)),
                   pl.BlockSpec((seq_blk, 8, dff), lambda i: (i, 0, 0))],
        out_shape=[jax.ShapeDtypeStruct((m, d), F32), jax.ShapeDtypeStruct((n_tail, 8, dff), F32)],
        scratch_shapes=[pltpu.VMEM((8, dff), F32)],
        compiler_params=_cparams("arbitrary"),
        name="conv_ffn",
    )(x, g_pre.reshape(1, d), w_a, w_b, conv_w, conv_b.reshape(1, dff), w_d, g_post.reshape(1, d), conv_buf)
    if tiles_per_seq >= 1:
        tail = tail.reshape(n_seq, tiles_per_seq, 8, dff)[:, -1]
    return y, tail


def _rel_buckets(dil):
    max_exact = REL_BUCKETS // 2
    dist = np.arange(N_LAGS + 1, dtype=np.int32) * dil
    d = np.maximum(dist, 1).astype(np.float32)
    large = max_exact + (np.log(d / np.float32(max_exact)) / np.float32(math.log(REL_MAX_DIST / max_exact))
                         * np.float32(REL_BUCKETS - max_exact)).astype(np.int32)
    large = np.minimum(large, REL_BUCKETS - 1)
    return np.where(dist < max_exact, dist, large)


def _bias_table(rel_bias, g, dil, dist):
    dist = np.asarray(dist)
    ok = (dist >= 0) & (dist % dil == 0) & (dist <= N_LAGS * dil)
    bucket = np.where(ok, _rel_buckets(dil)[np.clip(dist // dil, 0, N_LAGS)], -1)
    onehot = bucket[..., None] == np.arange(REL_BUCKETS)
    heads = rel_bias.astype(F32)[:, g * N_HEADS:(g + 1) * N_HEADS].T
    picked = jnp.sum(jnp.where(onehot[None], heads.reshape((N_HEADS,) + (1,) * dist.ndim + (REL_BUCKETS,)), 0.0), -1)
    return jnp.where(ok[None], picked, NEG_INF)


def _prompt_bias(rel_bias, g, dil):
    return _bias_table(rel_bias, g, dil, (BAND + np.arange(BAND)[:, None] - np.arange(2 * BAND)[None, :]) * dil)


def _sample_bias(rel_bias, g, dil, cached_pos, n_past, t_new):
    t = np.arange(t_new)[:, None]
    return (_bias_table(rel_bias, g, dil, n_past + t - np.asarray(cached_pos)[None, :]),
            _bias_table(rel_bias, g, dil, t - np.arange(t_new)[None, :]))


def _layer(x, batch, t, weights, rel_bias, lb_logits, win_caches, hg_state0, conv_buf0, mem_kv,
           *, prompt, tm_proj, tm_merge, tm_ffn, tq_mem, chunk, n_sub, shifted=None, side_shift=(), side_shift_drop=0):
    (w_in16, b_in, gain_pre, gain_post, hg_norm, w_att, w_hg, w_mem, w_out,
     gain_fpre, gain_fpost, w_fa, w_fb, conv_w, conv_b, w_fd) = weights
    act_dtype = BF16 if prompt else F32
    b_in = b_in.reshape(1, -1)
    if prompt:
        tm_plain = 2 * tm_proj
        qkv = [proj(x, gain_pre, w_in16, b_in, QKV_BLOCKS(g), batch, t, dil, act_dtype,
                    tm_plain if dil == 1 else tm_proj) for g, (_, dil) in enumerate(ATT_GROUPS)]
        qkv[0] = qkv[0].reshape(batch, 1, t, QKV_COLS)
        rec = proj(x, gain_pre, w_in16, b_in, REC_BLOCKS, 1, batch * t, 1, act_dtype, tm_plain)
        gates = proj(x, gain_pre, w_in16, b_in, GATE_BLOCKS, 1, batch * t, 1, F32, tm_plain)
    else:
        groups = [QKV_BLOCKS(g) for g in range(len(ATT_GROUPS))] + [REC_BLOCKS, GATE_BLOCKS]
        *qkv, rec, gates = proj_groups(x, gain_pre, w_in16, b_in, groups, [act_dtype] * 4 + [F32], tm_proj)
        qkv = [z.reshape(batch, t, QKV_COLS) for z in qkv]

    o_groups, lse_groups, new_caches = [], [], []
    for g, (win, dil) in enumerate(ATT_GROUPS):
        if prompt:
            o, lse = dilated_prompt(qkv[g], _prompt_bias(rel_bias, g, dil), g, dil, batch, t, 4 if dil == 1 else 1)
        else:
            cache = win_caches[g]
            n_past = cache.shape[1] // KV_ROWS
            assert n_past >= win
            bcache, bnew = _sample_bias(rel_bias, g, dil, _sample_positions(n_past, dil, t), n_past, t)
            bb = math.gcd(batch, max(1, SAMPLE_STEP_BYTES // (cache.shape[1] * HEAD_DIM * 4)))
            o, lse, newc = dilated_sample(qkv[g], cache, shifted[g], bcache, bnew, g, dil, batch, t, bb)
            new_caches.append(newc)
        o_groups.append(o)
        lse_groups.append(lse)

    largest = max(range(len(side_shift)), key=lambda i: side_shift[i].size) if side_shift else None
    with_hgrn = [s for i, s in enumerate(side_shift) if i == largest]
    with_merge = [s for i, s in enumerate(side_shift) if i != largest]
    hg_bb = 1 if prompt else math.gcd(batch, HGRN_SAMPLE_ROWS)
    hg_o, hg_state, from_hgrn = hgrn2(rec, gates, lb_logits, hg_norm, hg_state0, batch, t, chunk, n_sub, hg_bb,
                                      act_dtype, with_hgrn, side_shift_drop)
    mem_bb = 1 if prompt else math.gcd(batch, max(1, SAMPLE_STEP_BYTES // (mem_kv.shape[1] * HEAD_DIM * 4)))
    mem_o = memory_attention(rec, mem_kv, batch, t, tq_mem, mem_bb, act_dtype)
    x1, from_merge = merge(x, o_groups, lse_groups, hg_o, mem_o, gain_pre, w_in16, b_in, w_att, w_hg, w_mem, w_out,
                           gain_post, tm_merge, with_merge, side_shift_drop)
    from_hgrn, from_merge = list(from_hgrn), list(from_merge)
    shifted_out = [from_hgrn.pop(0) if i == largest else from_merge.pop(0) for i in range(len(side_shift))]
    y, tail = conv_ffn(x1, conv_buf0, gain_fpre, w_fa, w_fb, conv_w, conv_b, w_fd, gain_fpost, t, tm_ffn, 2048)
    return y, qkv, new_caches, hg_state, tail[:, 6:8, :], shifted_out


def kernel(x_prompt, x_sample, mem_prompt, cache_win1_kv, cache_win2_kv, cache_win3_kv, cache_mem_kv, state_hgrn, state_ffn_conv, rel_bias, hg_lb_logits, norm_mix_pre, norm_mix_post, w_in, b_in, hg_norm, mem_norm, w_mem_kv, w_br_att, w_br_hg, w_br_mem, w_out, norm_ffn_pre, norm_ffn_post, w_ffn_a, w_ffn_b, ffn_conv_w, ffn_conv_b, w_ffn_d):
    depth = w_in.shape[0]
    assert depth == 1
    bsz, seq, d = x_prompt.shape
    dbsz, dseq, _ = x_sample.shape
    mem_tokens = mem_prompt.shape[1]
    dff = w_ffn_a.shape[2]
    layer = 0

    assert w_in.shape[2] == W_IN_BLOCKS * BRANCH_W
    weights = (w_in[layer].astype(BF16), b_in[layer],
               norm_mix_pre[layer], norm_mix_post[layer], hg_norm[layer],
               w_br_att[layer].astype(BF16), w_br_hg[layer].astype(BF16), w_br_mem[layer].astype(BF16),
               w_out[layer].astype(BF16), norm_ffn_pre[layer], norm_ffn_post[layer],
               w_ffn_a[layer].astype(BF16), w_ffn_b[layer].astype(BF16), ffn_conv_w[layer], ffn_conv_b[layer],
               w_ffn_d[layer].astype(BF16))

    mem_kv = norm_matmul(mem_prompt.reshape(bsz * mem_tokens, d), mem_norm[layer], w_mem_kv[layer].astype(BF16),
                         jnp.zeros((2 * BRANCH_W,), F32), F32, bsz * mem_tokens, 1024)
    as_rows = lambda c: c[layer].reshape(dbsz, c.shape[2] * KV_ROWS, HEAD_DIM)
    win_caches = [as_rows(c) for c in (cache_win1_kv, cache_win2_kv, cache_win3_kv)]
    yp, qkv_p, _, hg_p, conv_p, shifted = _layer(
        x_prompt.reshape(bsz * seq, d), bsz, seq, weights, rel_bias, hg_lb_logits, None,
        jnp.zeros((bsz, N_HEADS, HEAD_DIM, HEAD_DIM), F32), jnp.zeros((bsz, 2, dff), F32),
        mem_kv.reshape(bsz, mem_tokens * KV_ROWS, HEAD_DIM),
        prompt=True, tm_proj=1024, tm_merge=512, tm_ffn=512, tq_mem=1024, chunk=64, n_sub=8,
        side_shift=win_caches, side_shift_drop=dseq * KV_ROWS)
    p_win = []
    for g, (win, dil) in enumerate(ATT_GROUPS):
        n = min(win, seq)
        assert n % dil == 0
        tail = qkv_p[g][:, :, (seq - n) // dil:, BRANCH_W:]
        tail = jnp.swapaxes(tail, 1, 2).astype(F32)
        p_win.append(tail.reshape(1, bsz, n, 2, N_HEADS, HEAD_DIM))

    ys, _, new_caches, hg_s, conv_s, _ = _layer(
        x_sample.reshape(dbsz * dseq, d), dbsz, dseq, weights, rel_bias, hg_lb_logits, win_caches,
        state_hgrn[layer], state_ffn_conv[layer], as_rows(cache_mem_kv),
        prompt=False, tm_proj=dbsz * dseq, tm_merge=dbsz * dseq, tm_ffn=dbsz * dseq, tq_mem=dseq, chunk=16, n_sub=1,
        shifted=shifted)
    s_win = [c.reshape(1, dbsz, c.shape[1] // KV_ROWS, 2, N_HEADS, HEAD_DIM) for c in new_caches]

    return (yp.reshape(bsz, seq, d), ys.reshape(dbsz, dseq, d),
            p_win[0], p_win[1], p_win[2],
            hg_p[None], conv_p[None], mem_kv.reshape(1, bsz, mem_tokens, 2, N_HEADS, HEAD_DIM),
            s_win[0], s_win[1], s_win[2],
            hg_s[None], conv_s[None])
```

```python
import functools
import math

import numpy as np
import jax
import jax.numpy as jnp
from jax import lax
from jax.experimental import pallas as pl
from jax.experimental.pallas import tpu as pltpu

F32 = jnp.float32
BF16 = jnp.bfloat16

NORM_EPS = 1e-6
NEG_INF = -1e30
HEAD_DIM = 128
N_HEADS = 4
BRANCH_W = N_HEADS * HEAD_DIM
N_LAGS = 128
BAND = 128
ATT_GROUPS = ((128, 1), (512, 4), (2048, 16))
REL_BUCKETS = 32
REL_MAX_DIST = 2048
LSE_LANES = 32
HG_BASE = 8
VMEM_LIMIT = 56 * 1024 * 1024
SAMPLE_STEP_BYTES = 8 * 1024 * 1024
DEINTERLEAVE_STRIDE = 4
HGRN_SAMPLE_ROWS = 4
ATT_STEP_BLOCKS = 8

NT_DIMS = (((1,), (1,)), ((), ()))
TN_DIMS = (((0,), (0,)), ((), ()))

W_IN_BLOCKS = 20
QKV_BLOCKS = lambda g: (g, 3 + g, 6 + g)
REC_BLOCKS = (9, 11, 13)
GATE_BLOCKS = (10, 12)
GATE_MERGE_BLOCKS = (7, 8, 9)
QKV_COLS = 3 * BRANCH_W
REC_COLS = 3 * BRANCH_W
REC_HQ, REC_HI, REC_MQ = 0, 1, 2
G_COLS = 2 * BRANCH_W
KV_ROWS = 2 * N_HEADS


def _cparams(*sem):
    return pltpu.CompilerParams(dimension_semantics=sem, vmem_limit_bytes=VMEM_LIMIT)


def _rms(x, gain):
    return x * lax.rsqrt(jnp.mean(x * x, axis=-1, keepdims=True) + NORM_EPS) * gain


def _sigmoid(x):
    return 0.5 * jnp.tanh(0.5 * x) + 0.5


def _norm_matmul_kernel(x_ref, g_ref, w_ref, b_ref, o_ref, hn_ref):
    @pl.when(pl.program_id(1) == 0)
    def _():
        hn_ref[...] = _rms(x_ref[...], g_ref[...]).astype(BF16)

    acc = jnp.dot(hn_ref[...], w_ref[...], preferred_element_type=F32)
    o_ref[...] = (acc + b_ref[...]).astype(o_ref.dtype)


def norm_matmul(x, gain, w, bias, out_dtype, tm, tn):
    m, k = x.shape
    n = w.shape[1]
    assert m % tm == 0 and n % tn == 0
    return pl.pallas_call(
        _norm_matmul_kernel,
        grid=(m // tm, n // tn),
        in_specs=[
            pl.BlockSpec((tm, k), lambda i, j: (i, 0)),
            pl.BlockSpec((1, k), lambda i, j: (0, 0)),
            pl.BlockSpec((k, tn), lambda i, j: (0, j)),
            pl.BlockSpec((1, tn), lambda i, j: (0, j)),
        ],
        out_specs=pl.BlockSpec((tm, tn), lambda i, j: (i, j)),
        out_shape=jax.ShapeDtypeStruct((m, n), out_dtype),
        scratch_shapes=[pltpu.VMEM((tm, k), BF16)],
        compiler_params=_cparams("parallel", "arbitrary"),
        name="norm_matmul",
    )(x, gain.reshape(1, k), w, bias.reshape(1, n))


def _proj_kernel(x_ref, g_ref, *refs, n_blk, dil):
    w_refs, b_refs, o_ref = refs[:n_blk], refs[n_blk:2 * n_blk], refs[2 * n_blk]
    tm = x_ref.shape[0]
    hn = _rms(x_ref[...], g_ref[...]).astype(BF16)
    for n in range(n_blk):
        acc = jnp.dot(hn, w_refs[n][...], preferred_element_type=F32) + b_refs[n][...]
        if dil == 1:
            o_ref[:, n * BRANCH_W:(n + 1) * BRANCH_W] = acc.astype(o_ref.dtype)
            continue
        scr_ref = refs[2 * n_blk + 1]
        for c in range(N_HEADS):
            scr_ref[n * N_HEADS + c] = acc[:, c * HEAD_DIM:(c + 1) * HEAD_DIM]
        for c in range(N_HEADS):
            slab = n * N_HEADS + c
            lo = n * BRANCH_W + c * HEAD_DIM
            if dil <= DEINTERLEAVE_STRIDE:
                for r in range(dil):
                    o_ref[r, :, lo:lo + HEAD_DIM] = (
                        scr_ref[slab, pl.ds(r, tm // dil, stride=dil), :].astype(o_ref.dtype))
                continue
            s1, s2 = DEINTERLEAVE_STRIDE, dil // DEINTERLEAVE_STRIDE
            mid_ref = refs[2 * n_blk + 2]
            for a in range(s1):
                mid_ref[slab, a] = scr_ref[slab, pl.ds(a, tm // s1, stride=s1), :]
            for a in range(s1):
                for b in range(s2):
                    o_ref[a + s1 * b, :, lo:lo + HEAD_DIM] = (
                        mid_ref[slab, a, pl.ds(b, tm // dil, stride=s2), :].astype(o_ref.dtype))


def proj(x, gain, w, bias, blocks, batch, seq, dil, out_dtype, tm):
    m, k = x.shape
    n_blk = len(blocks)
    cols = n_blk * BRANCH_W
    assert m == batch * seq and seq % tm == 0 and tm % dil == 0
    tiles_per_b = seq // tm
    w_specs = [pl.BlockSpec((k, BRANCH_W), lambda i, c=c: (0, c)) for c in blocks]
    b_specs = [pl.BlockSpec((1, BRANCH_W), lambda i, c=c: (0, c)) for c in blocks]
    if dil == 1:
        out_spec = pl.BlockSpec((tm, cols), lambda i: (i, 0))
        out_shape = jax.ShapeDtypeStruct((m, cols), out_dtype)
        scratch = []
    else:
        out_spec = pl.BlockSpec((None, dil, tm // dil, cols), lambda i: (i // tiles_per_b, 0, i % tiles_per_b, 0))
        out_shape = jax.ShapeDtypeStruct((batch, dil, seq // dil, cols), out_dtype)
        scratch = [pltpu.VMEM((n_blk * N_HEADS, tm, HEAD_DIM), F32)]
        if dil > DEINTERLEAVE_STRIDE:
            assert dil % DEINTERLEAVE_STRIDE == 0 and dil // DEINTERLEAVE_STRIDE <= DEINTERLEAVE_STRIDE
            scratch.append(pltpu.VMEM((n_blk * N_HEADS, DEINTERLEAVE_STRIDE, tm // DEINTERLEAVE_STRIDE, HEAD_DIM), F32))
    return pl.pallas_call(
        functools.partial(_proj_kernel, n_blk=n_blk, dil=dil),
        grid=(m // tm,),
        in_specs=[pl.BlockSpec((tm, k), lambda i: (i, 0)), pl.BlockSpec((1, k), lambda i: (0, 0))] + w_specs + b_specs,
        out_specs=out_spec,
        out_shape=out_shape,
        scratch_shapes=scratch,
        compiler_params=_cparams("parallel"),
        name="proj",
    )(x, gain.reshape(1, k), *([w] * n_blk), *([bias] * n_blk))


def _proj_groups_kernel(x_ref, g_ref, *refs, sizes):
    n = sum(sizes)
    w_refs, b_refs, o_refs = refs[:n], refs[n:2 * n], refs[2 * n:]
    hn = _rms(x_ref[...], g_ref[...]).astype(BF16)
    k = 0
    for o_ref, size in zip(o_refs, sizes):
        for j in range(size):
            acc = jnp.dot(hn, w_refs[k][...], preferred_element_type=F32) + b_refs[k][...]
            o_ref[:, j * BRANCH_W:(j + 1) * BRANCH_W] = acc.astype(o_ref.dtype)
            k += 1


def proj_groups(x, gain, w, bias, groups, out_dtypes, tm):
    m, k = x.shape
    assert m % tm == 0
    blocks = [c for grp in groups for c in grp]
    once = lambda shape, c: pl.BlockSpec(shape, lambda i, c=c: (0, c), pipeline_mode=pl.Buffered(1))
    return pl.pallas_call(
        functools.partial(_proj_groups_kernel, sizes=tuple(len(grp) for grp in groups)),
        grid=(m // tm,),
        in_specs=[pl.BlockSpec((tm, k), lambda i: (i, 0)), pl.BlockSpec((1, k), lambda i: (0, 0))]
        + [once((k, BRANCH_W), c) for c in blocks] + [once((1, BRANCH_W), c) for c in blocks],
        out_specs=[pl.BlockSpec((tm, len(grp) * BRANCH_W), lambda i: (i, 0)) for grp in groups],
        out_shape=[jax.ShapeDtypeStruct((m, len(grp) * BRANCH_W), dt) for grp, dt in zip(groups, out_dtypes)],
        compiler_params=_cparams("parallel"),
        name="proj_groups",
    )(x, gain.reshape(1, k), *([w] * len(blocks)), *([bias] * len(blocks)))


def _dil_prompt_kernel(q_ref, kp_ref, kc_ref, vp_ref, vc_ref, bias_ref, o_ref, lse_ref, *scratch, dil, qb):
    scale = 1.0 / math.sqrt(HEAD_DIM)
    no_prev = (pl.program_id(1) == 0) & (lax.broadcasted_iota(jnp.int32, (1, 2 * BAND), 1) < BAND)

    def band_blocks(blocks):
        items = [(r, j, h, slice(j * BAND, (j + 1) * BAND), slice(h * HEAD_DIM, (h + 1) * HEAD_DIM))
                 for r, j in blocks for h in range(N_HEADS)]

        def keys(cur_ref, prev_ref, r, j, rows, hs):
            if j == 0:
                return jnp.concatenate([prev_ref[r, :, hs], cur_ref[r, rows, hs]], axis=0)
            return cur_ref[r, (j - 1) * BAND:(j + 1) * BAND, hs]

        logits = []
        for r, j, h, rows, hs in items:
            l = lax.dot_general(q_ref[r, rows, hs], keys(kc_ref, kp_ref, r, j, rows, hs), NT_DIMS,
                                preferred_element_type=F32) * scale + bias_ref[h]
            logits.append(jnp.where(no_prev, NEG_INF, l) if j == 0 else l)
        probs = []
        for l in logits:
            m = jnp.max(jnp.maximum(l[:, :BAND], l[:, BAND:]), axis=-1, keepdims=True)
            p = jnp.exp(l - m)
            s = jnp.sum(p[:, :BAND] + p[:, BAND:], axis=-1, keepdims=True)
            probs.append((p.astype(BF16), s, jnp.broadcast_to(m + jnp.log(s), (BAND, LSE_LANES))))
        outs = [jnp.dot(p, keys(vc_ref, vp_ref, r, j, rows, hs), preferred_element_type=F32) / s
                for (r, j, h, rows, hs), (p, s, _) in zip(items, probs)]
        return [(outs[N_HEADS * n:N_HEADS * (n + 1)],
                 jnp.concatenate([lse for _, _, lse in probs[N_HEADS * n:N_HEADS * (n + 1)]], axis=1))
                for n in range(len(blocks))]

    if dil == 1:
        for j, (outs, lse) in enumerate(band_blocks([(0, j) for j in range(qb)])):
            for h in range(N_HEADS):
                o_ref[j * BAND:(j + 1) * BAND, h * HEAD_DIM:(h + 1) * HEAD_DIM] = outs[h].astype(o_ref.dtype)
            lse_ref[j * BAND:(j + 1) * BAND, :] = lse
        return

    o_scr, lse_scr = scratch
    per_iter = min(dil, 4)

    def body(it, carry):
        blocks = [(it * per_iter + rr, j) for rr in range(per_iter) for j in range(qb)]
        for (r, j), (outs, lse) in zip(blocks, band_blocks(blocks)):
            for h in range(N_HEADS):
                o_scr[h, pl.ds(r + j * dil * BAND, BAND, stride=dil), :] = outs[h]
            lse_scr[pl.ds(r + j * dil * BAND, BAND, stride=dil), :] = lse
        return carry

    lax.fori_loop(0, dil // per_iter, body, 0)
    for h in range(N_HEADS):
        o_ref[:, h * HEAD_DIM:(h + 1) * HEAD_DIM] = o_scr[h].astype(o_ref.dtype)
    lse_ref[...] = lse_scr[...]


def dilated_prompt(qkv, bias, g, dil, batch, seq, qb):
    assert seq % (dil * BAND * qb) == 0
    sub = seq // dil
    span = dil * BAND * qb
    cur = lambda part: pl.BlockSpec((None, dil, qb * BAND, BRANCH_W), lambda b, i: (b, 0, i, part))
    prev = lambda part: pl.BlockSpec((None, dil, BAND, BRANCH_W),
                                     lambda b, i: (b, 0, jnp.maximum(i * qb - 1, 0), part))
    bias_spec = pl.BlockSpec((N_HEADS, BAND, 2 * BAND), lambda b, i: (0, 0, 0))
    scratch = [] if dil == 1 else [pltpu.VMEM((N_HEADS, span, HEAD_DIM), F32),
                                   pltpu.VMEM((span, N_HEADS * LSE_LANES), F32)]
    o, lse = pl.pallas_call(
        functools.partial(_dil_prompt_kernel, dil=dil, qb=qb),
        grid=(batch, sub // (BAND * qb)),
        in_specs=[cur(0), prev(1), cur(1), prev(2), cur(2), bias_spec],
        out_specs=[pl.BlockSpec((None, span, BRANCH_W), lambda b, i: (b, i, 0)),
                   pl.BlockSpec((None, span, N_HEADS * LSE_LANES), lambda b, i: (b, i, 0))],
        out_shape=[jax.ShapeDtypeStruct((batch, seq, BRANCH_W), BF16),
                   jax.ShapeDtypeStruct((batch, seq, N_HEADS * LSE_LANES), F32)],
        scratch_shapes=scratch,
        compiler_params=_cparams("parallel", "arbitrary"),
        name=f"dilated_prompt_g{g}",
    )(qkv, qkv, qkv, qkv, qkv, bias)
    return o.reshape(batch * seq, BRANCH_W), lse.reshape(batch * seq, N_HEADS * LSE_LANES)


def _dil_sample_kernel(q_ref, kn_ref, vn_ref, cache_ref, bc_ref, bn_ref, shifted_ref, o_ref, lse_ref, newrows_ref,
                       *, t_new):
    del shifted_ref
    scale = 1.0 / math.sqrt(HEAD_DIM)
    bb, n_grp, grp_rows, _ = cache_ref.shape
    n_keys = n_grp * grp_rows // KV_ROWS
    rows_of = lambda b, first: cache_ref[b, :, pl.ds(first, grp_rows // KV_ROWS, stride=KV_ROWS), :].reshape(
        n_keys, HEAD_DIM).astype(BF16)
    heads = [(b, h, slice(h * HEAD_DIM, (h + 1) * HEAD_DIM)) for b in range(bb) for h in range(N_HEADS)]
    logits = []
    for b, h, hs in heads:
        q = q_ref[b, :, hs]
        lc = lax.dot_general(q.astype(BF16), rows_of(b, h), NT_DIMS, preferred_element_type=F32) * scale + bc_ref[h]
        ln = lax.dot_general(q, kn_ref[b, :, hs], NT_DIMS, preferred_element_type=F32) * scale + bn_ref[h]
        logits.append((lc, ln))
    probs = []
    for lc, ln in logits:
        m = jnp.maximum(jnp.max(lc, axis=-1, keepdims=True), jnp.max(ln, axis=-1, keepdims=True))
        pc = jnp.exp(lc - m)
        pn = jnp.exp(ln - m)
        s = jnp.sum(pc, axis=-1, keepdims=True) + jnp.sum(pn, axis=-1, keepdims=True)
        probs.append((pc.astype(BF16), pn, s, m + jnp.log(s)))
    for (b, h, hs), (pc, pn, s, lse) in zip(heads, probs):
        kn = kn_ref[b, :, hs]
        vn = vn_ref[b, :, hs]
        o = (jnp.dot(pc, rows_of(b, N_HEADS + h), preferred_element_type=F32)
             + jnp.dot(pn, vn, preferred_element_type=F32))
        o_ref[b, :, hs] = o / s
        lse_ref[b, :, h * LSE_LANES:(h + 1) * LSE_LANES] = jnp.broadcast_to(lse, (t_new, LSE_LANES))
        newrows_ref[b, pl.ds(h, t_new, stride=KV_ROWS), :] = kn
        newrows_ref[b, pl.ds(N_HEADS + h, t_new, stride=KV_ROWS), :] = vn


def _sample_period(win, dil, t_new):
    return (dil, t_new) if (dil > t_new and win % dil == 0) else (win, win)


def _sample_positions(win, dil, t_new):
    period, take = _sample_period(win, dil, t_new)
    return (np.arange(win // period)[:, None] * period + np.arange(take)[None, :]).reshape(-1)


def dilated_sample(qkv, cache, shifted, bias_cache, bias_new, g, dil, batch, t_new, bb):
    win = cache.shape[1] // KV_ROWS
    assert win % t_new == 0 and shifted.shape == cache.shape and batch % bb == 0
    period, take = _sample_period(win, dil, t_new)
    n_grp = win // period
    cache4 = cache.reshape(batch, n_grp, period * KV_ROWS, HEAD_DIM)
    assert bias_cache.shape == (N_HEADS, t_new, n_grp * take)
    blk = (bb, t_new, BRANCH_W)
    o, lse, newc = pl.pallas_call(
        functools.partial(_dil_sample_kernel, t_new=t_new),
        grid=(batch // bb,),
        in_specs=[pl.BlockSpec(blk, lambda b: (b, 0, 0)),
                  pl.BlockSpec(blk, lambda b: (b, 0, 1)),
                  pl.BlockSpec(blk, lambda b: (b, 0, 2)),
                  pl.BlockSpec((bb, n_grp, take * KV_ROWS, HEAD_DIM), lambda b: (b, 0, 0, 0)),
                  pl.BlockSpec((N_HEADS, t_new, n_grp * take), lambda b: (0, 0, 0)),
                  pl.BlockSpec((N_HEADS, t_new, t_new), lambda b: (0, 0, 0)),
                  pl.BlockSpec(memory_space=pl.ANY)],
        out_specs=[pl.BlockSpec(blk, lambda b: (b, 0, 0)),
                   pl.BlockSpec((bb, t_new, N_HEADS * LSE_LANES), lambda b: (b, 0, 0)),
                   pl.BlockSpec((bb, t_new * KV_ROWS, HEAD_DIM), lambda b: (b, win // t_new - 1, 0))],
        out_shape=[jax.ShapeDtypeStruct((batch, t_new, BRANCH_W), F32),
                   jax.ShapeDtypeStruct((batch, t_new, N_HEADS * LSE_LANES), F32),
                   jax.ShapeDtypeStruct(cache.shape, F32)],
        input_output_aliases={6: 2},
        compiler_params=_cparams("parallel"),
        name=f"dilated_sample_g{g}",
    )(qkv, qkv, qkv, cache4, bias_cache, bias_new, shifted)
    return o.reshape(batch * t_new, BRANCH_W), lse.reshape(batch * t_new, N_HEADS * LSE_LANES), newc


def _mem_attn_kernel(q_ref, kv_ref, o_ref, *, mem):
    scale = 1.0 / math.sqrt(HEAD_DIM)
    heads = [(b, h, slice(h * HEAD_DIM, (h + 1) * HEAD_DIM)) for b in range(q_ref.shape[0]) for h in range(N_HEADS)]
    rows_of = lambda b, first: kv_ref[b, pl.ds(first, mem, stride=KV_ROWS), :].astype(BF16)
    logits = [lax.dot_general(q_ref[b, :, hs].astype(BF16), rows_of(b, h), NT_DIMS, preferred_element_type=F32) * scale
              for b, h, hs in heads]
    probs = []
    for l in logits:
        half = l.shape[1] // 2
        m = jnp.max(jnp.maximum(l[:, :half], l[:, half:]), axis=-1, keepdims=True)
        p = jnp.exp(l - m)
        probs.append((p.astype(BF16), jnp.sum(p[:, :half] + p[:, half:], axis=-1, keepdims=True)))
    for (b, h, hs), (p, s) in zip(heads, probs):
        o = jnp.dot(p, rows_of(b, N_HEADS + h), preferred_element_type=F32)
        o_ref[b, :, hs] = (o / s).astype(o_ref.dtype)


def memory_attention(rec, mem_kv, batch, t, tq, bb, out_dtype):
    mem = mem_kv.shape[1] // KV_ROWS
    assert batch % bb == 0
    x = rec.reshape(batch, t, REC_COLS)
    o = pl.pallas_call(
        functools.partial(_mem_attn_kernel, mem=mem),
        grid=(batch // bb, t // tq),
        in_specs=[pl.BlockSpec((bb, tq, BRANCH_W), lambda b, i: (b, i, REC_MQ)),
                  pl.BlockSpec((bb, mem * KV_ROWS, HEAD_DIM), lambda b, i: (b, 0, 0))],
        out_specs=pl.BlockSpec((bb, tq, BRANCH_W), lambda b, i: (b, i, 0)),
        out_shape=jax.ShapeDtypeStruct((batch, t, BRANCH_W), out_dtype),
        compiler_params=_cparams("parallel", "parallel"),
        name="memory_attention",
    )(x, mem_kv)
    return o.reshape(batch * t, BRANCH_W)


def _hgrn_tables(chunk):
    c, b = chunk, HG_BASE
    t = np.arange(c)[:, None]
    u = np.arange(c)[None, :]
    masks = []
    size = 2 * b
    while size <= c:
        half = size // 2
        masks.append(((t // size) == (u // size)) & ((t % size) >= half) & ((u % size) < half))
        size *= 2
    masks.append(((t // b) == (u // b)) & (u <= t))
    erep = np.zeros((b * HEAD_DIM, c), np.float32)
    for s in range(b):
        erep[s * HEAD_DIM:(s + 1) * HEAD_DIM, np.arange(c) % b == s] = 1.0
    return (u <= t).astype(np.float32), erep, np.stack(masks).astype(np.float32)


def _shift_step(old_refs, new_refs, buf_refs, zero_ref, sem_in, sem_out, sem_zero, step, n_steps, drop):
    slot = step % 2
    other = 1 - slot

    def copies(g, chunk_idx, buf_slot):
        old, new, buf = old_refs[g], new_refs[g], buf_refs[g]
        rows_c = buf.shape[1]
        per_batch = (old.shape[1] - drop) // rows_c
        b = chunk_idx // per_batch
        lo = pl.multiple_of((chunk_idx % per_batch) * rows_c, 8)
        return (pltpu.make_async_copy(old.at[b, pl.ds(lo + drop, rows_c), :], buf.at[buf_slot], sem_in.at[g, buf_slot]),
                pltpu.make_async_copy(buf.at[buf_slot], new.at[b, pl.ds(lo, rows_c), :], sem_out.at[g, buf_slot]))

    def zero_copy(g):
        keep = old_refs[g].shape[1] - drop
        return pltpu.make_async_copy(zero_ref, new_refs[g].at[:, pl.ds(keep, drop), :], sem_zero.at[g])

    groups = range(len(old_refs))

    @pl.when(step == 0)
    def _():
        zero_ref[...] = jnp.zeros_like(zero_ref)
        for g in groups:
            zero_copy(g).start()
            copies(g, 0, 0)[0].start()

    for g in groups:
        copies(g, step, slot)[0].wait()

    @pl.when(step >= 1)
    def _():
        for g in groups:
            copies(g, step - 1, other)[1].wait()

    for g in groups:
        copies(g, step, slot)[1].start()

    @pl.when(step + 1 < n_steps)
    def _():
        for g in groups:
            copies(g, step + 1, other)[0].start()

    @pl.when(step == n_steps - 1)
    def _():
        for g in groups:
            copies(g, step, slot)[1].wait()
            zero_copy(g).wait()


def _shift_scratch(shift, drop, n_steps):
    if not shift:
        return []
    sb = shift[0].shape[0]
    assert n_steps % sb == 0 and all(s.shape[0] == sb and s.shape[2] == HEAD_DIM for s in shift)
    per_batch = n_steps // sb
    chunk_rows = [(s.shape[1] - drop) // per_batch for s in shift]
    assert all(r % 8 == 0 and r * per_batch == s.shape[1] - drop for r, s in zip(chunk_rows, shift))
    n = len(shift)
    return ([pltpu.VMEM((2, r, HEAD_DIM), F32) for r in chunk_rows]
            + [pltpu.VMEM((sb, drop, HEAD_DIM), F32), pltpu.SemaphoreType.DMA((n, 2)),
               pltpu.SemaphoreType.DMA((n, 2)), pltpu.SemaphoreType.DMA((n,))])


def _hgrn_kernel(q_ref, v_ref, f_ref, gate_ref, lbl_ref, gn_ref, s0_ref, tri_ref, erep_ref, mask_ref, *rest,
                 chunk, rows_in, n_sub, bb, n_levels, n_shift, shift_drop, n_steps):
    old_refs, rest = rest[:n_shift], rest[n_shift:]
    o_ref, s_out_ref = rest[:2]
    new_refs, rest = rest[2:2 + n_shift], rest[2 + n_shift:]
    st_ref = rest[0]
    c = chunk
    step = pl.program_id(1)
    if n_shift:
        buf_refs, (zero_ref, sem_in, sem_out, sem_zero) = rest[1:1 + n_shift], rest[1 + n_shift:]
        _shift_step(old_refs, new_refs, buf_refs, zero_ref, sem_in, sem_out, sem_zero,
                    pl.program_id(0) * pl.num_programs(1) + step, n_steps, shift_drop)

    @pl.when(step == 0)
    def _():
        for b in range(bb):
            for h in range(N_HEADS):
                st_ref[b * N_HEADS + h] = s0_ref[b, h].T

    lbl = lbl_ref[...]
    e = jnp.exp(lbl - jnp.max(lbl, axis=0, keepdims=True))
    lb = e[0:1] / jnp.sum(e, axis=0, keepdims=True)
    tri = tri_ref[...]

    def rows(ref, b, ci):
        x = ref[b, ci * rows_in:(ci + 1) * rows_in, :].astype(F32)
        if rows_in < c:
            x = jnp.concatenate([x, jnp.zeros((c - rows_in, x.shape[1]), F32)], axis=0)
        return x

    def at_row(x, size, idx):
        x3 = x.reshape(c // size, size, x.shape[1])
        return jnp.broadcast_to(x3[:, idx:idx + 1, :], x3.shape).reshape(x.shape)

    chunk_ids = [(b, ci) for b in range(bb) for ci in range(n_sub)]
    chunks = range(len(chunk_ids))
    heads = [(h, slice(h * HEAD_DIM, (h + 1) * HEAD_DIM)) for h in range(N_HEADS)]

    q, v, kk, cum = [], [], [], []
    for b, ci in chunk_ids:
        q.append(rows(q_ref, b, ci))
        v.append(rows(v_ref, b, ci).astype(BF16))
        f = lb + (1.0 - lb) * _sigmoid(rows(f_ref, b, ci))
        g = jnp.log(f)
        k_in = 1.0 - f
        if rows_in < c:
            live = lax.broadcasted_iota(jnp.int32, (c, 1), 0) < rows_in
            g = jnp.where(live, g, 0.0)
            k_in = jnp.where(live, k_in, 0.0)
        kk.append(k_in)
        g1 = g.astype(BF16)
        r1 = g - g1.astype(F32)
        g2 = r1.astype(BF16)
        g3 = (r1 - g2.astype(F32)).astype(BF16)
        cum.append(jnp.dot(tri, g1, preferred_element_type=F32) + jnp.dot(tri, g2, preferred_element_type=F32)
                   + jnp.dot(tri, g3, preferred_element_type=F32))

    q_inter, k_end, g_end, q_lvl, k_lvl, x_diag = [], [], [], [], [], []
    for ci in chunks:
        q_inter.append((q[ci] * jnp.exp(cum[ci])).astype(BF16))
        g_end.append(cum[ci][c - 1:c])
        k_end.append((kk[ci] * jnp.exp(g_end[ci] - cum[ci])).astype(BF16))
        ql, kl = [], []
        for l in range(n_levels):
            size = 2 * HG_BASE << l
            ref = at_row(cum[ci], size, size // 2 - 1)
            ql.append((q[ci] * jnp.exp(cum[ci] - ref)).astype(BF16))
            kl.append((kk[ci] * jnp.exp(ref - cum[ci])).astype(BF16))
        q_lvl.append(ql)
        k_lvl.append(kl)
        xd = []
        for s in range(HG_BASE):
            decay = jnp.exp(jnp.minimum(cum[ci] - at_row(cum[ci], HG_BASE, s), 0.0))
            xd.append((q[ci] * at_row(kk[ci], HG_BASE, s) * decay).astype(BF16))
        x_diag.append(xd)

    level_mask = [mask_ref[l] > 0.5 for l in range(n_levels)]
    scores, state_in = [], []
    for ci in chunks:
        sc, si = [], []
        for h, hs in heads:
            s_h = mask_ref[n_levels] * jnp.dot(jnp.concatenate([x[:, hs] for x in x_diag[ci]], axis=1),
                                               erep_ref[...], preferred_element_type=F32)
            for l in range(n_levels):
                s_h += jnp.where(level_mask[l], lax.dot_general(q_lvl[ci][l][:, hs], k_lvl[ci][l][:, hs], NT_DIMS,
                                                                preferred_element_type=F32), 0.0)
            sc.append(s_h.astype(BF16))
            si.append(lax.dot_general(v[ci][:, hs], k_end[ci][:, hs], TN_DIMS, preferred_element_type=F32))
        scores.append(sc)
        state_in.append(si)

    states = {}
    for b in range(bb):
        for h, hs in heads:
            st = st_ref[b * N_HEADS + h]
            for ci in range(n_sub):
                idx = b * n_sub + ci
                states[idx, h] = st.astype(BF16)
                st = st * jnp.exp(g_end[idx][:, hs]) + state_in[idx][h]
            st_ref[b * N_HEADS + h] = st
    for idx, (b, ci) in enumerate(chunk_ids):
        gate = _sigmoid(rows(gate_ref, b, ci))
        for h, hs in heads:
            o = (lax.dot_general(q_inter[idx][:, hs], states[idx, h], NT_DIMS, preferred_element_type=F32)
                 + jnp.dot(scores[idx][h], v[idx][:, hs], preferred_element_type=F32))
            y = _rms(o, gn_ref[...]) * gate[:, hs]
            o_ref[b, ci * rows_in:(ci + 1) * rows_in, hs] = y[0:rows_in].astype(o_ref.dtype)

    @pl.when(step == pl.num_programs(1) - 1)
    def _():
        for b in range(bb):
            for h in range(N_HEADS):
                s_out_ref[b, h] = st_ref[b * N_HEADS + h].T


def hgrn2(rec, gates, lb_logits, hg_norm, state0, batch, t, chunk, n_sub, bb, out_dtype, shift=(), shift_drop=0):
    rows_in = min(chunk, t)
    step_rows = rows_in * n_sub
    assert t % step_rows == 0 and chunk % HG_BASE == 0 and (n_sub == 1 or rows_in == chunk) and batch % bb == 0
    n_steps = (batch // bb) * (t // step_rows)
    n_shift = len(shift)
    any_spec = pl.BlockSpec(memory_space=pl.ANY)
    shift_scratch = _shift_scratch(shift, shift_drop, n_steps)
    tri, erep, masks = _hgrn_tables(chunk)
    n_levels = masks.shape[0] - 1
    a = rec.reshape(batch, t, REC_COLS)
    gt = gates.reshape(batch, t, G_COLS)
    blk = (bb, step_rows, BRANCH_W)
    full = lambda arr: pl.BlockSpec(arr.shape, lambda b, i: (0,) * arr.ndim)
    st_spec = pl.BlockSpec((bb, N_HEADS, HEAD_DIM, HEAD_DIM), lambda b, i: (b, 0, 0, 0))
    tri = jnp.asarray(tri, BF16)
    erep = jnp.asarray(erep, BF16)
    masks = jnp.asarray(masks, F32)
    lbl = lb_logits.astype(F32)
    gn = hg_norm.reshape(1, HEAD_DIM).astype(F32)
    o, s_fin, *shifted = pl.pallas_call(
        functools.partial(_hgrn_kernel, chunk=chunk, rows_in=rows_in, n_sub=n_sub, bb=bb, n_levels=n_levels,
                          n_shift=n_shift, shift_drop=shift_drop, n_steps=n_steps),
        grid=(batch // bb, t // step_rows),
        in_specs=[pl.BlockSpec(blk, lambda b, i: (b, i, REC_HQ)),
                  pl.BlockSpec(blk, lambda b, i: (b, i, REC_HI)),
                  pl.BlockSpec(blk, lambda b, i: (b, i, 0)),
                  pl.BlockSpec(blk, lambda b, i: (b, i, 1)),
                  full(lbl), full(gn), st_spec, full(tri), full(erep), full(masks)] + [any_spec] * n_shift,
        out_specs=[pl.BlockSpec(blk, lambda b, i: (b, i, 0)), st_spec] + [any_spec] * n_shift,
        out_shape=[jax.ShapeDtypeStruct((batch, t, BRANCH_W), out_dtype),
                   jax.ShapeDtypeStruct(state0.shape, F32)] + [jax.ShapeDtypeStruct(s.shape, F32) for s in shift],
        scratch_shapes=[pltpu.VMEM((bb * N_HEADS, HEAD_DIM, HEAD_DIM), F32)] + shift_scratch,
        compiler_params=_cparams("arbitrary", "arbitrary"),
        name="hgrn2",
    )(a, a, gt, gt, lbl, gn, state0, tri, erep, masks, *shift)
    return o.reshape(batch * t, BRANCH_W), s_fin, shifted


def _merge_kernel(x_ref, o1_ref, o2_ref, o3_ref, l1_ref, l2_ref, l3_ref, hg_ref, mem_ref, gpre_ref,
                  wga_ref, wgh_ref, wgm_ref, bga_ref, bgh_ref, bgm_ref,
                  wa_ref, wh_ref, wm_ref, wo_ref, gain_ref, *rest, n_shift, shift_drop, n_steps):
    old_refs, out_ref, new_refs, scratch = rest[:n_shift], rest[n_shift], rest[n_shift + 1:2 * n_shift + 1], \
        rest[2 * n_shift + 1:]
    if n_shift:
        _shift_step(old_refs, new_refs, scratch[:n_shift], *scratch[n_shift:], pl.program_id(0), n_steps, shift_drop)
    tm = x_ref.shape[0]
    x = x_ref[...]
    hn = _rms(x, gpre_ref[...]).astype(BF16)
    gate_logits = [jnp.dot(hn, w_ref[...], preferred_element_type=F32) + b_ref[...]
                   for w_ref, b_ref in ((wga_ref, bga_ref), (wgh_ref, bgh_ref), (wgm_ref, bgm_ref))]
    from_hg = jnp.dot(hg_ref[...].astype(BF16), wh_ref[...], preferred_element_type=F32)
    from_mem = jnp.dot(mem_ref[...].astype(BF16), wm_ref[...], preferred_element_type=F32)
    l1, l2, l3 = l1_ref[...], l2_ref[...], l3_ref[...]
    m = jnp.maximum(jnp.maximum(l1, l2), l3)
    e1, e2, e3 = jnp.exp(l1 - m), jnp.exp(l2 - m), jnp.exp(l3 - m)
    den = e1 + e2 + e3
    w1, w2, w3 = e1 / den, e2 / den, e3 / den
    att = []
    for h in range(N_HEADS):
        hs = slice(h * HEAD_DIM, (h + 1) * HEAD_DIM)
        col = slice(h * LSE_LANES, h * LSE_LANES + 1)
        bc = lambda w: jnp.broadcast_to(w[:, col], (tm, HEAD_DIM))
        att.append((bc(w1) * o1_ref[:, hs].astype(F32) + bc(w2) * o2_ref[:, hs].astype(F32)
                    + bc(w3) * o3_ref[:, hs].astype(F32)).astype(BF16))
    att = jnp.concatenate(att, axis=1)
    ga, gh, gm = (_sigmoid(z) for z in gate_logits)
    merged = ga * jnp.dot(att, wa_ref[...], preferred_element_type=F32) + gh * from_hg + gm * from_mem
    y = jnp.dot(merged.astype(BF16), wo_ref[...], preferred_element_type=F32)
    out_ref[...] = x + _rms(y, gain_ref[...])


def merge(x, o_groups, lse_groups, hg_o, mem_o, gain_pre, w_in, b_in, w_att, w_hg, w_mem, w_out, gain, tm,
          shift=(), shift_drop=0):
    m, d = x.shape
    assert m % tm == 0 and GATE_MERGE_BLOCKS[0] * d == 14 * BRANCH_W
    n_steps = m // tm
    n_shift = len(shift)
    any_spec = pl.BlockSpec(memory_space=pl.ANY)
    row = lambda w: pl.BlockSpec((tm, w), lambda i: (i, 0))
    resident = lambda shape, j=0: pl.BlockSpec(shape, lambda i: (0, j), pipeline_mode=pl.Buffered(1))
    x1, *shifted = pl.pallas_call(
        functools.partial(_merge_kernel, n_shift=n_shift, shift_drop=shift_drop, n_steps=n_steps),
        grid=(n_steps,),
        in_specs=[row(d), row(BRANCH_W), row(BRANCH_W), row(BRANCH_W),
                  row(N_HEADS * LSE_LANES), row(N_HEADS * LSE_LANES), row(N_HEADS * LSE_LANES),
                  row(BRANCH_W), row(BRANCH_W), resident((1, d))]
        + [resident((d, d), j) for j in GATE_MERGE_BLOCKS] + [resident((1, d), j) for j in GATE_MERGE_BLOCKS]
        + [resident(w_att.shape), resident(w_hg.shape), resident(w_mem.shape), resident(w_out.shape),
           resident((1, d))] + [any_spec] * n_shift,
        out_specs=[row(d)] + [any_spec] * n_shift,
        out_shape=[jax.ShapeDtypeStruct((m, d), F32)] + [jax.ShapeDtypeStruct(s.shape, F32) for s in shift],
        scratch_shapes=_shift_scratch(shift, shift_drop, n_steps),
        compiler_params=_cparams("arbitrary"),
        name="merge",
    )(x, *o_groups, *lse_groups, hg_o, mem_o, gain_pre.reshape(1, d), w_in, w_in, w_in, b_in, b_in, b_in,
      w_att, w_hg, w_mem, w_out, gain.reshape(1, d), *shift)
    return x1, shifted


def _ffn_kernel(x_ref, gpre_ref, wa_ref, wb_ref, cw_ref, cb_ref, wd_ref, gpost_ref, cbuf_ref,
                out_ref, tail_ref, carry_ref, *, t_seq, tiles_per_seq, tf):
    tm = x_ref.shape[0]
    dff = wa_ref.shape[1]
    x = x_ref[...]
    hn = _rms(x, gpre_ref[...]).astype(BF16)

    if tiles_per_seq >= 1:
        @pl.when((pl.program_id(0) % tiles_per_seq) == 0)
        def _():
            carry_ref[6:8, :] = cbuf_ref[0]

        t_idx = lax.broadcasted_iota(jnp.int32, (tm, 1), 0)
    else:
        n_seq = tm // t_seq
        t_idx = lax.broadcasted_iota(jnp.int32, (tm, 1), 0) % t_seq

    n_chunks = dff // tf

    def up_proj(j):
        cols = slice(j * tf, (j + 1) * tf)
        return (jnp.dot(hn, wa_ref[:, cols], preferred_element_type=F32),
                jnp.dot(hn, wb_ref[:, cols], preferred_element_type=F32))

    acc = jnp.zeros(x.shape, F32)
    ahead = up_proj(0)
    for j in range(n_chunks):
        cols = slice(j * tf, (j + 1) * tf)
        a, up = ahead
        if j + 1 < n_chunks:
            ahead = up_proj(j + 1)
        if tiles_per_seq >= 1:
            prev1 = carry_ref[7:8, cols]
            prev2 = carry_ref[6:7, cols]
            carry_ref[:, cols] = a[tm - 8:tm]
            tail_ref[0, :, cols] = a[tm - 8:tm]
        else:
            prev1 = jnp.broadcast_to(cbuf_ref[:, 1:2, cols], (n_seq, t_seq, tf)).reshape(tm, tf)
            prev2 = jnp.broadcast_to(cbuf_ref[:, 0:1, cols], (n_seq, t_seq, tf)).reshape(tm, tf)
            tail_ref[:, :, cols] = a.reshape(n_seq, t_seq, tf)
        a1 = jnp.where(t_idx >= 1, pltpu.roll(a, 1, 0), prev1)
        a2 = jnp.where(t_idx >= 2, pltpu.roll(a, 2, 0), jnp.where(t_idx == 1, prev1, prev2))
        conv = cb_ref[:, cols] + a2 * cw_ref[0:1, cols] + a1 * cw_ref[1:2, cols] + a * cw_ref[2:3, cols]
        act = conv * _sigmoid(conv) * up
        acc += jnp.dot(act.astype(BF16), wd_ref[cols, :], preferred_element_type=F32)

    out_ref[...] = x + _rms(acc, gpost_ref[...])


def conv_ffn(x, conv_buf, g_pre, w_a, w_b, conv_w, conv_b, w_d, g_post, t_seq, tm, tf):
    m, d = x.shape
    dff = w_a.shape[1]
    n_seq = m // t_seq
    assert m % tm == 0 and dff % tf == 0 and t_seq >= 8
    if tm <= t_seq:
        assert t_seq % tm == 0
        tiles_per_seq = t_seq // tm
        seq_blk = 1
        seq_idx = lambda i: (i // tiles_per_seq, 0, 0)
    else:
        assert tm % t_seq == 0 and t_seq == 8
        tiles_per_seq = 0
        seq_blk = tm // t_seq
        seq_idx = lambda i: (i, 0, 0)
    n_tail = (m // tm) * seq_blk
    resident = lambda shape: pl.BlockSpec(shape, lambda i: (0, 0), pipeline_mode=pl.Buffered(1))
    y, tail = pl.pallas_call(
        functools.partial(_ffn_kernel, t_seq=t_seq, tiles_per_seq=tiles_per_seq, tf=tf),
        grid=(m // tm,),
        in_specs=[pl.BlockSpec((tm, d), lambda i: (i, 0)), resident((1, d)),
                  resident((d, dff)), resident((d, dff)), resident((3, dff)), resident((1, dff)),
                  resident((dff, d)), resident((1, d)),
                  pl.BlockSpec((seq_blk, 2, dff), seq_idx)],
        out_specs=[pl.BlockSpec((tm, d), lambda i: (i, 0)),
                   pl.BlockSpec((seq_blk, 8, dff), lambda i: (i, 0, 0))],
        out_shape=[jax.ShapeDtypeStruct((m, d), F32), jax.ShapeDtypeStruct((n_tail, 8, dff), F32)],
        scratch_shapes=[pltpu.VMEM((8, dff), F32)],
        compiler_params=_cparams("arbitrary"),
        name="conv_ffn",
    )(x, g_pre.reshape(1, d), w_a, w_b, conv_w, conv_b.reshape(1, dff), w_d, g_post.reshape(1, d), conv_buf)
    if tiles_per_seq >= 1:
        tail = tail.reshape(n_seq, tiles_per_seq, 8, dff)[:, -1]
    return y, tail


def _rel_buckets(dil):
    max_exact = REL_BUCKETS // 2
    dist = np.arange(N_LAGS + 1, dtype=np.int32) * dil
    d = np.maximum(dist, 1).astype(np.float32)
    large = max_exact + (np.log(d / np.float32(max_exact)) / np.float32(math.log(REL_MAX_DIST / max_exact))
                         * np.float32(REL_BUCKETS - max_exact)).astype(np.int32)
    large = np.minimum(large, REL_BUCKETS - 1)
    return np.where(dist < max_exact, dist, large)


def _bias_table(rel_bias, g, dil, dist):
    dist = np.asarray(dist)
    ok = (dist >= 0) & (dist % dil == 0) & (dist <= N_LAGS * dil)
    bucket = np.where(ok, _rel_buckets(dil)[np.clip(dist // dil, 0, N_LAGS)], -1)
    onehot = bucket[..., None] == np.arange(REL_BUCKETS)
    heads = rel_bias.astype(F32)[:, g * N_HEADS:(g + 1) * N_HEADS].T
    picked = jnp.sum(jnp.where(onehot[None], heads.reshape((N_HEADS,) + (1,) * dist.ndim + (REL_BUCKETS,)), 0.0), -1)
    return jnp.where(ok[None], picked, NEG_INF)


def _prompt_bias(rel_bias, g, dil):
    return _bias_table(rel_bias, g, dil, (BAND + np.arange(BAND)[:, None] - np.arange(2 * BAND)[None, :]) * dil)


def _sample_bias(rel_bias, g, dil, cached_pos, n_past, t_new):
    t = np.arange(t_new)[:, None]
    return (_bias_table(rel_bias, g, dil, n_past + t - np.asarray(cached_pos)[None, :]),
            _bias_table(rel_bias, g, dil, t - np.arange(t_new)[None, :]))


def _layer(x, batch, t, weights, rel_bias, lb_logits, win_caches, hg_state0, conv_buf0, mem_kv,
           *, prompt, tm_proj, tm_merge, tm_ffn, tq_mem, chunk, n_sub, shifted=None, side_shift=(), side_shift_drop=0):
    (w_in16, b_in, gain_pre, gain_post, hg_norm, w_att, w_hg, w_mem, w_out,
     gain_fpre, gain_fpost, w_fa, w_fb, conv_w, conv_b, w_fd) = weights
    act_dtype = BF16 if prompt else F32
    b_in = b_in.reshape(1, -1)
    if prompt:
        plain = [g for g, (_, dil) in enumerate(ATT_GROUPS) if dil == 1]
        *qkv_plain, rec, gates = proj_groups(x, gain_pre, w_in16, b_in, [QKV_BLOCKS(g) for g in plain]
                                             + [REC_BLOCKS, GATE_BLOCKS], [act_dtype] * (len(plain) + 1) + [F32], tm_proj)
        qkv = [qkv_plain[plain.index(g)].reshape(batch, 1, t, QKV_COLS) if g in plain
               else proj(x, gain_pre, w_in16, b_in, QKV_BLOCKS(g), batch, t, dil, act_dtype, tm_proj)
               for g, (_, dil) in enumerate(ATT_GROUPS)]
    else:
        groups = [QKV_BLOCKS(g) for g in range(len(ATT_GROUPS))] + [REC_BLOCKS, GATE_BLOCKS]
        *qkv, rec, gates = proj_groups(x, gain_pre, w_in16, b_in, groups, [act_dtype] * 4 + [F32], tm_proj)
        qkv = [z.reshape(batch, t, QKV_COLS) for z in qkv]

    o_groups, lse_groups, new_caches = [], [], []
    for g, (win, dil) in enumerate(ATT_GROUPS):
        if prompt:
            qb = max(1, min(ATT_STEP_BLOCKS // dil, t // (dil * BAND)))
            o, lse = dilated_prompt(qkv[g], _prompt_bias(rel_bias, g, dil), g, dil, batch, t, qb)
        else:
            cache = win_caches[g]
            n_past = cache.shape[1] // KV_ROWS
            assert n_past >= win
            bcache, bnew = _sample_bias(rel_bias, g, dil, _sample_positions(n_past, dil, t), n_past, t)
            bb = math.gcd(batch, max(1, SAMPLE_STEP_BYTES // (cache.shape[1] * HEAD_DIM * 4)))
            o, lse, newc = dilated_sample(qkv[g], cache, shifted[g], bcache, bnew, g, dil, batch, t, bb)
            new_caches.append(newc)
        o_groups.append(o)
        lse_groups.append(lse)

    largest = max(range(len(side_shift)), key=lambda i: side_shift[i].size) if side_shift else None
    with_hgrn = [s for i, s in enumerate(side_shift) if i == largest]
    with_merge = [s for i, s in enumerate(side_shift) if i != largest]
    hg_bb = 1 if prompt else math.gcd(batch, HGRN_SAMPLE_ROWS)
    hg_o, hg_state, from_hgrn = hgrn2(rec, gates, lb_logits, hg_norm, hg_state0, batch, t, chunk, n_sub, hg_bb,
                                      act_dtype, with_hgrn, side_shift_drop)
    mem_bb = 1 if prompt else math.gcd(batch, max(1, SAMPLE_STEP_BYTES // (mem_kv.shape[1] * HEAD_DIM * 4)))
    mem_o = memory_attention(rec, mem_kv, batch, t, tq_mem, mem_bb, act_dtype)
    x1, from_merge = merge(x, o_groups, lse_groups, hg_o, mem_o, gain_pre, w_in16, b_in, w_att, w_hg, w_mem, w_out,
                           gain_post, tm_merge, with_merge, side_shift_drop)
    from_hgrn, from_merge = list(from_hgrn), list(from_merge)
    shifted_out = [from_hgrn.pop(0) if i == largest else from_merge.pop(0) for i in range(len(side_shift))]
    y, tail = conv_ffn(x1, conv_buf0, gain_fpre, w_fa, w_fb, conv_w, conv_b, w_fd, gain_fpost, t, tm_ffn, 2048)
    return y, qkv, new_caches, hg_state, tail[:, 6:8, :], shifted_out


def kernel(x_prompt, x_sample, mem_prompt, cache_win1_kv, cache_win2_kv, cache_win3_kv, cache_mem_kv, state_hgrn, state_ffn_conv, rel_bias, hg_lb_logits, norm_mix_pre, norm_mix_post, w_in, b_in, hg_norm, mem_norm, w_mem_kv, w_br_att, w_br_hg, w_br_mem, w_out, norm_ffn_pre, norm_ffn_post, w_ffn_a, w_ffn_b, ffn_conv_w, ffn_conv_b, w_ffn_d):
    depth = w_in.shape[0]
    assert depth == 1
    bsz, seq, d = x_prompt.shape
    dbsz, dseq, _ = x_sample.shape
    mem_tokens = mem_prompt.shape[1]
    dff = w_ffn_a.shape[2]
    layer = 0

    assert w_in.shape[2] == W_IN_BLOCKS * BRANCH_W
    weights = (w_in[layer].astype(BF16), b_in[layer],
               norm_mix_pre[layer], norm_mix_post[layer], hg_norm[layer],
               w_br_att[layer].astype(BF16), w_br_hg[layer].astype(BF16), w_br_mem[layer].astype(BF16),
               w_out[layer].astype(BF16), norm_ffn_pre[layer], norm_ffn_post[layer],
               w_ffn_a[layer].astype(BF16), w_ffn_b[layer].astype(BF16), ffn_conv_w[layer], ffn_conv_b[layer],
               w_ffn_d[layer].astype(BF16))

    mem_kv = norm_matmul(mem_prompt.reshape(bsz * mem_tokens, d), mem_norm[layer], w_mem_kv[layer].astype(BF16),
                         jnp.zeros((2 * BRANCH_W,), F32), F32, bsz * mem_tokens, 1024)
    as_rows = lambda c: c[layer].reshape(dbsz, c.shape[2] * KV_ROWS, HEAD_DIM)
    win_caches = [as_rows(c) for c in (cache_win1_kv, cache_win2_kv, cache_win3_kv)]
    yp, qkv_p, _, hg_p, conv_p, shifted = _layer(
        x_prompt.reshape(bsz * seq, d), bsz, seq, weights, rel_bias, hg_lb_logits, None,
        jnp.zeros((bsz, N_HEADS, HEAD_DIM, HEAD_DIM), F32), jnp.zeros((bsz, 2, dff), F32),
        mem_kv.reshape(bsz, mem_tokens * KV_ROWS, HEAD_DIM),
        prompt=True, tm_proj=1024, tm_merge=512, tm_ffn=512, tq_mem=1024, chunk=64, n_sub=8,
        side_shift=win_caches, side_shift_drop=dseq * KV_ROWS)
    p_win = []
    for g, (win, dil) in enumerate(ATT_GROUPS):
        n = min(win, seq)
        assert n % dil == 0
        tail = qkv_p[g][:, :, (seq - n) // dil:, BRANCH_W:]
        tail = jnp.swapaxes(tail, 1, 2).astype(F32)
        p_win.append(tail.reshape(1, bsz, n, 2, N_HEADS, HEAD_DIM))

    ys, _, new_caches, hg_s, conv_s, _ = _layer(
        x_sample.reshape(dbsz * dseq, d), dbsz, dseq, weights, rel_bias, hg_lb_logits, win_caches,
        state_hgrn[layer], state_ffn_conv[layer], as_rows(cache_mem_kv),
        prompt=False, tm_proj=dbsz * dseq, tm_merge=dbsz * dseq, tm_ffn=dbsz * dseq, tq_mem=dseq, chunk=16, n_sub=1,
        shifted=shifted)
    s_win = [c.reshape(1, dbsz, c.shape[1] // KV_ROWS, 2, N_HEADS, HEAD_DIM) for c in new_caches]

    return (yp.reshape(bsz, seq, d), ys.reshape(dbsz, dseq, d),
            p_win[0], p_win[1], p_win[2],
            hg_p[None], conv_p[None], mem_kv.reshape(1, bsz, mem_tokens, 2, N_HEADS, HEAD_DIM),
            s_win[0], s_win[1], s_win[2],
            hg_s[None], conv_s[None])
```

```python
import functools
import math

import numpy as np
import jax
import jax.numpy as jnp
from jax import lax
from jax.experimental import pallas as pl
from jax.experimental.pallas import tpu as pltpu

F32 = jnp.float32
BF16 = jnp.bfloat16

NORM_EPS = 1e-6
NEG_INF = -1e30
HEAD_DIM = 128
N_HEADS = 4
BRANCH_W = N_HEADS * HEAD_DIM
N_LAGS = 128
BAND = 128
ATT_GROUPS = ((128, 1), (512, 4), (2048, 16))
REL_BUCKETS = 32
REL_MAX_DIST = 2048
LSE_LANES = 32
HG_BASE = 8
VMEM_LIMIT = 56 * 1024 * 1024
SAMPLE_STEP_BYTES = 8 * 1024 * 1024
DEINTERLEAVE_STRIDE = 4
HGRN_SAMPLE_ROWS = 4
ATT_STEP_BLOCKS = 8

NT_DIMS = (((1,), (1,)), ((), ()))
TN_DIMS = (((0,), (0,)), ((), ()))

W_IN_BLOCKS = 20
QKV_BLOCKS = lambda g: (g, 3 + g, 6 + g)
REC_BLOCKS = (9, 11, 13)
GATE_BLOCKS = (10, 12)
GATE_MERGE_BLOCKS = (7, 8, 9)
QKV_COLS = 3 * BRANCH_W
REC_COLS = 3 * BRANCH_W
REC_HQ, REC_HI, REC_MQ = 0, 1, 2
G_COLS = 2 * BRANCH_W
KV_ROWS = 2 * N_HEADS


def _cparams(*sem):
    return pltpu.CompilerParams(dimension_semantics=sem, vmem_limit_bytes=VMEM_LIMIT)


def _rms(x, gain):
    return x * lax.rsqrt(jnp.mean(x * x, axis=-1, keepdims=True) + NORM_EPS) * gain


def _sigmoid(x):
    return 0.5 * jnp.tanh(0.5 * x) + 0.5


def _norm_matmul_kernel(x_ref, g_ref, w_ref, b_ref, o_ref, hn_ref):
    @pl.when(pl.program_id(1) == 0)
    def _():
        hn_ref[...] = _rms(x_ref[...], g_ref[...]).astype(BF16)

    acc = jnp.dot(hn_ref[...], w_ref[...], preferred_element_type=F32)
    o_ref[...] = (acc + b_ref[...]).astype(o_ref.dtype)


def norm_matmul(x, gain, w, bias, out_dtype, tm, tn):
    m, k = x.shape
    n = w.shape[1]
    assert m % tm == 0 and n % tn == 0
    return pl.pallas_call(
        _norm_matmul_kernel,
        grid=(m // tm, n // tn),
        in_specs=[
            pl.BlockSpec((tm, k), lambda i, j: (i, 0)),
            pl.BlockSpec((1, k), lambda i, j: (0, 0)),
            pl.BlockSpec((k, tn), lambda i, j: (0, j)),
            pl.BlockSpec((1, tn), lambda i, j: (0, j)),
        ],
        out_specs=pl.BlockSpec((tm, tn), lambda i, j: (i, j)),
        out_shape=jax.ShapeDtypeStruct((m, n), out_dtype),
        scratch_shapes=[pltpu.VMEM((tm, k), BF16)],
        compiler_params=_cparams("parallel", "arbitrary"),
        name="norm_matmul",
    )(x, gain.reshape(1, k), w, bias.reshape(1, n))


def _proj_kernel(x_ref, g_ref, *refs, n_blk, dil, tail_tiles, tiles_per_b):
    w_refs, b_refs, o_ref = refs[:n_blk], refs[n_blk:2 * n_blk], refs[2 * n_blk]
    refs = refs[1:] if tail_tiles else refs
    tm = x_ref.shape[0]
    hn = _rms(x_ref[...], g_ref[...]).astype(BF16)
    for n in range(n_blk):
        acc = jnp.dot(hn, w_refs[n][...], preferred_element_type=F32) + b_refs[n][...]
        if tail_tiles and n >= 1:
            @pl.when(pl.program_id(0) % tiles_per_b >= tiles_per_b - tail_tiles)
            def _(acc=acc, n=n):
                tail_ref = refs[2 * n_blk]
                for c in range(N_HEADS):
                    tail_ref[pl.ds((n - 1) * N_HEADS + c, tm, stride=KV_ROWS), :] = acc[:, c * HEAD_DIM:(c + 1) * HEAD_DIM]
        if dil == 1:
            o_ref[:, n * BRANCH_W:(n + 1) * BRANCH_W] = acc.astype(o_ref.dtype)
            continue
        scr_ref = refs[2 * n_blk + 1]
        for c in range(N_HEADS):
            scr_ref[n * N_HEADS + c] = acc[:, c * HEAD_DIM:(c + 1) * HEAD_DIM]
        for c in range(N_HEADS):
            slab = n * N_HEADS + c
            lo = n * BRANCH_W + c * HEAD_DIM
            if dil <= DEINTERLEAVE_STRIDE:
                for r in range(dil):
                    o_ref[r, :, lo:lo + HEAD_DIM] = (
                        scr_ref[slab, pl.ds(r, tm // dil, stride=dil), :].astype(o_ref.dtype))
                continue
            s1, s2 = DEINTERLEAVE_STRIDE, dil // DEINTERLEAVE_STRIDE
            mid_ref = refs[2 * n_blk + 2]
            for a in range(s1):
                mid_ref[slab, a] = scr_ref[slab, pl.ds(a, tm // s1, stride=s1), :]
            for a in range(s1):
                for b in range(s2):
                    o_ref[a + s1 * b, :, lo:lo + HEAD_DIM] = (
                        mid_ref[slab, a, pl.ds(b, tm // dil, stride=s2), :].astype(o_ref.dtype))


def proj(x, gain, w, bias, blocks, batch, seq, dil, out_dtype, tm, tail_rows=0):
    m, k = x.shape
    n_blk = len(blocks)
    cols = n_blk * BRANCH_W
    assert m == batch * seq and seq % tm == 0 and tm % dil == 0 and tail_rows % tm == 0
    assert not tail_rows or n_blk == 3
    tiles_per_b = seq // tm
    tail_tiles = tail_rows // tm
    w_specs = [pl.BlockSpec((k, BRANCH_W), lambda i, c=c: (0, c)) for c in blocks]
    b_specs = [pl.BlockSpec((1, BRANCH_W), lambda i, c=c: (0, c)) for c in blocks]
    if dil == 1:
        out_spec = pl.BlockSpec((tm, cols), lambda i: (i, 0))
        out_shape = jax.ShapeDtypeStruct((m, cols), out_dtype)
        scratch = []
    else:
        out_spec = pl.BlockSpec((None, dil, tm // dil, cols), lambda i: (i // tiles_per_b, 0, i % tiles_per_b, 0))
        out_shape = jax.ShapeDtypeStruct((batch, dil, seq // dil, cols), out_dtype)
        scratch = [pltpu.VMEM((n_blk * N_HEADS, tm, HEAD_DIM), F32)]
        if dil > DEINTERLEAVE_STRIDE:
            assert dil % DEINTERLEAVE_STRIDE == 0 and dil // DEINTERLEAVE_STRIDE <= DEINTERLEAVE_STRIDE
            scratch.append(pltpu.VMEM((n_blk * N_HEADS, DEINTERLEAVE_STRIDE, tm // DEINTERLEAVE_STRIDE, HEAD_DIM), F32))
    out_specs, out_shapes = [out_spec], [out_shape]
    if tail_tiles:
        first_tail = tiles_per_b - tail_tiles
        out_specs.append(pl.BlockSpec((None, tm * KV_ROWS, HEAD_DIM),
                                      lambda i: (i // tiles_per_b, jnp.maximum(i % tiles_per_b - first_tail, 0), 0)))
        out_shapes.append(jax.ShapeDtypeStruct((batch, tail_rows * KV_ROWS, HEAD_DIM), F32))
    outs = pl.pallas_call(
        functools.partial(_proj_kernel, n_blk=n_blk, dil=dil, tail_tiles=tail_tiles, tiles_per_b=tiles_per_b),
        grid=(m // tm,),
        in_specs=[pl.BlockSpec((tm, k), lambda i: (i, 0)), pl.BlockSpec((1, k), lambda i: (0, 0))] + w_specs + b_specs,
        out_specs=out_specs,
        out_shape=out_shapes,
        scratch_shapes=scratch,
        compiler_params=_cparams("arbitrary"),
        name="proj",
    )(x, gain.reshape(1, k), *([w] * n_blk), *([bias] * n_blk))
    return outs if tail_tiles else outs[0]


def _proj_groups_kernel(x_ref, g_ref, *refs, sizes):
    n = sum(sizes)
    w_refs, b_refs, o_refs = refs[:n], refs[n:2 * n], refs[2 * n:]
    hn = _rms(x_ref[...], g_ref[...]).astype(BF16)
    k = 0
    for o_ref, size in zip(o_refs, sizes):
        for j in range(size):
            acc = jnp.dot(hn, w_refs[k][...], preferred_element_type=F32) + b_refs[k][...]
            o_ref[:, j * BRANCH_W:(j + 1) * BRANCH_W] = acc.astype(o_ref.dtype)
            k += 1


def proj_groups(x, gain, w, bias, groups, out_dtypes, tm):
    m, k = x.shape
    assert m % tm == 0
    blocks = [c for grp in groups for c in grp]
    once = lambda shape, c: pl.BlockSpec(shape, lambda i, c=c: (0, c), pipeline_mode=pl.Buffered(1))
    return pl.pallas_call(
        functools.partial(_proj_groups_kernel, sizes=tuple(len(grp) for grp in groups)),
        grid=(m // tm,),
        in_specs=[pl.BlockSpec((tm, k), lambda i: (i, 0)), pl.BlockSpec((1, k), lambda i: (0, 0))]
        + [once((k, BRANCH_W), c) for c in blocks] + [once((1, BRANCH_W), c) for c in blocks],
        out_specs=[pl.BlockSpec((tm, len(grp) * BRANCH_W), lambda i: (i, 0)) for grp in groups],
        out_shape=[jax.ShapeDtypeStruct((m, len(grp) * BRANCH_W), dt) for grp, dt in zip(groups, out_dtypes)],
        compiler_params=_cparams("parallel"),
        name="proj_groups",
    )(x, gain.reshape(1, k), *([w] * len(blocks)), *([bias] * len(blocks)))


def _dil_prompt_kernel(q_ref, kp_ref, kc_ref, vp_ref, vc_ref, bias_ref, o_ref, lse_ref, *scratch, dil, qb):
    scale = 1.0 / math.sqrt(HEAD_DIM)
    no_prev = (pl.program_id(1) == 0) & (lax.broadcasted_iota(jnp.int32, (1, 2 * BAND), 1) < BAND)

    def band_blocks(blocks):
        items = [(r, j, h, slice(j * BAND, (j + 1) * BAND), slice(h * HEAD_DIM, (h + 1) * HEAD_DIM))
                 for r, j in blocks for h in range(N_HEADS)]

        def keys(cur_ref, prev_ref, r, j, rows, hs):
            if j == 0:
                return jnp.concatenate([prev_ref[r, :, hs], cur_ref[r, rows, hs]], axis=0)
            return cur_ref[r, (j - 1) * BAND:(j + 1) * BAND, hs]

        logits = []
        for r, j, h, rows, hs in items:
            l = lax.dot_general(q_ref[r, rows, hs], keys(kc_ref, kp_ref, r, j, rows, hs), NT_DIMS,
                                preferred_element_type=F32) * scale + bias_ref[h]
            logits.append(jnp.where(no_prev, NEG_INF, l) if j == 0 else l)
        probs = []
        for l in logits:
            m = jnp.max(jnp.maximum(l[:, :BAND], l[:, BAND:]), axis=-1, keepdims=True)
            p = jnp.exp(l - m)
            s = jnp.sum(p[:, :BAND] + p[:, BAND:], axis=-1, keepdims=True)
            probs.append((p.astype(BF16), s, jnp.broadcast_to(m + jnp.log(s), (BAND, LSE_LANES))))
        outs = [jnp.dot(p, keys(vc_ref, vp_ref, r, j, rows, hs), preferred_element_type=F32) / s
                for (r, j, h, rows, hs), (p, s, _) in zip(items, probs)]
        return [(outs[N_HEADS * n:N_HEADS * (n + 1)],
                 jnp.concatenate([lse for _, _, lse in probs[N_HEADS * n:N_HEADS * (n + 1)]], axis=1))
                for n in range(len(blocks))]

    if dil == 1:
        for j, (outs, lse) in enumerate(band_blocks([(0, j) for j in range(qb)])):
            for h in range(N_HEADS):
                o_ref[j * BAND:(j + 1) * BAND, h * HEAD_DIM:(h + 1) * HEAD_DIM] = outs[h].astype(o_ref.dtype)
            lse_ref[j * BAND:(j + 1) * BAND, :] = lse
        return

    o_scr, lse_scr = scratch
    per_iter = min(dil, 4)

    def body(it, carry):
        blocks = [(it * per_iter + rr, j) for rr in range(per_iter) for j in range(qb)]
        for (r, j), (outs, lse) in zip(blocks, band_blocks(blocks)):
            for h in range(N_HEADS):
                o_scr[h, pl.ds(r + j * dil * BAND, BAND, stride=dil), :] = outs[h]
            lse_scr[pl.ds(r + j * dil * BAND, BAND, stride=dil), :] = lse
        return carry

    lax.fori_loop(0, dil // per_iter, body, 0)
    for h in range(N_HEADS):
        o_ref[:, h * HEAD_DIM:(h + 1) * HEAD_DIM] = o_scr[h].astype(o_ref.dtype)
    lse_ref[...] = lse_scr[...]


def dilated_prompt(qkv, bias, g, dil, batch, seq, qb):
    assert seq % (dil * BAND * qb) == 0
    sub = seq // dil
    span = dil * BAND * qb
    cur = lambda part: pl.BlockSpec((None, dil, qb * BAND, BRANCH_W), lambda b, i: (b, 0, i, part))
    prev = lambda part: pl.BlockSpec((None, dil, BAND, BRANCH_W),
                                     lambda b, i: (b, 0, jnp.maximum(i * qb - 1, 0), part))
    bias_spec = pl.BlockSpec((N_HEADS, BAND, 2 * BAND), lambda b, i: (0, 0, 0))
    scratch = [] if dil == 1 else [pltpu.VMEM((N_HEADS, span, HEAD_DIM), F32),
                                   pltpu.VMEM((span, N_HEADS * LSE_LANES), F32)]
    o, lse = pl.pallas_call(
        functools.partial(_dil_prompt_kernel, dil=dil, qb=qb),
        grid=(batch, sub // (BAND * qb)),
        in_specs=[cur(0), prev(1), cur(1), prev(2), cur(2), bias_spec],
        out_specs=[pl.BlockSpec((None, span, BRANCH_W), lambda b, i: (b, i, 0)),
                   pl.BlockSpec((None, span, N_HEADS * LSE_LANES), lambda b, i: (b, i, 0))],
        out_shape=[jax.ShapeDtypeStruct((batch, seq, BRANCH_W), BF16),
                   jax.ShapeDtypeStruct((batch, seq, N_HEADS * LSE_LANES), F32)],
        scratch_shapes=scratch,
        compiler_params=_cparams("parallel", "arbitrary"),
        name=f"dilated_prompt_g{g}",
    )(qkv, qkv, qkv, qkv, qkv, bias)
    return o.reshape(batch * seq, BRANCH_W), lse.reshape(batch * seq, N_HEADS * LSE_LANES)


def _dil_sample_kernel(q_ref, kn_ref, vn_ref, cache_ref, bc_ref, bn_ref, shifted_ref, o_ref, lse_ref, newrows_ref,
                       *, t_new):
    del shifted_ref
    scale = 1.0 / math.sqrt(HEAD_DIM)
    bb, n_grp, grp_rows, _ = cache_ref.shape
    n_keys = n_grp * grp_rows // KV_ROWS
    rows_of = lambda b, first: cache_ref[b, :, pl.ds(first, grp_rows // KV_ROWS, stride=KV_ROWS), :].reshape(
        n_keys, HEAD_DIM).astype(BF16)
    heads = [(b, h, slice(h * HEAD_DIM, (h + 1) * HEAD_DIM)) for b in range(bb) for h in range(N_HEADS)]
    logits = []
    for b, h, hs in heads:
        q = q_ref[b, :, hs]
        lc = lax.dot_general(q.astype(BF16), rows_of(b, h), NT_DIMS, preferred_element_type=F32) * scale + bc_ref[h]
        ln = lax.dot_general(q, kn_ref[b, :, hs], NT_DIMS, preferred_element_type=F32) * scale + bn_ref[h]
        logits.append((lc, ln))
    probs = []
    for lc, ln in logits:
        m = jnp.maximum(jnp.max(lc, axis=-1, keepdims=True), jnp.max(ln, axis=-1, keepdims=True))
        pc = jnp.exp(lc - m)
        pn = jnp.exp(ln - m)
        s = jnp.sum(pc, axis=-1, keepdims=True) + jnp.sum(pn, axis=-1, keepdims=True)
        probs.append((pc.astype(BF16), pn, s, m + jnp.log(s)))
    for (b, h, hs), (pc, pn, s, lse) in zip(heads, probs):
        kn = kn_ref[b, :, hs]
        vn = vn_ref[b, :, hs]
        o = (jnp.dot(pc, rows_of(b, N_HEADS + h), preferred_element_type=F32)
             + jnp.dot(pn, vn, preferred_element_type=F32))
        o_ref[b, :, hs] = o / s
        lse_ref[b, :, h * LSE_LANES:(h + 1) * LSE_LANES] = jnp.broadcast_to(lse, (t_new, LSE_LANES))
        newrows_ref[b, pl.ds(h, t_new, stride=KV_ROWS), :] = kn
        newrows_ref[b, pl.ds(N_HEADS + h, t_new, stride=KV_ROWS), :] = vn


def _sample_period(win, dil, t_new):
    return (dil, t_new) if (dil > t_new and win % dil == 0) else (win, win)


def _sample_positions(win, dil, t_new):
    period, take = _sample_period(win, dil, t_new)
    return (np.arange(win // period)[:, None] * period + np.arange(take)[None, :]).reshape(-1)


def dilated_sample(qkv, cache, shifted, bias_cache, bias_new, g, dil, batch, t_new, bb):
    win = cache.shape[1] // KV_ROWS
    assert win % t_new == 0 and shifted.shape == cache.shape and batch % bb == 0
    period, take = _sample_period(win, dil, t_new)
    n_grp = win // period
    cache4 = cache.reshape(batch, n_grp, period * KV_ROWS, HEAD_DIM)
    assert bias_cache.shape == (N_HEADS, t_new, n_grp * take)
    blk = (bb, t_new, BRANCH_W)
    o, lse, newc = pl.pallas_call(
        functools.partial(_dil_sample_kernel, t_new=t_new),
        grid=(batch // bb,),
        in_specs=[pl.BlockSpec(blk, lambda b: (b, 0, 0)),
                  pl.BlockSpec(blk, lambda b: (b, 0, 1)),
                  pl.BlockSpec(blk, lambda b: (b, 0, 2)),
                  pl.BlockSpec((bb, n_grp, take * KV_ROWS, HEAD_DIM), lambda b: (b, 0, 0, 0)),
                  pl.BlockSpec((N_HEADS, t_new, n_grp * take), lambda b: (0, 0, 0)),
                  pl.BlockSpec((N_HEADS, t_new, t_new), lambda b: (0, 0, 0)),
                  pl.BlockSpec(memory_space=pl.ANY)],
        out_specs=[pl.BlockSpec(blk, lambda b: (b, 0, 0)),
                   pl.BlockSpec((bb, t_new, N_HEADS * LSE_LANES), lambda b: (b, 0, 0)),
                   pl.BlockSpec((bb, t_new * KV_ROWS, HEAD_DIM), lambda b: (b, win // t_new - 1, 0))],
        out_shape=[jax.ShapeDtypeStruct((batch, t_new, BRANCH_W), F32),
                   jax.ShapeDtypeStruct((batch, t_new, N_HEADS * LSE_LANES), F32),
                   jax.ShapeDtypeStruct(cache.shape, F32)],
        input_output_aliases={6: 2},
        compiler_params=_cparams("parallel"),
        name=f"dilated_sample_g{g}",
    )(qkv, qkv, qkv, cache4, bias_cache, bias_new, shifted)
    return o.reshape(batch * t_new, BRANCH_W), lse.reshape(batch * t_new, N_HEADS * LSE_LANES), newc


def _mem_attn_kernel(q_ref, kv_ref, o_ref, *, mem):
    scale = 1.0 / math.sqrt(HEAD_DIM)
    heads = [(b, h, slice(h * HEAD_DIM, (h + 1) * HEAD_DIM)) for b in range(q_ref.shape[0]) for h in range(N_HEADS)]
    rows_of = lambda b, first: kv_ref[b, pl.ds(first, mem, stride=KV_ROWS), :].astype(BF16)
    logits = [lax.dot_general(q_ref[b, :, hs].astype(BF16), rows_of(b, h), NT_DIMS, preferred_element_type=F32) * scale
              for b, h, hs in heads]
    probs = []
    for l in logits:
        half = l.shape[1] // 2
        m = jnp.max(jnp.maximum(l[:, :half], l[:, half:]), axis=-1, keepdims=True)
        p = jnp.exp(l - m)
        probs.append((p.astype(BF16), jnp.sum(p[:, :half] + p[:, half:], axis=-1, keepdims=True)))
    for (b, h, hs), (p, s) in zip(heads, probs):
        o = jnp.dot(p, rows_of(b, N_HEADS + h), preferred_element_type=F32)
        o_ref[b, :, hs] = (o / s).astype(o_ref.dtype)


def memory_attention(rec, mem_kv, batch, t, tq, bb, out_dtype):
    mem = mem_kv.shape[1] // KV_ROWS
    assert batch % bb == 0
    x = rec.reshape(batch, t, REC_COLS)
    o = pl.pallas_call(
        functools.partial(_mem_attn_kernel, mem=mem),
        grid=(batch // bb, t // tq),
        in_specs=[pl.BlockSpec((bb, tq, BRANCH_W), lambda b, i: (b, i, REC_MQ)),
                  pl.BlockSpec((bb, mem * KV_ROWS, HEAD_DIM), lambda b, i: (b, 0, 0))],
        out_specs=pl.BlockSpec((bb, tq, BRANCH_W), lambda b, i: (b, i, 0)),
        out_shape=jax.ShapeDtypeStruct((batch, t, BRANCH_W), out_dtype),
        compiler_params=_cparams("parallel", "parallel"),
        name="memory_attention",
    )(x, mem_kv)
    return o.reshape(batch * t, BRANCH_W)


def _hgrn_tables(chunk):
    c, b = chunk, HG_BASE
    t = np.arange(c)[:, None]
    u = np.arange(c)[None, :]
    masks = []
    size = 2 * b
    while size <= c:
        half = size // 2
        masks.append(((t // size) == (u // size)) & ((t % size) >= half) & ((u % size) < half))
        size *= 2
    masks.append(((t // b) == (u // b)) & (u <= t))
    erep = np.zeros((b * HEAD_DIM, c), np.float32)
    for s in range(b):
        erep[s * HEAD_DIM:(s + 1) * HEAD_DIM, np.arange(c) % b == s] = 1.0
    return (u <= t).astype(np.float32), erep, np.stack(masks).astype(np.float32)


def _shift_step(old_refs, new_refs, buf_refs, zero_ref, sem_in, sem_out, sem_zero, step, n_steps, drop):
    slot = step % 2
    other = 1 - slot

    def copies(g, chunk_idx, buf_slot):
        old, new, buf = old_refs[g], new_refs[g], buf_refs[g]
        rows_c = buf.shape[1]
        per_batch = (old.shape[1] - drop) // rows_c
        b = chunk_idx // per_batch
        lo = pl.multiple_of((chunk_idx % per_batch) * rows_c, 8)
        return (pltpu.make_async_copy(old.at[b, pl.ds(lo + drop, rows_c), :], buf.at[buf_slot], sem_in.at[g, buf_slot]),
                pltpu.make_async_copy(buf.at[buf_slot], new.at[b, pl.ds(lo, rows_c), :], sem_out.at[g, buf_slot]))

    def zero_copy(g):
        keep = old_refs[g].shape[1] - drop
        return pltpu.make_async_copy(zero_ref, new_refs[g].at[:, pl.ds(keep, drop), :], sem_zero.at[g])

    groups = range(len(old_refs))

    @pl.when(step == 0)
    def _():
        zero_ref[...] = jnp.zeros_like(zero_ref)
        for g in groups:
            zero_copy(g).start()
            copies(g, 0, 0)[0].start()

    for g in groups:
        copies(g, step, slot)[0].wait()

    @pl.when(step >= 1)
    def _():
        for g in groups:
            copies(g, step - 1, other)[1].wait()

    for g in groups:
        copies(g, step, slot)[1].start()

    @pl.when(step + 1 < n_steps)
    def _():
        for g in groups:
            copies(g, step + 1, other)[0].start()

    @pl.when(step == n_steps - 1)
    def _():
        for g in groups:
            copies(g, step, slot)[1].wait()
            zero_copy(g).wait()


def _shift_scratch(shift, drop, n_steps):
    if not shift:
        return []
    sb = shift[0].shape[0]
    assert n_steps % sb == 0 and all(s.shape[0] == sb and s.shape[2] == HEAD_DIM for s in shift)
    per_batch = n_steps // sb
    chunk_rows = [(s.shape[1] - drop) // per_batch for s in shift]
    assert all(r % 8 == 0 and r * per_batch == s.shape[1] - drop for r, s in zip(chunk_rows, shift))
    n = len(shift)
    return ([pltpu.VMEM((2, r, HEAD_DIM), F32) for r in chunk_rows]
            + [pltpu.VMEM((sb, drop, HEAD_DIM), F32), pltpu.SemaphoreType.DMA((n, 2)),
               pltpu.SemaphoreType.DMA((n, 2)), pltpu.SemaphoreType.DMA((n,))])


def _hgrn_kernel(q_ref, v_ref, f_ref, gate_ref, lbl_ref, gn_ref, s0_ref, tri_ref, erep_ref, mask_ref, *rest,
                 chunk, rows_in, n_sub, bb, n_levels, n_shift, shift_drop, n_steps):
    old_refs, rest = rest[:n_shift], rest[n_shift:]
    o_ref, s_out_ref = rest[:2]
    new_refs, rest = rest[2:2 + n_shift], rest[2 + n_shift:]
    st_ref = rest[0]
    c = chunk
    step = pl.program_id(1)
    if n_shift:
        buf_refs, (zero_ref, sem_in, sem_out, sem_zero) = rest[1:1 + n_shift], rest[1 + n_shift:]
        _shift_step(old_refs, new_refs, buf_refs, zero_ref, sem_in, sem_out, sem_zero,
                    pl.program_id(0) * pl.num_programs(1) + step, n_steps, shift_drop)

    @pl.when(step == 0)
    def _():
        for b in range(bb):
            for h in range(N_HEADS):
                st_ref[b * N_HEADS + h] = s0_ref[b, h].T

    lbl = lbl_ref[...]
    e = jnp.exp(lbl - jnp.max(lbl, axis=0, keepdims=True))
    lb = e[0:1] / jnp.sum(e, axis=0, keepdims=True)
    tri = tri_ref[...]

    def rows(ref, b, ci):
        x = ref[b, ci * rows_in:(ci + 1) * rows_in, :].astype(F32)
        if rows_in < c:
            x = jnp.concatenate([x, jnp.zeros((c - rows_in, x.shape[1]), F32)], axis=0)
        return x

    def at_row(x, size, idx):
        x3 = x.reshape(c // size, size, x.shape[1])
        return jnp.broadcast_to(x3[:, idx:idx + 1, :], x3.shape).reshape(x.shape)

    chunk_ids = [(b, ci) for b in range(bb) for ci in range(n_sub)]
    chunks = range(len(chunk_ids))
    heads = [(h, slice(h * HEAD_DIM, (h + 1) * HEAD_DIM)) for h in range(N_HEADS)]

    q, v, kk, cum = [], [], [], []
    for b, ci in chunk_ids:
        q.append(rows(q_ref, b, ci))
        v.append(rows(v_ref, b, ci).astype(BF16))
        f = lb + (1.0 - lb) * _sigmoid(rows(f_ref, b, ci))
        g = jnp.log(f)
        k_in = 1.0 - f
        if rows_in < c:
            live = lax.broadcasted_iota(jnp.int32, (c, 1), 0) < rows_in
            g = jnp.where(live, g, 0.0)
            k_in = jnp.where(live, k_in, 0.0)
        kk.append(k_in)
        g1 = g.astype(BF16)
        r1 = g - g1.astype(F32)
        g2 = r1.astype(BF16)
        g3 = (r1 - g2.astype(F32)).astype(BF16)
        cum.append(jnp.dot(tri, g1, preferred_element_type=F32) + jnp.dot(tri, g2, preferred_element_type=F32)
                   + jnp.dot(tri, g3, preferred_element_type=F32))

    q_inter, k_end, g_end, q_lvl, k_lvl, x_diag = [], [], [], [], [], []
    for ci in chunks:
        q_inter.append((q[ci] * jnp.exp(cum[ci])).astype(BF16))
        g_end.append(cum[ci][c - 1:c])
        k_end.append((kk[ci] * jnp.exp(g_end[ci] - cum[ci])).astype(BF16))
        ql, kl = [], []
        for l in range(n_levels):
            size = 2 * HG_BASE << l
            ref = at_row(cum[ci], size, size // 2 - 1)
            ql.append((q[ci] * jnp.exp(cum[ci] - ref)).astype(BF16))
            kl.append((kk[ci] * jnp.exp(ref - cum[ci])).astype(BF16))
        q_lvl.append(ql)
        k_lvl.append(kl)
        xd = []
        for s in range(HG_BASE):
            decay = jnp.exp(jnp.minimum(cum[ci] - at_row(cum[ci], HG_BASE, s), 0.0))
            xd.append((q[ci] * at_row(kk[ci], HG_BASE, s) * decay).astype(BF16))
        x_diag.append(xd)

    level_mask = [mask_ref[l] > 0.5 for l in range(n_levels)]
    scores, state_in = [], []
    for ci in chunks:
        sc, si = [], []
        for h, hs in heads:
            s_h = mask_ref[n_levels] * jnp.dot(jnp.concatenate([x[:, hs] for x in x_diag[ci]], axis=1),
                                               erep_ref[...], preferred_element_type=F32)
            for l in range(n_levels):
                s_h += jnp.where(level_mask[l], lax.dot_general(q_lvl[ci][l][:, hs], k_lvl[ci][l][:, hs], NT_DIMS,
                                                                preferred_element_type=F32), 0.0)
            sc.append(s_h.astype(BF16))
            si.append(lax.dot_general(v[ci][:, hs], k_end[ci][:, hs], TN_DIMS, preferred_element_type=F32))
        scores.append(sc)
        state_in.append(si)

    states = {}
    for b in range(bb):
        for h, hs in heads:
            st = st_ref[b * N_HEADS + h]
            for ci in range(n_sub):
                idx = b * n_sub + ci
                states[idx, h] = st.astype(BF16)
                st = st * jnp.exp(g_end[idx][:, hs]) + state_in[idx][h]
            st_ref[b * N_HEADS + h] = st
    for idx, (b, ci) in enumerate(chunk_ids):
        gate = _sigmoid(rows(gate_ref, b, ci))
        for h, hs in heads:
            o = (lax.dot_general(q_inter[idx][:, hs], states[idx, h], NT_DIMS, preferred_element_type=F32)
                 + jnp.dot(scores[idx][h], v[idx][:, hs], preferred_element_type=F32))
            y = _rms(o, gn_ref[...]) * gate[:, hs]
            o_ref[b, ci * rows_in:(ci + 1) * rows_in, hs] = y[0:rows_in].astype(o_ref.dtype)

    @pl.when(step == pl.num_programs(1) - 1)
    def _():
        for b in range(bb):
            for h in range(N_HEADS):
                s_out_ref[b, h] = st_ref[b * N_HEADS + h].T


def hgrn2(rec, gates, lb_logits, hg_norm, state0, batch, t, chunk, n_sub, bb, out_dtype, shift=(), shift_drop=0):
    rows_in = min(chunk, t)
    step_rows = rows_in * n_sub
    assert t % step_rows == 0 and chunk % HG_BASE == 0 and (n_sub == 1 or rows_in == chunk) and batch % bb == 0
    n_steps = (batch // bb) * (t // step_rows)
    n_shift = len(shift)
    any_spec = pl.BlockSpec(memory_space=pl.ANY)
    shift_scratch = _shift_scratch(shift, shift_drop, n_steps)
    tri, erep, masks = _hgrn_tables(chunk)
    n_levels = masks.shape[0] - 1
    a = rec.reshape(batch, t, REC_COLS)
    gt = gates.reshape(batch, t, G_COLS)
    blk = (bb, step_rows, BRANCH_W)
    full = lambda arr: pl.BlockSpec(arr.shape, lambda b, i: (0,) * arr.ndim)
    st_spec = pl.BlockSpec((bb, N_HEADS, HEAD_DIM, HEAD_DIM), lambda b, i: (b, 0, 0, 0))
    tri = jnp.asarray(tri, BF16)
    erep = jnp.asarray(erep, BF16)
    masks = jnp.asarray(masks, F32)
    lbl = lb_logits.astype(F32)
    gn = hg_norm.reshape(1, HEAD_DIM).astype(F32)
    o, s_fin, *shifted = pl.pallas_call(
        functools.partial(_hgrn_kernel, chunk=chunk, rows_in=rows_in, n_sub=n_sub, bb=bb, n_levels=n_levels,
                          n_shift=n_shift, shift_drop=shift_drop, n_steps=n_steps),
        grid=(batch // bb, t // step_rows),
        in_specs=[pl.BlockSpec(blk, lambda b, i: (b, i, REC_HQ)),
                  pl.BlockSpec(blk, lambda b, i: (b, i, REC_HI)),
                  pl.BlockSpec(blk, lambda b, i: (b, i, 0)),
                  pl.BlockSpec(blk, lambda b, i: (b, i, 1)),
                  full(lbl), full(gn), st_spec, full(tri), full(erep), full(masks)] + [any_spec] * n_shift,
        out_specs=[pl.BlockSpec(blk, lambda b, i: (b, i, 0)), st_spec] + [any_spec] * n_shift,
        out_shape=[jax.ShapeDtypeStruct((batch, t, BRANCH_W), out_dtype),
                   jax.ShapeDtypeStruct(state0.shape, F32)] + [jax.ShapeDtypeStruct(s.shape, F32) for s in shift],
        scratch_shapes=[pltpu.VMEM((bb * N_HEADS, HEAD_DIM, HEAD_DIM), F32)] + shift_scratch,
        compiler_params=_cparams("arbitrary", "arbitrary"),
        name="hgrn2",
    )(a, a, gt, gt, lbl, gn, state0, tri, erep, masks, *shift)
    return o.reshape(batch * t, BRANCH_W), s_fin, shifted


def _merge_kernel(x_ref, o1_ref, o2_ref, o3_ref, l1_ref, l2_ref, l3_ref, hg_ref, mem_ref, gpre_ref,
                  wga_ref, wgh_ref, wgm_ref, bga_ref, bgh_ref, bgm_ref,
                  wa_ref, wh_ref, wm_ref, wo_ref, gain_ref, *rest, n_shift, shift_drop, n_steps):
    old_refs, out_ref, new_refs, scratch = rest[:n_shift], rest[n_shift], rest[n_shift + 1:2 * n_shift + 1], \
        rest[2 * n_shift + 1:]
    if n_shift:
        _shift_step(old_refs, new_refs, scratch[:n_shift], *scratch[n_shift:], pl.program_id(0), n_steps, shift_drop)
    tm = x_ref.shape[0]
    x = x_ref[...]
    hn = _rms(x, gpre_ref[...]).astype(BF16)
    gate_logits = [jnp.dot(hn, w_ref[...], preferred_element_type=F32) + b_ref[...]
                   for w_ref, b_ref in ((wga_ref, bga_ref), (wgh_ref, bgh_ref), (wgm_ref, bgm_ref))]
    from_hg = jnp.dot(hg_ref[...].astype(BF16), wh_ref[...], preferred_element_type=F32)
    from_mem = jnp.dot(mem_ref[...].astype(BF16), wm_ref[...], preferred_element_type=F32)
    l1, l2, l3 = l1_ref[...], l2_ref[...], l3_ref[...]
    m = jnp.maximum(jnp.maximum(l1, l2), l3)
    e1, e2, e3 = jnp.exp(l1 - m), jnp.exp(l2 - m), jnp.exp(l3 - m)
    den = e1 + e2 + e3
    w1, w2, w3 = e1 / den, e2 / den, e3 / den
    att = []
    for h in range(N_HEADS):
        hs = slice(h * HEAD_DIM, (h + 1) * HEAD_DIM)
        col = slice(h * LSE_LANES, h * LSE_LANES + 1)
        bc = lambda w: jnp.broadcast_to(w[:, col], (tm, HEAD_DIM))
        att.append((bc(w1) * o1_ref[:, hs].astype(F32) + bc(w2) * o2_ref[:, hs].astype(F32)
                    + bc(w3) * o3_ref[:, hs].astype(F32)).astype(BF16))
    att = jnp.concatenate(att, axis=1)
    ga, gh, gm = (_sigmoid(z) for z in gate_logits)
    merged = ga * jnp.dot(att, wa_ref[...], preferred_element_type=F32) + gh * from_hg + gm * from_mem
    y = jnp.dot(merged.astype(BF16), wo_ref[...], preferred_element_type=F32)
    out_ref[...] = x + _rms(y, gain_ref[...])


def merge(x, o_groups, lse_groups, hg_o, mem_o, gain_pre, w_in, b_in, w_att, w_hg, w_mem, w_out, gain, tm,
          shift=(), shift_drop=0):
    m, d = x.shape
    assert m % tm == 0 and GATE_MERGE_BLOCKS[0] * d == 14 * BRANCH_W
    n_steps = m // tm
    n_shift = len(shift)
    any_spec = pl.BlockSpec(memory_space=pl.ANY)
    row = lambda w: pl.BlockSpec((tm, w), lambda i: (i, 0))
    resident = lambda shape, j=0: pl.BlockSpec(shape, lambda i: (0, j), pipeline_mode=pl.Buffered(1))
    x1, *shifted = pl.pallas_call(
        functools.partial(_merge_kernel, n_shift=n_shift, shift_drop=shift_drop, n_steps=n_steps),
        grid=(n_steps,),
        in_specs=[row(d), row(BRANCH_W), row(BRANCH_W), row(BRANCH_W),
                  row(N_HEADS * LSE_LANES), row(N_HEADS * LSE_LANES), row(N_HEADS * LSE_LANES),
                  row(BRANCH_W), row(BRANCH_W), resident((1, d))]
        + [resident((d, d), j) for j in GATE_MERGE_BLOCKS] + [resident((1, d), j) for j in GATE_MERGE_BLOCKS]
        + [resident(w_att.shape), resident(w_hg.shape), resident(w_mem.shape), resident(w_out.shape),
           resident((1, d))] + [any_spec] * n_shift,
        out_specs=[row(d)] + [any_spec] * n_shift,
        out_shape=[jax.ShapeDtypeStruct((m, d), F32)] + [jax.ShapeDtypeStruct(s.shape, F32) for s in shift],
        scratch_shapes=_shift_scratch(shift, shift_drop, n_steps),
        compiler_params=_cparams("arbitrary"),
        name="merge",
    )(x, *o_groups, *lse_groups, hg_o, mem_o, gain_pre.reshape(1, d), w_in, w_in, w_in, b_in, b_in, b_in,
      w_att, w_hg, w_mem, w_out, gain.reshape(1, d), *shift)
    return x1, shifted


def _ffn_kernel(x_ref, gpre_ref, wa_ref, wb_ref, cw_ref, cb_ref, wd_ref, gpost_ref, cbuf_ref,
                out_ref, tail_ref, carry_ref, *, t_seq, tiles_per_seq, tf):
    tm = x_ref.shape[0]
    dff = wa_ref.shape[1]
    x = x_ref[...]
    hn = _rms(x, gpre_ref[...]).astype(BF16)

    if tiles_per_seq >= 1:
        @pl.when((pl.program_id(0) % tiles_per_seq) == 0)
        def _():
            carry_ref[6:8, :] = cbuf_ref[0]

        t_idx = lax.broadcasted_iota(jnp.int32, (tm, 1), 0)
    else:
        n_seq = tm // t_seq
        t_idx = lax.broadcasted_iota(jnp.int32, (tm, 1), 0) % t_seq

    n_chunks = dff // tf

    def up_proj(j):
        cols = slice(j * tf, (j + 1) * tf)
        return (jnp.dot(hn, wa_ref[:, cols], preferred_element_type=F32),
                jnp.dot(hn, wb_ref[:, cols], preferred_element_type=F32))

    acc = jnp.zeros(x.shape, F32)
    ahead = up_proj(0)
    for j in range(n_chunks):
        cols = slice(j * tf, (j + 1) * tf)
        a, up = ahead
        if j + 1 < n_chunks:
            ahead = up_proj(j + 1)
        if tiles_per_seq >= 1:
            prev1 = carry_ref[7:8, cols]
            prev2 = carry_ref[6:7, cols]
            carry_ref[:, cols] = a[tm - 8:tm]
            tail_ref[0, :, cols] = a[tm - 8:tm]
        else:
            prev1 = jnp.broadcast_to(cbuf_ref[:, 1:2, cols], (n_seq, t_seq, tf)).reshape(tm, tf)
            prev2 = jnp.broadcast_to(cbuf_ref[:, 0:1, cols], (n_seq, t_seq, tf)).reshape(tm, tf)
            tail_ref[:, :, cols] = a.reshape(n_seq, t_seq, tf)
        a1 = jnp.where(t_idx >= 1, pltpu.roll(a, 1, 0), prev1)
        a2 = jnp.where(t_idx >= 2, pltpu.roll(a, 2, 0), jnp.where(t_idx == 1, prev1, prev2))
        conv = cb_ref[:, cols] + a2 * cw_ref[0:1, cols] + a1 * cw_ref[1:2, cols] + a * cw_ref[2:3, cols]
        act = conv * _sigmoid(conv) * up
        acc += jnp.dot(act.astype(BF16), wd_ref[cols, :], preferred_element_type=F32)

    out_ref[...] = x + _rms(acc, gpost_ref[...])


def conv_ffn(x, conv_buf, g_pre, w_a, w_b, conv_w, conv_b, w_d, g_post, t_seq, tm, tf):
    m, d = x.shape
    dff = w_a.shape[1]
    n_seq = m // t_seq
    assert m % tm == 0 and dff % tf == 0 and t_seq >= 8
    if tm <= t_seq:
        assert t_seq % tm == 0
        tiles_per_seq = t_seq // tm
        seq_blk = 1
        seq_idx = lambda i: (i // tiles_per_seq, 0, 0)
    else:
        assert tm % t_seq == 0 and t_seq == 8
        tiles_per_seq = 0
        seq_blk = tm // t_seq
        seq_idx = lambda i: (i, 0, 0)
    n_tail = (m // tm) * seq_blk
    resident = lambda shape: pl.BlockSpec(shape, lambda i: (0, 0), pipeline_mode=pl.Buffered(1))
    y, tail = pl.pallas_call(
        functools.partial(_ffn_kernel, t_seq=t_seq, tiles_per_seq=tiles_per_seq, tf=tf),
        grid=(m // tm,),
        in_specs=[pl.BlockSpec((tm, d), lambda i: (i, 0)), resident((1, d)),
                  resident((d, dff)), resident((d, dff)), resident((3, dff)), resident((1, dff)),
                  resident((dff, d)), resident((1, d)),
                  pl.BlockSpec((seq_blk, 2, dff), seq_idx)],
        out_specs=[pl.BlockSpec((tm, d), lambda i: (i, 0)),
                   pl.BlockSpec((seq_blk, 8, dff), lambda i: (i, 0, 0))],
        out_shape=[jax.ShapeDtypeStruct((m, d), F32), jax.ShapeDtypeStruct((n_tail, 8, dff), F32)],
        scratch_shapes=[pltpu.VMEM((8, dff), F32)],
        compiler_params=_cparams("arbitrary"),
        name="conv_ffn",
    )(x, g_pre.reshape(1, d), w_a, w_b, conv_w, conv_b.reshape(1, dff), w_d, g_post.reshape(1, d), conv_buf)
    if tiles_per_seq >= 1:
        tail = tail.reshape(n_seq, tiles_per_seq, 8, dff)[:, -1]
    return y, tail


def _rel_buckets(dil):
    max_exact = REL_BUCKETS // 2
    dist = np.arange(N_LAGS + 1, dtype=np.int32) * dil
    d = np.maximum(dist, 1).astype(np.float32)
    large = max_exact + (np.log(d / np.float32(max_exact)) / np.float32(math.log(REL_MAX_DIST / max_exact))
                         * np.float32(REL_BUCKETS - max_exact)).astype(np.int32)
    large = np.minimum(large, REL_BUCKETS - 1)
    return np.where(dist < max_exact, dist, large)


def _bias_table(rel_bias, g, dil, dist):
    dist = np.asarray(dist)
    ok = (dist >= 0) & (dist % dil == 0) & (dist <= N_LAGS * dil)
    bucket = np.where(ok, _rel_buckets(dil)[np.clip(dist // dil, 0, N_LAGS)], -1)
    onehot = bucket[..., None] == np.arange(REL_BUCKETS)
    heads = rel_bias.astype(F32)[:, g * N_HEADS:(g + 1) * N_HEADS].T
    picked = jnp.sum(jnp.where(onehot[None], heads.reshape((N_HEADS,) + (1,) * dist.ndim + (REL_BUCKETS,)), 0.0), -1)
    return jnp.where(ok[None], picked, NEG_INF)


def _prompt_bias(rel_bias, g, dil):
    return _bias_table(rel_bias, g, dil, (BAND + np.arange(BAND)[:, None] - np.arange(2 * BAND)[None, :]) * dil)


def _sample_bias(rel_bias, g, dil, cached_pos, n_past, t_new):
    t = np.arange(t_new)[:, None]
    return (_bias_table(rel_bias, g, dil, n_past + t - np.asarray(cached_pos)[None, :]),
            _bias_table(rel_bias, g, dil, t - np.arange(t_new)[None, :]))


def _layer(x, batch, t, weights, rel_bias, lb_logits, win_caches, hg_state0, conv_buf0, mem_kv,
           *, prompt, tm_proj, tm_merge, tm_ffn, tf_ffn, tq_mem, chunk, n_sub, shifted=None, side_shift=(),
           side_shift_drop=0):
    (w_in16, b_in, gain_pre, gain_post, hg_norm, w_att, w_hg, w_mem, w_out,
     gain_fpre, gain_fpost, w_fa, w_fb, conv_w, conv_b, w_fd) = weights
    act_dtype = BF16 if prompt else F32
    b_in = b_in.reshape(1, -1)
    kv_tails = {}
    if prompt:
        plain = [g for g, (_, dil) in enumerate(ATT_GROUPS) if dil == 1]
        *qkv_plain, rec, gates = proj_groups(x, gain_pre, w_in16, b_in, [QKV_BLOCKS(g) for g in plain]
                                             + [REC_BLOCKS, GATE_BLOCKS], [act_dtype] * (len(plain) + 1) + [F32], tm_proj)
        qkv = []
        for g, (win, dil) in enumerate(ATT_GROUPS):
            if g in plain:
                qkv.append(qkv_plain[plain.index(g)].reshape(batch, 1, t, QKV_COLS))
                continue
            tail_rows = win if (win % tm_proj == 0 and win <= t) else 0
            out = proj(x, gain_pre, w_in16, b_in, QKV_BLOCKS(g), batch, t, dil, act_dtype, tm_proj, tail_rows)
            if tail_rows:
                out, kv_tails[g] = out
            qkv.append(out)
    else:
        groups = [QKV_BLOCKS(g) for g in range(len(ATT_GROUPS))] + [REC_BLOCKS, GATE_BLOCKS]
        *qkv, rec, gates = proj_groups(x, gain_pre, w_in16, b_in, groups, [act_dtype] * 4 + [F32], tm_proj)
        qkv = [z.reshape(batch, t, QKV_COLS) for z in qkv]

    o_groups, lse_groups, new_caches = [], [], []
    for g, (win, dil) in enumerate(ATT_GROUPS):
        if prompt:
            qb = max(1, min(ATT_STEP_BLOCKS // dil, t // (dil * BAND)))
            o, lse = dilated_prompt(qkv[g], _prompt_bias(rel_bias, g, dil), g, dil, batch, t, qb)
        else:
            cache = win_caches[g]
            n_past = cache.shape[1] // KV_ROWS
            assert n_past >= win
            bcache, bnew = _sample_bias(rel_bias, g, dil, _sample_positions(n_past, dil, t), n_past, t)
            bb = math.gcd(batch, max(1, SAMPLE_STEP_BYTES // (cache.shape[1] * HEAD_DIM * 4)))
            o, lse, newc = dilated_sample(qkv[g], cache, shifted[g], bcache, bnew, g, dil, batch, t, bb)
            new_caches.append(newc)
        o_groups.append(o)
        lse_groups.append(lse)

    largest = max(range(len(side_shift)), key=lambda i: side_shift[i].size) if side_shift else None
    with_hgrn = [s for i, s in enumerate(side_shift) if i == largest]
    with_merge = [s for i, s in enumerate(side_shift) if i != largest]
    hg_bb = 1 if prompt else math.gcd(batch, HGRN_SAMPLE_ROWS)
    hg_o, hg_state, from_hgrn = hgrn2(rec, gates, lb_logits, hg_norm, hg_state0, batch, t, chunk, n_sub, hg_bb,
                                      act_dtype, with_hgrn, side_shift_drop)
    mem_bb = 1 if prompt else math.gcd(batch, max(1, SAMPLE_STEP_BYTES // (mem_kv.shape[1] * HEAD_DIM * 4)))
    mem_o = memory_attention(rec, mem_kv, batch, t, tq_mem, mem_bb, act_dtype)
    x1, from_merge = merge(x, o_groups, lse_groups, hg_o, mem_o, gain_pre, w_in16, b_in, w_att, w_hg, w_mem, w_out,
                           gain_post, tm_merge, with_merge, side_shift_drop)
    from_hgrn, from_merge = list(from_hgrn), list(from_merge)
    shifted_out = [from_hgrn.pop(0) if i == largest else from_merge.pop(0) for i in range(len(side_shift))]
    y, tail = conv_ffn(x1, conv_buf0, gain_fpre, w_fa, w_fb, conv_w, conv_b, w_fd, gain_fpost, t, tm_ffn, tf_ffn)
    return y, (qkv, kv_tails), new_caches, hg_state, tail[:, 6:8, :], shifted_out


def kernel(x_prompt, x_sample, mem_prompt, cache_win1_kv, cache_win2_kv, cache_win3_kv, cache_mem_kv, state_hgrn, state_ffn_conv, rel_bias, hg_lb_logits, norm_mix_pre, norm_mix_post, w_in, b_in, hg_norm, mem_norm, w_mem_kv, w_br_att, w_br_hg, w_br_mem, w_out, norm_ffn_pre, norm_ffn_post, w_ffn_a, w_ffn_b, ffn_conv_w, ffn_conv_b, w_ffn_d):
    depth = w_in.shape[0]
    assert depth == 1
    bsz, seq, d = x_prompt.shape
    dbsz, dseq, _ = x_sample.shape
    mem_tokens = mem_prompt.shape[1]
    dff = w_ffn_a.shape[2]
    layer = 0

    assert w_in.shape[2] == W_IN_BLOCKS * BRANCH_W
    weights = (w_in[layer].astype(BF16), b_in[layer],
               norm_mix_pre[layer], norm_mix_post[layer], hg_norm[layer],
               w_br_att[layer].astype(BF16), w_br_hg[layer].astype(BF16), w_br_mem[layer].astype(BF16),
               w_out[layer].astype(BF16), norm_ffn_pre[layer], norm_ffn_post[layer],
               w_ffn_a[layer].astype(BF16), w_ffn_b[layer].astype(BF16), ffn_conv_w[layer], ffn_conv_b[layer],
               w_ffn_d[layer].astype(BF16))

    mem_kv = norm_matmul(mem_prompt.reshape(bsz * mem_tokens, d), mem_norm[layer], w_mem_kv[layer].astype(BF16),
                         jnp.zeros((2 * BRANCH_W,), F32), F32, bsz * mem_tokens, 1024)
    as_rows = lambda c: c[layer].reshape(dbsz, c.shape[2] * KV_ROWS, HEAD_DIM)
    win_caches = [as_rows(c) for c in (cache_win1_kv, cache_win2_kv, cache_win3_kv)]
    yp, (qkv_p, kv_tails), _, hg_p, conv_p, shifted = _layer(
        x_prompt.reshape(bsz * seq, d), bsz, seq, weights, rel_bias, hg_lb_logits, None,
        jnp.zeros((bsz, N_HEADS, HEAD_DIM, HEAD_DIM), F32), jnp.zeros((bsz, 2, dff), F32),
        mem_kv.reshape(bsz, mem_tokens * KV_ROWS, HEAD_DIM),
        prompt=True, tm_proj=1024, tm_merge=512, tm_ffn=512, tf_ffn=2048, tq_mem=1024, chunk=64, n_sub=8,
        side_shift=win_caches, side_shift_drop=dseq * KV_ROWS)
    p_win = []
    for g, (win, dil) in enumerate(ATT_GROUPS):
        n = min(win, seq)
        assert n % dil == 0
        if g in kv_tails:
            p_win.append(kv_tails[g].reshape(1, bsz, n, 2, N_HEADS, HEAD_DIM))
            continue
        tail = qkv_p[g][:, :, (seq - n) // dil:, BRANCH_W:]
        tail = jnp.swapaxes(tail, 1, 2).astype(F32)
        p_win.append(tail.reshape(1, bsz, n, 2, N_HEADS, HEAD_DIM))

    ys, _, new_caches, hg_s, conv_s, _ = _layer(
        x_sample.reshape(dbsz * dseq, d), dbsz, dseq, weights, rel_bias, hg_lb_logits, win_caches,
        state_hgrn[layer], state_ffn_conv[layer], as_rows(cache_mem_kv),
        prompt=False, tm_proj=dbsz * dseq, tm_merge=dbsz * dseq, tm_ffn=dbsz * dseq, tf_ffn=2048, tq_mem=dseq,
        chunk=16, n_sub=1,
        shifted=shifted)
    s_win = [c.reshape(1, dbsz, c.shape[1] // KV_ROWS, 2, N_HEADS, HEAD_DIM) for c in new_caches]

    return (yp.reshape(bsz, seq, d), ys.reshape(dbsz, dseq, d),
            p_win[0], p_win[1], p_win[2],
            hg_p[None], conv_p[None], mem_kv.reshape(1, bsz, mem_tokens, 2, N_HEADS, HEAD_DIM),
            s_win[0], s_win[1], s_win[2],
            hg_s[None], conv_s[None])
```

```python
import functools
import math

import numpy as np
import jax
import jax.numpy as jnp
from jax import lax
from jax.experimental import pallas as pl
from jax.experimental.pallas import tpu as pltpu

F32 = jnp.float32
BF16 = jnp.bfloat16

NORM_EPS = 1e-6
NEG_INF = -1e30
HEAD_DIM = 128
N_HEADS = 4
BRANCH_W = N_HEADS * HEAD_DIM
N_LAGS = 128
BAND = 128
ATT_GROUPS = ((128, 1), (512, 4), (2048, 16))
REL_BUCKETS = 32
REL_MAX_DIST = 2048
LSE_LANES = 32
HG_BASE = 8
VMEM_LIMIT = 56 * 1024 * 1024
SAMPLE_STEP_BYTES = 8 * 1024 * 1024
DEINTERLEAVE_STRIDE = 4
HGRN_SAMPLE_ROWS = 4
ATT_STEP_BLOCKS = 8

NT_DIMS = (((1,), (1,)), ((), ()))
TN_DIMS = (((0,), (0,)), ((), ()))

W_IN_BLOCKS = 20
QKV_BLOCKS = lambda g: (g, 3 + g, 6 + g)
REC_BLOCKS = (9, 11, 13)
GATE_BLOCKS = (10, 12)
GATE_MERGE_BLOCKS = (7, 8, 9)
QKV_COLS = 3 * BRANCH_W
REC_COLS = 3 * BRANCH_W
REC_HQ, REC_HI, REC_MQ = 0, 1, 2
G_COLS = 2 * BRANCH_W
KV_ROWS = 2 * N_HEADS


def _cparams(*sem):
    return pltpu.CompilerParams(dimension_semantics=sem, vmem_limit_bytes=VMEM_LIMIT)


def _rms(x, gain):
    return x * lax.rsqrt(jnp.mean(x * x, axis=-1, keepdims=True) + NORM_EPS) * gain


def _sigmoid(x):
    return 0.5 * jnp.tanh(0.5 * x) + 0.5


def _norm_matmul_kernel(x_ref, g_ref, w_ref, b_ref, o_ref, hn_ref):
    @pl.when(pl.program_id(1) == 0)
    def _():
        hn_ref[...] = _rms(x_ref[...], g_ref[...]).astype(BF16)

    acc = jnp.dot(hn_ref[...], w_ref[...], preferred_element_type=F32)
    o_ref[...] = (acc + b_ref[...]).astype(o_ref.dtype)


def norm_matmul(x, gain, w, bias, out_dtype, tm, tn):
    m, k = x.shape
    n = w.shape[1]
    assert m % tm == 0 and n % tn == 0
    return pl.pallas_call(
        _norm_matmul_kernel,
        grid=(m // tm, n // tn),
        in_specs=[
            pl.BlockSpec((tm, k), lambda i, j: (i, 0)),
            pl.BlockSpec((1, k), lambda i, j: (0, 0)),
            pl.BlockSpec((k, tn), lambda i, j: (0, j)),
            pl.BlockSpec((1, tn), lambda i, j: (0, j)),
        ],
        out_specs=pl.BlockSpec((tm, tn), lambda i, j: (i, j)),
        out_shape=jax.ShapeDtypeStruct((m, n), out_dtype),
        scratch_shapes=[pltpu.VMEM((tm, k), BF16)],
        compiler_params=_cparams("parallel", "arbitrary"),
        name="norm_matmul",
    )(x, gain.reshape(1, k), w, bias.reshape(1, n))


def _proj_kernel(x_ref, g_ref, *refs, n_blk, dil, tail_tiles, tiles_per_b):
    w_refs, b_refs, o_ref = refs[:n_blk], refs[n_blk:2 * n_blk], refs[2 * n_blk]
    refs = refs[1:] if tail_tiles else refs
    tm = x_ref.shape[0]
    hn = _rms(x_ref[...], g_ref[...]).astype(BF16)
    for n in range(n_blk):
        acc = jnp.dot(hn, w_refs[n][...], preferred_element_type=F32) + b_refs[n][...]
        if tail_tiles and n >= 1:
            @pl.when(pl.program_id(0) % tiles_per_b >= tiles_per_b - tail_tiles)
            def _(acc=acc, n=n):
                tail_ref = refs[2 * n_blk]
                for c in range(N_HEADS):
                    tail_ref[pl.ds((n - 1) * N_HEADS + c, tm, stride=KV_ROWS), :] = acc[:, c * HEAD_DIM:(c + 1) * HEAD_DIM]
        if dil == 1:
            o_ref[:, n * BRANCH_W:(n + 1) * BRANCH_W] = acc.astype(o_ref.dtype)
            continue
        scr_ref = refs[2 * n_blk + 1]
        for c in range(N_HEADS):
            scr_ref[n * N_HEADS + c] = acc[:, c * HEAD_DIM:(c + 1) * HEAD_DIM]
        for c in range(N_HEADS):
            slab = n * N_HEADS + c
            lo = n * BRANCH_W + c * HEAD_DIM
            if dil <= DEINTERLEAVE_STRIDE:
                for r in range(dil):
                    o_ref[r, :, lo:lo + HEAD_DIM] = (
                        scr_ref[slab, pl.ds(r, tm // dil, stride=dil), :].astype(o_ref.dtype))
                continue
            s1, s2 = DEINTERLEAVE_STRIDE, dil // DEINTERLEAVE_STRIDE
            mid_ref = refs[2 * n_blk + 2]
            for a in range(s1):
                mid_ref[slab, a] = scr_ref[slab, pl.ds(a, tm // s1, stride=s1), :]
            for a in range(s1):
                for b in range(s2):
                    o_ref[a + s1 * b, :, lo:lo + HEAD_DIM] = (
                        mid_ref[slab, a, pl.ds(b, tm // dil, stride=s2), :].astype(o_ref.dtype))


def proj(x, gain, w, bias, blocks, batch, seq, dil, out_dtype, tm, tail_rows=0):
    m, k = x.shape
    n_blk = len(blocks)
    cols = n_blk * BRANCH_W
    assert m == batch * seq and seq % tm == 0 and tm % dil == 0 and tail_rows % tm == 0
    assert not tail_rows or n_blk == 3
    tiles_per_b = seq // tm
    tail_tiles = tail_rows // tm
    w_specs = [pl.BlockSpec((k, BRANCH_W), lambda i, c=c: (0, c)) for c in blocks]
    b_specs = [pl.BlockSpec((1, BRANCH_W), lambda i, c=c: (0, c)) for c in blocks]
    if dil == 1:
        out_spec = pl.BlockSpec((tm, cols), lambda i: (i, 0))
        out_shape = jax.ShapeDtypeStruct((m, cols), out_dtype)
        scratch = []
    else:
        out_spec = pl.BlockSpec((None, dil, tm // dil, cols), lambda i: (i // tiles_per_b, 0, i % tiles_per_b, 0))
        out_shape = jax.ShapeDtypeStruct((batch, dil, seq // dil, cols), out_dtype)
        scratch = [pltpu.VMEM((n_blk * N_HEADS, tm, HEAD_DIM), F32)]
        if dil > DEINTERLEAVE_STRIDE:
            assert dil % DEINTERLEAVE_STRIDE == 0 and dil // DEINTERLEAVE_STRIDE <= DEINTERLEAVE_STRIDE
            scratch.append(pltpu.VMEM((n_blk * N_HEADS, DEINTERLEAVE_STRIDE, tm // DEINTERLEAVE_STRIDE, HEAD_DIM), F32))
    out_specs, out_shapes = [out_spec], [out_shape]
    if tail_tiles:
        first_tail = tiles_per_b - tail_tiles
        out_specs.append(pl.BlockSpec((None, tm * KV_ROWS, HEAD_DIM),
                                      lambda i: (i // tiles_per_b, jnp.maximum(i % tiles_per_b - first_tail, 0), 0)))
        out_shapes.append(jax.ShapeDtypeStruct((batch, tail_rows * KV_ROWS, HEAD_DIM), F32))
    outs = pl.pallas_call(
        functools.partial(_proj_kernel, n_blk=n_blk, dil=dil, tail_tiles=tail_tiles, tiles_per_b=tiles_per_b),
        grid=(m // tm,),
        in_specs=[pl.BlockSpec((tm, k), lambda i: (i, 0)), pl.BlockSpec((1, k), lambda i: (0, 0))] + w_specs + b_specs,
        out_specs=out_specs,
        out_shape=out_shapes,
        scratch_shapes=scratch,
        compiler_params=_cparams("arbitrary"),
        name="proj",
    )(x, gain.reshape(1, k), *([w] * n_blk), *([bias] * n_blk))
    return outs if tail_tiles else outs[0]


def _proj_groups_kernel(x_ref, g_ref, *refs, sizes):
    n = sum(sizes)
    w_refs, b_refs, o_refs = refs[:n], refs[n:2 * n], refs[2 * n:]
    hn = _rms(x_ref[...], g_ref[...]).astype(BF16)
    k = 0
    for o_ref, size in zip(o_refs, sizes):
        for j in range(size):
            acc = jnp.dot(hn, w_refs[k][...], preferred_element_type=F32) + b_refs[k][...]
            o_ref[:, j * BRANCH_W:(j + 1) * BRANCH_W] = acc.astype(o_ref.dtype)
            k += 1


def proj_groups(x, gain, w, bias, groups, out_dtypes, tm):
    m, k = x.shape
    assert m % tm == 0
    blocks = [c for grp in groups for c in grp]
    once = lambda shape, c: pl.BlockSpec(shape, lambda i, c=c: (0, c), pipeline_mode=pl.Buffered(1))
    return pl.pallas_call(
        functools.partial(_proj_groups_kernel, sizes=tuple(len(grp) for grp in groups)),
        grid=(m // tm,),
        in_specs=[pl.BlockSpec((tm, k), lambda i: (i, 0)), pl.BlockSpec((1, k), lambda i: (0, 0))]
        + [once((k, BRANCH_W), c) for c in blocks] + [once((1, BRANCH_W), c) for c in blocks],
        out_specs=[pl.BlockSpec((tm, len(grp) * BRANCH_W), lambda i: (i, 0)) for grp in groups],
        out_shape=[jax.ShapeDtypeStruct((m, len(grp) * BRANCH_W), dt) for grp, dt in zip(groups, out_dtypes)],
        compiler_params=_cparams("parallel"),
        name="proj_groups",
    )(x, gain.reshape(1, k), *([w] * len(blocks)), *([bias] * len(blocks)))


def _dil_prompt_kernel(q_ref, kp_ref, kc_ref, vp_ref, vc_ref, bias_ref, o_ref, lse_ref, *scratch, dil, qb):
    scale = 1.0 / math.sqrt(HEAD_DIM)
    no_prev = (pl.program_id(1) == 0) & (lax.broadcasted_iota(jnp.int32, (1, 2 * BAND), 1) < BAND)

    def band_blocks(blocks):
        items = [(r, j, h, slice(j * BAND, (j + 1) * BAND), slice(h * HEAD_DIM, (h + 1) * HEAD_DIM))
                 for r, j in blocks for h in range(N_HEADS)]

        def keys(cur_ref, prev_ref, r, j, rows, hs):
            if j == 0:
                return jnp.concatenate([prev_ref[r, :, hs], cur_ref[r, rows, hs]], axis=0)
            return cur_ref[r, (j - 1) * BAND:(j + 1) * BAND, hs]

        logits = []
        for r, j, h, rows, hs in items:
            l = lax.dot_general(q_ref[r, rows, hs], keys(kc_ref, kp_ref, r, j, rows, hs), NT_DIMS,
                                preferred_element_type=F32) * scale + bias_ref[h]
            logits.append(jnp.where(no_prev, NEG_INF, l) if j == 0 else l)
        probs = []
        for l in logits:
            m = jnp.max(jnp.maximum(l[:, :BAND], l[:, BAND:]), axis=-1, keepdims=True)
            p = jnp.exp(l - m)
            s = jnp.sum(p[:, :BAND] + p[:, BAND:], axis=-1, keepdims=True)
            probs.append((p.astype(BF16), s, jnp.broadcast_to(m + jnp.log(s), (BAND, LSE_LANES))))
        outs = [jnp.dot(p, keys(vc_ref, vp_ref, r, j, rows, hs), preferred_element_type=F32) / s
                for (r, j, h, rows, hs), (p, s, _) in zip(items, probs)]
        return [(outs[N_HEADS * n:N_HEADS * (n + 1)],
                 jnp.concatenate([lse for _, _, lse in probs[N_HEADS * n:N_HEADS * (n + 1)]], axis=1))
                for n in range(len(blocks))]

    if dil == 1:
        for j, (outs, lse) in enumerate(band_blocks([(0, j) for j in range(qb)])):
            for h in range(N_HEADS):
                o_ref[j * BAND:(j + 1) * BAND, h * HEAD_DIM:(h + 1) * HEAD_DIM] = outs[h].astype(o_ref.dtype)
            lse_ref[j * BAND:(j + 1) * BAND, :] = lse
        return

    o_scr, lse_scr = scratch
    per_iter = min(dil, ATT_STEP_BLOCKS // qb)

    def body(it, carry):
        blocks = [(it * per_iter + rr, j) for rr in range(per_iter) for j in range(qb)]
        for (r, j), (outs, lse) in zip(blocks, band_blocks(blocks)):
            for h in range(N_HEADS):
                o_scr[h, pl.ds(r + j * dil * BAND, BAND, stride=dil), :] = outs[h]
            lse_scr[pl.ds(r + j * dil * BAND, BAND, stride=dil), :] = lse
        return carry

    lax.fori_loop(0, dil // per_iter, body, 0)
    for h in range(N_HEADS):
        o_ref[:, h * HEAD_DIM:(h + 1) * HEAD_DIM] = o_scr[h].astype(o_ref.dtype)
    lse_ref[...] = lse_scr[...]


def dilated_prompt(qkv, bias, g, dil, batch, seq, qb):
    assert seq % (dil * BAND * qb) == 0
    sub = seq // dil
    span = dil * BAND * qb
    cur = lambda part: pl.BlockSpec((None, dil, qb * BAND, BRANCH_W), lambda b, i: (b, 0, i, part))
    prev = lambda part: pl.BlockSpec((None, dil, BAND, BRANCH_W),
                                     lambda b, i: (b, 0, jnp.maximum(i * qb - 1, 0), part))
    bias_spec = pl.BlockSpec((N_HEADS, BAND, 2 * BAND), lambda b, i: (0, 0, 0))
    scratch = [] if dil == 1 else [pltpu.VMEM((N_HEADS, span, HEAD_DIM), F32),
                                   pltpu.VMEM((span, N_HEADS * LSE_LANES), F32)]
    o, lse = pl.pallas_call(
        functools.partial(_dil_prompt_kernel, dil=dil, qb=qb),
        grid=(batch, sub // (BAND * qb)),
        in_specs=[cur(0), prev(1), cur(1), prev(2), cur(2), bias_spec],
        out_specs=[pl.BlockSpec((None, span, BRANCH_W), lambda b, i: (b, i, 0)),
                   pl.BlockSpec((None, span, N_HEADS * LSE_LANES), lambda b, i: (b, i, 0))],
        out_shape=[jax.ShapeDtypeStruct((batch, seq, BRANCH_W), BF16),
                   jax.ShapeDtypeStruct((batch, seq, N_HEADS * LSE_LANES), F32)],
        scratch_shapes=scratch,
        compiler_params=_cparams("parallel", "arbitrary"),
        name=f"dilated_prompt_g{g}",
    )(qkv, qkv, qkv, qkv, qkv, bias)
    return o.reshape(batch * seq, BRANCH_W), lse.reshape(batch * seq, N_HEADS * LSE_LANES)


def _dil_sample_kernel(q_ref, kn_ref, vn_ref, cache_ref, bc_ref, bn_ref, shifted_ref, o_ref, lse_ref, newrows_ref,
                       *, t_new):
    del shifted_ref
    scale = 1.0 / math.sqrt(HEAD_DIM)
    bb, n_grp, grp_rows, _ = cache_ref.shape
    n_keys = n_grp * grp_rows // KV_ROWS
    rows_of = lambda b, first: cache_ref[b, :, pl.ds(first, grp_rows // KV_ROWS, stride=KV_ROWS), :].reshape(
        n_keys, HEAD_DIM).astype(BF16)
    heads = [(b, h, slice(h * HEAD_DIM, (h + 1) * HEAD_DIM)) for b in range(bb) for h in range(N_HEADS)]
    logits = []
    for b, h, hs in heads:
        q = q_ref[b, :, hs]
        lc = lax.dot_general(q.astype(BF16), rows_of(b, h), NT_DIMS, preferred_element_type=F32) * scale + bc_ref[h]
        ln = lax.dot_general(q, kn_ref[b, :, hs], NT_DIMS, preferred_element_type=F32) * scale + bn_ref[h]
        logits.append((lc, ln))
    probs = []
    for lc, ln in logits:
        m = jnp.maximum(jnp.max(lc, axis=-1, keepdims=True), jnp.max(ln, axis=-1, keepdims=True))
        pc = jnp.exp(lc - m)
        pn = jnp.exp(ln - m)
        s = jnp.sum(pc, axis=-1, keepdims=True) + jnp.sum(pn, axis=-1, keepdims=True)
        probs.append((pc.astype(BF16), pn, s, m + jnp.log(s)))
    for (b, h, hs), (pc, pn, s, lse) in zip(heads, probs):
        kn = kn_ref[b, :, hs]
        vn = vn_ref[b, :, hs]
        o = (jnp.dot(pc, rows_of(b, N_HEADS + h), preferred_element_type=F32)
             + jnp.dot(pn, vn, preferred_element_type=F32))
        o_ref[b, :, hs] = o / s
        lse_ref[b, :, h * LSE_LANES:(h + 1) * LSE_LANES] = jnp.broadcast_to(lse, (t_new, LSE_LANES))
        newrows_ref[b, pl.ds(h, t_new, stride=KV_ROWS), :] = kn
        newrows_ref[b, pl.ds(N_HEADS + h, t_new, stride=KV_ROWS), :] = vn


def _sample_period(win, dil, t_new):
    return (dil, t_new) if (dil > t_new and win % dil == 0) else (win, win)


def _sample_positions(win, dil, t_new):
    period, take = _sample_period(win, dil, t_new)
    return (np.arange(win // period)[:, None] * period + np.arange(take)[None, :]).reshape(-1)


def dilated_sample(qkv, cache, shifted, bias_cache, bias_new, g, dil, batch, t_new, bb):
    win = cache.shape[1] // KV_ROWS
    assert win % t_new == 0 and shifted.shape == cache.shape and batch % bb == 0
    period, take = _sample_period(win, dil, t_new)
    n_grp = win // period
    cache4 = cache.reshape(batch, n_grp, period * KV_ROWS, HEAD_DIM)
    assert bias_cache.shape == (N_HEADS, t_new, n_grp * take)
    blk = (bb, t_new, BRANCH_W)
    o, lse, newc = pl.pallas_call(
        functools.partial(_dil_sample_kernel, t_new=t_new),
        grid=(batch // bb,),
        in_specs=[pl.BlockSpec(blk, lambda b: (b, 0, 0)),
                  pl.BlockSpec(blk, lambda b: (b, 0, 1)),
                  pl.BlockSpec(blk, lambda b: (b, 0, 2)),
                  pl.BlockSpec((bb, n_grp, take * KV_ROWS, HEAD_DIM), lambda b: (b, 0, 0, 0)),
                  pl.BlockSpec((N_HEADS, t_new, n_grp * take), lambda b: (0, 0, 0)),
                  pl.BlockSpec((N_HEADS, t_new, t_new), lambda b: (0, 0, 0)),
                  pl.BlockSpec(memory_space=pl.ANY)],
        out_specs=[pl.BlockSpec(blk, lambda b: (b, 0, 0)),
                   pl.BlockSpec((bb, t_new, N_HEADS * LSE_LANES), lambda b: (b, 0, 0)),
                   pl.BlockSpec((bb, t_new * KV_ROWS, HEAD_DIM), lambda b: (b, win // t_new - 1, 0))],
        out_shape=[jax.ShapeDtypeStruct((batch, t_new, BRANCH_W), F32),
                   jax.ShapeDtypeStruct((batch, t_new, N_HEADS * LSE_LANES), F32),
                   jax.ShapeDtypeStruct(cache.shape, F32)],
        input_output_aliases={6: 2},
        compiler_params=_cparams("parallel"),
        name=f"dilated_sample_g{g}",
    )(qkv, qkv, qkv, cache4, bias_cache, bias_new, shifted)
    return o.reshape(batch * t_new, BRANCH_W), lse.reshape(batch * t_new, N_HEADS * LSE_LANES), newc


def _mem_attn_kernel(q_ref, kv_ref, o_ref, *, mem):
    scale = 1.0 / math.sqrt(HEAD_DIM)
    heads = [(b, h, slice(h * HEAD_DIM, (h + 1) * HEAD_DIM)) for b in range(q_ref.shape[0]) for h in range(N_HEADS)]
    rows_of = lambda b, first: kv_ref[b, pl.ds(first, mem, stride=KV_ROWS), :].astype(BF16)
    logits = [lax.dot_general(q_ref[b, :, hs].astype(BF16), rows_of(b, h), NT_DIMS, preferred_element_type=F32) * scale
              for b, h, hs in heads]
    probs = []
    for l in logits:
        half = l.shape[1] // 2
        m = jnp.max(jnp.maximum(l[:, :half], l[:, half:]), axis=-1, keepdims=True)
        p = jnp.exp(l - m)
        probs.append((p.astype(BF16), jnp.sum(p[:, :half] + p[:, half:], axis=-1, keepdims=True)))
    for (b, h, hs), (p, s) in zip(heads, probs):
        o = jnp.dot(p, rows_of(b, N_HEADS + h), preferred_element_type=F32)
        o_ref[b, :, hs] = (o / s).astype(o_ref.dtype)


def memory_attention(rec, mem_kv, batch, t, tq, bb, out_dtype):
    mem = mem_kv.shape[1] // KV_ROWS
    assert batch % bb == 0
    x = rec.reshape(batch, t, REC_COLS)
    o = pl.pallas_call(
        functools.partial(_mem_attn_kernel, mem=mem),
        grid=(batch // bb, t // tq),
        in_specs=[pl.BlockSpec((bb, tq, BRANCH_W), lambda b, i: (b, i, REC_MQ)),
                  pl.BlockSpec((bb, mem * KV_ROWS, HEAD_DIM), lambda b, i: (b, 0, 0))],
        out_specs=pl.BlockSpec((bb, tq, BRANCH_W), lambda b, i: (b, i, 0)),
        out_shape=jax.ShapeDtypeStruct((batch, t, BRANCH_W), out_dtype),
        compiler_params=_cparams("parallel", "parallel"),
        name="memory_attention",
    )(x, mem_kv)
    return o.reshape(batch * t, BRANCH_W)


def _hgrn_tables(chunk):
    c, b = chunk, HG_BASE
    t = np.arange(c)[:, None]
    u = np.arange(c)[None, :]
    masks = []
    size = 2 * b
    while size <= c:
        half = size // 2
        masks.append(((t // size) == (u // size)) & ((t % size) >= half) & ((u % size) < half))
        size *= 2
    masks.append(((t // b) == (u // b)) & (u <= t))
    erep = np.zeros((b * HEAD_DIM, c), np.float32)
    for s in range(b):
        erep[s * HEAD_DIM:(s + 1) * HEAD_DIM, np.arange(c) % b == s] = 1.0
    return (u <= t).astype(np.float32), erep, np.stack(masks).astype(np.float32)


def _shift_step(old_refs, new_refs, buf_refs, zero_ref, sem_in, sem_out, sem_zero, step, n_steps, drop):
    slot = step % 2
    other = 1 - slot

    def copies(g, chunk_idx, buf_slot):
        old, new, buf = old_refs[g], new_refs[g], buf_refs[g]
        rows_c = buf.shape[1]
        per_batch = (old.shape[1] - drop) // rows_c
        b = chunk_idx // per_batch
        lo = pl.multiple_of((chunk_idx % per_batch) * rows_c, 8)
        return (pltpu.make_async_copy(old.at[b, pl.ds(lo + drop, rows_c), :], buf.at[buf_slot], sem_in.at[g, buf_slot]),
                pltpu.make_async_copy(buf.at[buf_slot], new.at[b, pl.ds(lo, rows_c), :], sem_out.at[g, buf_slot]))

    def zero_copy(g):
        keep = old_refs[g].shape[1] - drop
        return pltpu.make_async_copy(zero_ref, new_refs[g].at[:, pl.ds(keep, drop), :], sem_zero.at[g])

    groups = range(len(old_refs))

    @pl.when(step == 0)
    def _():
        zero_ref[...] = jnp.zeros_like(zero_ref)
        for g in groups:
            zero_copy(g).start()
            copies(g, 0, 0)[0].start()

    for g in groups:
        copies(g, step, slot)[0].wait()

    @pl.when(step >= 1)
    def _():
        for g in groups:
            copies(g, step - 1, other)[1].wait()

    for g in groups:
        copies(g, step, slot)[1].start()

    @pl.when(step + 1 < n_steps)
    def _():
        for g in groups:
            copies(g, step + 1, other)[0].start()

    @pl.when(step == n_steps - 1)
    def _():
        for g in groups:
            copies(g, step, slot)[1].wait()
            zero_copy(g).wait()


def _shift_scratch(shift, drop, n_steps):
    if not shift:
        return []
    sb = shift[0].shape[0]
    assert n_steps % sb == 0 and all(s.shape[0] == sb and s.shape[2] == HEAD_DIM for s in shift)
    per_batch = n_steps // sb
    chunk_rows = [(s.shape[1] - drop) // per_batch for s in shift]
    assert all(r % 8 == 0 and r * per_batch == s.shape[1] - drop for r, s in zip(chunk_rows, shift))
    n = len(shift)
    return ([pltpu.VMEM((2, r, HEAD_DIM), F32) for r in chunk_rows]
            + [pltpu.VMEM((sb, drop, HEAD_DIM), F32), pltpu.SemaphoreType.DMA((n, 2)),
               pltpu.SemaphoreType.DMA((n, 2)), pltpu.SemaphoreType.DMA((n,))])


def _hgrn_kernel(q_ref, v_ref, f_ref, gate_ref, lbl_ref, gn_ref, s0_ref, tri_ref, erep_ref, mask_ref, *rest,
                 chunk, rows_in, n_sub, bb, n_levels, n_shift, shift_drop, n_steps):
    old_refs, rest = rest[:n_shift], rest[n_shift:]
    o_ref, s_out_ref = rest[:2]
    new_refs, rest = rest[2:2 + n_shift], rest[2 + n_shift:]
    st_ref = rest[0]
    c = chunk
    step = pl.program_id(1)
    if n_shift:
        buf_refs, (zero_ref, sem_in, sem_out, sem_zero) = rest[1:1 + n_shift], rest[1 + n_shift:]
        _shift_step(old_refs, new_refs, buf_refs, zero_ref, sem_in, sem_out, sem_zero,
                    pl.program_id(0) * pl.num_programs(1) + step, n_steps, shift_drop)

    @pl.when(step == 0)
    def _():
        for b in range(bb):
            for h in range(N_HEADS):
                st_ref[b * N_HEADS + h] = s0_ref[b, h].T

    lbl = lbl_ref[...]
    e = jnp.exp(lbl - jnp.max(lbl, axis=0, keepdims=True))
    lb = e[0:1] / jnp.sum(e, axis=0, keepdims=True)
    tri = tri_ref[...]

    def rows(ref, b, ci):
        x = ref[b, ci * rows_in:(ci + 1) * rows_in, :].astype(F32)
        if rows_in < c:
            x = jnp.concatenate([x, jnp.zeros((c - rows_in, x.shape[1]), F32)], axis=0)
        return x

    def at_row(x, size, idx):
        x3 = x.reshape(c // size, size, x.shape[1])
        return jnp.broadcast_to(x3[:, idx:idx + 1, :], x3.shape).reshape(x.shape)

    chunk_ids = [(b, ci) for b in range(bb) for ci in range(n_sub)]
    chunks = range(len(chunk_ids))
    heads = [(h, slice(h * HEAD_DIM, (h + 1) * HEAD_DIM)) for h in range(N_HEADS)]

    q, v, kk, cum = [], [], [], []
    for b, ci in chunk_ids:
        q.append(rows(q_ref, b, ci))
        v.append(rows(v_ref, b, ci).astype(BF16))
        f = lb + (1.0 - lb) * _sigmoid(rows(f_ref, b, ci))
        g = jnp.log(f)
        k_in = 1.0 - f
        if rows_in < c:
            live = lax.broadcasted_iota(jnp.int32, (c, 1), 0) < rows_in
            g = jnp.where(live, g, 0.0)
            k_in = jnp.where(live, k_in, 0.0)
        kk.append(k_in)
        g1 = g.astype(BF16)
        r1 = g - g1.astype(F32)
        g2 = r1.astype(BF16)
        g3 = (r1 - g2.astype(F32)).astype(BF16)
        cum.append(jnp.dot(tri, g1, preferred_element_type=F32) + jnp.dot(tri, g2, preferred_element_type=F32)
                   + jnp.dot(tri, g3, preferred_element_type=F32))

    q_inter, k_end, g_end, q_lvl, k_lvl, x_diag = [], [], [], [], [], []
    for ci in chunks:
        q_inter.append((q[ci] * jnp.exp(cum[ci])).astype(BF16))
        g_end.append(cum[ci][c - 1:c])
        k_end.append((kk[ci] * jnp.exp(g_end[ci] - cum[ci])).astype(BF16))
        ql, kl = [], []
        for l in range(n_levels):
            size = 2 * HG_BASE << l
            ref = at_row(cum[ci], size, size // 2 - 1)
            ql.append((q[ci] * jnp.exp(cum[ci] - ref)).astype(BF16))
            kl.append((kk[ci] * jnp.exp(ref - cum[ci])).astype(BF16))
        q_lvl.append(ql)
        k_lvl.append(kl)
        xd = []
        for s in range(HG_BASE):
            decay = jnp.exp(jnp.minimum(cum[ci] - at_row(cum[ci], HG_BASE, s), 0.0))
            xd.append((q[ci] * at_row(kk[ci], HG_BASE, s) * decay).astype(BF16))
        x_diag.append(xd)

    level_mask = [mask_ref[l] > 0.5 for l in range(n_levels)]
    scores, state_in = [], []
    for ci in chunks:
        sc, si = [], []
        for h, hs in heads:
            s_h = mask_ref[n_levels] * jnp.dot(jnp.concatenate([x[:, hs] for x in x_diag[ci]], axis=1),
                                               erep_ref[...], preferred_element_type=F32)
            for l in range(n_levels):
                s_h += jnp.where(level_mask[l], lax.dot_general(q_lvl[ci][l][:, hs], k_lvl[ci][l][:, hs], NT_DIMS,
                                                                preferred_element_type=F32), 0.0)
            sc.append(s_h.astype(BF16))
            si.append(lax.dot_general(v[ci][:, hs], k_end[ci][:, hs], TN_DIMS, preferred_element_type=F32))
        scores.append(sc)
        state_in.append(si)

    states = {}
    for b in range(bb):
        for h, hs in heads:
            st = st_ref[b * N_HEADS + h]
            for ci in range(n_sub):
                idx = b * n_sub + ci
                states[idx, h] = st.astype(BF16)
                st = st * jnp.exp(g_end[idx][:, hs]) + state_in[idx][h]
            st_ref[b * N_HEADS + h] = st
    for idx, (b, ci) in enumerate(chunk_ids):
        gate = _sigmoid(rows(gate_ref, b, ci))
        for h, hs in heads:
            o = (lax.dot_general(q_inter[idx][:, hs], states[idx, h], NT_DIMS, preferred_element_type=F32)
                 + jnp.dot(scores[idx][h], v[idx][:, hs], preferred_element_type=F32))
            y = _rms(o, gn_ref[...]) * gate[:, hs]
            o_ref[b, ci * rows_in:(ci + 1) * rows_in, hs] = y[0:rows_in].astype(o_ref.dtype)

    @pl.when(step == pl.num_programs(1) - 1)
    def _():
        for b in range(bb):
            for h in range(N_HEADS):
                s_out_ref[b, h] = st_ref[b * N_HEADS + h].T


def hgrn2(rec, gates, lb_logits, hg_norm, state0, batch, t, chunk, n_sub, bb, out_dtype, shift=(), shift_drop=0):
    rows_in = min(chunk, t)
    step_rows = rows_in * n_sub
    assert t % step_rows == 0 and chunk % HG_BASE == 0 and (n_sub == 1 or rows_in == chunk) and batch % bb == 0
    n_steps = (batch // bb) * (t // step_rows)
    n_shift = len(shift)
    any_spec = pl.BlockSpec(memory_space=pl.ANY)
    shift_scratch = _shift_scratch(shift, shift_drop, n_steps)
    tri, erep, masks = _hgrn_tables(chunk)
    n_levels = masks.shape[0] - 1
    a = rec.reshape(batch, t, REC_COLS)
    gt = gates.reshape(batch, t, G_COLS)
    blk = (bb, step_rows, BRANCH_W)
    full = lambda arr: pl.BlockSpec(arr.shape, lambda b, i: (0,) * arr.ndim)
    st_spec = pl.BlockSpec((bb, N_HEADS, HEAD_DIM, HEAD_DIM), lambda b, i: (b, 0, 0, 0))
    tri = jnp.asarray(tri, BF16)
    erep = jnp.asarray(erep, BF16)
    masks = jnp.asarray(masks, F32)
    lbl = lb_logits.astype(F32)
    gn = hg_norm.reshape(1, HEAD_DIM).astype(F32)
    o, s_fin, *shifted = pl.pallas_call(
        functools.partial(_hgrn_kernel, chunk=chunk, rows_in=rows_in, n_sub=n_sub, bb=bb, n_levels=n_levels,
                          n_shift=n_shift, shift_drop=shift_drop, n_steps=n_steps),
        grid=(batch // bb, t // step_rows),
        in_specs=[pl.BlockSpec(blk, lambda b, i: (b, i, REC_HQ)),
                  pl.BlockSpec(blk, lambda b, i: (b, i, REC_HI)),
                  pl.BlockSpec(blk, lambda b, i: (b, i, 0)),
                  pl.BlockSpec(blk, lambda b, i: (b, i, 1)),
                  full(lbl), full(gn), st_spec, full(tri), full(erep), full(masks)] + [any_spec] * n_shift,
        out_specs=[pl.BlockSpec(blk, lambda b, i: (b, i, 0)), st_spec] + [any_spec] * n_shift,
        out_shape=[jax.ShapeDtypeStruct((batch, t, BRANCH_W), out_dtype),
                   jax.ShapeDtypeStruct(state0.shape, F32)] + [jax.ShapeDtypeStruct(s.shape, F32) for s in shift],
        scratch_shapes=[pltpu.VMEM((bb * N_HEADS, HEAD_DIM, HEAD_DIM), F32)] + shift_scratch,
        compiler_params=_cparams("arbitrary", "arbitrary"),
        name="hgrn2",
    )(a, a, gt, gt, lbl, gn, state0, tri, erep, masks, *shift)
    return o.reshape(batch * t, BRANCH_W), s_fin, shifted


def _merge_kernel(x_ref, o1_ref, o2_ref, o3_ref, l1_ref, l2_ref, l3_ref, hg_ref, mem_ref, gpre_ref,
                  wga_ref, wgh_ref, wgm_ref, bga_ref, bgh_ref, bgm_ref,
                  wa_ref, wh_ref, wm_ref, wo_ref, gain_ref, *rest, n_shift, shift_drop, n_steps):
    old_refs, out_ref, new_refs, scratch = rest[:n_shift], rest[n_shift], rest[n_shift + 1:2 * n_shift + 1], \
        rest[2 * n_shift + 1:]
    if n_shift:
        _shift_step(old_refs, new_refs, scratch[:n_shift], *scratch[n_shift:], pl.program_id(0), n_steps, shift_drop)
    tm = x_ref.shape[0]
    x = x_ref[...]
    hn = _rms(x, gpre_ref[...]).astype(BF16)
    gate_logits = [jnp.dot(hn, w_ref[...], preferred_element_type=F32) + b_ref[...]
                   for w_ref, b_ref in ((wga_ref, bga_ref), (wgh_ref, bgh_ref), (wgm_ref, bgm_ref))]
    from_hg = jnp.dot(hg_ref[...].astype(BF16), wh_ref[...], preferred_element_type=F32)
    from_mem = jnp.dot(mem_ref[...].astype(BF16), wm_ref[...], preferred_element_type=F32)
    l1, l2, l3 = l1_ref[...], l2_ref[...], l3_ref[...]
    m = jnp.maximum(jnp.maximum(l1, l2), l3)
    e1, e2, e3 = jnp.exp(l1 - m), jnp.exp(l2 - m), jnp.exp(l3 - m)
    den = e1 + e2 + e3
    w1, w2, w3 = e1 / den, e2 / den, e3 / den
    att = []
    for h in range(N_HEADS):
        hs = slice(h * HEAD_DIM, (h + 1) * HEAD_DIM)
        col = slice(h * LSE_LANES, h * LSE_LANES + 1)
        bc = lambda w: jnp.broadcast_to(w[:, col], (tm, HEAD_DIM))
        att.append((bc(w1) * o1_ref[:, hs].astype(F32) + bc(w2) * o2_ref[:, hs].astype(F32)
                    + bc(w3) * o3_ref[:, hs].astype(F32)).astype(BF16))
    att = jnp.concatenate(att, axis=1)
    ga, gh, gm = (_sigmoid(z) for z in gate_logits)
    merged = ga * jnp.dot(att, wa_ref[...], preferred_element_type=F32) + gh * from_hg + gm * from_mem
    y = jnp.dot(merged.astype(BF16), wo_ref[...], preferred_element_type=F32)
    out_ref[...] = x + _rms(y, gain_ref[...])


def merge(x, o_groups, lse_groups, hg_o, mem_o, gain_pre, w_in, b_in, w_att, w_hg, w_mem, w_out, gain, tm,
          shift=(), shift_drop=0):
    m, d = x.shape
    assert m % tm == 0 and GATE_MERGE_BLOCKS[0] * d == 14 * BRANCH_W
    n_steps = m // tm
    n_shift = len(shift)
    any_spec = pl.BlockSpec(memory_space=pl.ANY)
    row = lambda w: pl.BlockSpec((tm, w), lambda i: (i, 0))
    resident = lambda shape, j=0: pl.BlockSpec(shape, lambda i: (0, j), pipeline_mode=pl.Buffered(1))
    x1, *shifted = pl.pallas_call(
        functools.partial(_merge_kernel, n_shift=n_shift, shift_drop=shift_drop, n_steps=n_steps),
        grid=(n_steps,),
        in_specs=[row(d), row(BRANCH_W), row(BRANCH_W), row(BRANCH_W),
                  row(N_HEADS * LSE_LANES), row(N_HEADS * LSE_LANES), row(N_HEADS * LSE_LANES),
                  row(BRANCH_W), row(BRANCH_W), resident((1, d))]
        + [resident((d, d), j) for j in GATE_MERGE_BLOCKS] + [resident((1, d), j) for j in GATE_MERGE_BLOCKS]
        + [resident(w_att.shape), resident(w_hg.shape), resident(w_mem.shape), resident(w_out.shape),
           resident((1, d))] + [any_spec] * n_shift,
        out_specs=[row(d)] + [any_spec] * n_shift,
        out_shape=[jax.ShapeDtypeStruct((m, d), F32)] + [jax.ShapeDtypeStruct(s.shape, F32) for s in shift],
        scratch_shapes=_shift_scratch(shift, shift_drop, n_steps),
        compiler_params=_cparams("arbitrary"),
        name="merge",
    )(x, *o_groups, *lse_groups, hg_o, mem_o, gain_pre.reshape(1, d), w_in, w_in, w_in, b_in, b_in, b_in,
      w_att, w_hg, w_mem, w_out, gain.reshape(1, d), *shift)
    return x1, shifted


def _ffn_kernel(x_ref, gpre_ref, wa_ref, wb_ref, cw_ref, cb_ref, wd_ref, gpost_ref, cbuf_ref,
                out_ref, tail_ref, carry_ref, *, t_seq, tiles_per_seq, tf):
    tm = x_ref.shape[0]
    dff = wa_ref.shape[1]
    x = x_ref[...]
    hn = _rms(x, gpre_ref[...]).astype(BF16)

    if tiles_per_seq >= 1:
        @pl.when((pl.program_id(0) % tiles_per_seq) == 0)
        def _():
            carry_ref[6:8, :] = cbuf_ref[0]

        t_idx = lax.broadcasted_iota(jnp.int32, (tm, 1), 0)
    else:
        n_seq = tm // t_seq
        t_idx = lax.broadcasted_iota(jnp.int32, (tm, 1), 0) % t_seq

    n_chunks = dff // tf

    def up_proj(j):
        cols = slice(j * tf, (j + 1) * tf)
        return (jnp.dot(hn, wa_ref[:, cols], preferred_element_type=F32),
                jnp.dot(hn, wb_ref[:, cols], preferred_element_type=F32))

    acc = jnp.zeros(x.shape, F32)
    ahead = up_proj(0)
    for j in range(n_chunks):
        cols = slice(j * tf, (j + 1) * tf)
        a, up = ahead
        if j + 1 < n_chunks:
            ahead = up_proj(j + 1)
        if tiles_per_seq >= 1:
            prev1 = carry_ref[7:8, cols]
            prev2 = carry_ref[6:7, cols]
            carry_ref[:, cols] = a[tm - 8:tm]
            tail_ref[0, :, cols] = a[tm - 8:tm]
        else:
            prev1 = jnp.broadcast_to(cbuf_ref[:, 1:2, cols], (n_seq, t_seq, tf)).reshape(tm, tf)
            prev2 = jnp.broadcast_to(cbuf_ref[:, 0:1, cols], (n_seq, t_seq, tf)).reshape(tm, tf)
            tail_ref[:, :, cols] = a.reshape(n_seq, t_seq, tf)
        a1 = jnp.where(t_idx >= 1, pltpu.roll(a, 1, 0), prev1)
        a2 = jnp.where(t_idx >= 2, pltpu.roll(a, 2, 0), jnp.where(t_idx == 1, prev1, prev2))
        conv = cb_ref[:, cols] + a2 * cw_ref[0:1, cols] + a1 * cw_ref[1:2, cols] + a * cw_ref[2:3, cols]
        act = conv * _sigmoid(conv) * up
        acc += jnp.dot(act.astype(BF16), wd_ref[cols, :], preferred_element_type=F32)

    out_ref[...] = x + _rms(acc, gpost_ref[...])


def conv_ffn(x, conv_buf, g_pre, w_a, w_b, conv_w, conv_b, w_d, g_post, t_seq, tm, tf):
    m, d = x.shape
    dff = w_a.shape[1]
    n_seq = m // t_seq
    assert m % tm == 0 and dff % tf == 0 and t_seq >= 8
    if tm <= t_seq:
        assert t_seq % tm == 0
        tiles_per_seq = t_seq // tm
        seq_blk = 1
        seq_idx = lambda i: (i // tiles_per_seq, 0, 0)
    else:
        assert tm % t_seq == 0 and t_seq == 8
        tiles_per_seq = 0
        seq_blk = tm // t_seq
        seq_idx = lambda i: (i, 0, 0)
    n_tail = (m // tm) * seq_blk
    resident = lambda shape: pl.BlockSpec(shape, lambda i: (0, 0), pipeline_mode=pl.Buffered(1))
    y, tail = pl.pallas_call(
        functools.partial(_ffn_kernel, t_seq=t_seq, tiles_per_seq=tiles_per_seq, tf=tf),
        grid=(m // tm,),
        in_specs=[pl.BlockSpec((tm, d), lambda i: (i, 0)), resident((1, d)),
                  resident((d, dff)), resident((d, dff)), resident((3, dff)), resident((1, dff)),
                  resident((dff, d)), resident((1, d)),
                  pl.BlockSpec((seq_blk, 2, dff), seq_idx)],
        out_specs=[pl.BlockSpec((tm, d), lambda i: (i, 0)),
                   pl.BlockSpec((seq_blk, 8, dff), lambda i: (i, 0, 0))],
        out_shape=[jax.ShapeDtypeStruct((m, d), F32), jax.ShapeDtypeStruct((n_tail, 8, dff), F32)],
        scratch_shapes=[pltpu.VMEM((8, dff), F32)],
        compiler_params=_cparams("arbitrary"),
        name="conv_ffn",
    )(x, g_pre.reshape(1, d), w_a, w_b, conv_w, conv_b.reshape(1, dff), w_d, g_post.reshape(1, d), conv_buf)
    if tiles_per_seq >= 1:
        tail = tail.reshape(n_seq, tiles_per_seq, 8, dff)[:, -1]
    return y, tail


def _rel_buckets(dil):
    max_exact = REL_BUCKETS // 2
    dist = np.arange(N_LAGS + 1, dtype=np.int32) * dil
    d = np.maximum(dist, 1).astype(np.float32)
    large = max_exact + (np.log(d / np.float32(max_exact)) / np.float32(math.log(REL_MAX_DIST / max_exact))
                         * np.float32(REL_BUCKETS - max_exact)).astype(np.int32)
    large = np.minimum(large, REL_BUCKETS - 1)
    return np.where(dist < max_exact, dist, large)


def _bias_table(rel_bias, g, dil, dist):
    dist = np.asarray(dist)
    ok = (dist >= 0) & (dist % dil == 0) & (dist <= N_LAGS * dil)
    bucket = np.where(ok, _rel_buckets(dil)[np.clip(dist // dil, 0, N_LAGS)], -1)
    onehot = bucket[..., None] == np.arange(REL_BUCKETS)
    heads = rel_bias.astype(F32)[:, g * N_HEADS:(g + 1) * N_HEADS].T
    picked = jnp.sum(jnp.where(onehot[None], heads.reshape((N_HEADS,) + (1,) * dist.ndim + (REL_BUCKETS,)), 0.0), -1)
    return jnp.where(ok[None], picked, NEG_INF)


def _prompt_bias(rel_bias, g, dil):
    return _bias_table(rel_bias, g, dil, (BAND + np.arange(BAND)[:, None] - np.arange(2 * BAND)[None, :]) * dil)


def _sample_bias(rel_bias, g, dil, cached_pos, n_past, t_new):
    t = np.arange(t_new)[:, None]
    return (_bias_table(rel_bias, g, dil, n_past + t - np.asarray(cached_pos)[None, :]),
            _bias_table(rel_bias, g, dil, t - np.arange(t_new)[None, :]))


def _layer(x, batch, t, weights, rel_bias, lb_logits, win_caches, hg_state0, conv_buf0, mem_kv,
           *, prompt, tm_proj, tm_merge, tm_ffn, tf_ffn, tq_mem, chunk, n_sub, shifted=None, side_shift=(),
           side_shift_drop=0):
    (w_in16, b_in, gain_pre, gain_post, hg_norm, w_att, w_hg, w_mem, w_out,
     gain_fpre, gain_fpost, w_fa, w_fb, conv_w, conv_b, w_fd) = weights
    act_dtype = BF16 if prompt else F32
    b_in = b_in.reshape(1, -1)
    kv_tails = {}
    if prompt:
        plain = [g for g, (_, dil) in enumerate(ATT_GROUPS) if dil == 1]
        *qkv_plain, rec, gates = proj_groups(x, gain_pre, w_in16, b_in, [QKV_BLOCKS(g) for g in plain]
                                             + [REC_BLOCKS, GATE_BLOCKS], [act_dtype] * (len(plain) + 1) + [F32], tm_proj)
        qkv = []
        for g, (win, dil) in enumerate(ATT_GROUPS):
            if g in plain:
                qkv.append(qkv_plain[plain.index(g)].reshape(batch, 1, t, QKV_COLS))
                continue
            tail_rows = win if (win % tm_proj == 0 and win <= t) else 0
            out = proj(x, gain_pre, w_in16, b_in, QKV_BLOCKS(g), batch, t, dil, act_dtype, tm_proj, tail_rows)
            if tail_rows:
                out, kv_tails[g] = out
            qkv.append(out)
    else:
        groups = [QKV_BLOCKS(g) for g in range(len(ATT_GROUPS))] + [REC_BLOCKS, GATE_BLOCKS]
        *qkv, rec, gates = proj_groups(x, gain_pre, w_in16, b_in, groups, [act_dtype] * 4 + [F32], tm_proj)
        qkv = [z.reshape(batch, t, QKV_COLS) for z in qkv]

    o_groups, lse_groups, new_caches = [], [], []
    for g, (win, dil) in enumerate(ATT_GROUPS):
        if prompt:
            qb = max(1, min(ATT_STEP_BLOCKS // dil, t // (dil * BAND)))
            o, lse = dilated_prompt(qkv[g], _prompt_bias(rel_bias, g, dil), g, dil, batch, t, qb)
        else:
            cache = win_caches[g]
            n_past = cache.shape[1] // KV_ROWS
            assert n_past >= win
            bcache, bnew = _sample_bias(rel_bias, g, dil, _sample_positions(n_past, dil, t), n_past, t)
            period, take = _sample_period(n_past, dil, t)
            loaded_bytes = cache.shape[1] * HEAD_DIM * 4 * take // period
            bb = math.gcd(batch, max(1, SAMPLE_STEP_BYTES // loaded_bytes))
            o, lse, newc = dilated_sample(qkv[g], cache, shifted[g], bcache, bnew, g, dil, batch, t, bb)
            new_caches.append(newc)
        o_groups.append(o)
        lse_groups.append(lse)

    largest = max(range(len(side_shift)), key=lambda i: side_shift[i].size) if side_shift else None
    with_hgrn = [s for i, s in enumerate(side_shift) if i == largest]
    with_merge = [s for i, s in enumerate(side_shift) if i != largest]
    hg_bb = 1 if prompt else math.gcd(batch, HGRN_SAMPLE_ROWS)
    hg_o, hg_state, from_hgrn = hgrn2(rec, gates, lb_logits, hg_norm, hg_state0, batch, t, chunk, n_sub, hg_bb,
                                      act_dtype, with_hgrn, side_shift_drop)
    mem_bb = 1 if prompt else math.gcd(batch, max(1, SAMPLE_STEP_BYTES // (mem_kv.shape[1] * HEAD_DIM * 4)))
    mem_o = memory_attention(rec, mem_kv, batch, t, tq_mem, mem_bb, act_dtype)
    x1, from_merge = merge(x, o_groups, lse_groups, hg_o, mem_o, gain_pre, w_in16, b_in, w_att, w_hg, w_mem, w_out,
                           gain_post, tm_merge, with_merge, side_shift_drop)
    from_hgrn, from_merge = list(from_hgrn), list(from_merge)
    shifted_out = [from_hgrn.pop(0) if i == largest else from_merge.pop(0) for i in range(len(side_shift))]
    y, tail = conv_ffn(x1, conv_buf0, gain_fpre, w_fa, w_fb, conv_w, conv_b, w_fd, gain_fpost, t, tm_ffn, tf_ffn)
    return y, (qkv, kv_tails), new_caches, hg_state, tail[:, 6:8, :], shifted_out


def kernel(x_prompt, x_sample, mem_prompt, cache_win1_kv, cache_win2_kv, cache_win3_kv, cache_mem_kv, state_hgrn, state_ffn_conv, rel_bias, hg_lb_logits, norm_mix_pre, norm_mix_post, w_in, b_in, hg_norm, mem_norm, w_mem_kv, w_br_att, w_br_hg, w_br_mem, w_out, norm_ffn_pre, norm_ffn_post, w_ffn_a, w_ffn_b, ffn_conv_w, ffn_conv_b, w_ffn_d):
    depth = w_in.shape[0]
    assert depth == 1
    bsz, seq, d = x_prompt.shape
    dbsz, dseq, _ = x_sample.shape
    mem_tokens = mem_prompt.shape[1]
    dff = w_ffn_a.shape[2]
    layer = 0

    assert w_in.shape[2] == W_IN_BLOCKS * BRANCH_W
    weights = (w_in[layer].astype(BF16), b_in[layer],
               norm_mix_pre[layer], norm_mix_post[layer], hg_norm[layer],
               w_br_att[layer].astype(BF16), w_br_hg[layer].astype(BF16), w_br_mem[layer].astype(BF16),
               w_out[layer].astype(BF16), norm_ffn_pre[layer], norm_ffn_post[layer],
               w_ffn_a[layer].astype(BF16), w_ffn_b[layer].astype(BF16), ffn_conv_w[layer], ffn_conv_b[layer],
               w_ffn_d[layer].astype(BF16))

    mem_kv = norm_matmul(mem_prompt.reshape(bsz * mem_tokens, d), mem_norm[layer], w_mem_kv[layer].astype(BF16),
                         jnp.zeros((2 * BRANCH_W,), F32), F32, bsz * mem_tokens, 1024)
    as_rows = lambda c: c[layer].reshape(dbsz, c.shape[2] * KV_ROWS, HEAD_DIM)
    win_caches = [as_rows(c) for c in (cache_win1_kv, cache_win2_kv, cache_win3_kv)]
    yp, (qkv_p, kv_tails), _, hg_p, conv_p, shifted = _layer(
        x_prompt.reshape(bsz * seq, d), bsz, seq, weights, rel_bias, hg_lb_logits, None,
        jnp.zeros((bsz, N_HEADS, HEAD_DIM, HEAD_DIM), F32), jnp.zeros((bsz, 2, dff), F32),
        mem_kv.reshape(bsz, mem_tokens * KV_ROWS, HEAD_DIM),
        prompt=True, tm_proj=1024, tm_merge=512, tm_ffn=512, tf_ffn=2048, tq_mem=1024, chunk=64, n_sub=8,
        side_shift=win_caches, side_shift_drop=dseq * KV_ROWS)
    p_win = []
    for g, (win, dil) in enumerate(ATT_GROUPS):
        n = min(win, seq)
        assert n % dil == 0
        if g in kv_tails:
            p_win.append(kv_tails[g].reshape(1, bsz, n, 2, N_HEADS, HEAD_DIM))
            continue
        tail = qkv_p[g][:, :, (seq - n) // dil:, BRANCH_W:]
        tail = jnp.swapaxes(tail, 1, 2).astype(F32)
        p_win.append(tail.reshape(1, bsz, n, 2, N_HEADS, HEAD_DIM))

    ys, _, new_caches, hg_s, conv_s, _ = _layer(
        x_sample.reshape(dbsz * dseq, d), dbsz, dseq, weights, rel_bias, hg_lb_logits, win_caches,
        state_hgrn[layer], state_ffn_conv[layer], as_rows(cache_mem_kv),
        prompt=False, tm_proj=dbsz * dseq, tm_merge=dbsz * dseq, tm_ffn=dbsz * dseq, tf_ffn=2048, tq_mem=dseq,
        chunk=16, n_sub=1,
        shifted=shifted)
    s_win = [c.reshape(1, dbsz, c.shape[1] // KV_ROWS, 2, N_HEADS, HEAD_DIM) for c in new_caches]

    return (yp.reshape(bsz, seq, d), ys.reshape(dbsz, dseq, d),
            p_win[0], p_win[1], p_win[2],
            hg_p[None], conv_p[None], mem_kv.reshape(1, bsz, mem_tokens, 2, N_HEADS, HEAD_DIM),
            s_win[0], s_win[1], s_win[2],
            hg_s[None], conv_s[None])
```

```python
import functools
import math

import numpy as np
import jax
import jax.numpy as jnp
from jax import lax
from jax.experimental import pallas as pl
from jax.experimental.pallas import tpu as pltpu

F32 = jnp.float32
BF16 = jnp.bfloat16

NORM_EPS = 1e-6
NEG_INF = -1e30
HEAD_DIM = 128
N_HEADS = 4
BRANCH_W = N_HEADS * HEAD_DIM
N_LAGS = 128
BAND = 128
ATT_GROUPS = ((128, 1), (512, 4), (2048, 16))
REL_BUCKETS = 32
REL_MAX_DIST = 2048
LSE_LANES = 32
HG_BASE = 8
VMEM_LIMIT = 56 * 1024 * 1024
SAMPLE_STEP_BYTES = 8 * 1024 * 1024
DEINTERLEAVE_STRIDE = 4
HGRN_SAMPLE_ROWS = 4
SIDE_DMA_PRIORITY = 1
ATT_STEP_BLOCKS = 8

NT_DIMS = (((1,), (1,)), ((), ()))
TN_DIMS = (((0,), (0,)), ((), ()))

W_IN_BLOCKS = 20
QKV_BLOCKS = lambda g: (g, 3 + g, 6 + g)
REC_BLOCKS = (9, 11, 13)
GATE_BLOCKS = (10, 12)
GATE_MERGE_BLOCKS = (7, 8, 9)
QKV_COLS = 3 * BRANCH_W
REC_COLS = 3 * BRANCH_W
REC_HQ, REC_HI, REC_MQ = 0, 1, 2
G_COLS = 2 * BRANCH_W
KV_ROWS = 2 * N_HEADS


def _cparams(*sem):
    return pltpu.CompilerParams(dimension_semantics=sem, vmem_limit_bytes=VMEM_LIMIT)


def _rms(x, gain):
    return x * lax.rsqrt(jnp.mean(x * x, axis=-1, keepdims=True) + NORM_EPS) * gain


def _sigmoid(x):
    return 0.5 * jnp.tanh(0.5 * x) + 0.5


def _norm_matmul_kernel(x_ref, g_ref, w_ref, b_ref, o_ref, hn_ref):
    @pl.when(pl.program_id(1) == 0)
    def _():
        hn_ref[...] = _rms(x_ref[...], g_ref[...]).astype(BF16)

    acc = jnp.dot(hn_ref[...], w_ref[...], preferred_element_type=F32)
    o_ref[...] = (acc + b_ref[...]).astype(o_ref.dtype)


def norm_matmul(x, gain, w, bias, out_dtype, tm, tn):
    m, k = x.shape
    n = w.shape[1]
    assert m % tm == 0 and n % tn == 0
    return pl.pallas_call(
        _norm_matmul_kernel,
        grid=(m // tm, n // tn),
        in_specs=[
            pl.BlockSpec((tm, k), lambda i, j: (i, 0)),
            pl.BlockSpec((1, k), lambda i, j: (0, 0)),
            pl.BlockSpec((k, tn), lambda i, j: (0, j)),
            pl.BlockSpec((1, tn), lambda i, j: (0, j)),
        ],
        out_specs=pl.BlockSpec((tm, tn), lambda i, j: (i, j)),
        out_shape=jax.ShapeDtypeStruct((m, n), out_dtype),
        scratch_shapes=[pltpu.VMEM((tm, k), BF16)],
        compiler_params=_cparams("parallel", "arbitrary"),
        name="norm_matmul",
    )(x, gain.reshape(1, k), w, bias.reshape(1, n))


def _proj_kernel(x_ref, g_ref, *refs, n_blk, dil, tail_tiles, tiles_per_b):
    w_refs, b_refs, o_ref = refs[:n_blk], refs[n_blk:2 * n_blk], refs[2 * n_blk]
    refs = refs[1:] if tail_tiles else refs
    tm = x_ref.shape[0]
    hn = _rms(x_ref[...], g_ref[...]).astype(BF16)
    for n in range(n_blk):
        acc = jnp.dot(hn, w_refs[n][...], preferred_element_type=F32) + b_refs[n][...]
        if tail_tiles and n >= 1:
            @pl.when(pl.program_id(0) % tiles_per_b >= tiles_per_b - tail_tiles)
            def _(acc=acc, n=n):
                tail_ref = refs[2 * n_blk]
                for c in range(N_HEADS):
                    tail_ref[pl.ds((n - 1) * N_HEADS + c, tm, stride=KV_ROWS), :] = acc[:, c * HEAD_DIM:(c + 1) * HEAD_DIM]
        if dil == 1:
            o_ref[:, n * BRANCH_W:(n + 1) * BRANCH_W] = acc.astype(o_ref.dtype)
            continue
        scr_ref = refs[2 * n_blk + 1]
        for c in range(N_HEADS):
            scr_ref[n * N_HEADS + c] = acc[:, c * HEAD_DIM:(c + 1) * HEAD_DIM]
        for c in range(N_HEADS):
            slab = n * N_HEADS + c
            lo = n * BRANCH_W + c * HEAD_DIM
            if dil <= DEINTERLEAVE_STRIDE:
                for r in range(dil):
                    o_ref[r, :, lo:lo + HEAD_DIM] = (
                        scr_ref[slab, pl.ds(r, tm // dil, stride=dil), :].astype(o_ref.dtype))
                continue
            s1, s2 = DEINTERLEAVE_STRIDE, dil // DEINTERLEAVE_STRIDE
            mid_ref = refs[2 * n_blk + 2]
            for a in range(s1):
                mid_ref[slab, a] = scr_ref[slab, pl.ds(a, tm // s1, stride=s1), :]
            for a in range(s1):
                for b in range(s2):
                    o_ref[a + s1 * b, :, lo:lo + HEAD_DIM] = (
                        mid_ref[slab, a, pl.ds(b, tm // dil, stride=s2), :].astype(o_ref.dtype))


def proj(x, gain, w, bias, blocks, batch, seq, dil, out_dtype, tm, tail_rows=0):
    m, k = x.shape
    n_blk = len(blocks)
    cols = n_blk * BRANCH_W
    assert m == batch * seq and seq % tm == 0 and tm % dil == 0 and tail_rows % tm == 0
    assert not tail_rows or n_blk == 3
    tiles_per_b = seq // tm
    tail_tiles = tail_rows // tm
    w_specs = [pl.BlockSpec((k, BRANCH_W), lambda i, c=c: (0, c)) for c in blocks]
    b_specs = [pl.BlockSpec((1, BRANCH_W), lambda i, c=c: (0, c)) for c in blocks]
    if dil == 1:
        out_spec = pl.BlockSpec((tm, cols), lambda i: (i, 0))
        out_shape = jax.ShapeDtypeStruct((m, cols), out_dtype)
        scratch = []
    else:
        out_spec = pl.BlockSpec((None, dil, tm // dil, cols), lambda i: (i // tiles_per_b, 0, i % tiles_per_b, 0))
        out_shape = jax.ShapeDtypeStruct((batch, dil, seq // dil, cols), out_dtype)
        scratch = [pltpu.VMEM((n_blk * N_HEADS, tm, HEAD_DIM), F32)]
        if dil > DEINTERLEAVE_STRIDE:
            assert dil % DEINTERLEAVE_STRIDE == 0 and dil // DEINTERLEAVE_STRIDE <= DEINTERLEAVE_STRIDE
            scratch.append(pltpu.VMEM((n_blk * N_HEADS, DEINTERLEAVE_STRIDE, tm // DEINTERLEAVE_STRIDE, HEAD_DIM), F32))
    out_specs, out_shapes = [out_spec], [out_shape]
    if tail_tiles:
        first_tail = tiles_per_b - tail_tiles
        out_specs.append(pl.BlockSpec((None, tm * KV_ROWS, HEAD_DIM),
                                      lambda i: (i // tiles_per_b, jnp.maximum(i % tiles_per_b - first_tail, 0), 0)))
        out_shapes.append(jax.ShapeDtypeStruct((batch, tail_rows * KV_ROWS, HEAD_DIM), F32))
    outs = pl.pallas_call(
        functools.partial(_proj_kernel, n_blk=n_blk, dil=dil, tail_tiles=tail_tiles, tiles_per_b=tiles_per_b),
        grid=(m // tm,),
        in_specs=[pl.BlockSpec((tm, k), lambda i: (i, 0)), pl.BlockSpec((1, k), lambda i: (0, 0))] + w_specs + b_specs,
        out_specs=out_specs,
        out_shape=out_shapes,
        scratch_shapes=scratch,
        compiler_params=_cparams("arbitrary"),
        name="proj",
    )(x, gain.reshape(1, k), *([w] * n_blk), *([bias] * n_blk))
    return outs if tail_tiles else outs[0]


def _proj_groups_kernel(x_ref, g_ref, *refs, sizes):
    n = sum(sizes)
    w_refs, b_refs, o_refs = refs[:n], refs[n:2 * n], refs[2 * n:]
    hn = _rms(x_ref[...], g_ref[...]).astype(BF16)
    k = 0
    for o_ref, size in zip(o_refs, sizes):
        for j in range(size):
            acc = jnp.dot(hn, w_refs[k][...], preferred_element_type=F32) + b_refs[k][...]
            o_ref[:, j * BRANCH_W:(j + 1) * BRANCH_W] = acc.astype(o_ref.dtype)
            k += 1


def proj_groups(x, gain, w, bias, groups, out_dtypes, tm):
    m, k = x.shape
    assert m % tm == 0
    blocks = [c for grp in groups for c in grp]
    once = lambda shape, c: pl.BlockSpec(shape, lambda i, c=c: (0, c), pipeline_mode=pl.Buffered(1))
    return pl.pallas_call(
        functools.partial(_proj_groups_kernel, sizes=tuple(len(grp) for grp in groups)),
        grid=(m // tm,),
        in_specs=[pl.BlockSpec((tm, k), lambda i: (i, 0)), pl.BlockSpec((1, k), lambda i: (0, 0))]
        + [once((k, BRANCH_W), c) for c in blocks] + [once((1, BRANCH_W), c) for c in blocks],
        out_specs=[pl.BlockSpec((tm, len(grp) * BRANCH_W), lambda i: (i, 0)) for grp in groups],
        out_shape=[jax.ShapeDtypeStruct((m, len(grp) * BRANCH_W), dt) for grp, dt in zip(groups, out_dtypes)],
        compiler_params=_cparams("parallel"),
        name="proj_groups",
    )(x, gain.reshape(1, k), *([w] * len(blocks)), *([bias] * len(blocks)))


def _dil_prompt_kernel(q_ref, kp_ref, kc_ref, vp_ref, vc_ref, bias_ref, o_ref, lse_ref, *scratch, dil, qb):
    scale = 1.0 / math.sqrt(HEAD_DIM)
    no_prev = (pl.program_id(1) == 0) & (lax.broadcasted_iota(jnp.int32, (1, 2 * BAND), 1) < BAND)

    def band_blocks(blocks):
        items = [(r, j, h, slice(j * BAND, (j + 1) * BAND), slice(h * HEAD_DIM, (h + 1) * HEAD_DIM))
                 for r, j in blocks for h in range(N_HEADS)]

        def keys(cur_ref, prev_ref, r, j, rows, hs):
            if j == 0:
                return jnp.concatenate([prev_ref[r, :, hs], cur_ref[r, rows, hs]], axis=0)
            return cur_ref[r, (j - 1) * BAND:(j + 1) * BAND, hs]

        logits = []
        for r, j, h, rows, hs in items:
            l = lax.dot_general(q_ref[r, rows, hs], keys(kc_ref, kp_ref, r, j, rows, hs), NT_DIMS,
                                preferred_element_type=F32) * scale + bias_ref[h]
            logits.append(jnp.where(no_prev, NEG_INF, l) if j == 0 else l)
        probs = []
        for l in logits:
            m = jnp.max(jnp.maximum(l[:, :BAND], l[:, BAND:]), axis=-1, keepdims=True)
            p = jnp.exp(l - m)
            s = jnp.sum(p[:, :BAND] + p[:, BAND:], axis=-1, keepdims=True)
            probs.append((p.astype(BF16), s, jnp.broadcast_to(m + jnp.log(s), (BAND, LSE_LANES))))
        outs = [jnp.dot(p, keys(vc_ref, vp_ref, r, j, rows, hs), preferred_element_type=F32) / s
                for (r, j, h, rows, hs), (p, s, _) in zip(items, probs)]
        return [(outs[N_HEADS * n:N_HEADS * (n + 1)],
                 jnp.concatenate([lse for _, _, lse in probs[N_HEADS * n:N_HEADS * (n + 1)]], axis=1))
                for n in range(len(blocks))]

    if dil == 1:
        for j, (outs, lse) in enumerate(band_blocks([(0, j) for j in range(qb)])):
            for h in range(N_HEADS):
                o_ref[j * BAND:(j + 1) * BAND, h * HEAD_DIM:(h + 1) * HEAD_DIM] = outs[h].astype(o_ref.dtype)
            lse_ref[j * BAND:(j + 1) * BAND, :] = lse
        return

    o_scr, lse_scr = scratch
    per_iter = min(dil, ATT_STEP_BLOCKS // qb)

    def body(it, carry):
        blocks = [(it * per_iter + rr, j) for rr in range(per_iter) for j in range(qb)]
        for (r, j), (outs, lse) in zip(blocks, band_blocks(blocks)):
            for h in range(N_HEADS):
                o_scr[h, pl.ds(r + j * dil * BAND, BAND, stride=dil), :] = outs[h]
            lse_scr[pl.ds(r + j * dil * BAND, BAND, stride=dil), :] = lse
        return carry

    lax.fori_loop(0, dil // per_iter, body, 0)
    for h in range(N_HEADS):
        o_ref[:, h * HEAD_DIM:(h + 1) * HEAD_DIM] = o_scr[h].astype(o_ref.dtype)
    lse_ref[...] = lse_scr[...]


def dilated_prompt(qkv, bias, g, dil, batch, seq, qb):
    assert seq % (dil * BAND * qb) == 0
    sub = seq // dil
    span = dil * BAND * qb
    cur = lambda part: pl.BlockSpec((None, dil, qb * BAND, BRANCH_W), lambda b, i: (b, 0, i, part))
    prev = lambda part: pl.BlockSpec((None, dil, BAND, BRANCH_W),
                                     lambda b, i: (b, 0, jnp.maximum(i * qb - 1, 0), part))
    bias_spec = pl.BlockSpec((N_HEADS, BAND, 2 * BAND), lambda b, i: (0, 0, 0))
    scratch = [] if dil == 1 else [pltpu.VMEM((N_HEADS, span, HEAD_DIM), F32),
                                   pltpu.VMEM((span, N_HEADS * LSE_LANES), F32)]
    o, lse = pl.pallas_call(
        functools.partial(_dil_prompt_kernel, dil=dil, qb=qb),
        grid=(batch, sub // (BAND * qb)),
        in_specs=[cur(0), prev(1), cur(1), prev(2), cur(2), bias_spec],
        out_specs=[pl.BlockSpec((None, span, BRANCH_W), lambda b, i: (b, i, 0)),
                   pl.BlockSpec((None, span, N_HEADS * LSE_LANES), lambda b, i: (b, i, 0))],
        out_shape=[jax.ShapeDtypeStruct((batch, seq, BRANCH_W), BF16),
                   jax.ShapeDtypeStruct((batch, seq, N_HEADS * LSE_LANES), F32)],
        scratch_shapes=scratch,
        compiler_params=_cparams("parallel", "arbitrary"),
        name=f"dilated_prompt_g{g}",
    )(qkv, qkv, qkv, qkv, qkv, bias)
    return o.reshape(batch * seq, BRANCH_W), lse.reshape(batch * seq, N_HEADS * LSE_LANES)


def _dil_sample_kernel(q_ref, kn_ref, vn_ref, cache_ref, bc_ref, bn_ref, shifted_ref, o_ref, lse_ref, newrows_ref,
                       *, t_new):
    del shifted_ref
    scale = 1.0 / math.sqrt(HEAD_DIM)
    bb, n_grp, grp_rows, _ = cache_ref.shape
    n_keys = n_grp * grp_rows // KV_ROWS
    rows_of = lambda b, first: cache_ref[b, :, pl.ds(first, grp_rows // KV_ROWS, stride=KV_ROWS), :].reshape(
        n_keys, HEAD_DIM).astype(BF16)
    heads = [(b, h, slice(h * HEAD_DIM, (h + 1) * HEAD_DIM)) for b in range(bb) for h in range(N_HEADS)]
    logits = []
    for b, h, hs in heads:
        q = q_ref[b, :, hs]
        lc = lax.dot_general(q.astype(BF16), rows_of(b, h), NT_DIMS, preferred_element_type=F32) * scale + bc_ref[h]
        ln = lax.dot_general(q, kn_ref[b, :, hs], NT_DIMS, preferred_element_type=F32) * scale + bn_ref[h]
        logits.append((lc, ln))
    probs = []
    for lc, ln in logits:
        m = jnp.maximum(jnp.max(lc, axis=-1, keepdims=True), jnp.max(ln, axis=-1, keepdims=True))
        pc = jnp.exp(lc - m)
        pn = jnp.exp(ln - m)
        s = jnp.sum(pc, axis=-1, keepdims=True) + jnp.sum(pn, axis=-1, keepdims=True)
        probs.append((pc.astype(BF16), pn, s, m + jnp.log(s)))
    for (b, h, hs), (pc, pn, s, lse) in zip(heads, probs):
        kn = kn_ref[b, :, hs]
        vn = vn_ref[b, :, hs]
        o = (jnp.dot(pc, rows_of(b, N_HEADS + h), preferred_element_type=F32)
             + jnp.dot(pn, vn, preferred_element_type=F32))
        o_ref[b, :, hs] = o / s
        lse_ref[b, :, h * LSE_LANES:(h + 1) * LSE_LANES] = jnp.broadcast_to(lse, (t_new, LSE_LANES))
        newrows_ref[b, pl.ds(h, t_new, stride=KV_ROWS), :] = kn
        newrows_ref[b, pl.ds(N_HEADS + h, t_new, stride=KV_ROWS), :] = vn


def _sample_period(win, dil, t_new):
    return (dil, t_new) if (dil > t_new and win % dil == 0) else (win, win)


def _sample_positions(win, dil, t_new):
    period, take = _sample_period(win, dil, t_new)
    return (np.arange(win // period)[:, None] * period + np.arange(take)[None, :]).reshape(-1)


def dilated_sample(qkv, cache, shifted, bias_cache, bias_new, g, dil, batch, t_new, bb):
    win = cache.shape[1] // KV_ROWS
    assert win % t_new == 0 and shifted.shape == cache.shape and batch % bb == 0
    period, take = _sample_period(win, dil, t_new)
    n_grp = win // period
    cache4 = cache.reshape(batch, n_grp, period * KV_ROWS, HEAD_DIM)
    assert bias_cache.shape == (N_HEADS, t_new, n_grp * take)
    blk = (bb, t_new, BRANCH_W)
    o, lse, newc = pl.pallas_call(
        functools.partial(_dil_sample_kernel, t_new=t_new),
        grid=(batch // bb,),
        in_specs=[pl.BlockSpec(blk, lambda b: (b, 0, 0)),
                  pl.BlockSpec(blk, lambda b: (b, 0, 1)),
                  pl.BlockSpec(blk, lambda b: (b, 0, 2)),
                  pl.BlockSpec((bb, n_grp, take * KV_ROWS, HEAD_DIM), lambda b: (b, 0, 0, 0)),
                  pl.BlockSpec((N_HEADS, t_new, n_grp * take), lambda b: (0, 0, 0)),
                  pl.BlockSpec((N_HEADS, t_new, t_new), lambda b: (0, 0, 0)),
                  pl.BlockSpec(memory_space=pl.ANY)],
        out_specs=[pl.BlockSpec(blk, lambda b: (b, 0, 0)),
                   pl.BlockSpec((bb, t_new, N_HEADS * LSE_LANES), lambda b: (b, 0, 0)),
                   pl.BlockSpec((bb, t_new * KV_ROWS, HEAD_DIM), lambda b: (b, win // t_new - 1, 0))],
        out_shape=[jax.ShapeDtypeStruct((batch, t_new, BRANCH_W), F32),
                   jax.ShapeDtypeStruct((batch, t_new, N_HEADS * LSE_LANES), F32),
                   jax.ShapeDtypeStruct(cache.shape, F32)],
        input_output_aliases={6: 2},
        compiler_params=_cparams("parallel"),
        name=f"dilated_sample_g{g}",
    )(qkv, qkv, qkv, cache4, bias_cache, bias_new, shifted)
    return o.reshape(batch * t_new, BRANCH_W), lse.reshape(batch * t_new, N_HEADS * LSE_LANES), newc


def _mem_attn_kernel(q_ref, kv_ref, o_ref, *, mem):
    scale = 1.0 / math.sqrt(HEAD_DIM)
    heads = [(b, h, slice(h * HEAD_DIM, (h + 1) * HEAD_DIM)) for b in range(q_ref.shape[0]) for h in range(N_HEADS)]
    rows_of = lambda b, first: kv_ref[b, pl.ds(first, mem, stride=KV_ROWS), :].astype(BF16)
    logits = [lax.dot_general(q_ref[b, :, hs].astype(BF16), rows_of(b, h), NT_DIMS, preferred_element_type=F32) * scale
              for b, h, hs in heads]
    probs = []
    for l in logits:
        half = l.shape[1] // 2
        m = jnp.max(jnp.maximum(l[:, :half], l[:, half:]), axis=-1, keepdims=True)
        p = jnp.exp(l - m)
        probs.append((p.astype(BF16), jnp.sum(p[:, :half] + p[:, half:], axis=-1, keepdims=True)))
    for (b, h, hs), (p, s) in zip(heads, probs):
        o = jnp.dot(p, rows_of(b, N_HEADS + h), preferred_element_type=F32)
        o_ref[b, :, hs] = (o / s).astype(o_ref.dtype)


def memory_attention(rec, mem_kv, batch, t, tq, bb, out_dtype):
    mem = mem_kv.shape[1] // KV_ROWS
    assert batch % bb == 0
    x = rec.reshape(batch, t, REC_COLS)
    o = pl.pallas_call(
        functools.partial(_mem_attn_kernel, mem=mem),
        grid=(batch // bb, t // tq),
        in_specs=[pl.BlockSpec((bb, tq, BRANCH_W), lambda b, i: (b, i, REC_MQ)),
                  pl.BlockSpec((bb, mem * KV_ROWS, HEAD_DIM), lambda b, i: (b, 0, 0))],
        out_specs=pl.BlockSpec((bb, tq, BRANCH_W), lambda b, i: (b, i, 0)),
        out_shape=jax.ShapeDtypeStruct((batch, t, BRANCH_W), out_dtype),
        compiler_params=_cparams("parallel", "parallel"),
        name="memory_attention",
    )(x, mem_kv)
    return o.reshape(batch * t, BRANCH_W)


def _hgrn_tables(chunk):
    c, b = chunk, HG_BASE
    t = np.arange(c)[:, None]
    u = np.arange(c)[None, :]
    masks = []
    size = 2 * b
    while size <= c:
        half = size // 2
        masks.append(((t // size) == (u // size)) & ((t % size) >= half) & ((u % size) < half))
        size *= 2
    masks.append(((t // b) == (u // b)) & (u <= t))
    erep = np.zeros((b * HEAD_DIM, c), np.float32)
    for s in range(b):
        erep[s * HEAD_DIM:(s + 1) * HEAD_DIM, np.arange(c) % b == s] = 1.0
    return (u <= t).astype(np.float32), erep, np.stack(masks).astype(np.float32)


def _shift_step(old_refs, new_refs, buf_refs, zero_ref, sem_in, sem_out, sem_zero, step, n_steps, drop):
    slot = step % 2
    other = 1 - slot

    def copies(g, chunk_idx, buf_slot):
        old, new, buf = old_refs[g], new_refs[g], buf_refs[g]
        rows_c = buf.shape[1]
        per_batch = (old.shape[1] - drop) // rows_c
        b = chunk_idx // per_batch
        lo = pl.multiple_of((chunk_idx % per_batch) * rows_c, 8)
        return (pltpu.make_async_copy(old.at[b, pl.ds(lo + drop, rows_c), :], buf.at[buf_slot], sem_in.at[g, buf_slot]),
                pltpu.make_async_copy(buf.at[buf_slot], new.at[b, pl.ds(lo, rows_c), :], sem_out.at[g, buf_slot]))

    def zero_copy(g):
        keep = old_refs[g].shape[1] - drop
        return pltpu.make_async_copy(zero_ref, new_refs[g].at[:, pl.ds(keep, drop), :], sem_zero.at[g])

    groups = range(len(old_refs))

    @pl.when(step == 0)
    def _():
        zero_ref[...] = jnp.zeros_like(zero_ref)
        for g in groups:
            zero_copy(g).start(priority=SIDE_DMA_PRIORITY)
            copies(g, 0, 0)[0].start(priority=SIDE_DMA_PRIORITY)

    for g in groups:
        copies(g, step, slot)[0].wait()

    @pl.when(step >= 1)
    def _():
        for g in groups:
            copies(g, step - 1, other)[1].wait()

    for g in groups:
        copies(g, step, slot)[1].start(priority=SIDE_DMA_PRIORITY)

    @pl.when(step + 1 < n_steps)
    def _():
        for g in groups:
            copies(g, step + 1, other)[0].start(priority=SIDE_DMA_PRIORITY)

    @pl.when(step == n_steps - 1)
    def _():
        for g in groups:
            copies(g, step, slot)[1].wait()
            zero_copy(g).wait()


def _shift_scratch(shift, drop, n_steps):
    if not shift:
        return []
    sb = shift[0].shape[0]
    assert n_steps % sb == 0 and all(s.shape[0] == sb and s.shape[2] == HEAD_DIM for s in shift)
    per_batch = n_steps // sb
    chunk_rows = [(s.shape[1] - drop) // per_batch for s in shift]
    assert all(r % 8 == 0 and r * per_batch == s.shape[1] - drop for r, s in zip(chunk_rows, shift))
    n = len(shift)
    return ([pltpu.VMEM((2, r, HEAD_DIM), F32) for r in chunk_rows]
            + [pltpu.VMEM((sb, drop, HEAD_DIM), F32), pltpu.SemaphoreType.DMA((n, 2)),
               pltpu.SemaphoreType.DMA((n, 2)), pltpu.SemaphoreType.DMA((n,))])


def _hgrn_kernel(q_ref, v_ref, f_ref, gate_ref, lbl_ref, gn_ref, s0_ref, tri_ref, erep_ref, mask_ref, *rest,
                 chunk, rows_in, n_sub, bb, n_levels, n_shift, shift_drop, n_steps):
    old_refs, rest = rest[:n_shift], rest[n_shift:]
    o_ref, s_out_ref = rest[:2]
    new_refs, rest = rest[2:2 + n_shift], rest[2 + n_shift:]
    st_ref = rest[0]
    c = chunk
    step = pl.program_id(1)
    if n_shift:
        buf_refs, (zero_ref, sem_in, sem_out, sem_zero) = rest[1:1 + n_shift], rest[1 + n_shift:]
        _shift_step(old_refs, new_refs, buf_refs, zero_ref, sem_in, sem_out, sem_zero,
                    pl.program_id(0) * pl.num_programs(1) + step, n_steps, shift_drop)

    @pl.when(step == 0)
    def _():
        for b in range(bb):
            for h in range(N_HEADS):
                st_ref[b * N_HEADS + h] = s0_ref[b, h].T

    lbl = lbl_ref[...]
    e = jnp.exp(lbl - jnp.max(lbl, axis=0, keepdims=True))
    lb = e[0:1] / jnp.sum(e, axis=0, keepdims=True)
    tri = tri_ref[...]

    def rows(ref, b, ci):
        x = ref[b, ci * rows_in:(ci + 1) * rows_in, :].astype(F32)
        if rows_in < c:
            x = jnp.concatenate([x, jnp.zeros((c - rows_in, x.shape[1]), F32)], axis=0)
        return x

    def at_row(x, size, idx):
        x3 = x.reshape(c // size, size, x.shape[1])
        return jnp.broadcast_to(x3[:, idx:idx + 1, :], x3.shape).reshape(x.shape)

    chunk_ids = [(b, ci) for b in range(bb) for ci in range(n_sub)]
    chunks = range(len(chunk_ids))
    heads = [(h, slice(h * HEAD_DIM, (h + 1) * HEAD_DIM)) for h in range(N_HEADS)]

    q, v, kk, cum = [], [], [], []
    for b, ci in chunk_ids:
        q.append(rows(q_ref, b, ci))
        v.append(rows(v_ref, b, ci).astype(BF16))
        f = lb + (1.0 - lb) * _sigmoid(rows(f_ref, b, ci))
        g = jnp.log(f)
        k_in = 1.0 - f
        if rows_in < c:
            live = lax.broadcasted_iota(jnp.int32, (c, 1), 0) < rows_in
            g = jnp.where(live, g, 0.0)
            k_in = jnp.where(live, k_in, 0.0)
        kk.append(k_in)
        g1 = g.astype(BF16)
        r1 = g - g1.astype(F32)
        g2 = r1.astype(BF16)
        g3 = (r1 - g2.astype(F32)).astype(BF16)
        cum.append(jnp.dot(tri, g1, preferred_element_type=F32) + jnp.dot(tri, g2, preferred_element_type=F32)
                   + jnp.dot(tri, g3, preferred_element_type=F32))

    q_inter, k_end, g_end, q_lvl, k_lvl, x_diag = [], [], [], [], [], []
    for ci in chunks:
        q_inter.append((q[ci] * jnp.exp(cum[ci])).astype(BF16))
        g_end.append(cum[ci][c - 1:c])
        k_end.append((kk[ci] * jnp.exp(g_end[ci] - cum[ci])).astype(BF16))
        ql, kl = [], []
        for l in range(n_levels):
            size = 2 * HG_BASE << l
            ref = at_row(cum[ci], size, size // 2 - 1)
            ql.append((q[ci] * jnp.exp(cum[ci] - ref)).astype(BF16))
            kl.append((kk[ci] * jnp.exp(ref - cum[ci])).astype(BF16))
        q_lvl.append(ql)
        k_lvl.append(kl)
        xd = []
        for s in range(HG_BASE):
            decay = jnp.exp(jnp.minimum(cum[ci] - at_row(cum[ci], HG_BASE, s), 0.0))
            xd.append((q[ci] * at_row(kk[ci], HG_BASE, s) * decay).astype(BF16))
        x_diag.append(xd)

    level_mask = [mask_ref[l] > 0.5 for l in range(n_levels)]
    scores, state_in = [], []
    for ci in chunks:
        sc, si = [], []
        for h, hs in heads:
            s_h = mask_ref[n_levels] * jnp.dot(jnp.concatenate([x[:, hs] for x in x_diag[ci]], axis=1),
                                               erep_ref[...], preferred_element_type=F32)
            for l in range(n_levels):
                s_h += jnp.where(level_mask[l], lax.dot_general(q_lvl[ci][l][:, hs], k_lvl[ci][l][:, hs], NT_DIMS,
                                                                preferred_element_type=F32), 0.0)
            sc.append(s_h.astype(BF16))
            si.append(lax.dot_general(v[ci][:, hs], k_end[ci][:, hs], TN_DIMS, preferred_element_type=F32))
        scores.append(sc)
        state_in.append(si)

    states = {}
    for b in range(bb):
        for h, hs in heads:
            st = st_ref[b * N_HEADS + h]
            for ci in range(n_sub):
                idx = b * n_sub + ci
                states[idx, h] = st.astype(BF16)
                st = st * jnp.exp(g_end[idx][:, hs]) + state_in[idx][h]
            st_ref[b * N_HEADS + h] = st
    for idx, (b, ci) in enumerate(chunk_ids):
        gate = _sigmoid(rows(gate_ref, b, ci))
        for h, hs in heads:
            o = (lax.dot_general(q_inter[idx][:, hs], states[idx, h], NT_DIMS, preferred_element_type=F32)
                 + jnp.dot(scores[idx][h], v[idx][:, hs], preferred_element_type=F32))
            y = _rms(o, gn_ref[...]) * gate[:, hs]
            o_ref[b, ci * rows_in:(ci + 1) * rows_in, hs] = y[0:rows_in].astype(o_ref.dtype)

    @pl.when(step == pl.num_programs(1) - 1)
    def _():
        for b in range(bb):
            for h in range(N_HEADS):
                s_out_ref[b, h] = st_ref[b * N_HEADS + h].T


def hgrn2(rec, gates, lb_logits, hg_norm, state0, batch, t, chunk, n_sub, bb, out_dtype, shift=(), shift_drop=0):
    rows_in = min(chunk, t)
    step_rows = rows_in * n_sub
    assert t % step_rows == 0 and chunk % HG_BASE == 0 and (n_sub == 1 or rows_in == chunk) and batch % bb == 0
    n_steps = (batch // bb) * (t // step_rows)
    n_shift = len(shift)
    any_spec = pl.BlockSpec(memory_space=pl.ANY)
    shift_scratch = _shift_scratch(shift, shift_drop, n_steps)
    tri, erep, masks = _hgrn_tables(chunk)
    n_levels = masks.shape[0] - 1
    a = rec.reshape(batch, t, REC_COLS)
    gt = gates.reshape(batch, t, G_COLS)
    blk = (bb, step_rows, BRANCH_W)
    full = lambda arr: pl.BlockSpec(arr.shape, lambda b, i: (0,) * arr.ndim)
    st_spec = pl.BlockSpec((bb, N_HEADS, HEAD_DIM, HEAD_DIM), lambda b, i: (b, 0, 0, 0))
    tri = jnp.asarray(tri, BF16)
    erep = jnp.asarray(erep, BF16)
    masks = jnp.asarray(masks, F32)
    lbl = lb_logits.astype(F32)
    gn = hg_norm.reshape(1, HEAD_DIM).astype(F32)
    o, s_fin, *shifted = pl.pallas_call(
        functools.partial(_hgrn_kernel, chunk=chunk, rows_in=rows_in, n_sub=n_sub, bb=bb, n_levels=n_levels,
                          n_shift=n_shift, shift_drop=shift_drop, n_steps=n_steps),
        grid=(batch // bb, t // step_rows),
        in_specs=[pl.BlockSpec(blk, lambda b, i: (b, i, REC_HQ)),
                  pl.BlockSpec(blk, lambda b, i: (b, i, REC_HI)),
                  pl.BlockSpec(blk, lambda b, i: (b, i, 0)),
                  pl.BlockSpec(blk, lambda b, i: (b, i, 1)),
                  full(lbl), full(gn), st_spec, full(tri), full(erep), full(masks)] + [any_spec] * n_shift,
        out_specs=[pl.BlockSpec(blk, lambda b, i: (b, i, 0)), st_spec] + [any_spec] * n_shift,
        out_shape=[jax.ShapeDtypeStruct((batch, t, BRANCH_W), out_dtype),
                   jax.ShapeDtypeStruct(state0.shape, F32)] + [jax.ShapeDtypeStruct(s.shape, F32) for s in shift],
        scratch_shapes=[pltpu.VMEM((bb * N_HEADS, HEAD_DIM, HEAD_DIM), F32)] + shift_scratch,
        compiler_params=_cparams("arbitrary", "arbitrary"),
        name="hgrn2",
    )(a, a, gt, gt, lbl, gn, state0, tri, erep, masks, *shift)
    return o.reshape(batch * t, BRANCH_W), s_fin, shifted


def _merge_kernel(x_ref, o1_ref, o2_ref, o3_ref, l1_ref, l2_ref, l3_ref, hg_ref, mem_ref, gpre_ref,
                  wga_ref, wgh_ref, wgm_ref, bga_ref, bgh_ref, bgm_ref,
                  wa_ref, wh_ref, wm_ref, wo_ref, gain_ref, *rest, n_shift, shift_drop, n_steps):
    old_refs, out_ref, new_refs, scratch = rest[:n_shift], rest[n_shift], rest[n_shift + 1:2 * n_shift + 1], \
        rest[2 * n_shift + 1:]
    if n_shift:
        _shift_step(old_refs, new_refs, scratch[:n_shift], *scratch[n_shift:], pl.program_id(0), n_steps, shift_drop)
    tm = x_ref.shape[0]
    x = x_ref[...]
    hn = _rms(x, gpre_ref[...]).astype(BF16)
    gate_logits = [jnp.dot(hn, w_ref[...], preferred_element_type=F32) + b_ref[...]
                   for w_ref, b_ref in ((wga_ref, bga_ref), (wgh_ref, bgh_ref), (wgm_ref, bgm_ref))]
    from_hg = jnp.dot(hg_ref[...].astype(BF16), wh_ref[...], preferred_element_type=F32)
    from_mem = jnp.dot(mem_ref[...].astype(BF16), wm_ref[...], preferred_element_type=F32)
    l1, l2, l3 = l1_ref[...], l2_ref[...], l3_ref[...]
    m = jnp.maximum(jnp.maximum(l1, l2), l3)
    e1, e2, e3 = jnp.exp(l1 - m), jnp.exp(l2 - m), jnp.exp(l3 - m)
    den = e1 + e2 + e3
    w1, w2, w3 = e1 / den, e2 / den, e3 / den
    att = []
    for h in range(N_HEADS):
        hs = slice(h * HEAD_DIM, (h + 1) * HEAD_DIM)
        col = slice(h * LSE_LANES, h * LSE_LANES + 1)
        bc = lambda w: jnp.broadcast_to(w[:, col], (tm, HEAD_DIM))
        att.append((bc(w1) * o1_ref[:, hs].astype(F32) + bc(w2) * o2_ref[:, hs].astype(F32)
                    + bc(w3) * o3_ref[:, hs].astype(F32)).astype(BF16))
    att = jnp.concatenate(att, axis=1)
    ga, gh, gm = (_sigmoid(z) for z in gate_logits)
    merged = ga * jnp.dot(att, wa_ref[...], preferred_element_type=F32) + gh * from_hg + gm * from_mem
    y = jnp.dot(merged.astype(BF16), wo_ref[...], preferred_element_type=F32)
    out_ref[...] = x + _rms(y, gain_ref[...])


def merge(x, o_groups, lse_groups, hg_o, mem_o, gain_pre, w_in, b_in, w_att, w_hg, w_mem, w_out, gain, tm,
          shift=(), shift_drop=0):
    m, d = x.shape
    assert m % tm == 0 and GATE_MERGE_BLOCKS[0] * d == 14 * BRANCH_W
    n_steps = m // tm
    n_shift = len(shift)
    any_spec = pl.BlockSpec(memory_space=pl.ANY)
    row = lambda w: pl.BlockSpec((tm, w), lambda i: (i, 0))
    resident = lambda shape, j=0: pl.BlockSpec(shape, lambda i: (0, j), pipeline_mode=pl.Buffered(1))
    x1, *shifted = pl.pallas_call(
        functools.partial(_merge_kernel, n_shift=n_shift, shift_drop=shift_drop, n_steps=n_steps),
        grid=(n_steps,),
        in_specs=[row(d), row(BRANCH_W), row(BRANCH_W), row(BRANCH_W),
                  row(N_HEADS * LSE_LANES), row(N_HEADS * LSE_LANES), row(N_HEADS * LSE_LANES),
                  row(BRANCH_W), row(BRANCH_W), resident((1, d))]
        + [resident((d, d), j) for j in GATE_MERGE_BLOCKS] + [resident((1, d), j) for j in GATE_MERGE_BLOCKS]
        + [resident(w_att.shape), resident(w_hg.shape), resident(w_mem.shape), resident(w_out.shape),
           resident((1, d))] + [any_spec] * n_shift,
        out_specs=[row(d)] + [any_spec] * n_shift,
        out_shape=[jax.ShapeDtypeStruct((m, d), F32)] + [jax.ShapeDtypeStruct(s.shape, F32) for s in shift],
        scratch_shapes=_shift_scratch(shift, shift_drop, n_steps),
        compiler_params=_cparams("arbitrary"),
        name="merge",
    )(x, *o_groups, *lse_groups, hg_o, mem_o, gain_pre.reshape(1, d), w_in, w_in, w_in, b_in, b_in, b_in,
      w_att, w_hg, w_mem, w_out, gain.reshape(1, d), *shift)
    return x1, shifted


def _ffn_kernel(x_ref, gpre_ref, wa_ref, wb_ref, cw_ref, cb_ref, wd_ref, gpost_ref, cbuf_ref,
                out_ref, tail_ref, carry_ref, *, t_seq, tiles_per_seq, tf):
    tm = x_ref.shape[0]
    dff = wa_ref.shape[1]
    x = x_ref[...]
    hn = _rms(x, gpre_ref[...]).astype(BF16)

    if tiles_per_seq >= 1:
        @pl.when((pl.program_id(0) % tiles_per_seq) == 0)
        def _():
            carry_ref[6:8, :] = cbuf_ref[0]

        t_idx = lax.broadcasted_iota(jnp.int32, (tm, 1), 0)
    else:
        n_seq = tm // t_seq
        t_idx = lax.broadcasted_iota(jnp.int32, (tm, 1), 0) % t_seq

    n_chunks = dff // tf

    def up_proj(j):
        cols = slice(j * tf, (j + 1) * tf)
        return (jnp.dot(hn, wa_ref[:, cols], preferred_element_type=F32),
                jnp.dot(hn, wb_ref[:, cols], preferred_element_type=F32))

    acc = jnp.zeros(x.shape, F32)
    ahead = up_proj(0)
    for j in range(n_chunks):
        cols = slice(j * tf, (j + 1) * tf)
        a, up = ahead
        if j + 1 < n_chunks:
            ahead = up_proj(j + 1)
        if tiles_per_seq >= 1:
            prev1 = carry_ref[7:8, cols]
            prev2 = carry_ref[6:7, cols]
            carry_ref[:, cols] = a[tm - 8:tm]
            tail_ref[0, :, cols] = a[tm - 8:tm]
        else:
            prev1 = jnp.broadcast_to(cbuf_ref[:, 1:2, cols], (n_seq, t_seq, tf)).reshape(tm, tf)
            prev2 = jnp.broadcast_to(cbuf_ref[:, 0:1, cols], (n_seq, t_seq, tf)).reshape(tm, tf)
            tail_ref[:, :, cols] = a.reshape(n_seq, t_seq, tf)
        a1 = jnp.where(t_idx >= 1, pltpu.roll(a, 1, 0), prev1)
        a2 = jnp.where(t_idx >= 2, pltpu.roll(a, 2, 0), jnp.where(t_idx == 1, prev1, prev2))
        conv = cb_ref[:, cols] + a2 * cw_ref[0:1, cols] + a1 * cw_ref[1:2, cols] + a * cw_ref[2:3, cols]
        act = conv * _sigmoid(conv) * up
        acc += jnp.dot(act.astype(BF16), wd_ref[cols, :], preferred_element_type=F32)

    out_ref[...] = x + _rms(acc, gpost_ref[...])


def conv_ffn(x, conv_buf, g_pre, w_a, w_b, conv_w, conv_b, w_d, g_post, t_seq, tm, tf):
    m, d = x.shape
    dff = w_a.shape[1]
    n_seq = m // t_seq
    assert m % tm == 0 and dff % tf == 0 and t_seq >= 8
    if tm <= t_seq:
        assert t_seq % tm == 0
        tiles_per_seq = t_seq // tm
        seq_blk = 1
        seq_idx = lambda i: (i // tiles_per_seq, 0, 0)
    else:
        assert tm % t_seq == 0 and t_seq == 8
        tiles_per_seq = 0
        seq_blk = tm // t_seq
        seq_idx = lambda i: (i, 0, 0)
    n_tail = (m // tm) * seq_blk
    resident = lambda shape: pl.BlockSpec(shape, lambda i: (0, 0), pipeline_mode=pl.Buffered(1))
    y, tail = pl.pallas_call(
        functools.partial(_ffn_kernel, t_seq=t_seq, tiles_per_seq=tiles_per_seq, tf=tf),
        grid=(m // tm,),
        in_specs=[pl.BlockSpec((tm, d), lambda i: (i, 0)), resident((1, d)),
                  resident((d, dff)), resident((d, dff)), resident((3, dff)), resident((1, dff)),
                  resident((dff, d)), resident((1, d)),
                  pl.BlockSpec((seq_blk, 2, dff), seq_idx)],
        out_specs=[pl.BlockSpec((tm, d), lambda i: (i, 0)),
                   pl.BlockSpec((seq_blk, 8, dff), lambda i: (i, 0, 0))],
        out_shape=[jax.ShapeDtypeStruct((m, d), F32), jax.ShapeDtypeStruct((n_tail, 8, dff), F32)],
        scratch_shapes=[pltpu.VMEM((8, dff), F32)],
        compiler_params=_cparams("arbitrary"),
        name="conv_ffn",
    )(x, g_pre.reshape(1, d), w_a, w_b, conv_w, conv_b.reshape(1, dff), w_d, g_post.reshape(1, d), conv_buf)
    if tiles_per_seq >= 1:
        tail = tail.reshape(n_seq, tiles_per_seq, 8, dff)[:, -1]
    return y, tail


def _rel_buckets(dil):
    max_exact = REL_BUCKETS // 2
    dist = np.arange(N_LAGS + 1, dtype=np.int32) * dil
    d = np.maximum(dist, 1).astype(np.float32)
    large = max_exact + (np.log(d / np.float32(max_exact)) / np.float32(math.log(REL_MAX_DIST / max_exact))
                         * np.float32(REL_BUCKETS - max_exact)).astype(np.int32)
    large = np.minimum(large, REL_BUCKETS - 1)
    return np.where(dist < max_exact, dist, large)


def _bias_table(rel_bias, g, dil, dist):
    dist = np.asarray(dist)
    ok = (dist >= 0) & (dist % dil == 0) & (dist <= N_LAGS * dil)
    bucket = np.where(ok, _rel_buckets(dil)[np.clip(dist // dil, 0, N_LAGS)], -1)
    onehot = bucket[..., None] == np.arange(REL_BUCKETS)
    heads = rel_bias.astype(F32)[:, g * N_HEADS:(g + 1) * N_HEADS].T
    picked = jnp.sum(jnp.where(onehot[None], heads.reshape((N_HEADS,) + (1,) * dist.ndim + (REL_BUCKETS,)), 0.0), -1)
    return jnp.where(ok[None], picked, NEG_INF)


def _prompt_bias(rel_bias, g, dil):
    return _bias_table(rel_bias, g, dil, (BAND + np.arange(BAND)[:, None] - np.arange(2 * BAND)[None, :]) * dil)


def _sample_bias(rel_bias, g, dil, cached_pos, n_past, t_new):
    t = np.arange(t_new)[:, None]
    return (_bias_table(rel_bias, g, dil, n_past + t - np.asarray(cached_pos)[None, :]),
            _bias_table(rel_bias, g, dil, t - np.arange(t_new)[None, :]))


def _layer(x, batch, t, weights, rel_bias, lb_logits, win_caches, hg_state0, conv_buf0, mem_kv,
           *, prompt, tm_proj, tm_merge, tm_ffn, tf_ffn, tq_mem, chunk, n_sub, shifted=None, side_shift=(),
           side_shift_drop=0):
    (w_in16, b_in, gain_pre, gain_post, hg_norm, w_att, w_hg, w_mem, w_out,
     gain_fpre, gain_fpost, w_fa, w_fb, conv_w, conv_b, w_fd) = weights
    act_dtype = BF16 if prompt else F32
    b_in = b_in.reshape(1, -1)
    kv_tails = {}
    if prompt:
        plain = [g for g, (_, dil) in enumerate(ATT_GROUPS) if dil == 1]
        *qkv_plain, rec, gates = proj_groups(x, gain_pre, w_in16, b_in, [QKV_BLOCKS(g) for g in plain]
                                             + [REC_BLOCKS, GATE_BLOCKS], [act_dtype] * (len(plain) + 1) + [F32], tm_proj)
        qkv = []
        for g, (win, dil) in enumerate(ATT_GROUPS):
            if g in plain:
                qkv.append(qkv_plain[plain.index(g)].reshape(batch, 1, t, QKV_COLS))
                continue
            tail_rows = win if (win % tm_proj == 0 and win <= t) else 0
            out = proj(x, gain_pre, w_in16, b_in, QKV_BLOCKS(g), batch, t, dil, act_dtype, tm_proj, tail_rows)
            if tail_rows:
                out, kv_tails[g] = out
            qkv.append(out)
    else:
        groups = [QKV_BLOCKS(g) for g in range(len(ATT_GROUPS))] + [REC_BLOCKS, GATE_BLOCKS]
        *qkv, rec, gates = proj_groups(x, gain_pre, w_in16, b_in, groups, [act_dtype] * 4 + [F32], tm_proj)
        qkv = [z.reshape(batch, t, QKV_COLS) for z in qkv]

    o_groups, lse_groups, new_caches = [], [], []
    for g, (win, dil) in enumerate(ATT_GROUPS):
        if prompt:
            qb = max(1, min(ATT_STEP_BLOCKS // dil, t // (dil * BAND)))
            o, lse = dilated_prompt(qkv[g], _prompt_bias(rel_bias, g, dil), g, dil, batch, t, qb)
        else:
            cache = win_caches[g]
            n_past = cache.shape[1] // KV_ROWS
            assert n_past >= win
            bcache, bnew = _sample_bias(rel_bias, g, dil, _sample_positions(n_past, dil, t), n_past, t)
            period, take = _sample_period(n_past, dil, t)
            loaded_bytes = cache.shape[1] * HEAD_DIM * 4 * take // period
            bb = math.gcd(batch, max(1, SAMPLE_STEP_BYTES // loaded_bytes))
            o, lse, newc = dilated_sample(qkv[g], cache, shifted[g], bcache, bnew, g, dil, batch, t, bb)
            new_caches.append(newc)
        o_groups.append(o)
        lse_groups.append(lse)

    largest = max(range(len(side_shift)), key=lambda i: side_shift[i].size) if side_shift else None
    with_hgrn = [s for i, s in enumerate(side_shift) if i == largest]
    with_merge = [s for i, s in enumerate(side_shift) if i != largest]
    hg_bb = 1 if prompt else math.gcd(batch, HGRN_SAMPLE_ROWS)
    hg_o, hg_state, from_hgrn = hgrn2(rec, gates, lb_logits, hg_norm, hg_state0, batch, t, chunk, n_sub, hg_bb,
                                      act_dtype, with_hgrn, side_shift_drop)
    mem_bb = 1 if prompt else math.gcd(batch, max(1, SAMPLE_STEP_BYTES // (mem_kv.shape[1] * HEAD_DIM * 4)))
    mem_o = memory_attention(rec, mem_kv, batch, t, tq_mem, mem_bb, act_dtype)
    x1, from_merge = merge(x, o_groups, lse_groups, hg_o, mem_o, gain_pre, w_in16, b_in, w_att, w_hg, w_mem, w_out,
                           gain_post, tm_merge, with_merge, side_shift_drop)
    from_hgrn, from_merge = list(from_hgrn), list(from_merge)
    shifted_out = [from_hgrn.pop(0) if i == largest else from_merge.pop(0) for i in range(len(side_shift))]
    y, tail = conv_ffn(x1, conv_buf0, gain_fpre, w_fa, w_fb, conv_w, conv_b, w_fd, gain_fpost, t, tm_ffn, tf_ffn)
    return y, (qkv, kv_tails), new_caches, hg_state, tail[:, 6:8, :], shifted_out


def kernel(x_prompt, x_sample, mem_prompt, cache_win1_kv, cache_win2_kv, cache_win3_kv, cache_mem_kv, state_hgrn, state_ffn_conv, rel_bias, hg_lb_logits, norm_mix_pre, norm_mix_post, w_in, b_in, hg_norm, mem_norm, w_mem_kv, w_br_att, w_br_hg, w_br_mem, w_out, norm_ffn_pre, norm_ffn_post, w_ffn_a, w_ffn_b, ffn_conv_w, ffn_conv_b, w_ffn_d):
    depth = w_in.shape[0]
    assert depth == 1
    bsz, seq, d = x_prompt.shape
    dbsz, dseq, _ = x_sample.shape
    mem_tokens = mem_prompt.shape[1]
    dff = w_ffn_a.shape[2]
    layer = 0

    assert w_in.shape[2] == W_IN_BLOCKS * BRANCH_W
    weights = (w_in[layer].astype(BF16), b_in[layer],
               norm_mix_pre[layer], norm_mix_post[layer], hg_norm[layer],
               w_br_att[layer].astype(BF16), w_br_hg[layer].astype(BF16), w_br_mem[layer].astype(BF16),
               w_out[layer].astype(BF16), norm_ffn_pre[layer], norm_ffn_post[layer],
               w_ffn_a[layer].astype(BF16), w_ffn_b[layer].astype(BF16), ffn_conv_w[layer], ffn_conv_b[layer],
               w_ffn_d[layer].astype(BF16))

    mem_kv = norm_matmul(mem_prompt.reshape(bsz * mem_tokens, d), mem_norm[layer], w_mem_kv[layer].astype(BF16),
                         jnp.zeros((2 * BRANCH_W,), F32), F32, bsz * mem_tokens, 1024)
    as_rows = lambda c: c[layer].reshape(dbsz, c.shape[2] * KV_ROWS, HEAD_DIM)
    win_caches = [as_rows(c) for c in (cache_win1_kv, cache_win2_kv, cache_win3_kv)]
    yp, (qkv_p, kv_tails), _, hg_p, conv_p, shifted = _layer(
        x_prompt.reshape(bsz * seq, d), bsz, seq, weights, rel_bias, hg_lb_logits, None,
        jnp.zeros((bsz, N_HEADS, HEAD_DIM, HEAD_DIM), F32), jnp.zeros((bsz, 2, dff), F32),
        mem_kv.reshape(bsz, mem_tokens * KV_ROWS, HEAD_DIM),
        prompt=True, tm_proj=1024, tm_merge=512, tm_ffn=512, tf_ffn=2048, tq_mem=1024, chunk=64, n_sub=8,
        side_shift=win_caches, side_shift_drop=dseq * KV_ROWS)
    p_win = []
    for g, (win, dil) in enumerate(ATT_GROUPS):
        n = min(win, seq)
        assert n % dil == 0
        if g in kv_tails:
            p_win.append(kv_tails[g].reshape(1, bsz, n, 2, N_HEADS, HEAD_DIM))
            continue
        tail = qkv_p[g][:, :, (seq - n) // dil:, BRANCH_W:]
        tail = jnp.swapaxes(tail, 1, 2).astype(F32)
        p_win.append(tail.reshape(1, bsz, n, 2, N_HEADS, HEAD_DIM))

    ys, _, new_caches, hg_s, conv_s, _ = _layer(
        x_sample.reshape(dbsz * dseq, d), dbsz, dseq, weights, rel_bias, hg_lb_logits, win_caches,
        state_hgrn[layer], state_ffn_conv[layer], as_rows(cache_mem_kv),
        prompt=False, tm_proj=dbsz * dseq, tm_merge=dbsz * dseq, tm_ffn=dbsz * dseq, tf_ffn=2048, tq_mem=dseq,
        chunk=16, n_sub=1,
        shifted=shifted)
    s_win = [c.reshape(1, dbsz, c.shape[1] // KV_ROWS, 2, N_HEADS, HEAD_DIM) for c in new_caches]

    return (yp.reshape(bsz, seq, d), ys.reshape(dbsz, dseq, d),
            p_win[0], p_win[1], p_win[2],
            hg_p[None], conv_p[None], mem_kv.reshape(1, bsz, mem_tokens, 2, N_HEADS, HEAD_DIM),
            s_win[0], s_win[1], s_win[2],
            hg_s[None], conv_s[None])
```
